```python
import math
import jax, jax.numpy as jnp
from jax import lax
import numpy as np

D_MODEL = 2048
BATCH = 4
SEQ = 4096
DEPTH = 2

CHUNK = 64
Q_BLOCK = 128
N_MIXERS = 2
NORM_EPS = 1e-6
ROPE_THETA = 500000.0

DA_HEADS = 8
DA_HEAD_DIM = D_MODEL // (2 * DA_HEADS)
DA_V_DIM = 2 * DA_HEAD_DIM
ROPE_DIM = DA_HEAD_DIM // 4
DA_IN_COLS = 3 * DA_HEADS * DA_V_DIM

ML_HEADS = 8
ML_V_DIM = D_MODEL // ML_HEADS
ML_QK_DIM = ML_V_DIM // 2
ML_CONV = 4
ML_QK_COLS = 2 * ML_HEADS * ML_QK_DIM
ML_V_COLS = ML_HEADS * ML_V_DIM
ML_IN_COLS = ML_QK_COLS + 2 * ML_V_COLS + 2 * ML_HEADS

MOE_GROUPS = 4
MOE_PER_GROUP = 8
MOE_EXPERTS = MOE_GROUPS * MOE_PER_GROUP
MOE_TOPK = 2
MOE_HIDDEN = D_MODEL // 2
MOE_BLOCK = 128

N_ATTN_LAYERS = (DEPTH + 1) // N_MIXERS
N_MLSTM_LAYERS = DEPTH // N_MIXERS

kernel_name = 'hybrid_diffattn_mlstm_hmoe'


def rms_norm(x, g):
    xf = x.astype(jnp.float32)
    y = xf * lax.rsqrt(jnp.mean(xf * xf, axis=-1, keepdims=True) + NORM_EPS)
    return (y * g.astype(jnp.float32)).astype(x.dtype)


def partial_rope(t, cos, sin):
    half = ROPE_DIM // 2
    t1 = t[..., :half]
    t2 = t[..., half:ROPE_DIM]
    rot = jnp.concatenate([t1 * cos - t2 * sin, t2 * cos + t1 * sin], axis=-1)
    return jnp.concatenate([rot, t[..., ROPE_DIM:]], axis=-1)


def diff_attention(h, positions, w_in, w_out, lq1, lk1, lq2, lk2, head_g, layer_idx):
    B, S, _ = h.shape
    q, k, v = jnp.split(h @ w_in, 3, axis=-1)
    q = q.reshape(B, S, DA_HEADS, 2, DA_HEAD_DIM)
    k = k.reshape(B, S, DA_HEADS, 2, DA_HEAD_DIM)
    v = v.reshape(B, S, DA_HEADS, DA_V_DIM)
    half = ROPE_DIM // 2
    inv_freq = ROPE_THETA ** (-jnp.arange(half, dtype=jnp.float32) * 2.0 / ROPE_DIM)
    ang = positions.astype(jnp.float32)[..., None] * inv_freq
    cos = jnp.cos(ang)[:, :, None, None, :].astype(h.dtype)
    sin = jnp.sin(ang)[:, :, None, None, :].astype(h.dtype)
    q = partial_rope(q, cos, sin) * (DA_HEAD_DIM ** -0.5)
    k = partial_rope(k, cos, sin)
    lam_init = 0.8 - 0.6 * math.exp(-0.3 * layer_idx)
    lam = (jnp.exp(jnp.sum(lq1.astype(jnp.float32) * lk1.astype(jnp.float32)))
           - jnp.exp(jnp.sum(lq2.astype(jnp.float32) * lk2.astype(jnp.float32))) + lam_init)
    outs = []
    for qb in range(S // Q_BLOCK):
        q0 = qb * Q_BLOCK
        kv_end = q0 + Q_BLOCK
        s = jnp.einsum('bqhmd,bkhmd->bhmqk', q[:, q0:kv_end], k[:, :kv_end]).astype(jnp.float32)
        q_chunk = (q0 + jnp.arange(Q_BLOCK)) // CHUNK
        k_chunk = jnp.arange(kv_end) // CHUNK
        mask = k_chunk[None, :] <= q_chunk[:, None]
        p = jax.nn.softmax(jnp.where(mask, s, -jnp.inf), axis=-1)
        a = p[:, :, 0] - lam * p[:, :, 1]
        outs.append(jnp.einsum('bhqk,bkhe->bqhe', a.astype(v.dtype), v[:, :kv_end]))
    o = jnp.concatenate(outs, axis=1)
    o = rms_norm(o, head_g) * (1.0 - lam_init)
    return o.reshape(B, S, DA_HEADS * DA_V_DIM) @ w_out


def mlstm_chunk_step(carry, xs):
    C, n, m = carry
    q, k, v, ig, lf = xs
    causal = jnp.tril(jnp.ones((CHUNK, CHUNK), dtype=bool))
    b = jnp.cumsum(lf, axis=-1)
    dmat = jnp.where(causal, b[..., :, None] - b[..., None, :] + ig[..., None, :], -jnp.inf)
    inter = b + m[..., None]
    m_t = jnp.maximum(inter, jnp.max(dmat, axis=-1))
    w = jnp.exp(dmat - m_t[..., None])
    s = jnp.einsum('bhtd,bhsd->bhts', q, k) * w
    decay = jnp.exp(inter - m_t)
    num = jnp.einsum('bhts,bhse->bhte', s, v) + decay[..., None] * jnp.einsum('bhed,bhtd->bhte', C, q)
    den = jnp.sum(s, axis=-1) + decay * jnp.einsum('bhd,bhtd->bht', n, q)
    h = num / jnp.maximum(jnp.abs(den), jnp.exp(-m_t))[..., None]
    b_last = b[..., -1]
    g = b_last[..., None] - b + ig
    m_new = jnp.maximum(b_last + m, jnp.max(g, axis=-1))
    carry_decay = jnp.exp(b_last + m - m_new)
    wg = jnp.exp(g - m_new[..., None])
    C_new = carry_decay[..., None, None] * C + jnp.einsum('bhs,bhse,bhsd->bhed', wg, v, k)
    n_new = carry_decay[..., None] * n + jnp.einsum('bhs,bhsd->bhd', wg, k)
    return (C_new, n_new, m_new), h


def mlstm_mixer(h, w_in, conv_w, conv_b, gate_b, head_g, w_out):
    B, S, _ = h.shape
    H = ML_HEADS
    proj = h @ w_in
    qk_pre = proj[..., :ML_QK_COLS]
    v = proj[..., ML_QK_COLS:ML_QK_COLS + ML_V_COLS]
    o_pre = proj[..., ML_QK_COLS + ML_V_COLS:ML_QK_COLS + 2 * ML_V_COLS]
    gates = proj[..., ML_QK_COLS + 2 * ML_V_COLS:].astype(jnp.float32) + gate_b.astype(jnp.float32)
    qk = lax.conv_general_dilated(qk_pre, conv_w[:, None, :], window_strides=(1,),
                                  padding=[(ML_CONV - 1, 0)], dimension_numbers=('NWC', 'WIO', 'NWC'),
                                  feature_group_count=ML_QK_COLS)
    qk = jax.nn.silu(qk + conv_b)
    q, k = jnp.split(qk, 2, axis=-1)
    q = q.reshape(B, S, H, ML_QK_DIM).astype(jnp.float32)
    k = k.reshape(B, S, H, ML_QK_DIM).astype(jnp.float32) * (ML_QK_DIM ** -0.5)
    vf = v.reshape(B, S, H, ML_V_DIM).astype(jnp.float32)
    ig = gates[..., :H]
    lf = jax.nn.log_sigmoid(gates[..., H:])
    nc = S // CHUNK

    def to_chunks(t):
        return t.reshape(B, nc, CHUNK, H, -1).transpose(1, 0, 3, 2, 4)

    def gate_chunks(t):
        return t.reshape(B, nc, CHUNK, H).transpose(1, 0, 3, 2)

    init = (jnp.zeros((B, H, ML_V_DIM, ML_QK_DIM), jnp.float32),
            jnp.zeros((B, H, ML_QK_DIM), jnp.float32),
            jnp.zeros((B, H), jnp.float32))
    _, hs = lax.scan(mlstm_chunk_step, init,
                     (to_chunks(q), to_chunks(k), to_chunks(vf), gate_chunks(ig), gate_chunks(lf)))
    hs = hs.transpose(1, 0, 3, 2, 4).reshape(B, S, H, ML_V_DIM)
    hn = rms_norm(hs, head_g.reshape(H, ML_V_DIM)).reshape(B, S, H * ML_V_DIM).astype(h.dtype)
    return (jax.nn.sigmoid(o_pre) * hn) @ w_out


def hier_moe(h, w_group, b_group, w_expert, b_expert, w_gu, w_down):
    B, S, D = h.shape
    xt = h.reshape(-1, D)
    N = xt.shape[0]
    gl = (xt @ w_group + b_group).astype(jnp.float32)
    grp = jnp.argmax(gl, axis=-1)
    p_group = jnp.take_along_axis(jax.nn.softmax(gl, axis=-1), grp[:, None], axis=-1)
    el = (xt @ w_expert + b_expert).astype(jnp.float32).reshape(N, MOE_GROUPS, MOE_PER_GROUP)
    el_g = jnp.take_along_axis(el, grp[:, None, None], axis=1)[:, 0]
    top_v, top_i = lax.top_k(el_g, MOE_TOPK)
    gate = jax.nn.softmax(top_v, axis=-1) * p_group
    eid = (grp[:, None] * MOE_PER_GROUP + top_i).reshape(-1).astype(jnp.int32)
    tok = jnp.repeat(jnp.arange(N, dtype=jnp.int32), MOE_TOPK)
    wts = gate.reshape(-1)
    A = N * MOE_TOPK
    order = jnp.argsort(eid)
    se, stok, sw = eid[order], tok[order], wts[order]
    counts = jnp.bincount(eid, length=MOE_EXPERTS)
    padded = ((counts + MOE_BLOCK - 1) // MOE_BLOCK) * MOE_BLOCK
    start = jnp.cumsum(counts) - counts
    pad_end = jnp.cumsum(padded)
    pad_start = pad_end - padded
    dest = pad_start[se] + jnp.arange(A, dtype=jnp.int32) - start[se]
    P = A + MOE_EXPERTS * MOE_BLOCK
    row_tok = jnp.zeros((P,), jnp.int32).at[dest].set(stok)
    row_w = jnp.zeros((P,), jnp.float32).at[dest].set(sw)
    nb = P // MOE_BLOCK
    block_e = jnp.minimum(jnp.searchsorted(pad_end, jnp.arange(nb) * MOE_BLOCK, side='right'),
                          MOE_EXPERTS - 1)
    xin = xt[row_tok].reshape(nb, MOE_BLOCK, D)

    def expert_block(args):
        xb, e = args
        gt, up = jnp.split(xb @ w_gu[e], 2, axis=-1)
        return (jax.nn.silu(gt) * up) @ w_down[e]

    y = lax.map(expert_block, (xin, block_e)).reshape(P, D)
    out = jnp.zeros_like(xt).at[row_tok].add(y * row_w[:, None].astype(y.dtype))
    return out.reshape(B, S, D)


def setup_inputs(seed: int = 0) -> dict:
    key = jax.random.key(seed)
    ks = jax.random.split(key, 32)
    f32 = jnp.float32
    D = D_MODEL

    def nrm(k, shape, scale):
        return jax.random.normal(k, shape, f32) * scale

    forget_b = jnp.broadcast_to(jnp.linspace(3.0, 6.0, ML_HEADS, dtype=f32), (N_MLSTM_LAYERS, ML_HEADS))
    gate_b = jnp.concatenate([nrm(ks[20], (N_MLSTM_LAYERS, ML_HEADS), 0.1),
                              forget_b + nrm(ks[21], (N_MLSTM_LAYERS, ML_HEADS), 0.1)], axis=-1)
    return {
        'x': nrm(ks[0], (BATCH, SEQ, D), 1.0),
        'c': nrm(ks[1], (BATCH, D), 1.0),
        'positions': (jnp.arange(SEQ, dtype=jnp.int32)[None, :]
                      + jax.random.randint(ks[2], (BATCH, 1), 0, 4096, dtype=jnp.int32)),
        'ada_w': nrm(ks[3], (DEPTH, D, 6 * D), 0.5 * D ** -0.5),
        'ada_b': nrm(ks[4], (DEPTH, 6 * D), 0.02),
        'norm_mix_g': 1.0 + nrm(ks[5], (DEPTH, D), 0.02),
        'norm_ffn_g': 1.0 + nrm(ks[6], (DEPTH, D), 0.02),
        'final_norm_g': 1.0 + nrm(ks[7], (D,), 0.02),
        'attn_w_in': nrm(ks[8], (N_ATTN_LAYERS, D, DA_IN_COLS), D ** -0.5),
        'attn_w_out': nrm(ks[9], (N_ATTN_LAYERS, DA_HEADS * DA_V_DIM, D), (DA_HEADS * DA_V_DIM) ** -0.5),
        'attn_lambda_q1': nrm(ks[10], (N_ATTN_LAYERS, DA_HEAD_DIM), 0.1),
        'attn_lambda_k1': nrm(ks[11], (N_ATTN_LAYERS, DA_HEAD_DIM), 0.1),
        'attn_lambda_q2': nrm(ks[12], (N_ATTN_LAYERS, DA_HEAD_DIM), 0.1),
        'attn_lambda_k2': nrm(ks[13], (N_ATTN_LAYERS, DA_HEAD_DIM), 0.1),
        'attn_head_norm_g': 1.0 + nrm(ks[14], (N_ATTN_LAYERS, DA_V_DIM), 0.02),
        'mlstm_w_in': nrm(ks[15], (N_MLSTM_LAYERS, D, ML_IN_COLS), D ** -0.5),
        'mlstm_conv_w': nrm(ks[16], (N_MLSTM_LAYERS, ML_CONV, ML_QK_COLS), ML_CONV ** -0.5),
        'mlstm_conv_b': nrm(ks[17], (N_MLSTM_LAYERS, ML_QK_COLS), 0.02),
        'mlstm_gate_b': gate_b,
        'mlstm_head_norm_g': 1.0 + nrm(ks[18], (N_MLSTM_LAYERS, ML_V_COLS), 0.02),
        'mlstm_w_out': nrm(ks[19], (N_MLSTM_LAYERS, ML_V_COLS, D), ML_V_COLS ** -0.5),
        'moe_w_group': nrm(ks[22], (DEPTH, D, MOE_GROUPS), D ** -0.5),
        'moe_b_group': nrm(ks[23], (DEPTH, MOE_GROUPS), 0.01),
        'moe_w_expert': nrm(ks[24], (DEPTH, D, MOE_EXPERTS), D ** -0.5),
        'moe_b_expert': nrm(ks[25], (DEPTH, MOE_EXPERTS), 0.01),
        'moe_w_gu': nrm(ks[26], (DEPTH, MOE_EXPERTS, D, 2 * MOE_HIDDEN), D ** -0.5),
        'moe_w_down': nrm(ks[27], (DEPTH, MOE_EXPERTS, MOE_HIDDEN, D), MOE_HIDDEN ** -0.5),
    }


def reference(x, c, positions, ada_w, ada_b, norm_mix_g, norm_ffn_g, final_norm_g,
              attn_w_in, attn_w_out, attn_lambda_q1, attn_lambda_k1, attn_lambda_q2, attn_lambda_k2,
              attn_head_norm_g, mlstm_w_in, mlstm_conv_w, mlstm_conv_b, mlstm_gate_b,
              mlstm_head_norm_g, mlstm_w_out, moe_w_group, moe_b_group, moe_w_expert, moe_b_expert,
              moe_w_gu, moe_w_down):
    cond = jax.nn.silu(c)
    for i in range(DEPTH):
        mod = (cond @ ada_w[i] + ada_b[i])[:, None, :]
        sh1, sc1, gt1, sh2, sc2, gt2 = jnp.split(mod, 6, axis=-1)
        hmix = rms_norm(x, norm_mix_g[i]) * (1.0 + sc1) + sh1
        j = i // N_MIXERS
        if i % N_MIXERS == 0:
            y = diff_attention(hmix, positions, attn_w_in[j], attn_w_out[j], attn_lambda_q1[j],
                               attn_lambda_k1[j], attn_lambda_q2[j], attn_lambda_k2[j],
                               attn_head_norm_g[j], i)
        else:
            y = mlstm_mixer(hmix, mlstm_w_in[j], mlstm_conv_w[j], mlstm_conv_b[j], mlstm_gate_b[j],
                            mlstm_head_norm_g[j], mlstm_w_out[j])
        x = x + gt1 * y
        hffn = rms_norm(x, norm_ffn_g[i]) * (1.0 + sc2) + sh2
        x = x + gt2 * hier_moe(hffn, moe_w_group[i], moe_b_group[i], moe_w_expert[i], moe_b_expert[i],
                               moe_w_gu[i], moe_w_down[i])
    return rms_norm(x, final_norm_g)
```

```python
import functools
import math

import jax
import jax.numpy as jnp
from jax import lax
from jax.experimental import pallas as pl
from jax.experimental.pallas import tpu as pltpu

F32 = jnp.float32
BF16 = jnp.bfloat16
U32 = jnp.uint32
I32 = jnp.int32

NORM_EPS = 1e-6
ROPE_THETA = 500000.0
ATTN_CHUNK = 64
DA_HEADS = 8
DA_HEAD_DIM = 128
ROPE_DIM = 32
ML_HEADS = 8
ML_QK_DIM = 128
ML_V_DIM = 256
ML_CONV = 4
MOE_GROUPS = 4
MOE_PER_GROUP = 8
MOE_EXPERTS = 32
LANES = 128
NEG_BIG = -1e30

VMEM_LIMIT = 56 * 1024 * 1024


def _cparams(sem):
    return pltpu.CompilerParams(dimension_semantics=sem, vmem_limit_bytes=VMEM_LIMIT)


def _split_hi_lo(a):
    hi = a.astype(BF16)
    lo = (a - hi.astype(F32)).astype(BF16)
    return hi, lo


def _dot3(a, w):
    ah, al = _split_hi_lo(a)
    wh, wl = _split_hi_lo(w)
    d = functools.partial(jnp.dot, preferred_element_type=F32)
    return d(ah, wh) + (d(ah, wl) + d(al, wh))


def _pack_bf16_pair(lo_f32, hi_f32):
    lo_bits = lax.bitcast_convert_type(lo_f32.astype(BF16).astype(F32), U32)
    hi_bits = lax.bitcast_convert_type(hi_f32.astype(BF16).astype(F32), U32)
    return hi_bits | (lo_bits >> 16)


def _unpack_bf16_pair(word):
    lo = lax.bitcast_convert_type(word << 16, F32)
    hi = lax.bitcast_convert_type(word & jnp.uint32(0xFFFF0000), F32)
    return lo, hi


def _rms_modulate(x, g, shift, scale):
    ms = jnp.mean(x * x, axis=-1, keepdims=True)
    y = x * lax.rsqrt(ms + NORM_EPS) * g
    return y * (1.0 + scale) + shift


def _adaln_kernel(c_ref, w_ref, b_ref, o_ref):
    c = c_ref[...]
    cond = c * jax.nn.sigmoid(c)
    acc = jnp.dot(cond.astype(BF16), w_ref[...].astype(BF16), preferred_element_type=F32)
    o_ref[...] = acc + b_ref[...]


def adaln(c, ada_w, ada_b, tn=1024):
    depth, d, n6 = ada_w.shape
    bsz = c.shape[0]
    rows = 8
    cp = jnp.zeros((rows, d), F32).at[:bsz].set(c)
    out = pl.pallas_call(
        _adaln_kernel,
        grid=(depth, n6 // tn),
        in_specs=[pl.BlockSpec((rows, d), lambda l, j: (0, 0)),
                  pl.BlockSpec((None, d, tn), lambda l, j: (l, 0, j)),
                  pl.BlockSpec((None, 1, tn), lambda l, j: (l, 0, j))],
        out_specs=pl.BlockSpec((None, rows, tn), lambda l, j: (l, 0, j)),
        out_shape=jax.ShapeDtypeStruct((depth, rows, n6), F32),
        compiler_params=_cparams(("arbitrary", "arbitrary")),
        name="adaln",
    )(cp, ada_w, ada_b.reshape(depth, 1, n6))
    return out[:, :bsz].reshape(depth, bsz, 6, d)


def _norm_matmul_kernel(*refs, rope_tiles, q_tiles, has_extra, tn):
    if has_extra:
        x_ref, g_ref, mod_ref, w_ref, c_ref, s1_ref, s2_ref, we_ref, be_ref, o_ref, oe_ref, h_scr = refs
    else:
        x_ref, g_ref, mod_ref, w_ref, c_ref, s1_ref, s2_ref, o_ref, h_scr = refs
    j = pl.program_id(1)

    @pl.when(j == 0)
    def _():
        h = _rms_modulate(x_ref[...], g_ref[...], mod_ref[0:1, :], mod_ref[1:2, :])
        h_scr[...] = h.astype(BF16)
        if has_extra:
            oe_ref[...] = _dot3(h, we_ref[...]) + be_ref[...]

    acc = jnp.dot(h_scr[...], w_ref[...], preferred_element_type=F32)
    if rope_tiles == 0:
        o_ref[...] = acc.astype(o_ref.dtype)
    else:
        @pl.when(j < rope_tiles)
        def _():
            scale = jnp.where(j < q_tiles, DA_HEAD_DIM ** -0.5, 1.0).astype(F32)
            cs, s1, s2 = c_ref[...] * scale, s1_ref[...] * scale, s2_ref[...] * scale
            for m in range(tn // LANES):
                t = acc[:, m * LANES:(m + 1) * LANES]
                r = t * cs + pltpu.roll(t, LANES - ROPE_DIM // 2, 1) * s1 + pltpu.roll(t, ROPE_DIM // 2, 1) * s2
                o_ref[:, m * LANES:(m + 1) * LANES] = r.astype(o_ref.dtype)

        @pl.when(j >= rope_tiles)
        def _():
            o_ref[...] = acc.astype(o_ref.dtype)


def norm_matmul(x, g, mod, w, seq, rope=None, extra=None, tm=1024, tn=512):
    n, k = x.shape
    m = w.shape[1]
    tiles_per_seq = seq // tm
    has_extra = extra is not None
    if rope is None:
        dummy = jnp.zeros((8, LANES), F32)
        tabs = (dummy, dummy, dummy)
        tab_spec = pl.BlockSpec((8, LANES), lambda i, j: (0, 0))
        rope_tiles = q_tiles = 0
    else:
        tabs = rope[:3]
        tab_spec = pl.BlockSpec((tm, LANES), lambda i, j: (i, 0))
        rope_tiles, q_tiles = rope[3] // tn, rope[4] // tn
    in_specs = [pl.BlockSpec((tm, k), lambda i, j: (i, 0)),
                pl.BlockSpec((1, k), lambda i, j: (0, 0)),
                pl.BlockSpec((None, 6, k), lambda i, j: (i // tiles_per_seq, 0, 0)),
                pl.BlockSpec((k, tn), lambda i, j: (0, j)),
                tab_spec, tab_spec, tab_spec]
    args = [x, g.reshape(1, k), mod, w, *tabs]
    out_specs = pl.BlockSpec((tm, tn), lambda i, j: (i, j))
    out_shape = jax.ShapeDtypeStruct((n, m), BF16)
    if has_extra:
        we, be = extra
        in_specs += [pl.BlockSpec((k, LANES), lambda i, j: (0, 0)),
                     pl.BlockSpec((1, LANES), lambda i, j: (0, 0))]
        args += [we, be]
        out_specs = [out_specs, pl.BlockSpec((tm, LANES), lambda i, j: (i, 0))]
        out_shape = [out_shape, jax.ShapeDtypeStruct((n, LANES), F32)]
    return pl.pallas_call(
        functools.partial(_norm_matmul_kernel, rope_tiles=rope_tiles, q_tiles=q_tiles,
                          has_extra=has_extra, tn=tn),
        grid=(n // tm, m // tn),
        in_specs=in_specs,
        out_specs=out_specs,
        out_shape=out_shape,
        scratch_shapes=[pltpu.VMEM((tm, k), BF16)],
        compiler_params=_cparams(("arbitrary", "arbitrary")),
        name="norm_matmul",
    )(*args)


def _matmul_res_kernel(a_ref, w_ref, res_ref, mod_ref, o_ref, *, gate_row):
    acc = jnp.dot(a_ref[...], w_ref[...], preferred_element_type=F32)
    o_ref[...] = res_ref[...] + mod_ref[gate_row:gate_row + 1, :] * acc


def matmul_res(a, w, res, mod, seq, gate_row, tm=1024, tn=512):
    n, k = a.shape
    m = w.shape[1]
    tiles_per_seq = seq // tm
    return pl.pallas_call(
        functools.partial(_matmul_res_kernel, gate_row=gate_row),
        grid=(n // tm, m // tn),
        in_specs=[pl.BlockSpec((tm, k), lambda i, j: (i, 0)),
                  pl.BlockSpec((k, tn), lambda i, j: (0, j)),
                  pl.BlockSpec((tm, tn), lambda i, j: (i, j)),
                  pl.BlockSpec((None, 6, tn), lambda i, j: (i // tiles_per_seq, 0, j))],
        out_specs=pl.BlockSpec((tm, tn), lambda i, j: (i, j)),
        out_shape=jax.ShapeDtypeStruct((n, m), F32),
        compiler_params=_cparams(("arbitrary", "arbitrary")),
        name="matmul_res",
    )(a, w, res, mod)


def _attn_kernel(q_ref, k_ref, v_ref, lq1_ref, lk1_ref, lq2_ref, lk2_ref, g_ref, o_ref,
                 acc1, acc2, *, lam_init, tq, seq):
    d = DA_HEAD_DIM
    lam = (jnp.exp(jnp.sum(lq1_ref[...] * lk1_ref[...], axis=-1, keepdims=True))
           - jnp.exp(jnp.sum(lq2_ref[...] * lk2_ref[...], axis=-1, keepdims=True)) + lam_init)
    shift = ATTN_CHUNK.bit_length() - 1
    row_chunk = jnp.right_shift(lax.broadcasted_iota(I32, (tq, tq), 0), shift)
    col_chunk = jnp.right_shift(lax.broadcasted_iota(I32, (tq, tq), 1), shift)
    diag_mask = col_chunk <= row_chunk
    nt = (((1,), (1,)), ((), ()))

    def q_body(qi, _):
        qs = pl.multiple_of(qi * tq, tq)
        q1 = q_ref[pl.ds(qs, tq), 0:d]
        q2 = q_ref[pl.ds(qs, tq), d:2 * d]
        acc1[...] = jnp.zeros_like(acc1)
        acc2[...] = jnp.zeros_like(acc2)

        def step(ks, carry, masked):
            m1, l1, m2, l2 = carry
            k1 = k_ref[pl.ds(ks, tq), 0:d]
            k2 = k_ref[pl.ds(ks, tq), d:2 * d]
            v = v_ref[pl.ds(ks, tq), :]
            out = []
            for (q, k, m, l, acc) in ((q1, k1, m1, l1, acc1), (q2, k2, m2, l2, acc2)):
                s = lax.dot_general(q, k, nt, preferred_element_type=F32)
                if masked:
                    s = jnp.where(diag_mask, s, NEG_BIG)
                m_new = jnp.maximum(m, jnp.max(s, axis=-1, keepdims=True))
                alpha = jnp.exp(m - m_new)
                p = jnp.exp(s - m_new)
                l_new = alpha * l + jnp.sum(p, axis=-1, keepdims=True)
                acc[...] = alpha * acc[...] + jnp.dot(p.astype(BF16), v, preferred_element_type=F32)
                out += [m_new, l_new]
            return tuple(out)

        init = (jnp.full((tq, 1), NEG_BIG, F32), jnp.zeros((tq, 1), F32),
                jnp.full((tq, 1), NEG_BIG, F32), jnp.zeros((tq, 1), F32))
        carry = lax.fori_loop(0, qi, lambda j, c: step(pl.multiple_of(j * tq, tq), c, False), init)
        m1, l1, m2, l2 = step(qs, carry, True)
        o = acc1[...] / l1 - lam * (acc2[...] / l2)
        ms = jnp.mean(o * o, axis=-1, keepdims=True)
        o = o * lax.rsqrt(ms + NORM_EPS) * g_ref[...] * (1.0 - lam_init)
        o_ref[pl.ds(qs, tq), :] = o.astype(o_ref.dtype)
        return 0

    lax.fori_loop(0, seq // tq, q_body, 0)


def diff_attention(qkv, lq1, lk1, lq2, lk2, head_g, lam_init, tq=256):
    bsz, seq, _ = qkv.shape
    h, dv = DA_HEADS, 2 * DA_HEAD_DIM
    vec = lambda a: a.reshape(1, -1).astype(F32)
    small = lambda n: pl.BlockSpec((1, n), lambda b, hh: (0, 0))
    return pl.pallas_call(
        functools.partial(_attn_kernel, lam_init=lam_init, tq=tq, seq=seq),
        grid=(bsz, h),
        in_specs=[pl.BlockSpec((None, seq, dv), lambda b, hh: (b, 0, hh)),
                  pl.BlockSpec((None, seq, dv), lambda b, hh: (b, 0, h + hh)),
                  pl.BlockSpec((None, seq, dv), lambda b, hh: (b, 0, 2 * h + hh)),
                  small(DA_HEAD_DIM), small(DA_HEAD_DIM), small(DA_HEAD_DIM), small(DA_HEAD_DIM),
                  small(dv)],
        out_specs=pl.BlockSpec((None, seq, dv), lambda b, hh: (b, 0, hh)),
        out_shape=jax.ShapeDtypeStruct((bsz, seq, h * dv), BF16),
        scratch_shapes=[pltpu.VMEM((tq, dv), F32), pltpu.VMEM((tq, dv), F32)],
        compiler_params=_cparams(("arbitrary", "arbitrary")),
        name="diff_attention",
    )(qkv, qkv, qkv, vec(lq1), vec(lk1), vec(lq2), vec(lk2), vec(head_g))


def _conv_silu_kernel(x_ref, w_ref, b_ref, o_ref, *, k_tile0):
    j = pl.program_id(1)
    x = x_ref[...].astype(F32)
    row = lax.broadcasted_iota(I32, x.shape, 0)
    y = x * w_ref[ML_CONV - 1:ML_CONV, :] + b_ref[...]
    for s in range(1, ML_CONV):
        shifted = jnp.where(row >= s, pltpu.roll(x, s, 0), 0.0)
        y = y + shifted * w_ref[ML_CONV - 1 - s:ML_CONV - s, :]
    y = y * jax.nn.sigmoid(y)
    scale = jnp.where(j >= k_tile0, ML_QK_DIM ** -0.5, 1.0).astype(F32)
    o_ref[...] = (y * scale).astype(o_ref.dtype)


def conv_silu(proj, conv_w, conv_b, tc=128):
    bsz, seq, _ = proj.shape
    cols = conv_w.shape[1]
    return pl.pallas_call(
        functools.partial(_conv_silu_kernel, k_tile0=(cols // 2) // tc),
        grid=(bsz, cols // tc),
        in_specs=[pl.BlockSpec((None, seq, tc), lambda b, j: (b, 0, j)),
                  pl.BlockSpec((ML_CONV, tc), lambda b, j: (0, j)),
                  pl.BlockSpec((1, tc), lambda b, j: (0, j))],
        out_specs=pl.BlockSpec((None, seq, tc), lambda b, j: (b, 0, j)),
        out_shape=jax.ShapeDtypeStruct((bsz, seq, cols), BF16),
        compiler_params=_cparams(("arbitrary", "arbitrary")),
        name="conv_silu",
    )(proj, conv_w, conv_b.reshape(1, cols))


def _mlstm_kernel(q_ref, k_ref, v_ref, op_ref, gc_ref, gr_ref, hg_ref, o_ref,
                  ct_scr, n_scr, m_scr, *, chunk):
    c = pl.program_id(1)
    nh, dqk, dv = ML_HEADS, ML_QK_DIM, ML_V_DIM

    @pl.when(c == 0)
    def _():
        ct_scr[...] = jnp.zeros_like(ct_scr)
        n_scr[...] = jnp.zeros_like(n_scr)
        m_scr[...] = jnp.zeros_like(m_scr)

    gc = gc_ref[...]
    gr = gr_ref[...]
    lf_c = jax.nn.log_sigmoid(gc)
    lf_r = jax.nn.log_sigmoid(gr)
    r_i = lax.broadcasted_iota(I32, (chunk, chunk), 0)
    c_i = lax.broadcasted_iota(I32, (chunk, chunk), 1)
    causal = c_i <= r_i
    tril = causal.astype(F32)
    triu = (r_i <= c_i).astype(F32)
    b_c = _dot3(tril, lf_c)
    b_r = _dot3(lf_r, triu)
    nt = (((1,), (1,)), ((), ()))
    tn_ = (((0,), (0,)), ((), ()))

    for h in range(nh):
        q = q_ref[:, h * dqk:(h + 1) * dqk]
        k = k_ref[:, h * dqk:(h + 1) * dqk]
        v = v_ref[:, h * dv:(h + 1) * dv]
        bc = b_c[:, nh + h:nh + h + 1]
        br = b_r[nh + h:nh + h + 1, :]
        ig_c = gc[:, h:h + 1]
        ig_r = gr[h:h + 1, :]
        m_prev = m_scr[h:h + 1, :]
        dmat = jnp.where(causal, bc - br + ig_r, NEG_BIG)
        inter = bc + m_prev
        m_t = jnp.maximum(inter, jnp.max(dmat, axis=-1, keepdims=True))
        w = jnp.exp(dmat - m_t)
        s = lax.dot_general(q, k, nt, preferred_element_type=F32) * w
        decay = jnp.exp(inter - m_t)
        ct = ct_scr[h]
        num = (jnp.dot(s.astype(BF16), v, preferred_element_type=F32)
               + decay * jnp.dot(q, ct.astype(BF16), preferred_element_type=F32))
        qn = jnp.sum(q.astype(F32) * n_scr[h:h + 1, :], axis=-1, keepdims=True)
        den = jnp.sum(s, axis=-1, keepdims=True) + decay * qn
        hh = num / jnp.maximum(jnp.abs(den), jnp.exp(-m_t))
        b_last = bc[chunk - 1:chunk, :]
        g = b_last - bc + ig_c
        m_new = jnp.maximum(b_last + m_prev, jnp.max(g, axis=0, keepdims=True))
        carry_decay = jnp.exp(b_last + m_prev - m_new)
        wg = jnp.exp(g - m_new)
        wv = (wg * v.astype(F32)).astype(BF16)
        ct_scr[h] = carry_decay * ct + lax.dot_general(k, wv, tn_, preferred_element_type=F32)
        n_scr[h:h + 1, :] = carry_decay * n_scr[h:h + 1, :] + jnp.sum(wg * k.astype(F32), axis=0, keepdims=True)
        m_scr[h:h + 1, :] = m_new
        ms = jnp.mean(hh * hh, axis=-1, keepdims=True)
        hn = hh * lax.rsqrt(ms + NORM_EPS) * hg_ref[:, h * dv:(h + 1) * dv]
        og = jax.nn.sigmoid(op_ref[:, h * dv:(h + 1) * dv].astype(F32))
        o_ref[:, h * dv:(h + 1) * dv] = (og * hn).astype(o_ref.dtype)


def mlstm(qk, proj, gates_c, gates_r, head_g, chunk=128):
    bsz, seq, _ = qk.shape
    nh = ML_HEADS
    qw, vw = nh * ML_QK_DIM, nh * ML_V_DIM
    v_blk = (2 * qw) // vw
    return pl.pallas_call(
        functools.partial(_mlstm_kernel, chunk=chunk),
        grid=(bsz, seq // chunk),
        in_specs=[pl.BlockSpec((None, chunk, qw), lambda b, c: (b, c, 0)),
                  pl.BlockSpec((None, chunk, qw), lambda b, c: (b, c, 1)),
                  pl.BlockSpec((None, chunk, vw), lambda b, c: (b, c, v_blk)),
                  pl.BlockSpec((None, chunk, vw), lambda b, c: (b, c, v_blk + 1)),
                  pl.BlockSpec((None, chunk, LANES), lambda b, c: (b, c, 0)),
                  pl.BlockSpec((None, 2 * nh, chunk), lambda b, c: (b, 0, c)),
                  pl.BlockSpec((1, vw), lambda b, c: (0, 0))],
        out_specs=pl.BlockSpec((None, chunk, vw), lambda b, c: (b, c, 0)),
        out_shape=jax.ShapeDtypeStruct((bsz, seq, vw), BF16),
        scratch_shapes=[pltpu.VMEM((nh, ML_QK_DIM, ML_V_DIM), F32),
                        pltpu.VMEM((nh, ML_QK_DIM), F32),
                        pltpu.VMEM((nh, 1), F32)],
        compiler_params=_cparams(("arbitrary", "arbitrary")),
        name="mlstm",
    )(qk, qk, proj, proj, gates_c, gates_r, head_g.reshape(1, vw))


def _router_kernel(x_ref, g_ref, mod_ref, w_ref, b_ref, hp_ref, rt_ref):
    h = _rms_modulate(x_ref[...], g_ref[...], mod_ref[3:4, :], mod_ref[4:5, :])
    half = h.shape[1] // 2
    hp_ref[...] = _pack_bf16_pair(h[:, :half], h[:, half:])
    logits = _dot3(h, w_ref[...]) + b_ref[...]
    lane = lax.broadcasted_iota(I32, logits.shape, 1).astype(F32)
    ng = float(MOE_GROUPS)
    is_g = lane < ng
    gl = jnp.where(is_g, logits, NEG_BIG)
    gmax = jnp.max(gl, axis=-1, keepdims=True)
    grp = jnp.min(jnp.where(gl == gmax, lane, float(LANES)), axis=-1, keepdims=True)
    p_group = 1.0 / jnp.sum(jnp.where(is_g, jnp.exp(gl - gmax), 0.0), axis=-1, keepdims=True)
    lo = ng + float(MOE_PER_GROUP) * grp
    el = jnp.where((lane >= lo) & (lane < lo + float(MOE_PER_GROUP)), logits, NEG_BIG)
    v1 = jnp.max(el, axis=-1, keepdims=True)
    i1 = jnp.min(jnp.where(el == v1, lane, float(LANES)), axis=-1, keepdims=True)
    el2 = jnp.where(lane == i1, NEG_BIG, el)
    v2 = jnp.max(el2, axis=-1, keepdims=True)
    i2 = jnp.min(jnp.where(el2 == v2, lane, float(LANES)), axis=-1, keepdims=True)
    ex = jnp.exp(v2 - v1)
    w1 = p_group / (1.0 + ex)
    w2 = p_group * (ex / (1.0 + ex))
    rt_ref[...] = jnp.where(lane == 0.0, i1 - ng,
                  jnp.where(lane == 1.0, i2 - ng,
                  jnp.where(lane == 2.0, w1,
                  jnp.where(lane == 3.0, w2, 0.0))))


def moe_router(x, g, mod, w_r, b_r, seq, tm=512):
    n, d = x.shape
    tiles_per_seq = seq // tm
    return pl.pallas_call(
        _router_kernel,
        grid=(n // tm,),
        in_specs=[pl.BlockSpec((tm, d), lambda i: (i, 0)),
                  pl.BlockSpec((1, d), lambda i: (0, 0)),
                  pl.BlockSpec((None, 6, d), lambda i: (i // tiles_per_seq, 0, 0)),
                  pl.BlockSpec((d, LANES), lambda i: (0, 0)),
                  pl.BlockSpec((1, LANES), lambda i: (0, 0))],
        out_specs=[pl.BlockSpec((tm, d // 2), lambda i: (i, 0)),
                   pl.BlockSpec((tm, LANES), lambda i: (i, 0))],
        out_shape=[jax.ShapeDtypeStruct((n, d // 2), U32),
                   jax.ShapeDtypeStruct((n, LANES), F32)],
        compiler_params=_cparams(("arbitrary",)),
        name="moe_router",
    )(x, g.reshape(1, d), mod, w_r, b_r)


def _dispatch_kernel(dest_ref, hp_ref, xin_in_ref, xin_ref, sem, *, tt):
    del xin_in_ref

    def row_copy(r, kk):
        return pltpu.make_async_copy(hp_ref.at[pl.ds(r, 1)],
                                     xin_ref.at[pl.ds(dest_ref[0, 0, 2 * r + kk], 1)], sem)

    def start(r, _):
        row_copy(r, 0).start()
        row_copy(r, 1).start()
        return 0

    def wait(r, _):
        row_copy(r, 0).wait()
        row_copy(r, 1).wait()
        return 0

    lax.fori_loop(0, tt, start, 0)
    lax.fori_loop(0, tt, wait, 0)


def moe_dispatch(hpack, dest, rows, tt=256):
    n, w = hpack.shape
    return pl.pallas_call(
        functools.partial(_dispatch_kernel, tt=tt),
        grid=(n // tt,),
        in_specs=[pl.BlockSpec((1, 1, 2 * tt), lambda i: (i, 0, 0), memory_space=pltpu.SMEM),
                  pl.BlockSpec((tt, w), lambda i: (i, 0)),
                  pl.BlockSpec(memory_space=pl.ANY)],
        out_specs=pl.BlockSpec(memory_space=pl.ANY),
        out_shape=jax.ShapeDtypeStruct((rows, w), U32),
        scratch_shapes=[pltpu.SemaphoreType.DMA(())],
        input_output_aliases={2: 0},
        compiler_params=_cparams(("arbitrary",)),
        name="moe_dispatch",
    )(dest.reshape(n // tt, 1, 2 * tt), hpack, jnp.zeros((rows, w), U32))


def _expert_kernel(be_ref, nu_ref, x_ref, wgu_ref, wd_ref, y_ref, xs_scr, acc_scr, *, th):
    i = pl.program_id(0)

    @pl.when(i < nu_ref[0])
    def _():
        half = x_ref.shape[1]
        hid = wd_ref.shape[0]
        lo, hi = _unpack_bf16_pair(x_ref[...])
        xs_scr[:, :half] = lo.astype(BF16)
        xs_scr[:, half:] = hi.astype(BF16)
        xs = xs_scr[...]
        for c in range(hid // th):
            gt = jnp.dot(xs, wgu_ref[:, c * th:(c + 1) * th], preferred_element_type=F32)
            up = jnp.dot(xs, wgu_ref[:, hid + c * th:hid + (c + 1) * th], preferred_element_type=F32)
            act = (gt * jax.nn.sigmoid(gt) * up).astype(BF16)
            part = jnp.dot(act, wd_ref[c * th:(c + 1) * th, :], preferred_element_type=F32)
            if c == 0:
                acc_scr[...] = part
            else:
                acc_scr[...] += part
        y = acc_scr[...]
        y_ref[...] = _pack_bf16_pair(y[:, :half], y[:, half:])


def moe_experts(xin, block_e, n_used, wgu, wd, tm, th=256):
    rows, half = xin.shape
    e, d, hid2 = wgu.shape
    hid = hid2 // 2
    nb = rows // tm

    def blk(i, be, nu):
        return jnp.minimum(i, nu[0] - 1)

    grid_spec = pltpu.PrefetchScalarGridSpec(
        num_scalar_prefetch=2,
        grid=(nb,),
        in_specs=[pl.BlockSpec((tm, half), lambda i, be, nu: (blk(i, be, nu), 0)),
                  pl.BlockSpec((None, d, hid2), lambda i, be, nu: (be[blk(i, be, nu)], 0, 0)),
                  pl.BlockSpec((None, hid, d), lambda i, be, nu: (be[blk(i, be, nu)], 0, 0))],
        out_specs=pl.BlockSpec((tm, half), lambda i, be, nu: (blk(i, be, nu), 0)),
        scratch_shapes=[pltpu.VMEM((tm, d), BF16), pltpu.VMEM((tm, d), F32)],
    )
    return pl.pallas_call(
        functools.partial(_expert_kernel, th=th),
        grid_spec=grid_spec,
        out_shape=jax.ShapeDtypeStruct((rows, half), U32),
        input_output_aliases={2: 0},
        compiler_params=_cparams(("arbitrary",)),
        name="moe_experts",
    )(block_e, n_used, xin, wgu, wd)


def _combine_kernel(dest_ref, x_ref, rt_ref, mod_ref, fg_ref, y_ref, o_ref, ya, yb, sem, *, tt, final_norm):
    def row_copy(r, kk):
        dst = ya if kk == 0 else yb
        return pltpu.make_async_copy(y_ref.at[pl.ds(dest_ref[0, 0, 2 * r + kk], 1)],
                                     dst.at[pl.ds(r, 1)], sem)

    def start(r, _):
        row_copy(r, 0).start()
        row_copy(r, 1).start()
        return 0

    def wait(r, _):
        row_copy(r, 0).wait()
        row_copy(r, 1).wait()
        return 0

    lax.fori_loop(0, tt, start, 0)
    lax.fori_loop(0, tt, wait, 0)
    half = ya.shape[1]
    w1 = rt_ref[:, 2:3]
    w2 = rt_ref[:, 3:4]
    a_lo, a_hi = _unpack_bf16_pair(ya[...])
    b_lo, b_hi = _unpack_bf16_pair(yb[...])
    gate = mod_ref[5:6, :]
    out_lo = x_ref[:, :half] + gate[:, :half] * (a_lo * w1 + b_lo * w2)
    out_hi = x_ref[:, half:] + gate[:, half:] * (a_hi * w1 + b_hi * w2)
    if final_norm:
        ms = (jnp.sum(out_lo * out_lo, axis=-1, keepdims=True)
              + jnp.sum(out_hi * out_hi, axis=-1, keepdims=True)) / (2 * half)
        r = lax.rsqrt(ms + NORM_EPS)
        out_lo = out_lo * r * fg_ref[:, :half]
        out_hi = out_hi * r * fg_ref[:, half:]
    o_ref[:, :half] = out_lo
    o_ref[:, half:] = out_hi


def moe_combine(x, y, dest, route, mod, final_g, seq, final_norm, tt=256):
    n, d = x.shape
    tiles_per_seq = seq // tt
    return pl.pallas_call(
        functools.partial(_combine_kernel, tt=tt, final_norm=final_norm),
        grid=(n // tt,),
        in_specs=[pl.BlockSpec((1, 1, 2 * tt), lambda i: (i, 0, 0), memory_space=pltpu.SMEM),
                  pl.BlockSpec((tt, d), lambda i: (i, 0)),
                  pl.BlockSpec((tt, LANES), lambda i: (i, 0)),
                  pl.BlockSpec((None, 6, d), lambda i: (i // tiles_per_seq, 0, 0)),
                  pl.BlockSpec((1, d), lambda i: (0, 0)),
                  pl.BlockSpec(memory_space=pl.ANY)],
        out_specs=pl.BlockSpec((tt, d), lambda i: (i, 0)),
        out_shape=jax.ShapeDtypeStruct((n, d), F32),
        scratch_shapes=[pltpu.VMEM((tt, d // 2), U32), pltpu.VMEM((tt, d // 2), U32),
                        pltpu.SemaphoreType.DMA(())],
        compiler_params=_cparams(("arbitrary",)),
        name="moe_combine",
    )(dest.reshape(n // tt, 1, 2 * tt), x, route, mod, final_g.reshape(1, d), y)


def _expert_row_offsets(eid, tm, nb):
    onehot = (eid[:, None] == jnp.arange(MOE_EXPERTS, dtype=I32)[None, :]).astype(I32)
    csum = jnp.cumsum(onehot, axis=0)
    rank = jnp.sum(onehot * csum, axis=1) - 1
    counts = csum[-1]
    padded = ((counts + tm - 1) // tm) * tm
    pad_end = jnp.cumsum(padded)
    pad_start = pad_end - padded
    dest = (pad_start[eid] + rank).astype(I32)
    n_used = (pad_end[-1] // tm).astype(I32)
    block_e = jnp.searchsorted(pad_end, jnp.arange(nb, dtype=I32) * tm, side='right')
    block_e = jnp.minimum(block_e, MOE_EXPERTS - 1).astype(I32)
    return dest, block_e, n_used.reshape(1)


def hier_moe_layer(x, g, mod, w_group, b_group, w_expert, b_expert, wgu, wd, final_g, seq,
                   final_norm, tm=256):
    n, d = x.shape
    ng, ne = w_group.shape[1], w_expert.shape[1]
    w_r = jnp.zeros((d, LANES), F32).at[:, :ng].set(w_group).at[:, ng:ng + ne].set(w_expert)
    b_r = jnp.zeros((1, LANES), F32).at[0, :ng].set(b_group).at[0, ng:ng + ne].set(b_expert)
    hpack, route = moe_router(x, g, mod, w_r, b_r, seq)
    eid = route[:, 0:2].astype(I32).reshape(-1)
    rows = 2 * n + MOE_EXPERTS * tm
    dest, block_e, n_used = _expert_row_offsets(eid, tm, rows // tm)
    xin = moe_dispatch(hpack, dest, rows)
    y = moe_experts(xin, block_e, n_used, wgu, wd, tm)
    return moe_combine(x, y, dest, route, mod, final_g, seq, final_norm)


def _rope_tables(positions):
    half = ROPE_DIM // 2
    inv_freq = ROPE_THETA ** (-jnp.arange(half, dtype=F32) * 2.0 / ROPE_DIM)
    ang = positions.astype(F32).reshape(-1, 1) * inv_freq
    cos, sin = jnp.cos(ang), jnp.sin(ang)
    n = ang.shape[0]
    pad = LANES - ROPE_DIM
    c = jnp.concatenate([cos, cos, jnp.ones((n, pad), F32)], axis=1)
    s_lo = jnp.concatenate([-sin, jnp.zeros((n, LANES - half), F32)], axis=1)
    s_hi = jnp.concatenate([jnp.zeros((n, half), F32), sin, jnp.zeros((n, pad), F32)], axis=1)
    return c, s_lo, s_hi


def kernel(x, c, positions, ada_w, ada_b, norm_mix_g, norm_ffn_g, final_norm_g, attn_w_in, attn_w_out, attn_lambda_q1, attn_lambda_k1, attn_lambda_q2, attn_lambda_k2, attn_head_norm_g, mlstm_w_in, mlstm_conv_w, mlstm_conv_b, mlstm_gate_b, mlstm_head_norm_g, mlstm_w_out, moe_w_group, moe_b_group, moe_w_expert, moe_b_expert, moe_w_gu, moe_w_down):
    bsz, seq, d = x.shape
    n = bsz * seq
    depth = ada_w.shape[0]
    mod = adaln(c, ada_w, ada_b)
    xf = x.reshape(n, d)
    for i in range(depth):
        jm = i // 2
        if i % 2 == 0:
            qk_cols = 2 * DA_HEADS * 2 * DA_HEAD_DIM
            tabs = _rope_tables(positions)
            qkv = norm_matmul(xf, norm_mix_g[i], mod[i], attn_w_in[jm].astype(BF16), seq,
                              rope=(*tabs, qk_cols, qk_cols // 2))
            lam_init = 0.8 - 0.6 * math.exp(-0.3 * i)
            mixed = diff_attention(qkv.reshape(bsz, seq, -1), attn_lambda_q1[jm], attn_lambda_k1[jm],
                                   attn_lambda_q2[jm], attn_lambda_k2[jm], attn_head_norm_g[jm], lam_init)
            w_out = attn_w_out[jm]
        else:
            qk_cols = 2 * ML_HEADS * ML_QK_DIM
            main_cols = qk_cols + 2 * ML_HEADS * ML_V_DIM
            w_in = mlstm_w_in[jm]
            ngate = 2 * ML_HEADS
            w_gate = jnp.zeros((d, LANES), F32).at[:, :ngate].set(w_in[:, main_cols:])
            b_gate = jnp.zeros((1, LANES), F32).at[0, :ngate].set(mlstm_gate_b[jm])
            proj, gates = norm_matmul(xf, norm_mix_g[i], mod[i], w_in[:, :main_cols].astype(BF16), seq,
                                      extra=(w_gate, b_gate))
            proj = proj.reshape(bsz, seq, main_cols)
            qk = conv_silu(proj, mlstm_conv_w[jm], mlstm_conv_b[jm])
            gates_c = gates.reshape(bsz, seq, LANES)
            gates_r = jnp.swapaxes(gates_c[:, :, :ngate], 1, 2)
            mixed = mlstm(qk, proj, gates_c, gates_r, mlstm_head_norm_g[jm])
            w_out = mlstm_w_out[jm]
        xf = matmul_res(mixed.reshape(n, -1), w_out.astype(BF16), xf, mod[i], seq, gate_row=2)
        xf = hier_moe_layer(xf, norm_ffn_g[i], mod[i], moe_w_group[i], moe_b_group[i], moe_w_expert[i],
                            moe_b_expert[i], moe_w_gu[i].astype(BF16), moe_w_down[i].astype(BF16),
                            final_norm_g, seq, final_norm=(i == depth - 1))
    return xf.reshape(bsz, seq, d)
```

```python
import functools
import math

import jax
import jax.numpy as jnp
from jax import lax
from jax.experimental import pallas as pl
from jax.experimental.pallas import tpu as pltpu

F32 = jnp.float32
BF16 = jnp.bfloat16
U32 = jnp.uint32
I32 = jnp.int32

NORM_EPS = 1e-6
ROPE_THETA = 500000.0
ATTN_CHUNK = 64
DA_HEADS = 8
DA_HEAD_DIM = 128
ROPE_DIM = 32
ML_HEADS = 8
ML_QK_DIM = 128
ML_V_DIM = 256
ML_CONV = 4
MOE_GROUPS = 4
MOE_PER_GROUP = 8
MOE_EXPERTS = 32
LANES = 128
NEG_BIG = -1e30
LOG2_E = math.log2(math.e)

VMEM_LIMIT = 56 * 1024 * 1024


def _cparams(sem):
    return pltpu.CompilerParams(dimension_semantics=sem, vmem_limit_bytes=VMEM_LIMIT)


def _split_hi_lo(a):
    hi = a.astype(BF16)
    lo = (a - hi.astype(F32)).astype(BF16)
    return hi, lo


def _dot3(a, w):
    ah, al = _split_hi_lo(a)
    wh, wl = _split_hi_lo(w)
    d = functools.partial(jnp.dot, preferred_element_type=F32)
    return d(ah, wh) + (d(ah, wl) + d(al, wh))


def _pack_bf16_pair(lo_f32, hi_f32):
    lo_bits = lax.bitcast_convert_type(lo_f32.astype(BF16).astype(F32), U32)
    hi_bits = lax.bitcast_convert_type(hi_f32.astype(BF16).astype(F32), U32)
    return hi_bits | (lo_bits >> 16)


def _unpack_bf16_pair(word):
    lo = lax.bitcast_convert_type(word << 16, F32)
    hi = lax.bitcast_convert_type(word & jnp.uint32(0xFFFF0000), F32)
    return lo, hi


def _rms_modulate(x, g, shift, scale):
    ms = jnp.mean(x * x, axis=-1, keepdims=True)
    y = x * lax.rsqrt(ms + NORM_EPS) * g
    return y * (1.0 + scale) + shift


def _adaln_kernel(c_ref, w_ref, b_ref, o_ref):
    c = c_ref[...]
    cond = c * jax.nn.sigmoid(c)
    acc = jnp.dot(cond.astype(BF16), w_ref[...].astype(BF16), preferred_element_type=F32)
    o_ref[...] = acc + b_ref[...]


def adaln(c, ada_w, ada_b, tn=1024):
    depth, d, n6 = ada_w.shape
    bsz = c.shape[0]
    rows = 8
    cp = jnp.zeros((rows, d), F32).at[:bsz].set(c)
    out = pl.pallas_call(
        _adaln_kernel,
        grid=(depth, n6 // tn),
        in_specs=[pl.BlockSpec((rows, d), lambda l, j: (0, 0)),
                  pl.BlockSpec((None, d, tn), lambda l, j: (l, 0, j)),
                  pl.BlockSpec((None, 1, tn), lambda l, j: (l, 0, j))],
        out_specs=pl.BlockSpec((None, rows, tn), lambda l, j: (l, 0, j)),
        out_shape=jax.ShapeDtypeStruct((depth, rows, n6), F32),
        compiler_params=_cparams(("arbitrary", "arbitrary")),
        name="adaln",
    )(cp, ada_w, ada_b.reshape(depth, 1, n6))
    return out[:, :bsz].reshape(depth, bsz, 6, d)


def _norm_matmul_kernel(*refs, rope_tiles, q_tiles, has_extra, tn):
    if has_extra:
        x_ref, g_ref, mod_ref, w_ref, c_ref, s1_ref, s2_ref, we_ref, be_ref, o_ref, oe_ref, h_scr = refs
    else:
        x_ref, g_ref, mod_ref, w_ref, c_ref, s1_ref, s2_ref, o_ref, h_scr = refs
    j = pl.program_id(1)

    @pl.when(j == 0)
    def _():
        h = _rms_modulate(x_ref[...], g_ref[...], mod_ref[0:1, :], mod_ref[1:2, :])
        h_scr[...] = h.astype(BF16)
        if has_extra:
            oe_ref[...] = _dot3(h, we_ref[...]) + be_ref[...]

    acc = jnp.dot(h_scr[...], w_ref[...], preferred_element_type=F32)
    if rope_tiles == 0:
        o_ref[...] = acc.astype(o_ref.dtype)
    else:
        @pl.when(j < rope_tiles)
        def _():
            scale = jnp.where(j < q_tiles, DA_HEAD_DIM ** -0.5 * LOG2_E, 1.0).astype(F32)
            cs, s1, s2 = c_ref[...] * scale, s1_ref[...] * scale, s2_ref[...] * scale
            for m in range(tn // LANES):
                t = acc[:, m * LANES:(m + 1) * LANES]
                r = t * cs + pltpu.roll(t, LANES - ROPE_DIM // 2, 1) * s1 + pltpu.roll(t, ROPE_DIM // 2, 1) * s2
                o_ref[:, m * LANES:(m + 1) * LANES] = r.astype(o_ref.dtype)

        @pl.when(j >= rope_tiles)
        def _():
            o_ref[...] = acc.astype(o_ref.dtype)


def norm_matmul(x, g, mod, w, seq, rope=None, extra=None, tm=1024, tn=512):
    n, k = x.shape
    m = w.shape[1]
    tiles_per_seq = seq // tm
    has_extra = extra is not None
    if rope is None:
        dummy = jnp.zeros((8, LANES), F32)
        tabs = (dummy, dummy, dummy)
        tab_spec = pl.BlockSpec((8, LANES), lambda i, j: (0, 0))
        rope_tiles = q_tiles = 0
    else:
        tabs = rope[:3]
        tab_spec = pl.BlockSpec((tm, LANES), lambda i, j: (i, 0))
        rope_tiles, q_tiles = rope[3] // tn, rope[4] // tn
    in_specs = [pl.BlockSpec((tm, k), lambda i, j: (i, 0)),
                pl.BlockSpec((1, k), lambda i, j: (0, 0)),
                pl.BlockSpec((None, 6, k), lambda i, j: (i // tiles_per_seq, 0, 0)),
                pl.BlockSpec((k, tn), lambda i, j: (0, j)),
                tab_spec, tab_spec, tab_spec]
    args = [x, g.reshape(1, k), mod, w, *tabs]
    out_specs = pl.BlockSpec((tm, tn), lambda i, j: (i, j))
    out_shape = jax.ShapeDtypeStruct((n, m), BF16)
    if has_extra:
        we, be = extra
        in_specs += [pl.BlockSpec((k, LANES), lambda i, j: (0, 0)),
                     pl.BlockSpec((1, LANES), lambda i, j: (0, 0))]
        args += [we, be]
        out_specs = [out_specs, pl.BlockSpec((tm, LANES), lambda i, j: (i, 0))]
        out_shape = [out_shape, jax.ShapeDtypeStruct((n, LANES), F32)]
    return pl.pallas_call(
        functools.partial(_norm_matmul_kernel, rope_tiles=rope_tiles, q_tiles=q_tiles,
                          has_extra=has_extra, tn=tn),
        grid=(n // tm, m // tn),
        in_specs=in_specs,
        out_specs=out_specs,
        out_shape=out_shape,
        scratch_shapes=[pltpu.VMEM((tm, k), BF16)],
        compiler_params=_cparams(("arbitrary", "arbitrary")),
        name="norm_matmul",
    )(*args)


def _matmul_res_kernel(a_ref, w_ref, res_ref, mod_ref, o_ref, *, gate_row):
    acc = jnp.dot(a_ref[...], w_ref[...], preferred_element_type=F32)
    o_ref[...] = res_ref[...] + mod_ref[gate_row:gate_row + 1, :] * acc


def matmul_res(a, w, res, mod, seq, gate_row, tm=1024, tn=512):
    n, k = a.shape
    m = w.shape[1]
    tiles_per_seq = seq // tm
    return pl.pallas_call(
        functools.partial(_matmul_res_kernel, gate_row=gate_row),
        grid=(n // tm, m // tn),
        in_specs=[pl.BlockSpec((tm, k), lambda i, j: (i, 0)),
                  pl.BlockSpec((k, tn), lambda i, j: (0, j)),
                  pl.BlockSpec((tm, tn), lambda i, j: (i, j)),
                  pl.BlockSpec((None, 6, tn), lambda i, j: (i // tiles_per_seq, 0, j))],
        out_specs=pl.BlockSpec((tm, tn), lambda i, j: (i, j)),
        out_shape=jax.ShapeDtypeStruct((n, m), F32),
        compiler_params=_cparams(("arbitrary", "arbitrary")),
        name="matmul_res",
    )(a, w, res, mod)


ATTN_GROUPS = (4, 2)


def _grouped_loop(n, body, groups):
    done = 0
    for group in groups:
        trips = (n - done) // group

        def grouped(gi, c, group=group, done=done):
            for t in range(group):
                c = body(done + gi * group + t, c)
            return c

        lax.fori_loop(0, trips, grouped, 0)
        done = done + trips * group
    lax.fori_loop(done, n, body, 0)


def _attn_kernel(q_ref, k_ref, v_ref, lq1_ref, lk1_ref, lq2_ref, lk2_ref, g_ref, o_ref,
                 s_scr, m_scr, l_scr, acc_scr, *, lam_init, tq, tk, seq):
    d = DA_HEAD_DIM
    nlane = tk // LANES
    ndiag = tq // tk
    lam = (jnp.exp(jnp.sum(lq1_ref[...] * lk1_ref[...], axis=-1, keepdims=True))
           - jnp.exp(jnp.sum(lq2_ref[...] * lk2_ref[...], axis=-1, keepdims=True)) + lam_init)
    shift = ATTN_CHUNK.bit_length() - 1
    row_chunk = jnp.right_shift(lax.broadcasted_iota(I32, (tq, tk), 0), shift)
    col_chunk = jnp.right_shift(lax.broadcasted_iota(I32, (tq, tk), 1), shift)
    nt = (((1,), (1,)), ((), ()))

    def lane_fold(a, op):
        part = a[:, 0:LANES]
        for cb in range(1, nlane):
            part = op(part, a[:, cb * LANES:(cb + 1) * LANES])
        return part

    def q_body(qi, _):
        qs = pl.multiple_of(qi * tq, tq)
        qm = (q_ref[pl.ds(qs, tq), 0:d], q_ref[pl.ds(qs, tq), d:2 * d])
        m_scr[...] = jnp.full_like(m_scr, NEG_BIG)

        def score(j, mask):
            ks = pl.multiple_of(j * tk, tk)
            for mp in range(2):
                k = k_ref[pl.ds(ks, tk), mp * d:(mp + 1) * d]
                s = lax.dot_general(qm[mp], k, nt, preferred_element_type=F32)
                if mask is not None:
                    s = jnp.where(mask, s, NEG_BIG)
                s_scr[mp, j] = s
                m_scr[mp] = jnp.maximum(m_scr[mp], lane_fold(s, jnp.maximum))

        def score_body(j, c):
            score(j, None)
            return c

        nfull = qi * ndiag
        _grouped_loop(nfull, score_body, ATTN_GROUPS)
        for t in range(ndiag):
            score(nfull + t, col_chunk + t * (tk // ATTN_CHUNK) <= row_chunk)
        m = [jnp.max(m_scr[mp], axis=-1, keepdims=True) for mp in range(2)]
        l_scr[...] = jnp.zeros_like(l_scr)
        acc_scr[...] = jnp.zeros_like(acc_scr)

        def pv_body(j, c):
            ks = pl.multiple_of(j * tk, tk)
            v = v_ref[pl.ds(ks, tk), :]
            for mp in range(2):
                p = jnp.exp2(s_scr[mp, j] - m[mp])
                l_scr[mp] += lane_fold(p, jnp.add)
                acc_scr[mp] += jnp.dot(p.astype(BF16), v, preferred_element_type=F32)
            return c

        _grouped_loop(nfull + ndiag, pv_body, ATTN_GROUPS)
        l = [jnp.sum(l_scr[mp], axis=-1, keepdims=True) for mp in range(2)]
        o = acc_scr[0] / l[0] - lam * (acc_scr[1] / l[1])
        ms = jnp.mean(o * o, axis=-1, keepdims=True)
        o = o * lax.rsqrt(ms + NORM_EPS) * g_ref[...] * (1.0 - lam_init)
        o_ref[pl.ds(qs, tq), :] = o.astype(o_ref.dtype)
        return 0

    lax.fori_loop(0, seq // tq, q_body, 0)


def diff_attention(qkv, lq1, lk1, lq2, lk2, head_g, lam_init, tq=512, tk=256):
    bsz, seq, _ = qkv.shape
    h, dv = DA_HEADS, 2 * DA_HEAD_DIM
    vec = lambda a: a.reshape(1, -1).astype(F32)
    small = lambda n: pl.BlockSpec((1, n), lambda b, hh: (0, 0))
    return pl.pallas_call(
        functools.partial(_attn_kernel, lam_init=lam_init, tq=tq, tk=tk, seq=seq),
        grid=(bsz, h),
        in_specs=[pl.BlockSpec((None, seq, dv), lambda b, hh: (b, 0, hh)),
                  pl.BlockSpec((None, seq, dv), lambda b, hh: (b, 0, h + hh)),
                  pl.BlockSpec((None, seq, dv), lambda b, hh: (b, 0, 2 * h + hh)),
                  small(DA_HEAD_DIM), small(DA_HEAD_DIM), small(DA_HEAD_DIM), small(DA_HEAD_DIM),
                  small(dv)],
        out_specs=pl.BlockSpec((None, seq, dv), lambda b, hh: (b, 0, hh)),
        out_shape=jax.ShapeDtypeStruct((bsz, seq, h * dv), BF16),
        scratch_shapes=[pltpu.VMEM((2, seq // tk, tq, tk), F32), pltpu.VMEM((2, tq, LANES), F32),
                        pltpu.VMEM((2, tq, LANES), F32), pltpu.VMEM((2, tq, dv), F32)],
        compiler_params=_cparams(("arbitrary", "arbitrary")),
        name="diff_attention",
    )(qkv, qkv, qkv, vec(lq1), vec(lk1), vec(lq2), vec(lk2), vec(head_g))


def _conv_silu_kernel(x_ref, w_ref, b_ref, o_ref, *, k_tile0):
    j = pl.program_id(1)
    x = x_ref[...].astype(F32)
    row = lax.broadcasted_iota(I32, x.shape, 0)
    y = x * w_ref[ML_CONV - 1:ML_CONV, :] + b_ref[...]
    for s in range(1, ML_CONV):
        shifted = jnp.where(row >= s, pltpu.roll(x, s, 0), 0.0)
        y = y + shifted * w_ref[ML_CONV - 1 - s:ML_CONV - s, :]
    y = y * jax.nn.sigmoid(y)
    scale = jnp.where(j >= k_tile0, ML_QK_DIM ** -0.5, 1.0).astype(F32)
    o_ref[...] = (y * scale).astype(o_ref.dtype)


def conv_silu(proj, conv_w, conv_b, tc=128):
    bsz, seq, _ = proj.shape
    cols = conv_w.shape[1]
    return pl.pallas_call(
        functools.partial(_conv_silu_kernel, k_tile0=(cols // 2) // tc),
        grid=(bsz, cols // tc),
        in_specs=[pl.BlockSpec((None, seq, tc), lambda b, j: (b, 0, j)),
                  pl.BlockSpec((ML_CONV, tc), lambda b, j: (0, j)),
                  pl.BlockSpec((1, tc), lambda b, j: (0, j))],
        out_specs=pl.BlockSpec((None, seq, tc), lambda b, j: (b, 0, j)),
        out_shape=jax.ShapeDtypeStruct((bsz, seq, cols), BF16),
        compiler_params=_cparams(("arbitrary", "arbitrary")),
        name="conv_silu",
    )(proj, conv_w, conv_b.reshape(1, cols))


def _mlstm_kernel(q_ref, k_ref, v_ref, op_ref, gc_ref, gr_ref, hg_ref, o_ref,
                  ct_scr, n_scr, m_scr, *, chunk):
    c = pl.program_id(1)
    nh, dqk, dv = ML_HEADS, ML_QK_DIM, ML_V_DIM

    @pl.when(c == 0)
    def _():
        ct_scr[...] = jnp.zeros_like(ct_scr)
        n_scr[...] = jnp.zeros_like(n_scr)
        m_scr[...] = jnp.zeros_like(m_scr)

    gc = gc_ref[...]
    gr = gr_ref[...]
    lf_c = jax.nn.log_sigmoid(gc)
    lf_r = jax.nn.log_sigmoid(gr)
    r_i = lax.broadcasted_iota(I32, (chunk, chunk), 0)
    c_i = lax.broadcasted_iota(I32, (chunk, chunk), 1)
    causal = c_i <= r_i
    tril = causal.astype(F32)
    triu = (r_i <= c_i).astype(F32)
    b_c = _dot3(tril, lf_c)
    b_r = _dot3(lf_r, triu)
    nt = (((1,), (1,)), ((), ()))
    tn_ = (((0,), (0,)), ((), ()))

    for h in range(nh):
        q = q_ref[:, h * dqk:(h + 1) * dqk]
        k = k_ref[:, h * dqk:(h + 1) * dqk]
        v = v_ref[:, h * dv:(h + 1) * dv]
        bc = b_c[:, nh + h:nh + h + 1]
        br = b_r[nh + h:nh + h + 1, :]
        ig_c = gc[:, h:h + 1]
        ig_r = gr[h:h + 1, :]
        m_prev = m_scr[h:h + 1, :]
        dmat = jnp.where(causal, bc - br + ig_r, NEG_BIG)
        inter = bc + m_prev
        m_t = jnp.maximum(inter, jnp.max(dmat, axis=-1, keepdims=True))
        w = jnp.exp(dmat - m_t)
        s = lax.dot_general(q, k, nt, preferred_element_type=F32) * w
        decay = jnp.exp(inter - m_t)
        ct = ct_scr[h]
        num = (jnp.dot(s.astype(BF16), v, preferred_element_type=F32)
               + decay * jnp.dot(q, ct.astype(BF16), preferred_element_type=F32))
        qn = jnp.sum(q.astype(F32) * n_scr[h:h + 1, :], axis=-1, keepdims=True)
        den = jnp.sum(s, axis=-1, keepdims=True) + decay * qn
        hh = num / jnp.maximum(jnp.abs(den), jnp.exp(-m_t))
        b_last = bc[chunk - 1:chunk, :]
        g = b_last - bc + ig_c
        m_new = jnp.maximum(b_last + m_prev, jnp.max(g, axis=0, keepdims=True))
        carry_decay = jnp.exp(b_last + m_prev - m_new)
        wg = jnp.exp(g - m_new)
        wv = (wg * v.astype(F32)).astype(BF16)
        ct_scr[h] = carry_decay * ct + lax.dot_general(k, wv, tn_, preferred_element_type=F32)
        n_scr[h:h + 1, :] = carry_decay * n_scr[h:h + 1, :] + jnp.sum(wg * k.astype(F32), axis=0, keepdims=True)
        m_scr[h:h + 1, :] = m_new
        ms = jnp.mean(hh * hh, axis=-1, keepdims=True)
        hn = hh * lax.rsqrt(ms + NORM_EPS) * hg_ref[:, h * dv:(h + 1) * dv]
        og = jax.nn.sigmoid(op_ref[:, h * dv:(h + 1) * dv].astype(F32))
        o_ref[:, h * dv:(h + 1) * dv] = (og * hn).astype(o_ref.dtype)


def mlstm(qk, proj, gates_c, gates_r, head_g, chunk=128):
    bsz, seq, _ = qk.shape
    nh = ML_HEADS
    qw, vw = nh * ML_QK_DIM, nh * ML_V_DIM
    v_blk = (2 * qw) // vw
    return pl.pallas_call(
        functools.partial(_mlstm_kernel, chunk=chunk),
        grid=(bsz, seq // chunk),
        in_specs=[pl.BlockSpec((None, chunk, qw), lambda b, c: (b, c, 0)),
                  pl.BlockSpec((None, chunk, qw), lambda b, c: (b, c, 1)),
                  pl.BlockSpec((None, chunk, vw), lambda b, c: (b, c, v_blk)),
                  pl.BlockSpec((None, chunk, vw), lambda b, c: (b, c, v_blk + 1)),
                  pl.BlockSpec((None, chunk, LANES), lambda b, c: (b, c, 0)),
                  pl.BlockSpec((None, 2 * nh, chunk), lambda b, c: (b, 0, c)),
                  pl.BlockSpec((1, vw), lambda b, c: (0, 0))],
        out_specs=pl.BlockSpec((None, chunk, vw), lambda b, c: (b, c, 0)),
        out_shape=jax.ShapeDtypeStruct((bsz, seq, vw), BF16),
        scratch_shapes=[pltpu.VMEM((nh, ML_QK_DIM, ML_V_DIM), F32),
                        pltpu.VMEM((nh, ML_QK_DIM), F32),
                        pltpu.VMEM((nh, 1), F32)],
        compiler_params=_cparams(("arbitrary", "arbitrary")),
        name="mlstm",
    )(qk, qk, proj, proj, gates_c, gates_r, head_g.reshape(1, vw))


def _router_kernel(x_ref, g_ref, mod_ref, w_ref, b_ref, hp_ref, rt_ref, cnt_ref, run_scr):
    i = pl.program_id(0)

    @pl.when(i == 0)
    def _():
        run_scr[...] = jnp.zeros_like(run_scr)

    h = _rms_modulate(x_ref[...], g_ref[...], mod_ref[3:4, :], mod_ref[4:5, :])
    half = h.shape[1] // 2
    hp_ref[...] = _pack_bf16_pair(h[:, :half], h[:, half:])
    logits = _dot3(h, w_ref[...]) + b_ref[...]
    tm = logits.shape[0]
    lane =lax.broadcasted_iota(I32, logits.shape, 1).astype(F32)
    ng = float(MOE_GROUPS)
    is_g = lane < ng
    gl = jnp.where(is_g, logits, NEG_BIG)
    gmax = jnp.max(gl, axis=-1, keepdims=True)
    grp = jnp.min(jnp.where(gl == gmax, lane, float(LANES)), axis=-1, keepdims=True)
    p_group = 1.0 / jnp.sum(jnp.where(is_g, jnp.exp(gl - gmax), 0.0), axis=-1, keepdims=True)
    lo = ng + float(MOE_PER_GROUP) * grp
    el = jnp.where((lane >= lo) & (lane < lo + float(MOE_PER_GROUP)), logits, NEG_BIG)
    v1 = jnp.max(el, axis=-1, keepdims=True)
    i1 = jnp.min(jnp.where(el == v1, lane, float(LANES)), axis=-1, keepdims=True)
    el2 = jnp.where(lane == i1, NEG_BIG, el)
    v2 = jnp.max(el2, axis=-1, keepdims=True)
    i2 = jnp.min(jnp.where(el2 == v2, lane, float(LANES)), axis=-1, keepdims=True)
    ex = jnp.exp(v2 - v1)
    w1 = p_group / (1.0 + ex)
    w2 = p_group * (ex / (1.0 + ex))
    oh1 = (lane == i1).astype(F32)
    oh2 = (lane == i2).astype(F32)
    oh = oh1 + oh2
    r_i = lax.broadcasted_iota(I32, (tm, tm), 0)
    c_i = lax.broadcasted_iota(I32, (tm, tm), 1)
    before = (c_i < r_i).astype(BF16)
    prior = jnp.dot(before, oh.astype(BF16), preferred_element_type=F32) + run_scr[...]
    rank1 = jnp.sum(oh1 * prior, axis=-1, keepdims=True)
    rank2 = jnp.sum(oh2 * prior, axis=-1, keepdims=True)
    run_scr[...] += jnp.sum(oh, axis=0, keepdims=True)
    cnt_ref[...] = run_scr[...]
    rt_ref[...] = jnp.where(lane == 0.0, i1 - ng,
                  jnp.where(lane == 1.0, i2 - ng,
                  jnp.where(lane == 2.0, w1,
                  jnp.where(lane == 3.0, w2,
                  jnp.where(lane == 4.0, rank1,
                  jnp.where(lane == 5.0, rank2, 0.0))))))


def moe_router(x, g, mod, w_r, b_r, seq, tm=512):
    n, d = x.shape
    tiles_per_seq = seq // tm
    return pl.pallas_call(
        _router_kernel,
        grid=(n // tm,),
        in_specs=[pl.BlockSpec((tm, d), lambda i: (i, 0)),
                  pl.BlockSpec((1, d), lambda i: (0, 0)),
                  pl.BlockSpec((None, 6, d), lambda i: (i // tiles_per_seq, 0, 0)),
                  pl.BlockSpec((d, LANES), lambda i: (0, 0)),
                  pl.BlockSpec((1, LANES), lambda i: (0, 0))],
        out_specs=[pl.BlockSpec((tm, d // 2), lambda i: (i, 0)),
                   pl.BlockSpec((tm, LANES), lambda i: (i, 0)),
                   pl.BlockSpec((1, LANES), lambda i: (0, 0))],
        out_shape=[jax.ShapeDtypeStruct((n, d // 2), U32),
                   jax.ShapeDtypeStruct((n, LANES), F32),
                   jax.ShapeDtypeStruct((1, LANES), F32)],
        scratch_shapes=[pltpu.VMEM((1, LANES), F32)],
        compiler_params=_cparams(("arbitrary",)),
        name="moe_router",
    )(x, g.reshape(1, d), mod, w_r, b_r)


def _dest_kernel(rt_ref, ps_ref, o_ref):
    rt = rt_ref[...]
    lane = lax.broadcasted_iota(I32, rt.shape, 1).astype(F32)
    ng = float(MOE_GROUPS)
    ps = ps_ref[...]
    d1 = jnp.sum(jnp.where(lane == rt[:, 0:1] + ng, ps, 0.0), axis=-1, keepdims=True) + rt[:, 4:5]
    d2 = jnp.sum(jnp.where(lane == rt[:, 1:2] + ng, ps, 0.0), axis=-1, keepdims=True) + rt[:, 5:6]
    o_ref[...] = jnp.where(lane == 0.0, d1, jnp.where(lane == 1.0, d2, 0.0)).astype(I32)


def moe_dest(route, pad_start_lanes, tm=2048):
    n = route.shape[0]
    return pl.pallas_call(
        _dest_kernel,
        grid=(n // tm,),
        in_specs=[pl.BlockSpec((tm, LANES), lambda i: (i, 0)),
                  pl.BlockSpec((1, LANES), lambda i: (0, 0))],
        out_specs=pl.BlockSpec((tm, LANES), lambda i: (i, 0)),
        out_shape=jax.ShapeDtypeStruct((n, LANES), I32),
        compiler_params=_cparams(("arbitrary",)),
        name="moe_dest",
    )(route, pad_start_lanes)


DMA_UNROLL = 8


def _wait_rows(src_rows, dst_rows, sem, copies):
    for _ in range(copies):
        pltpu.make_async_copy(src_rows, dst_rows, sem).wait()


def _dispatch_kernel(dest_ref, hp_ref, xin_in_ref, xin_ref, sems, *, tt, ntiles):
    del xin_in_ref
    i = pl.program_id(0)
    slot = i % 2
    base = i * tt

    def start(r, c):
        for kk in range(2):
            pltpu.make_async_copy(hp_ref.at[pl.ds(base + r, 1)],
                                  xin_ref.at[pl.ds(dest_ref[0, 0, 2 * r + kk], 1)], sems.at[slot]).start()
        return c

    lax.fori_loop(0, tt, start, 0, unroll=DMA_UNROLL)
    block = (hp_ref.at[pl.ds(0, tt)], xin_ref.at[pl.ds(0, tt)])

    @pl.when(i > 0)
    def _():
        _wait_rows(*block, sems.at[1 - slot], 2)

    @pl.when(i == ntiles - 1)
    def _():
        _wait_rows(*block, sems.at[slot], 2)


def moe_dispatch(hpack, dest, rows, tt=256):
    n, w = hpack.shape
    return pl.pallas_call(
        functools.partial(_dispatch_kernel, tt=tt, ntiles=n // tt),
        grid=(n // tt,),
        in_specs=[pl.BlockSpec((1, 1, 2 * tt), lambda i: (i, 0, 0), memory_space=pltpu.SMEM),
                  pl.BlockSpec(memory_space=pl.ANY),
                  pl.BlockSpec(memory_space=pl.ANY)],
        out_specs=pl.BlockSpec(memory_space=pl.ANY),
        out_shape=jax.ShapeDtypeStruct((rows, w), U32),
        scratch_shapes=[pltpu.SemaphoreType.DMA((2,))],
        input_output_aliases={2: 0},
        compiler_params=_cparams(("arbitrary",)),
        name="moe_dispatch",
    )(dest.reshape(n // tt, 1, 2 * tt), hpack, jnp.zeros((rows, w), U32))


def _expert_kernel(be_ref, first_ref, nxt_ref, nu_ref, x_ref, wgu_hbm, wd_hbm, y_ref,
                   wgu_bf, wd_bf, stg_gu, stg_d, xs_scr, acc_scr, sems, *, layer, th, cr):
    i = pl.program_id(0)

    def fetch(e):
        return (pltpu.make_async_copy(wgu_hbm.at[layer, e], stg_gu, sems.at[0]),
                pltpu.make_async_copy(wd_hbm.at[layer, e], stg_d, sems.at[1]))

    @pl.when(i < nu_ref[0])
    def _():
        @pl.when(first_ref[i] == 1)
        def _():
            @pl.when(i == 0)
            def _():
                for cp in fetch(be_ref[i]):
                    cp.start()

            for cp in fetch(be_ref[i]):
                cp.wait()

            def cast_rows(src, dst):
                def body(r, c):
                    rs = pl.multiple_of(r * cr, cr)
                    dst[pl.ds(rs, cr), :] = src[pl.ds(rs, cr), :].astype(BF16)
                    return c
                lax.fori_loop(0, src.shape[0] // cr, body, 0)

            cast_rows(stg_gu, wgu_bf)
            cast_rows(stg_d, wd_bf)

            @pl.when(nxt_ref[i] >= 0)
            def _():
                for cp in fetch(nxt_ref[i]):
                    cp.start()

        half = x_ref.shape[1]
        hid = wd_bf.shape[0]
        lo, hi = _unpack_bf16_pair(x_ref[...])
        xs_scr[:, :half] = lo.astype(BF16)
        xs_scr[:, half:] = hi.astype(BF16)
        xs = xs_scr[...]
        for c in range(hid // th):
            gt = jnp.dot(xs, wgu_bf[:, c * th:(c + 1) * th], preferred_element_type=F32)
            up = jnp.dot(xs, wgu_bf[:, hid + c * th:hid + (c + 1) * th], preferred_element_type=F32)
            act = (gt * jax.nn.sigmoid(gt) * up).astype(BF16)
            part = jnp.dot(act, wd_bf[c * th:(c + 1) * th, :], preferred_element_type=F32)
            if c == 0:
                acc_scr[...] = part
            else:
                acc_scr[...] += part
        y = acc_scr[...]
        y_ref[...] = _pack_bf16_pair(y[:, :half], y[:, half:])


def moe_experts(xin, sched, wgu_all, wd_all, layer, tm, th=256, cr=256):
    rows, half = xin.shape
    _, _, d, hid2 = wgu_all.shape
    hid = hid2 // 2
    nb = rows // tm
    block_e, first, nxt, n_used = sched

    def blk(i, be, fi, nx, nu):
        return (jnp.minimum(i, nu[0] - 1), 0)

    grid_spec = pltpu.PrefetchScalarGridSpec(
        num_scalar_prefetch=4,
        grid=(nb,),
        in_specs=[pl.BlockSpec((tm, half), blk),
                  pl.BlockSpec(memory_space=pl.ANY),
                  pl.BlockSpec(memory_space=pl.ANY)],
        out_specs=pl.BlockSpec((tm, half), blk),
        scratch_shapes=[pltpu.VMEM((d, hid2), BF16), pltpu.VMEM((hid, d), BF16),
                        pltpu.VMEM((d, hid2), F32), pltpu.VMEM((hid, d), F32),
                        pltpu.VMEM((tm, d), BF16), pltpu.VMEM((tm, d), F32),
                        pltpu.SemaphoreType.DMA((2,))],
    )
    return pl.pallas_call(
        functools.partial(_expert_kernel, layer=layer, th=th, cr=cr),
        grid_spec=grid_spec,
        out_shape=jax.ShapeDtypeStruct((rows, half), U32),
        input_output_aliases={4: 0},
        compiler_params=_cparams(("arbitrary",)),
        name="moe_experts",
    )(block_e, first, nxt, n_used, xin, wgu_all, wd_all)


def _combine_kernel(dcur_ref, dnxt_ref, x_ref, rt_ref, mod_ref, fg_ref, y_ref, o_ref, ya, yb, sems,
                    *, tt, ntiles, final_norm):
    i = pl.program_id(0)
    slot = i % 2

    def issue(dref, sl):
        def body(r, c):
            pltpu.make_async_copy(y_ref.at[pl.ds(dref[0, 0, 2 * r], 1)], ya.at[sl, pl.ds(r, 1)], sems.at[sl]).start()
            pltpu.make_async_copy(y_ref.at[pl.ds(dref[0, 0, 2 * r + 1], 1)], yb.at[sl, pl.ds(r, 1)], sems.at[sl]).start()
            return c
        lax.fori_loop(0, tt, body, 0, unroll=DMA_UNROLL)

    @pl.when(i == 0)
    def _():
        issue(dcur_ref, 0)

    @pl.when(i + 1 < ntiles)
    def _():
        issue(dnxt_ref, 1 - slot)

    _wait_rows(y_ref.at[pl.ds(0, tt)], ya.at[slot], sems.at[slot], 2)
    half = ya.shape[2]
    w1 = rt_ref[:, 2:3]
    w2 = rt_ref[:, 3:4]
    a_lo, a_hi = _unpack_bf16_pair(ya[slot])
    b_lo, b_hi = _unpack_bf16_pair(yb[slot])
    gate = mod_ref[5:6, :]
    out_lo = x_ref[:, :half] + gate[:, :half] * (a_lo * w1 + b_lo * w2)
    out_hi = x_ref[:, half:] + gate[:, half:] * (a_hi * w1 + b_hi * w2)
    if final_norm:
        ms = (jnp.sum(out_lo * out_lo, axis=-1, keepdims=True)
              + jnp.sum(out_hi * out_hi, axis=-1, keepdims=True)) / (2 * half)
        r = lax.rsqrt(ms + NORM_EPS)
        out_lo = out_lo * r * fg_ref[:, :half]
        out_hi = out_hi * r * fg_ref[:, half:]
    o_ref[:, :half] = out_lo
    o_ref[:, half:] = out_hi


def moe_combine(x, y, dest, route, mod, final_g, seq, final_norm, tt=256):
    n, d = x.shape
    tiles_per_seq = seq // tt
    ntiles = n // tt
    dest3 = dest.reshape(ntiles, 1, 2 * tt)
    return pl.pallas_call(
        functools.partial(_combine_kernel, tt=tt, ntiles=ntiles, final_norm=final_norm),
        grid=(ntiles,),
        in_specs=[pl.BlockSpec((1, 1, 2 * tt), lambda i: (i, 0, 0), memory_space=pltpu.SMEM),
                  pl.BlockSpec((1, 1, 2 * tt), lambda i: (jnp.minimum(i + 1, ntiles - 1), 0, 0),
                               memory_space=pltpu.SMEM),
                  pl.BlockSpec((tt, d), lambda i: (i, 0)),
                  pl.BlockSpec((tt, LANES), lambda i: (i, 0)),
                  pl.BlockSpec((None, 6, d), lambda i: (i // tiles_per_seq, 0, 0)),
                  pl.BlockSpec((1, d), lambda i: (0, 0)),
                  pl.BlockSpec(memory_space=pl.ANY)],
        out_specs=pl.BlockSpec((tt, d), lambda i: (i, 0)),
        out_shape=jax.ShapeDtypeStruct((n, d), F32),
        scratch_shapes=[pltpu.VMEM((2, tt, d // 2), U32), pltpu.VMEM((2, tt, d // 2), U32),
                        pltpu.SemaphoreType.DMA((2,))],
        compiler_params=_cparams(("arbitrary",)),
        name="moe_combine",
    )(dest3, dest3, x, route, mod, final_g.reshape(1, d), y)


def _expert_schedule(counts, tm, nb):
    ne = counts.shape[0]
    padded = ((counts + tm - 1) // tm) * tm
    pad_end = jnp.cumsum(padded)
    pad_start = pad_end - padded
    n_used = (pad_end[-1] // tm).astype(I32)
    blk0 = jnp.arange(nb, dtype=I32) * tm
    block_e = jnp.minimum(jnp.searchsorted(pad_end, blk0, side='right'), ne - 1).astype(I32)
    first = (blk0 == pad_start[block_e]).astype(I32)
    ids = jnp.where(counts > 0, jnp.arange(ne, dtype=I32), ne)
    later = jnp.concatenate([lax.cummin(ids[::-1])[::-1][1:], jnp.full((1,), ne, I32)])
    nxt = jnp.where(later < ne, later, -1).astype(I32)[block_e]
    return pad_start, (block_e, first, nxt, n_used.reshape(1))


def hier_moe_layer(x, g, mod, w_group, b_group, w_expert, b_expert, wgu_all, wd_all, layer, final_g, seq,
                   final_norm, tm=256):
    n, d = x.shape
    ng, ne = w_group.shape[1], w_expert.shape[1]
    w_r = jnp.zeros((d, LANES), F32).at[:, :ng].set(w_group).at[:, ng:ng + ne].set(w_expert)
    b_r = jnp.zeros((1, LANES), F32).at[0, :ng].set(b_group).at[0, ng:ng + ne].set(b_expert)
    hpack, route, cnt = moe_router(x, g, mod, w_r, b_r, seq)
    counts = cnt[0, ng:ng + ne].astype(I32)
    rows = 2 * n + ne * tm
    pad_start, sched = _expert_schedule(counts, tm, rows // tm)
    ps_lanes = jnp.zeros((1, LANES), F32).at[0, ng:ng + ne].set(pad_start.astype(F32))
    dest = moe_dest(route, ps_lanes, tm=min(2048, n))[:, :2].reshape(-1)
    xin = moe_dispatch(hpack, dest, rows)
    y = moe_experts(xin, sched, wgu_all, wd_all, layer, tm)
    return moe_combine(x, y, dest, route, mod, final_g, seq, final_norm)


def _rope_tables(positions):
    half = ROPE_DIM // 2
    inv_freq = ROPE_THETA ** (-jnp.arange(half, dtype=F32) * 2.0 / ROPE_DIM)
    ang = positions.astype(F32).reshape(-1, 1) * inv_freq
    cos, sin = jnp.cos(ang), jnp.sin(ang)
    n = ang.shape[0]
    pad = LANES - ROPE_DIM
    c = jnp.concatenate([cos, cos, jnp.ones((n, pad), F32)], axis=1)
    s_lo = jnp.concatenate([-sin, jnp.zeros((n, LANES - half), F32)], axis=1)
    s_hi = jnp.concatenate([jnp.zeros((n, half), F32), sin, jnp.zeros((n, pad), F32)], axis=1)
    return c, s_lo, s_hi


def kernel(x, c, positions, ada_w, ada_b, norm_mix_g, norm_ffn_g, final_norm_g, attn_w_in, attn_w_out, attn_lambda_q1, attn_lambda_k1, attn_lambda_q2, attn_lambda_k2, attn_head_norm_g, mlstm_w_in, mlstm_conv_w, mlstm_conv_b, mlstm_gate_b, mlstm_head_norm_g, mlstm_w_out, moe_w_group, moe_b_group, moe_w_expert, moe_b_expert, moe_w_gu, moe_w_down):
    bsz, seq, d = x.shape
    n = bsz * seq
    depth = ada_w.shape[0]
    mod = adaln(c, ada_w, ada_b)
    xf = x.reshape(n, d)
    for i in range(depth):
        jm = i // 2
        if i % 2 == 0:
            qk_cols = 2 * DA_HEADS * 2 * DA_HEAD_DIM
            tabs = _rope_tables(positions)
            qkv = norm_matmul(xf, norm_mix_g[i], mod[i], attn_w_in[jm].astype(BF16), seq,
                              rope=(*tabs, qk_cols, qk_cols // 2))
            lam_init = 0.8 - 0.6 * math.exp(-0.3 * i)
            mixed = diff_attention(qkv.reshape(bsz, seq, -1), attn_lambda_q1[jm], attn_lambda_k1[jm],
                                   attn_lambda_q2[jm], attn_lambda_k2[jm], attn_head_norm_g[jm], lam_init)
            w_out = attn_w_out[jm]
        else:
            qk_cols = 2 * ML_HEADS * ML_QK_DIM
            main_cols = qk_cols + 2 * ML_HEADS * ML_V_DIM
            w_in = mlstm_w_in[jm]
            ngate = 2 * ML_HEADS
            w_gate = jnp.zeros((d, LANES), F32).at[:, :ngate].set(w_in[:, main_cols:])
            b_gate = jnp.zeros((1, LANES), F32).at[0, :ngate].set(mlstm_gate_b[jm])
            proj, gates = norm_matmul(xf, norm_mix_g[i], mod[i], w_in[:, :main_cols].astype(BF16), seq,
                                      extra=(w_gate, b_gate))
            proj = proj.reshape(bsz, seq, main_cols)
            qk = conv_silu(proj, mlstm_conv_w[jm], mlstm_conv_b[jm])
            gates_c = gates.reshape(bsz, seq, LANES)
            gates_r = jnp.swapaxes(gates_c[:, :, :ngate], 1, 2)
            mixed = mlstm(qk, proj, gates_c, gates_r, mlstm_head_norm_g[jm])
            w_out = mlstm_w_out[jm]
        xf = matmul_res(mixed.reshape(n, -1), w_out.astype(BF16), xf, mod[i], seq, gate_row=2)
        xf = hier_moe_layer(xf, norm_ffn_g[i], mod[i], moe_w_group[i], moe_b_group[i], moe_w_expert[i],
                            moe_b_expert[i], moe_w_gu, moe_w_down, i, final_norm_g, seq,
                            final_norm=(i == depth - 1))
    return xf.reshape(bsz, seq, d)
```

```python
import functools
import math

import jax
import jax.numpy as jnp
from jax import lax
from jax.experimental import pallas as pl
from jax.experimental.pallas import tpu as pltpu

F32 = jnp.float32
BF16 = jnp.bfloat16
U32 = jnp.uint32
I32 = jnp.int32

NORM_EPS = 1e-6
ROPE_THETA = 500000.0
ATTN_CHUNK = 64
DA_HEADS = 8
DA_HEAD_DIM = 128
ROPE_DIM = 32
ML_HEADS = 8
ML_QK_DIM = 128
ML_V_DIM = 256
ML_CONV = 4
MOE_GROUPS = 4
MOE_PER_GROUP = 8
MOE_EXPERTS = 32
LANES = 128
NEG_BIG = -1e30
LOG2_E = math.log2(math.e)

VMEM_LIMIT = 56 * 1024 * 1024


def _cparams(sem):
    return pltpu.CompilerParams(dimension_semantics=sem, vmem_limit_bytes=VMEM_LIMIT)


def _split_hi_lo(a):
    hi = a.astype(BF16)
    lo = (a - hi.astype(F32)).astype(BF16)
    return hi, lo


def _dot3(a, w):
    ah, al = _split_hi_lo(a)
    wh, wl = _split_hi_lo(w)
    d = functools.partial(jnp.dot, preferred_element_type=F32)
    return d(ah, wh) + (d(ah, wl) + d(al, wh))


def _pack_bf16_pair(lo_f32, hi_f32):
    lo_bits = lax.bitcast_convert_type(lo_f32.astype(BF16).astype(F32), U32)
    hi_bits = lax.bitcast_convert_type(hi_f32.astype(BF16).astype(F32), U32)
    return hi_bits | (lo_bits >> 16)


def _unpack_bf16_pair(word):
    lo = lax.bitcast_convert_type(word << 16, F32)
    hi = lax.bitcast_convert_type(word & jnp.uint32(0xFFFF0000), F32)
    return lo, hi


def _rms_modulate(x, g, shift, scale):
    ms = jnp.mean(x * x, axis=-1, keepdims=True)
    y = x * lax.rsqrt(ms + NORM_EPS) * g
    return y * (1.0 + scale) + shift


def _adaln_kernel(c_ref, w_ref, b_ref, o_ref):
    c = c_ref[...]
    cond = c * jax.nn.sigmoid(c)
    acc = jnp.dot(cond.astype(BF16), w_ref[...].astype(BF16), preferred_element_type=F32)
    o_ref[...] = acc + b_ref[...]


def adaln(c, ada_w, ada_b, tn=1024):
    depth, d, n6 = ada_w.shape
    bsz = c.shape[0]
    rows = 8
    cp = jnp.zeros((rows, d), F32).at[:bsz].set(c)
    out = pl.pallas_call(
        _adaln_kernel,
        grid=(depth, n6 // tn),
        in_specs=[pl.BlockSpec((rows, d), lambda l, j: (0, 0)),
                  pl.BlockSpec((None, d, tn), lambda l, j: (l, 0, j)),
                  pl.BlockSpec((None, 1, tn), lambda l, j: (l, 0, j))],
        out_specs=pl.BlockSpec((None, rows, tn), lambda l, j: (l, 0, j)),
        out_shape=jax.ShapeDtypeStruct((depth, rows, n6), F32),
        compiler_params=_cparams(("arbitrary", "arbitrary")),
        name="adaln",
    )(cp, ada_w, ada_b.reshape(depth, 1, n6))
    return out[:, :bsz].reshape(depth, bsz, 6, d)


def _norm_matmul_kernel(*refs, rope_tiles, q_tiles, has_extra, tn):
    if has_extra:
        x_ref, g_ref, mod_ref, w_ref, c_ref, s1_ref, s2_ref, we_ref, be_ref, o_ref, oe_ref, h_scr = refs
    else:
        x_ref, g_ref, mod_ref, w_ref, c_ref, s1_ref, s2_ref, o_ref, h_scr = refs
    j = pl.program_id(1)

    @pl.when(j == 0)
    def _():
        h = _rms_modulate(x_ref[...], g_ref[...], mod_ref[0:1, :], mod_ref[1:2, :])
        h_scr[...] = h.astype(BF16)
        if has_extra:
            oe_ref[...] = _dot3(h, we_ref[...]) + be_ref[...]

    acc = jnp.dot(h_scr[...], w_ref[...], preferred_element_type=F32)
    if rope_tiles == 0:
        o_ref[...] = acc.astype(o_ref.dtype)
    else:
        @pl.when(j < rope_tiles)
        def _():
            scale = jnp.where(j < q_tiles, DA_HEAD_DIM ** -0.5 * LOG2_E, 1.0).astype(F32)
            cs, s1, s2 = c_ref[...] * scale, s1_ref[...] * scale, s2_ref[...] * scale
            for m in range(tn // LANES):
                t = acc[:, m * LANES:(m + 1) * LANES]
                r = t * cs + pltpu.roll(t, LANES - ROPE_DIM // 2, 1) * s1 + pltpu.roll(t, ROPE_DIM // 2, 1) * s2
                o_ref[:, m * LANES:(m + 1) * LANES] = r.astype(o_ref.dtype)

        @pl.when(j >= rope_tiles)
        def _():
            o_ref[...] = acc.astype(o_ref.dtype)


def norm_matmul(x, g, mod, w, seq, rope=None, extra=None, tm=1024, tn=512):
    n, k = x.shape
    m = w.shape[1]
    tiles_per_seq = seq // tm
    has_extra = extra is not None
    if rope is None:
        dummy = jnp.zeros((8, LANES), F32)
        tabs = (dummy, dummy, dummy)
        tab_spec = pl.BlockSpec((8, LANES), lambda i, j: (0, 0))
        rope_tiles = q_tiles = 0
    else:
        tabs = rope[:3]
        tab_spec = pl.BlockSpec((tm, LANES), lambda i, j: (i, 0))
        rope_tiles, q_tiles = rope[3] // tn, rope[4] // tn
    in_specs = [pl.BlockSpec((tm, k), lambda i, j: (i, 0)),
                pl.BlockSpec((1, k), lambda i, j: (0, 0)),
                pl.BlockSpec((None, 6, k), lambda i, j: (i // tiles_per_seq, 0, 0)),
                pl.BlockSpec((k, tn), lambda i, j: (0, j)),
                tab_spec, tab_spec, tab_spec]
    args = [x, g.reshape(1, k), mod, w, *tabs]
    out_specs = pl.BlockSpec((tm, tn), lambda i, j: (i, j))
    out_shape = jax.ShapeDtypeStruct((n, m), BF16)
    if has_extra:
        we, be = extra
        in_specs += [pl.BlockSpec((k, LANES), lambda i, j: (0, 0)),
                     pl.BlockSpec((1, LANES), lambda i, j: (0, 0))]
        args += [we, be]
        out_specs = [out_specs, pl.BlockSpec((tm, LANES), lambda i, j: (i, 0))]
        out_shape = [out_shape, jax.ShapeDtypeStruct((n, LANES), F32)]
    return pl.pallas_call(
        functools.partial(_norm_matmul_kernel, rope_tiles=rope_tiles, q_tiles=q_tiles,
                          has_extra=has_extra, tn=tn),
        grid=(n // tm, m // tn),
        in_specs=in_specs,
        out_specs=out_specs,
        out_shape=out_shape,
        scratch_shapes=[pltpu.VMEM((tm, k), BF16)],
        compiler_params=_cparams(("arbitrary", "arbitrary")),
        name="norm_matmul",
    )(*args)


def _matmul_res_kernel(a_ref, w_ref, res_ref, mod_ref, o_ref, *, gate_row):
    acc = jnp.dot(a_ref[...], w_ref[...], preferred_element_type=F32)
    o_ref[...] = res_ref[...] + mod_ref[gate_row:gate_row + 1, :] * acc


def matmul_res(a, w, res, mod, seq, gate_row, tm=512, tn=2048):
    n, k = a.shape
    m = w.shape[1]
    tiles_per_seq = seq // tm
    return pl.pallas_call(
        functools.partial(_matmul_res_kernel, gate_row=gate_row),
        grid=(n // tm, m // tn),
        in_specs=[pl.BlockSpec((tm, k), lambda i, j: (i, 0)),
                  pl.BlockSpec((k, tn), lambda i, j: (0, j)),
                  pl.BlockSpec((tm, tn), lambda i, j: (i, j)),
                  pl.BlockSpec((None, 6, tn), lambda i, j: (i // tiles_per_seq, 0, j))],
        out_specs=pl.BlockSpec((tm, tn), lambda i, j: (i, j)),
        out_shape=jax.ShapeDtypeStruct((n, m), F32),
        compiler_params=_cparams(("arbitrary", "arbitrary")),
        name="matmul_res",
    )(a, w, res, mod)


ATTN_GROUPS = (4, 2)


def _grouped_loop(n, body, groups):
    done = 0
    for group in groups:
        trips = (n - done) // group

        def grouped(gi, c, group=group, done=done):
            for t in range(group):
                c = body(done + gi * group + t, c)
            return c

        lax.fori_loop(0, trips, grouped, 0)
        done = done + trips * group
    lax.fori_loop(done, n, body, 0)


def _attn_kernel(q_ref, k_ref, v_ref, lq1_ref, lk1_ref, lq2_ref, lk2_ref, g_ref, o_ref,
                 s_scr, m_scr, l_scr, acc_scr, *, lam_init, tq, tk, seq):
    d = DA_HEAD_DIM
    nlane = tk // LANES
    ndiag = tq // tk
    lam = (jnp.exp(jnp.sum(lq1_ref[...] * lk1_ref[...], axis=-1, keepdims=True))
           - jnp.exp(jnp.sum(lq2_ref[...] * lk2_ref[...], axis=-1, keepdims=True)) + lam_init)
    shift = ATTN_CHUNK.bit_length() - 1
    row_chunk = jnp.right_shift(lax.broadcasted_iota(I32, (tq, tk), 0), shift)
    col_chunk = jnp.right_shift(lax.broadcasted_iota(I32, (tq, tk), 1), shift)
    nt = (((1,), (1,)), ((), ()))

    def lane_fold(a, op):
        part = a[:, 0:LANES]
        for cb in range(1, nlane):
            part = op(part, a[:, cb * LANES:(cb + 1) * LANES])
        return part

    def q_body(qi, _):
        qs = pl.multiple_of(qi * tq, tq)
        qm = (q_ref[pl.ds(qs, tq), 0:d], q_ref[pl.ds(qs, tq), d:2 * d])
        m_scr[...] = jnp.full_like(m_scr, NEG_BIG)

        def score(j, mask):
            ks = pl.multiple_of(j * tk, tk)
            for mp in range(2):
                k = k_ref[pl.ds(ks, tk), mp * d:(mp + 1) * d]
                s = lax.dot_general(qm[mp], k, nt, preferred_element_type=F32)
                if mask is not None:
                    s = jnp.where(mask, s, NEG_BIG)
                s_scr[mp, j] = s
                m_scr[mp] = jnp.maximum(m_scr[mp], lane_fold(s, jnp.maximum))

        def score_body(j, c):
            score(j, None)
            return c

        nfull = qi * ndiag
        _grouped_loop(nfull, score_body, ATTN_GROUPS)
        for t in range(ndiag):
            score(nfull + t, col_chunk + t * (tk // ATTN_CHUNK) <= row_chunk)
        m = [jnp.max(m_scr[mp], axis=-1, keepdims=True) for mp in range(2)]
        l_scr[...] = jnp.zeros_like(l_scr)
        acc_scr[...] = jnp.zeros_like(acc_scr)

        def pv_body(j, c):
            ks = pl.multiple_of(j * tk, tk)
            v = v_ref[pl.ds(ks, tk), :]
            for mp in range(2):
                p = jnp.exp2(s_scr[mp, j] - m[mp])
                l_scr[mp] += lane_fold(p, jnp.add)
                acc_scr[mp] += jnp.dot(p.astype(BF16), v, preferred_element_type=F32)
            return c

        _grouped_loop(nfull + ndiag, pv_body, ATTN_GROUPS)
        l = [jnp.sum(l_scr[mp], axis=-1, keepdims=True) for mp in range(2)]
        o = acc_scr[0] / l[0] - lam * (acc_scr[1] / l[1])
        ms = jnp.mean(o * o, axis=-1, keepdims=True)
        o = o * lax.rsqrt(ms + NORM_EPS) * g_ref[...] * (1.0 - lam_init)
        o_ref[pl.ds(qs, tq), :] = o.astype(o_ref.dtype)
        return 0

    lax.fori_loop(0, seq // tq, q_body, 0)


def diff_attention(qkv, lq1, lk1, lq2, lk2, head_g, lam_init, tq=512, tk=256):
    bsz, seq, _ = qkv.shape
    h, dv = DA_HEADS, 2 * DA_HEAD_DIM
    vec = lambda a: a.reshape(1, -1).astype(F32)
    small = lambda n: pl.BlockSpec((1, n), lambda b, hh: (0, 0))
    return pl.pallas_call(
        functools.partial(_attn_kernel, lam_init=lam_init, tq=tq, tk=tk, seq=seq),
        grid=(bsz, h),
        in_specs=[pl.BlockSpec((None, seq, dv), lambda b, hh: (b, 0, hh)),
                  pl.BlockSpec((None, seq, dv), lambda b, hh: (b, 0, h + hh)),
                  pl.BlockSpec((None, seq, dv), lambda b, hh: (b, 0, 2 * h + hh)),
                  small(DA_HEAD_DIM), small(DA_HEAD_DIM), small(DA_HEAD_DIM), small(DA_HEAD_DIM),
                  small(dv)],
        out_specs=pl.BlockSpec((None, seq, dv), lambda b, hh: (b, 0, hh)),
        out_shape=jax.ShapeDtypeStruct((bsz, seq, h * dv), BF16),
        scratch_shapes=[pltpu.VMEM((2, seq // tk, tq, tk), F32), pltpu.VMEM((2, tq, LANES), F32),
                        pltpu.VMEM((2, tq, LANES), F32), pltpu.VMEM((2, tq, dv), F32)],
        compiler_params=_cparams(("arbitrary", "arbitrary")),
        name="diff_attention",
    )(qkv, qkv, qkv, vec(lq1), vec(lk1), vec(lq2), vec(lk2), vec(head_g))


def _conv_silu_kernel(x_ref, w_ref, b_ref, o_ref, *, k_tile0):
    j = pl.program_id(1)
    x = x_ref[...].astype(F32)
    row = lax.broadcasted_iota(I32, x.shape, 0)
    y = x * w_ref[ML_CONV - 1:ML_CONV, :] + b_ref[...]
    for s in range(1, ML_CONV):
        shifted = jnp.where(row >= s, pltpu.roll(x, s, 0), 0.0)
        y = y + shifted * w_ref[ML_CONV - 1 - s:ML_CONV - s, :]
    y = y * jax.nn.sigmoid(y)
    scale = jnp.where(j >= k_tile0, ML_QK_DIM ** -0.5, 1.0).astype(F32)
    o_ref[...] = (y * scale).astype(o_ref.dtype)


def conv_silu(proj, conv_w, conv_b, tc=128):
    bsz, seq, _ = proj.shape
    cols = conv_w.shape[1]
    return pl.pallas_call(
        functools.partial(_conv_silu_kernel, k_tile0=(cols // 2) // tc),
        grid=(bsz, cols // tc),
        in_specs=[pl.BlockSpec((None, seq, tc), lambda b, j: (b, 0, j)),
                  pl.BlockSpec((ML_CONV, tc), lambda b, j: (0, j)),
                  pl.BlockSpec((1, tc), lambda b, j: (0, j))],
        out_specs=pl.BlockSpec((None, seq, tc), lambda b, j: (b, 0, j)),
        out_shape=jax.ShapeDtypeStruct((bsz, seq, cols), BF16),
        compiler_params=_cparams(("arbitrary", "arbitrary")),
        name="conv_silu",
    )(proj, conv_w, conv_b.reshape(1, cols))


def _mlstm_kernel(q_ref, k_ref, v_ref, op_ref, gc_ref, gr_ref, hg_ref, o_ref,
                  ct_scr, n_scr, m_scr, *, chunk):
    c = pl.program_id(1)
    nh, dqk, dv = ML_HEADS, ML_QK_DIM, ML_V_DIM

    @pl.when(c == 0)
    def _():
        ct_scr[...] = jnp.zeros_like(ct_scr)
        n_scr[...] = jnp.zeros_like(n_scr)
        m_scr[...] = jnp.zeros_like(m_scr)

    gc = gc_ref[...]
    gr = gr_ref[...]
    lf_c = jax.nn.log_sigmoid(gc)
    lf_r = jax.nn.log_sigmoid(gr)
    r_i = lax.broadcasted_iota(I32, (chunk, chunk), 0)
    c_i = lax.broadcasted_iota(I32, (chunk, chunk), 1)
    causal = c_i <= r_i
    tril = causal.astype(F32)
    triu = (r_i <= c_i).astype(F32)
    b_c = _dot3(tril, lf_c)
    b_r = _dot3(lf_r, triu)
    nt = (((1,), (1,)), ((), ()))
    tn_ = (((0,), (0,)), ((), ()))

    for h in range(nh):
        q = q_ref[:, h * dqk:(h + 1) * dqk]
        k = k_ref[:, h * dqk:(h + 1) * dqk]
        v = v_ref[:, h * dv:(h + 1) * dv]
        bc = b_c[:, nh + h:nh + h + 1]
        br = b_r[nh + h:nh + h + 1, :]
        ig_c = gc[:, h:h + 1]
        ig_r = gr[h:h + 1, :]
        m_prev = m_scr[h:h + 1, :]
        dmat = jnp.where(causal, bc - br + ig_r, NEG_BIG)
        inter = bc + m_prev
        m_t = jnp.maximum(inter, jnp.max(dmat, axis=-1, keepdims=True))
        w = jnp.exp(dmat - m_t)
        s = lax.dot_general(q, k, nt, preferred_element_type=F32) * w
        decay = jnp.exp(inter - m_t)
        ct = ct_scr[h]
        num = (jnp.dot(s.astype(BF16), v, preferred_element_type=F32)
               + decay * jnp.dot(q, ct.astype(BF16), preferred_element_type=F32))
        qn = jnp.sum(q.astype(F32) * n_scr[h:h + 1, :], axis=-1, keepdims=True)
        den = jnp.sum(s, axis=-1, keepdims=True) + decay * qn
        hh = num / jnp.maximum(jnp.abs(den), jnp.exp(-m_t))
        b_last = bc[chunk - 1:chunk, :]
        g = b_last - bc + ig_c
        m_new = jnp.maximum(b_last + m_prev, jnp.max(g, axis=0, keepdims=True))
        carry_decay = jnp.exp(b_last + m_prev - m_new)
        wg = jnp.exp(g - m_new)
        wv = (wg * v.astype(F32)).astype(BF16)
        ct_scr[h] = carry_decay * ct + lax.dot_general(k, wv, tn_, preferred_element_type=F32)
        n_scr[h:h + 1, :] = carry_decay * n_scr[h:h + 1, :] + jnp.sum(wg * k.astype(F32), axis=0, keepdims=True)
        m_scr[h:h + 1, :] = m_new
        ms = jnp.mean(hh * hh, axis=-1, keepdims=True)
        hn = hh * lax.rsqrt(ms + NORM_EPS) * hg_ref[:, h * dv:(h + 1) * dv]
        og = jax.nn.sigmoid(op_ref[:, h * dv:(h + 1) * dv].astype(F32))
        o_ref[:, h * dv:(h + 1) * dv] = (og * hn).astype(o_ref.dtype)


def mlstm(qk, proj, gates_c, gates_r, head_g, chunk=128):
    bsz, seq, _ = qk.shape
    nh = ML_HEADS
    qw, vw = nh * ML_QK_DIM, nh * ML_V_DIM
    v_blk = (2 * qw) // vw
    return pl.pallas_call(
        functools.partial(_mlstm_kernel, chunk=chunk),
        grid=(bsz, seq // chunk),
        in_specs=[pl.BlockSpec((None, chunk, qw), lambda b, c: (b, c, 0)),
                  pl.BlockSpec((None, chunk, qw), lambda b, c: (b, c, 1)),
                  pl.BlockSpec((None, chunk, vw), lambda b, c: (b, c, v_blk)),
                  pl.BlockSpec((None, chunk, vw), lambda b, c: (b, c, v_blk + 1)),
                  pl.BlockSpec((None, chunk, LANES), lambda b, c: (b, c, 0)),
                  pl.BlockSpec((None, 2 * nh, chunk), lambda b, c: (b, 0, c)),
                  pl.BlockSpec((1, vw), lambda b, c: (0, 0))],
        out_specs=pl.BlockSpec((None, chunk, vw), lambda b, c: (b, c, 0)),
        out_shape=jax.ShapeDtypeStruct((bsz, seq, vw), BF16),
        scratch_shapes=[pltpu.VMEM((nh, ML_QK_DIM, ML_V_DIM), F32),
                        pltpu.VMEM((nh, ML_QK_DIM), F32),
                        pltpu.VMEM((nh, 1), F32)],
        compiler_params=_cparams(("arbitrary", "arbitrary")),
        name="mlstm",
    )(qk, qk, proj, proj, gates_c, gates_r, head_g.reshape(1, vw))


def _router_kernel(x_ref, g_ref, mod_ref, w_ref, b_ref, hp_ref, rt_ref, cnt_ref, run_scr):
    i = pl.program_id(0)

    @pl.when(i == 0)
    def _():
        run_scr[...] = jnp.zeros_like(run_scr)

    h = _rms_modulate(x_ref[...], g_ref[...], mod_ref[3:4, :], mod_ref[4:5, :])
    half = h.shape[1] // 2
    hp_ref[...] = _pack_bf16_pair(h[:, :half], h[:, half:])
    logits = _dot3(h, w_ref[...]) + b_ref[...]
    tm = logits.shape[0]
    lane =lax.broadcasted_iota(I32, logits.shape, 1).astype(F32)
    ng = float(MOE_GROUPS)
    is_g = lane < ng
    gl = jnp.where(is_g, logits, NEG_BIG)
    gmax = jnp.max(gl, axis=-1, keepdims=True)
    grp = jnp.min(jnp.where(gl == gmax, lane, float(LANES)), axis=-1, keepdims=True)
    p_group = 1.0 / jnp.sum(jnp.where(is_g, jnp.exp(gl - gmax), 0.0), axis=-1, keepdims=True)
    lo = ng + float(MOE_PER_GROUP) * grp
    el = jnp.where((lane >= lo) & (lane < lo + float(MOE_PER_GROUP)), logits, NEG_BIG)
    v1 = jnp.max(el, axis=-1, keepdims=True)
    i1 = jnp.min(jnp.where(el == v1, lane, float(LANES)), axis=-1, keepdims=True)
    el2 = jnp.where(lane == i1, NEG_BIG, el)
    v2 = jnp.max(el2, axis=-1, keepdims=True)
    i2 = jnp.min(jnp.where(el2 == v2, lane, float(LANES)), axis=-1, keepdims=True)
    ex = jnp.exp(v2 - v1)
    w1 = p_group / (1.0 + ex)
    w2 = p_group * (ex / (1.0 + ex))
    oh1 = (lane == i1).astype(F32)
    oh2 = (lane == i2).astype(F32)
    oh = oh1 + oh2
    r_i = lax.broadcasted_iota(I32, (tm, tm), 0)
    c_i = lax.broadcasted_iota(I32, (tm, tm), 1)
    before = (c_i < r_i).astype(BF16)
    prior = jnp.dot(before, oh.astype(BF16), preferred_element_type=F32) + run_scr[...]
    rank1 = jnp.sum(oh1 * prior, axis=-1, keepdims=True)
    rank2 = jnp.sum(oh2 * prior, axis=-1, keepdims=True)
    run_scr[...] += jnp.sum(oh, axis=0, keepdims=True)
    cnt_ref[...] = run_scr[...]
    rt_ref[...] = jnp.where(lane == 0.0, i1 - ng,
                  jnp.where(lane == 1.0, i2 - ng,
                  jnp.where(lane == 2.0, w1,
                  jnp.where(lane == 3.0, w2,
                  jnp.where(lane == 4.0, rank1,
                  jnp.where(lane == 5.0, rank2, 0.0))))))


def moe_router(x, g, mod, w_r, b_r, seq, tm=512):
    n, d = x.shape
    tiles_per_seq = seq // tm
    return pl.pallas_call(
        _router_kernel,
        grid=(n // tm,),
        in_specs=[pl.BlockSpec((tm, d), lambda i: (i, 0)),
                  pl.BlockSpec((1, d), lambda i: (0, 0)),
                  pl.BlockSpec((None, 6, d), lambda i: (i // tiles_per_seq, 0, 0)),
                  pl.BlockSpec((d, LANES), lambda i: (0, 0)),
                  pl.BlockSpec((1, LANES), lambda i: (0, 0))],
        out_specs=[pl.BlockSpec((tm, d // 2), lambda i: (i, 0)),
                   pl.BlockSpec((tm, LANES), lambda i: (i, 0)),
                   pl.BlockSpec((1, LANES), lambda i: (0, 0))],
        out_shape=[jax.ShapeDtypeStruct((n, d // 2), U32),
                   jax.ShapeDtypeStruct((n, LANES), F32),
                   jax.ShapeDtypeStruct((1, LANES), F32)],
        scratch_shapes=[pltpu.VMEM((1, LANES), F32)],
        compiler_params=_cparams(("arbitrary",)),
        name="moe_router",
    )(x, g.reshape(1, d), mod, w_r, b_r)


def _dest_kernel(rt_ref, ps_ref, o_ref):
    rt = rt_ref[...]
    lane = lax.broadcasted_iota(I32, rt.shape, 1).astype(F32)
    ng = float(MOE_GROUPS)
    ps = ps_ref[...]
    d1 = jnp.sum(jnp.where(lane == rt[:, 0:1] + ng, ps, 0.0), axis=-1, keepdims=True) + rt[:, 4:5]
    d2 = jnp.sum(jnp.where(lane == rt[:, 1:2] + ng, ps, 0.0), axis=-1, keepdims=True) + rt[:, 5:6]
    o_ref[...] = jnp.where(lane == 0.0, d1, jnp.where(lane == 1.0, d2, 0.0)).astype(I32)


def moe_dest(route, pad_start_lanes, tm=2048):
    n = route.shape[0]
    return pl.pallas_call(
        _dest_kernel,
        grid=(n // tm,),
        in_specs=[pl.BlockSpec((tm, LANES), lambda i: (i, 0)),
                  pl.BlockSpec((1, LANES), lambda i: (0, 0))],
        out_specs=pl.BlockSpec((tm, LANES), lambda i: (i, 0)),
        out_shape=jax.ShapeDtypeStruct((n, LANES), I32),
        compiler_params=_cparams(("arbitrary",)),
        name="moe_dest",
    )(route, pad_start_lanes)


DMA_UNROLL = 8


def _wait_rows(src_rows, dst_rows, sem, copies):
    for _ in range(copies):
        pltpu.make_async_copy(src_rows, dst_rows, sem).wait()


def _dispatch_kernel(dest_ref, hp_ref, xin_in_ref, xin_ref, sems, *, tt, ntiles):
    del xin_in_ref, ntiles

    def start(r, c):
        for kk in range(2):
            pltpu.make_async_copy(hp_ref.at[pl.ds(r, 1)],
                                  xin_ref.at[pl.ds(dest_ref[0, 0, 2 * r + kk], 1)], sems.at[0]).start()
        return c

    lax.fori_loop(0, tt, start, 0, unroll=DMA_UNROLL)
    _wait_rows(hp_ref, xin_ref.at[pl.ds(0, tt)], sems.at[0], 2)


def moe_dispatch(hpack, dest, rows, tt=256):
    n, w = hpack.shape
    return pl.pallas_call(
        functools.partial(_dispatch_kernel, tt=tt, ntiles=n // tt),
        grid=(n // tt,),
        in_specs=[pl.BlockSpec((1, 1, 2 * tt), lambda i: (i, 0, 0), memory_space=pltpu.SMEM),
                  pl.BlockSpec((tt, w), lambda i: (i, 0)),
                  pl.BlockSpec(memory_space=pl.ANY)],
        out_specs=pl.BlockSpec(memory_space=pl.ANY),
        out_shape=jax.ShapeDtypeStruct((rows, w), U32),
        scratch_shapes=[pltpu.SemaphoreType.DMA((1,))],
        input_output_aliases={2: 0},
        compiler_params=_cparams(("arbitrary",)),
        name="moe_dispatch",
    )(dest.reshape(n // tt, 1, 2 * tt), hpack, jnp.zeros((rows, w), U32))


def _expert_kernel(be_ref, first_ref, nxt_ref, nu_ref, x_ref, wgu_hbm, wd_hbm, y_ref,
                   wgu_bf, wd_bf, stg_gu, stg_d, xs_scr, acc_scr, sems, *, layer, th, cr):
    i = pl.program_id(0)

    def fetch(e):
        return (pltpu.make_async_copy(wgu_hbm.at[layer, e], stg_gu, sems.at[0]),
                pltpu.make_async_copy(wd_hbm.at[layer, e], stg_d, sems.at[1]))

    @pl.when(i < nu_ref[0])
    def _():
        @pl.when(first_ref[i] == 1)
        def _():
            @pl.when(i == 0)
            def _():
                for cp in fetch(be_ref[i]):
                    cp.start()

            for cp in fetch(be_ref[i]):
                cp.wait()

            def cast_rows(src, dst):
                def body(r, c):
                    rs = pl.multiple_of(r * cr, cr)
                    dst[pl.ds(rs, cr), :] = src[pl.ds(rs, cr), :].astype(BF16)
                    return c
                lax.fori_loop(0, src.shape[0] // cr, body, 0)

            cast_rows(stg_gu, wgu_bf)
            cast_rows(stg_d, wd_bf)

            @pl.when(nxt_ref[i] >= 0)
            def _():
                for cp in fetch(nxt_ref[i]):
                    cp.start()

        half = x_ref.shape[1]
        hid = wd_bf.shape[0]
        lo, hi = _unpack_bf16_pair(x_ref[...])
        xs_scr[:, :half] = lo.astype(BF16)
        xs_scr[:, half:] = hi.astype(BF16)
        xs = xs_scr[...]
        for c in range(hid // th):
            gt = jnp.dot(xs, wgu_bf[:, c * th:(c + 1) * th], preferred_element_type=F32)
            up = jnp.dot(xs, wgu_bf[:, hid + c * th:hid + (c + 1) * th], preferred_element_type=F32)
            act = (gt * jax.nn.sigmoid(gt) * up).astype(BF16)
            part = jnp.dot(act, wd_bf[c * th:(c + 1) * th, :], preferred_element_type=F32)
            if c == 0:
                acc_scr[...] = part
            else:
                acc_scr[...] += part
        y = acc_scr[...]
        y_ref[...] = _pack_bf16_pair(y[:, :half], y[:, half:])


def moe_experts(xin, sched, wgu_all, wd_all, layer, tm, th=256, cr=256):
    rows, half = xin.shape
    _, _, d, hid2 = wgu_all.shape
    hid = hid2 // 2
    nb = rows // tm
    block_e, first, nxt, n_used = sched

    def blk(i, be, fi, nx, nu):
        return (jnp.minimum(i, nu[0] - 1), 0)

    grid_spec = pltpu.PrefetchScalarGridSpec(
        num_scalar_prefetch=4,
        grid=(nb,),
        in_specs=[pl.BlockSpec((tm, half), blk),
                  pl.BlockSpec(memory_space=pl.ANY),
                  pl.BlockSpec(memory_space=pl.ANY)],
        out_specs=pl.BlockSpec((tm, half), blk),
        scratch_shapes=[pltpu.VMEM((d, hid2), BF16), pltpu.VMEM((hid, d), BF16),
                        pltpu.VMEM((d, hid2), F32), pltpu.VMEM((hid, d), F32),
                        pltpu.VMEM((tm, d), BF16), pltpu.VMEM((tm, d), F32),
                        pltpu.SemaphoreType.DMA((2,))],
    )
    return pl.pallas_call(
        functools.partial(_expert_kernel, layer=layer, th=th, cr=cr),
        grid_spec=grid_spec,
        out_shape=jax.ShapeDtypeStruct((rows, half), U32),
        input_output_aliases={4: 0},
        compiler_params=_cparams(("arbitrary",)),
        name="moe_experts",
    )(block_e, first, nxt, n_used, xin, wgu_all, wd_all)


def _combine_kernel(dcur_ref, dnxt_ref, x_ref, rt_ref, mod_ref, fg_ref, y_ref, o_ref, ya, yb, sems,
                    *, tt, ntiles, final_norm):
    i = pl.program_id(0)
    slot = i % 2

    def issue(dref, sl):
        def body(r, c):
            pltpu.make_async_copy(y_ref.at[pl.ds(dref[0, 0, 2 * r], 1)], ya.at[sl, pl.ds(r, 1)], sems.at[sl]).start()
            pltpu.make_async_copy(y_ref.at[pl.ds(dref[0, 0, 2 * r + 1], 1)], yb.at[sl, pl.ds(r, 1)], sems.at[sl]).start()
            return c
        lax.fori_loop(0, tt, body, 0, unroll=DMA_UNROLL)

    @pl.when(i == 0)
    def _():
        issue(dcur_ref, 0)

    @pl.when(i + 1 < ntiles)
    def _():
        issue(dnxt_ref, 1 - slot)

    _wait_rows(y_ref.at[pl.ds(0, tt)], ya.at[slot], sems.at[slot], 2)
    half = ya.shape[2]
    w1 = rt_ref[:, 2:3]
    w2 = rt_ref[:, 3:4]
    a_lo, a_hi = _unpack_bf16_pair(ya[slot])
    b_lo, b_hi = _unpack_bf16_pair(yb[slot])
    gate = mod_ref[5:6, :]
    out_lo = x_ref[:, :half] + gate[:, :half] * (a_lo * w1 + b_lo * w2)
    out_hi = x_ref[:, half:] + gate[:, half:] * (a_hi * w1 + b_hi * w2)
    if final_norm:
        ms = (jnp.sum(out_lo * out_lo, axis=-1, keepdims=True)
              + jnp.sum(out_hi * out_hi, axis=-1, keepdims=True)) / (2 * half)
        r = lax.rsqrt(ms + NORM_EPS)
        out_lo = out_lo * r * fg_ref[:, :half]
        out_hi = out_hi * r * fg_ref[:, half:]
    o_ref[:, :half] = out_lo
    o_ref[:, half:] = out_hi


def moe_combine(x, y, dest, route, mod, final_g, seq, final_norm, tt=256):
    n, d = x.shape
    tiles_per_seq = seq // tt
    ntiles = n // tt
    dest3 = dest.reshape(ntiles, 1, 2 * tt)
    return pl.pallas_call(
        functools.partial(_combine_kernel, tt=tt, ntiles=ntiles, final_norm=final_norm),
        grid=(ntiles,),
        in_specs=[pl.BlockSpec((1, 1, 2 * tt), lambda i: (i, 0, 0), memory_space=pltpu.SMEM),
                  pl.BlockSpec((1, 1, 2 * tt), lambda i: (jnp.minimum(i + 1, ntiles - 1), 0, 0),
                               memory_space=pltpu.SMEM),
                  pl.BlockSpec((tt, d), lambda i: (i, 0)),
                  pl.BlockSpec((tt, LANES), lambda i: (i, 0)),
                  pl.BlockSpec((None, 6, d), lambda i: (i // tiles_per_seq, 0, 0)),
                  pl.BlockSpec((1, d), lambda i: (0, 0)),
                  pl.BlockSpec(memory_space=pl.ANY)],
        out_specs=pl.BlockSpec((tt, d), lambda i: (i, 0)),
        out_shape=jax.ShapeDtypeStruct((n, d), F32),
        scratch_shapes=[pltpu.VMEM((2, tt, d // 2), U32), pltpu.VMEM((2, tt, d // 2), U32),
                        pltpu.SemaphoreType.DMA((2,))],
        compiler_params=_cparams(("arbitrary",)),
        name="moe_combine",
    )(dest3, dest3, x, route, mod, final_g.reshape(1, d), y)


def _expert_schedule(counts, tm, nb):
    ne = counts.shape[0]
    padded = ((counts + tm - 1) // tm) * tm
    pad_end = jnp.cumsum(padded)
    pad_start = pad_end - padded
    n_used = (pad_end[-1] // tm).astype(I32)
    blk0 = jnp.arange(nb, dtype=I32) * tm
    block_e = jnp.sum((pad_end[None, :] <= blk0[:, None]).astype(I32), axis=1)
    block_e = jnp.minimum(block_e, ne - 1)
    first = (blk0 == pad_start[block_e]).astype(I32)
    ids = jnp.where(counts > 0, jnp.arange(ne, dtype=I32), ne)
    later = jnp.concatenate([lax.cummin(ids[::-1])[::-1][1:], jnp.full((1,), ne, I32)])
    nxt = jnp.where(later < ne, later, -1).astype(I32)[block_e]
    return pad_start, (block_e, first, nxt, n_used.reshape(1))


def hier_moe_layer(x, g, mod, w_group, b_group, w_expert, b_expert, wgu_all, wd_all, layer, final_g, seq,
                   final_norm, tm=256):
    n, d = x.shape
    ng, ne = w_group.shape[1], w_expert.shape[1]
    w_r = jnp.zeros((d, LANES), F32).at[:, :ng].set(w_group).at[:, ng:ng + ne].set(w_expert)
    b_r = jnp.zeros((1, LANES), F32).at[0, :ng].set(b_group).at[0, ng:ng + ne].set(b_expert)
    hpack, route, cnt = moe_router(x, g, mod, w_r, b_r, seq)
    counts = cnt[0, ng:ng + ne].astype(I32)
    rows = 2 * n + ne * tm
    pad_start, sched = _expert_schedule(counts, tm, rows // tm)
    ps_lanes = jnp.zeros((1, LANES), F32).at[0, ng:ng + ne].set(pad_start.astype(F32))
    dest = moe_dest(route, ps_lanes, tm=min(2048, n))[:, :2].reshape(-1)
    xin = moe_dispatch(hpack, dest, rows)
    y = moe_experts(xin, sched, wgu_all, wd_all, layer, tm)
    return moe_combine(x, y, dest, route, mod, final_g, seq, final_norm)


def _rope_tables(positions):
    half = ROPE_DIM // 2
    inv_freq = ROPE_THETA ** (-jnp.arange(half, dtype=F32) * 2.0 / ROPE_DIM)
    ang = positions.astype(F32).reshape(-1, 1) * inv_freq
    cos, sin = jnp.cos(ang), jnp.sin(ang)
    n = ang.shape[0]
    pad = LANES - ROPE_DIM
    c = jnp.concatenate([cos, cos, jnp.ones((n, pad), F32)], axis=1)
    s_lo = jnp.concatenate([-sin, jnp.zeros((n, LANES - half), F32)], axis=1)
    s_hi = jnp.concatenate([jnp.zeros((n, half), F32), sin, jnp.zeros((n, pad), F32)], axis=1)
    return c, s_lo, s_hi


def kernel(x, c, positions, ada_w, ada_b, norm_mix_g, norm_ffn_g, final_norm_g, attn_w_in, attn_w_out, attn_lambda_q1, attn_lambda_k1, attn_lambda_q2, attn_lambda_k2, attn_head_norm_g, mlstm_w_in, mlstm_conv_w, mlstm_conv_b, mlstm_gate_b, mlstm_head_norm_g, mlstm_w_out, moe_w_group, moe_b_group, moe_w_expert, moe_b_expert, moe_w_gu, moe_w_down):
    bsz, seq, d = x.shape
    n = bsz * seq
    depth = ada_w.shape[0]
    mod = adaln(c, ada_w, ada_b)
    xf = x.reshape(n, d)
    for i in range(depth):
        jm = i // 2
        if i % 2 == 0:
            qk_cols = 2 * DA_HEADS * 2 * DA_HEAD_DIM
            tabs = _rope_tables(positions)
            qkv = norm_matmul(xf, norm_mix_g[i], mod[i], attn_w_in[jm].astype(BF16), seq,
                              rope=(*tabs, qk_cols, qk_cols // 2))
            lam_init = 0.8 - 0.6 * math.exp(-0.3 * i)
            mixed = diff_attention(qkv.reshape(bsz, seq, -1), attn_lambda_q1[jm], attn_lambda_k1[jm],
                                   attn_lambda_q2[jm], attn_lambda_k2[jm], attn_head_norm_g[jm], lam_init)
            w_out = attn_w_out[jm]
        else:
            qk_cols = 2 * ML_HEADS * ML_QK_DIM
            main_cols = qk_cols + 2 * ML_HEADS * ML_V_DIM
            w_in = mlstm_w_in[jm]
            ngate = 2 * ML_HEADS
            w_gate = jnp.zeros((d, LANES), F32).at[:, :ngate].set(w_in[:, main_cols:])
            b_gate = jnp.zeros((1, LANES), F32).at[0, :ngate].set(mlstm_gate_b[jm])
            proj, gates = norm_matmul(xf, norm_mix_g[i], mod[i], w_in[:, :main_cols].astype(BF16), seq,
                                      extra=(w_gate, b_gate))
            proj = proj.reshape(bsz, seq, main_cols)
            qk = conv_silu(proj, mlstm_conv_w[jm], mlstm_conv_b[jm])
            gates_c = gates.reshape(bsz, seq, LANES)
            gates_r = jnp.swapaxes(gates_c[:, :, :ngate], 1, 2)
            mixed = mlstm(qk, proj, gates_c, gates_r, mlstm_head_norm_g[jm])
            w_out = mlstm_w_out[jm]
        xf = matmul_res(mixed.reshape(n, -1), w_out.astype(BF16), xf, mod[i], seq, gate_row=2)
        xf = hier_moe_layer(xf, norm_ffn_g[i], mod[i], moe_w_group[i], moe_b_group[i], moe_w_expert[i],
                            moe_b_expert[i], moe_w_gu, moe_w_down, i, final_norm_g, seq,
                            final_norm=(i == depth - 1))
    return xf.reshape(bsz, seq, d)
```

```python
import functools
import math

import jax
import jax.numpy as jnp
from jax import lax
from jax.experimental import pallas as pl
from jax.experimental.pallas import tpu as pltpu

F32 = jnp.float32
BF16 = jnp.bfloat16
U32 = jnp.uint32
I32 = jnp.int32

NORM_EPS = 1e-6
ROPE_THETA = 500000.0
ATTN_CHUNK = 64
DA_HEADS = 8
DA_HEAD_DIM = 128
ROPE_DIM = 32
ML_HEADS = 8
ML_QK_DIM = 128
ML_V_DIM = 256
ML_CONV = 4
MOE_GROUPS = 4
MOE_PER_GROUP = 8
MOE_EXPERTS = 32
LANES = 128
NEG_BIG = -1e30
LOG2_E = math.log2(math.e)

VMEM_LIMIT = 56 * 1024 * 1024


def _cparams(sem):
    return pltpu.CompilerParams(dimension_semantics=sem, vmem_limit_bytes=VMEM_LIMIT)


def _split_hi_lo(a):
    hi = a.astype(BF16)
    lo = (a - hi.astype(F32)).astype(BF16)
    return hi, lo


def _dot3(a, w):
    ah, al = _split_hi_lo(a)
    wh, wl = _split_hi_lo(w)
    d = functools.partial(jnp.dot, preferred_element_type=F32)
    return d(ah, wh) + (d(ah, wl) + d(al, wh))


def _pack_bf16_pair(lo_f32, hi_f32):
    lo_bits = lax.bitcast_convert_type(lo_f32.astype(BF16).astype(F32), U32)
    hi_bits = lax.bitcast_convert_type(hi_f32.astype(BF16).astype(F32), U32)
    return hi_bits | (lo_bits >> 16)


def _unpack_bf16_pair(word):
    lo = lax.bitcast_convert_type(word << 16, F32)
    hi = lax.bitcast_convert_type(word & jnp.uint32(0xFFFF0000), F32)
    return lo, hi


ROW_TILE = 8


def _store_token_rows(ref, words):
    t = words.shape[0]
    for sub in range(ROW_TILE):
        ref[pl.ds(sub, t, stride=ROW_TILE), :] = words[:, sub * LANES:(sub + 1) * LANES]


def _load_token_rows(ref, sub, t):
    return ref[pl.ds(sub, t, stride=ROW_TILE), :]


def _rms_modulate(x, g, shift, scale):
    ms = jnp.mean(x * x, axis=-1, keepdims=True)
    y = x * lax.rsqrt(ms + NORM_EPS) * g
    return y * (1.0 + scale) + shift


def _adaln_kernel(c_ref, w_ref, b_ref, o_ref):
    c = c_ref[...]
    cond = c * jax.nn.sigmoid(c)
    acc = jnp.dot(cond.astype(BF16), w_ref[...].astype(BF16), preferred_element_type=F32)
    o_ref[...] = acc + b_ref[...]


def adaln(c, ada_w, ada_b, tn=1024):
    depth, d, n6 = ada_w.shape
    bsz = c.shape[0]
    rows = 8
    cp = jnp.zeros((rows, d), F32).at[:bsz].set(c)
    out = pl.pallas_call(
        _adaln_kernel,
        grid=(depth, n6 // tn),
        in_specs=[pl.BlockSpec((rows, d), lambda l, j: (0, 0)),
                  pl.BlockSpec((None, d, tn), lambda l, j: (l, 0, j)),
                  pl.BlockSpec((None, 1, tn), lambda l, j: (l, 0, j))],
        out_specs=pl.BlockSpec((None, rows, tn), lambda l, j: (l, 0, j)),
        out_shape=jax.ShapeDtypeStruct((depth, rows, n6), F32),
        compiler_params=_cparams(("arbitrary", "arbitrary")),
        name="adaln",
    )(cp, ada_w, ada_b.reshape(depth, 1, n6))
    return out[:, :bsz].reshape(depth, bsz, 6, d)


ROPE_ROWS = 64

def _norm_matmul_kernel(*refs, rope_tiles, q_tiles, has_extra, tn):
    if has_extra:
        x_ref, g_ref, mod_ref, w_ref, c_ref, s_ref, we_ref, be_ref, o_ref, oe_ref, h_scr, acc_scr = refs
    else:
        x_ref, g_ref, mod_ref, w_ref, c_ref, s_ref, o_ref, h_scr, acc_scr = refs
    j = pl.program_id(1)

    @pl.when(j == 0)
    def _():
        h = _rms_modulate(x_ref[...], g_ref[...], mod_ref[0:1, :], mod_ref[1:2, :])
        h_scr[...] = h.astype(BF16)
        if has_extra:
            oe_ref[...] = _dot3(h, we_ref[...]) + be_ref[...]

    acc = jnp.dot(h_scr[...], w_ref[...], preferred_element_type=F32)
    if rope_tiles == 0:
        o_ref[...] = acc.astype(o_ref.dtype)
    else:
        @pl.when(j < rope_tiles)
        def _():
            acc_scr[...] = acc
            scale = jnp.where(j < q_tiles, DA_HEAD_DIM ** -0.5 * LOG2_E, 1.0).astype(F32)

            def rotate_rows(r, c):
                rows = pl.ds(pl.multiple_of(r * ROPE_ROWS, ROPE_ROWS), ROPE_ROWS)
                cs, sn = c_ref[rows, :] * scale, s_ref[rows, :] * scale
                for m in range(tn // LANES):
                    t = acc_scr[rows, m * LANES:(m + 1) * LANES]
                    rot = t * cs + pltpu.roll(t, LANES // 2, 1) * sn
                    o_ref[rows, m * LANES:(m + 1) * LANES] = rot.astype(o_ref.dtype)
                return c

            lax.fori_loop(0, acc_scr.shape[0] // ROPE_ROWS, rotate_rows, 0, unroll=2)

        @pl.when(j >= rope_tiles)
        def _():
            o_ref[...] = acc.astype(o_ref.dtype)


def norm_matmul(x, g, mod, w, seq, rope=None, extra=None, tm=1024, tn=512):
    n, k = x.shape
    m = w.shape[1]
    tiles_per_seq = seq // tm
    has_extra = extra is not None
    if rope is None:
        dummy = jnp.zeros((8, LANES), F32)
        tabs = (dummy, dummy)
        tab_spec = pl.BlockSpec((8, LANES), lambda i, j: (0, 0))
        rope_tiles = q_tiles = 0
    else:
        tabs = rope[:2]
        tab_spec = pl.BlockSpec((tm, LANES), lambda i, j: (i, 0))
        rope_tiles, q_tiles = rope[2] // tn, rope[3] // tn
    in_specs = [pl.BlockSpec((tm, k), lambda i, j: (i, 0)),
                pl.BlockSpec((1, k), lambda i, j: (0, 0)),
                pl.BlockSpec((None, 6, k), lambda i, j: (i // tiles_per_seq, 0, 0)),
                pl.BlockSpec((k, tn), lambda i, j: (0, j)),
                tab_spec, tab_spec]
    args = [x, g.reshape(1, k), mod, w, *tabs]
    out_specs = pl.BlockSpec((tm, tn), lambda i, j: (i, j))
    out_shape = jax.ShapeDtypeStruct((n, m), BF16)
    if has_extra:
        we, be = extra
        in_specs += [pl.BlockSpec((k, LANES), lambda i, j: (0, 0)),
                     pl.BlockSpec((1, LANES), lambda i, j: (0, 0))]
        args += [we, be]
        out_specs = [out_specs, pl.BlockSpec((tm, LANES), lambda i, j: (i, 0))]
        out_shape = [out_shape, jax.ShapeDtypeStruct((n, LANES), F32)]
    return pl.pallas_call(
        functools.partial(_norm_matmul_kernel, rope_tiles=rope_tiles, q_tiles=q_tiles,
                          has_extra=has_extra, tn=tn),
        grid=(n // tm, m // tn),
        in_specs=in_specs,
        out_specs=out_specs,
        out_shape=out_shape,
        scratch_shapes=[pltpu.VMEM((tm, k), BF16),
                        pltpu.VMEM((tm, tn) if rope_tiles else (8, LANES), F32)],
        compiler_params=_cparams(("arbitrary", "arbitrary")),
        name="norm_matmul",
    )(*args)


def _matmul_res_kernel(a_ref, w_ref, res_ref, mod_ref, o_ref, *, gate_row):
    acc = jnp.dot(a_ref[...], w_ref[...], preferred_element_type=F32)
    o_ref[...] = res_ref[...] + mod_ref[gate_row:gate_row + 1, :] * acc


def matmul_res(a, w, res, mod, seq, gate_row, tm=512, tn=2048):
    n, k = a.shape
    m = w.shape[1]
    tiles_per_seq = seq // tm
    return pl.pallas_call(
        functools.partial(_matmul_res_kernel, gate_row=gate_row),
        grid=(n // tm, m // tn),
        in_specs=[pl.BlockSpec((tm, k), lambda i, j: (i, 0)),
                  pl.BlockSpec((k, tn), lambda i, j: (0, j)),
                  pl.BlockSpec((tm, tn), lambda i, j: (i, j)),
                  pl.BlockSpec((None, 6, tn), lambda i, j: (i // tiles_per_seq, 0, j))],
        out_specs=pl.BlockSpec((tm, tn), lambda i, j: (i, j)),
        out_shape=jax.ShapeDtypeStruct((n, m), F32),
        compiler_params=_cparams(("arbitrary", "arbitrary")),
        name="matmul_res",
    )(a, w, res, mod)


ATTN_GROUPS = (4, 2)


def _grouped_loop(n, body, groups):
    done = 0
    for group in groups:
        trips = (n - done) // group

        def grouped(gi, c, group=group, done=done):
            for t in range(group):
                c = body(done + gi * group + t, c)
            return c

        lax.fori_loop(0, trips, grouped, 0)
        done = done + trips * group
    lax.fori_loop(done, n, body, 0)


def _attn_kernel(q_ref, k_ref, v_ref, lq1_ref, lk1_ref, lq2_ref, lk2_ref, g_ref, o_ref,
                 s_scr, m_scr, l_scr, acc_scr, *, lam_init, tq, tk, seq):
    d = DA_HEAD_DIM
    nlane = tk // LANES
    ndiag = tq // tk
    lam = (jnp.exp(jnp.sum(lq1_ref[...] * lk1_ref[...], axis=-1, keepdims=True))
           - jnp.exp(jnp.sum(lq2_ref[...] * lk2_ref[...], axis=-1, keepdims=True)) + lam_init)
    shift = ATTN_CHUNK.bit_length() - 1
    row_chunk = jnp.right_shift(lax.broadcasted_iota(I32, (tq, tk), 0), shift)
    col_chunk = jnp.right_shift(lax.broadcasted_iota(I32, (tq, tk), 1), shift)
    nt = (((1,), (1,)), ((), ()))

    def lane_fold(a, op):
        part = a[:, 0:LANES]
        for cb in range(1, nlane):
            part = op(part, a[:, cb * LANES:(cb + 1) * LANES])
        return part

    def q_body(qi, _):
        qs = pl.multiple_of(qi * tq, tq)
        qm = (q_ref[pl.ds(qs, tq), 0:d], q_ref[pl.ds(qs, tq), d:2 * d])
        m_scr[...] = jnp.full_like(m_scr, NEG_BIG)

        def score(j, mask):
            ks = pl.multiple_of(j * tk, tk)
            for mp in range(2):
                k = k_ref[pl.ds(ks, tk), mp * d:(mp + 1) * d]
                s = lax.dot_general(qm[mp], k, nt, preferred_element_type=F32)
                if mask is not None:
                    s = jnp.where(mask, s, NEG_BIG)
                s_scr[mp, j] = s
                m_scr[mp] = jnp.maximum(m_scr[mp], lane_fold(s, jnp.maximum))

        def score_body(j, c):
            score(j, None)
            return c

        nfull = qi * ndiag
        _grouped_loop(nfull, score_body, ATTN_GROUPS)
        for t in range(ndiag):
            score(nfull + t, col_chunk + t * (tk // ATTN_CHUNK) <= row_chunk)
        m = [jnp.max(m_scr[mp], axis=-1, keepdims=True) for mp in range(2)]
        l_scr[...] = jnp.zeros_like(l_scr)
        acc_scr[...] = jnp.zeros_like(acc_scr)

        def pv_body(j, c):
            ks = pl.multiple_of(j * tk, tk)
            v = v_ref[pl.ds(ks, tk), :]
            for mp in range(2):
                p = jnp.exp2(s_scr[mp, j] - m[mp])
                l_scr[mp] += lane_fold(p, jnp.add)
                acc_scr[mp] += jnp.dot(p.astype(BF16), v, preferred_element_type=F32)
            return c

        _grouped_loop(nfull + ndiag, pv_body, ATTN_GROUPS)
        l = [jnp.sum(l_scr[mp], axis=-1, keepdims=True) for mp in range(2)]
        o = acc_scr[0] / l[0] - lam * (acc_scr[1] / l[1])
        ms = jnp.mean(o * o, axis=-1, keepdims=True)
        o = o * lax.rsqrt(ms + NORM_EPS) * g_ref[...] * (1.0 - lam_init)
        o_ref[pl.ds(qs, tq), :] = o.astype(o_ref.dtype)
        return 0

    lax.fori_loop(0, seq // tq, q_body, 0)


def diff_attention(qkv, lq1, lk1, lq2, lk2, head_g, lam_init, tq=512, tk=256):
    bsz, seq, _ = qkv.shape
    h, dv = DA_HEADS, 2 * DA_HEAD_DIM
    vec = lambda a: a.reshape(1, -1).astype(F32)
    small = lambda n: pl.BlockSpec((1, n), lambda b, hh: (0, 0))
    return pl.pallas_call(
        functools.partial(_attn_kernel, lam_init=lam_init, tq=tq, tk=tk, seq=seq),
        grid=(bsz, h),
        in_specs=[pl.BlockSpec((None, seq, dv), lambda b, hh: (b, 0, hh)),
                  pl.BlockSpec((None, seq, dv), lambda b, hh: (b, 0, h + hh)),
                  pl.BlockSpec((None, seq, dv), lambda b, hh: (b, 0, 2 * h + hh)),
                  small(DA_HEAD_DIM), small(DA_HEAD_DIM), small(DA_HEAD_DIM), small(DA_HEAD_DIM),
                  small(dv)],
        out_specs=pl.BlockSpec((None, seq, dv), lambda b, hh: (b, 0, hh)),
        out_shape=jax.ShapeDtypeStruct((bsz, seq, h * dv), BF16),
        scratch_shapes=[pltpu.VMEM((2, seq // tk, tq, tk), F32), pltpu.VMEM((2, tq, LANES), F32),
                        pltpu.VMEM((2, tq, LANES), F32), pltpu.VMEM((2, tq, dv), F32)],
        compiler_params=_cparams(("arbitrary", "arbitrary")),
        name="diff_attention",
    )(qkv, qkv, qkv, vec(lq1), vec(lk1), vec(lq2), vec(lk2), vec(head_g))


def _conv_silu_kernel(x_ref, w_ref, b_ref, o_ref, *, k_tile0):
    j = pl.program_id(1)
    x = x_ref[...].astype(F32)
    row = lax.broadcasted_iota(I32, x.shape, 0)
    y = x * w_ref[ML_CONV - 1:ML_CONV, :] + b_ref[...]
    for s in range(1, ML_CONV):
        shifted = jnp.where(row >= s, pltpu.roll(x, s, 0), 0.0)
        y = y + shifted * w_ref[ML_CONV - 1 - s:ML_CONV - s, :]
    y = y * jax.nn.sigmoid(y)
    scale = jnp.where(j >= k_tile0, ML_QK_DIM ** -0.5, 1.0).astype(F32)
    o_ref[...] = (y * scale).astype(o_ref.dtype)


def conv_silu(proj, conv_w, conv_b, tc=128):
    bsz, seq, _ = proj.shape
    cols = conv_w.shape[1]
    return pl.pallas_call(
        functools.partial(_conv_silu_kernel, k_tile0=(cols // 2) // tc),
        grid=(bsz, cols // tc),
        in_specs=[pl.BlockSpec((None, seq, tc), lambda b, j: (b, 0, j)),
                  pl.BlockSpec((ML_CONV, tc), lambda b, j: (0, j)),
                  pl.BlockSpec((1, tc), lambda b, j: (0, j))],
        out_specs=pl.BlockSpec((None, seq, tc), lambda b, j: (b, 0, j)),
        out_shape=jax.ShapeDtypeStruct((bsz, seq, cols), BF16),
        compiler_params=_cparams(("arbitrary", "arbitrary")),
        name="conv_silu",
    )(proj, conv_w, conv_b.reshape(1, cols))


def _mlstm_kernel(q_ref, k_ref, v_ref, op_ref, gc_ref, gr_ref, hg_ref, o_ref,
                  cx_scr, m_scr, *, chunk):
    c = pl.program_id(1)
    nh, dqk, dv = ML_HEADS, ML_QK_DIM, ML_V_DIM

    @pl.when(c == 0)
    def _():
        cx_scr[...] = jnp.zeros_like(cx_scr)
        m_scr[...] = jnp.zeros_like(m_scr)

    gc = gc_ref[...]
    gr = gr_ref[...]
    lf_c = jax.nn.log_sigmoid(gc)
    lf_r = jax.nn.log_sigmoid(gr)
    r_i = lax.broadcasted_iota(I32, (chunk, chunk), 0)
    c_i = lax.broadcasted_iota(I32, (chunk, chunk), 1)
    causal = c_i <= r_i
    tril = causal.astype(F32)
    triu = (r_i <= c_i).astype(F32)
    b_c = _dot3(tril, lf_c)
    b_r = _dot3(lf_r, triu)
    nt = (((1,), (1,)), ((), ()))
    tn_ = (((0,), (0,)), ((), ()))
    ones_l = jnp.ones((chunk, LANES), BF16)
    ones_v = jnp.ones((dv, LANES), BF16)
    wide = lambda a: jnp.concatenate([a] * (dv // LANES), axis=1)

    for h in range(nh):
        q = q_ref[:, h * dqk:(h + 1) * dqk]
        k = k_ref[:, h * dqk:(h + 1) * dqk]
        v_ext = jnp.concatenate([v_ref[:, h * dv:(h + 1) * dv], ones_l], axis=1)
        bc = b_c[:, nh + h:nh + h + 1]
        br = b_r[nh + h:nh + h + 1, :]
        ig_c = gc[:, h:h + 1]
        ig_r = gr[h:h + 1, :]
        m_prev = m_scr[h:h + 1, :]
        dmat = jnp.where(causal, bc - br + ig_r, NEG_BIG)
        inter = bc + m_prev
        m_t = jnp.maximum(inter, jnp.max(dmat, axis=-1, keepdims=True))
        w = jnp.exp(dmat - m_t)
        s = lax.dot_general(q, k, nt, preferred_element_type=F32) * w
        decay = jnp.exp(inter - m_t)
        cx = cx_scr[h]
        tot = (jnp.dot(s.astype(BF16), v_ext, preferred_element_type=F32)
               + decay * jnp.dot(q, cx.astype(BF16), preferred_element_type=F32))
        den = jnp.maximum(jnp.abs(tot[:, dv:]), jnp.exp(-m_t))
        hh = tot[:, :dv] / wide(den)
        b_last = bc[chunk - 1:chunk, :]
        g = b_last - bc + ig_c
        m_new = jnp.maximum(b_last + m_prev, jnp.max(g, axis=0, keepdims=True))
        carry_decay = jnp.exp(b_last + m_prev - m_new)
        wg = jnp.exp(g - m_new)
        wv = (wg * v_ext.astype(F32)).astype(BF16)
        cx_scr[h] = carry_decay * cx + lax.dot_general(k, wv, tn_, preferred_element_type=F32)
        m_scr[h:h + 1, :] = m_new
        ms = jnp.dot((hh * hh).astype(BF16), ones_v, preferred_element_type=F32) * (1.0 / dv)
        hn = hh * wide(lax.rsqrt(ms + NORM_EPS)) * hg_ref[:, h * dv:(h + 1) * dv]
        og = jax.nn.sigmoid(op_ref[:, h * dv:(h + 1) * dv].astype(F32))
        o_ref[:, h * dv:(h + 1) * dv] = (og * hn).astype(o_ref.dtype)


def mlstm(qk, proj, gates_c, gates_r, head_g, chunk=128):
    bsz, seq, _ = qk.shape
    nh = ML_HEADS
    qw, vw = nh * ML_QK_DIM, nh * ML_V_DIM
    v_blk = (2 * qw) // vw
    return pl.pallas_call(
        functools.partial(_mlstm_kernel, chunk=chunk),
        grid=(bsz, seq // chunk),
        in_specs=[pl.BlockSpec((None, chunk, qw), lambda b, c: (b, c, 0)),
                  pl.BlockSpec((None, chunk, qw), lambda b, c: (b, c, 1)),
                  pl.BlockSpec((None, chunk, vw), lambda b, c: (b, c, v_blk)),
                  pl.BlockSpec((None, chunk, vw), lambda b, c: (b, c, v_blk + 1)),
                  pl.BlockSpec((None, chunk, LANES), lambda b, c: (b, c, 0)),
                  pl.BlockSpec((None, 2 * nh, chunk), lambda b, c: (b, 0, c)),
                  pl.BlockSpec((1, vw), lambda b, c: (0, 0))],
        out_specs=pl.BlockSpec((None, chunk, vw), lambda b, c: (b, c, 0)),
        out_shape=jax.ShapeDtypeStruct((bsz, seq, vw), BF16),
        scratch_shapes=[pltpu.VMEM((nh, ML_QK_DIM, ML_V_DIM + LANES), F32),
                        pltpu.VMEM((nh, 1), F32)],
        compiler_params=_cparams(("arbitrary", "arbitrary")),
        name="mlstm",
    )(qk, qk, proj, proj, gates_c, gates_r, head_g.reshape(1, vw))


def _router_kernel(x_ref, g_ref, mod_ref, w_ref, b_ref, hp_ref, rt_ref, cnt_ref, run_scr):
    i = pl.program_id(0)

    @pl.when(i == 0)
    def _():
        run_scr[...] = jnp.zeros_like(run_scr)

    h = _rms_modulate(x_ref[...], g_ref[...], mod_ref[3:4, :], mod_ref[4:5, :])
    half = h.shape[1] // 2
    _store_token_rows(hp_ref, _pack_bf16_pair(h[:, :half], h[:, half:]))
    logits = _dot3(h, w_ref[...]) + b_ref[...]
    tm = logits.shape[0]
    lane =lax.broadcasted_iota(I32, logits.shape, 1).astype(F32)
    ng = float(MOE_GROUPS)
    is_g = lane < ng
    gl = jnp.where(is_g, logits, NEG_BIG)
    gmax = jnp.max(gl, axis=-1, keepdims=True)
    grp = jnp.min(jnp.where(gl == gmax, lane, float(LANES)), axis=-1, keepdims=True)
    p_group = 1.0 / jnp.sum(jnp.where(is_g, jnp.exp(gl - gmax), 0.0), axis=-1, keepdims=True)
    lo = ng + float(MOE_PER_GROUP) * grp
    el = jnp.where((lane >= lo) & (lane < lo + float(MOE_PER_GROUP)), logits, NEG_BIG)
    v1 = jnp.max(el, axis=-1, keepdims=True)
    i1 = jnp.min(jnp.where(el == v1, lane, float(LANES)), axis=-1, keepdims=True)
    el2 = jnp.where(lane == i1, NEG_BIG, el)
    v2 = jnp.max(el2, axis=-1, keepdims=True)
    i2 = jnp.min(jnp.where(el2 == v2, lane, float(LANES)), axis=-1, keepdims=True)
    ex = jnp.exp(v2 - v1)
    w1 = p_group / (1.0 + ex)
    w2 = p_group * (ex / (1.0 + ex))
    oh1 = (lane == i1).astype(F32)
    oh2 = (lane == i2).astype(F32)
    oh = oh1 + oh2
    r_i = lax.broadcasted_iota(I32, (tm, tm), 0)
    c_i = lax.broadcasted_iota(I32, (tm, tm), 1)
    before = (c_i < r_i).astype(BF16)
    prior = jnp.dot(before, oh.astype(BF16), preferred_element_type=F32) + run_scr[...]
    rank1 = jnp.sum(oh1 * prior, axis=-1, keepdims=True)
    rank2 = jnp.sum(oh2 * prior, axis=-1, keepdims=True)
    run_scr[...] += jnp.sum(oh, axis=0, keepdims=True)
    cnt_ref[...] = run_scr[...]
    rt_ref[...] = jnp.where(lane == 0.0, i1 - ng,
                  jnp.where(lane == 1.0, i2 - ng,
                  jnp.where(lane == 2.0, w1,
                  jnp.where(lane == 3.0, w2,
                  jnp.where(lane == 4.0, rank1,
                  jnp.where(lane == 5.0, rank2, 0.0))))))


def moe_router(x, g, mod, w_r, b_r, seq, tm=512):
    n, d = x.shape
    tiles_per_seq = seq // tm
    return pl.pallas_call(
        _router_kernel,
        grid=(n // tm,),
        in_specs=[pl.BlockSpec((tm, d), lambda i: (i, 0)),
                  pl.BlockSpec((1, d), lambda i: (0, 0)),
                  pl.BlockSpec((None, 6, d), lambda i: (i // tiles_per_seq, 0, 0)),
                  pl.BlockSpec((d, LANES), lambda i: (0, 0)),
                  pl.BlockSpec((1, LANES), lambda i: (0, 0))],
        out_specs=[pl.BlockSpec((tm * ROW_TILE, LANES), lambda i: (i, 0)),
                   pl.BlockSpec((tm, LANES), lambda i: (i, 0)),
                   pl.BlockSpec((1, LANES), lambda i: (0, 0))],
        out_shape=[jax.ShapeDtypeStruct((n * ROW_TILE, LANES), U32),
                   jax.ShapeDtypeStruct((n, LANES), F32),
                   jax.ShapeDtypeStruct((1, LANES), F32)],
        scratch_shapes=[pltpu.VMEM((1, LANES), F32)],
        compiler_params=_cparams(("arbitrary",)),
        name="moe_router",
    )(x, g.reshape(1, d), mod, w_r, b_r)


def _dest_kernel(rt_ref, ps_ref, o_ref):
    rt = rt_ref[...]
    lane = lax.broadcasted_iota(I32, rt.shape, 1).astype(F32)
    ng = float(MOE_GROUPS)
    ps = ps_ref[...]
    d1 = jnp.sum(jnp.where(lane == rt[:, 0:1] + ng, ps, 0.0), axis=-1, keepdims=True) + rt[:, 4:5]
    d2 = jnp.sum(jnp.where(lane == rt[:, 1:2] + ng, ps, 0.0), axis=-1, keepdims=True) + rt[:, 5:6]
    o_ref[...] = jnp.where(lane == 0.0, d1, jnp.where(lane == 1.0, d2, 0.0)).astype(I32)


def moe_dest(route, pad_start_lanes, tm=2048):
    n = route.shape[0]
    return pl.pallas_call(
        _dest_kernel,
        grid=(n // tm,),
        in_specs=[pl.BlockSpec((tm, LANES), lambda i: (i, 0)),
                  pl.BlockSpec((1, LANES), lambda i: (0, 0))],
        out_specs=pl.BlockSpec((tm, LANES), lambda i: (i, 0)),
        out_shape=jax.ShapeDtypeStruct((n, LANES), I32),
        compiler_params=_cparams(("arbitrary",)),
        name="moe_dest",
    )(route, pad_start_lanes)


DMA_UNROLL = 8


def _wait_rows(src_rows, dst_rows, sem, copies):
    for _ in range(copies):
        pltpu.make_async_copy(src_rows, dst_rows, sem).wait()


def _dispatch_kernel(zb_ref, dest_ref, hp_ref, xin_ref, zero_scr, sems, *, tt, tm, nb):
    blk_rows = tm * ROW_TILE

    @pl.when(pl.program_id(0) == 0)
    def _():
        zero_scr[...] = jnp.zeros_like(zero_scr)

        def zero_copy(blk):
            return pltpu.make_async_copy(zero_scr, xin_ref.at[pl.ds(pl.multiple_of(blk * blk_rows, blk_rows), blk_rows)],
                                         sems.at[1])

        def start(blk, c):
            @pl.when(zb_ref[blk] == 1)
            def _():
                zero_copy(blk).start()
            return c

        def wait(blk, c):
            @pl.when(zb_ref[blk] == 1)
            def _():
                zero_copy(blk).wait()
            return c

        lax.fori_loop(0, nb, start, 0)
        lax.fori_loop(0, nb, wait, 0)

    def start_rows(r, c):
        src = hp_ref.at[pl.ds(pl.multiple_of(r * ROW_TILE, ROW_TILE), ROW_TILE)]
        for kk in range(2):
            row = pl.multiple_of(dest_ref[0, 0, 2 * r + kk] * ROW_TILE, ROW_TILE)
            pltpu.make_async_copy(src, xin_ref.at[pl.ds(row, ROW_TILE)], sems.at[0]).start()
        return c

    lax.fori_loop(0, tt, start_rows, 0, unroll=DMA_UNROLL)
    _wait_rows(hp_ref, xin_ref.at[pl.ds(0, tt * ROW_TILE)], sems.at[0], 2)


def moe_dispatch(hpack, dest, zero_blocks, rows, tm, tt=256):
    n = hpack.shape[0] // ROW_TILE
    grid_spec = pltpu.PrefetchScalarGridSpec(
        num_scalar_prefetch=1,
        grid=(n // tt,),
        in_specs=[pl.BlockSpec((1, 1, 2 * tt), lambda i, zb: (i, 0, 0), memory_space=pltpu.SMEM),
                  pl.BlockSpec((tt * ROW_TILE, LANES), lambda i, zb: (i, 0))],
        out_specs=pl.BlockSpec(memory_space=pl.ANY),
        scratch_shapes=[pltpu.VMEM((tm * ROW_TILE, LANES), U32), pltpu.SemaphoreType.DMA((2,))],
    )
    return pl.pallas_call(
        functools.partial(_dispatch_kernel, tt=tt, tm=tm, nb=rows // tm),
        grid_spec=grid_spec,
        out_shape=jax.ShapeDtypeStruct((rows * ROW_TILE, LANES), U32),
        compiler_params=_cparams(("arbitrary",)),
        name="moe_dispatch",
    )(zero_blocks, dest.reshape(n // tt, 1, 2 * tt), hpack)


def _expert_kernel(be_ref, first_ref, nxt_ref, nu_ref, x_ref, wgu_hbm, wd_hbm, y_ref,
                   wgu_bf, wd_bf, stg_gu, stg_d, xs_scr, acc_scr, sems, *, layer, th, cr):
    i = pl.program_id(0)

    def fetch(e):
        return (pltpu.make_async_copy(wgu_hbm.at[layer, e], stg_gu, sems.at[0]),
                pltpu.make_async_copy(wd_hbm.at[layer, e], stg_d, sems.at[1]))

    @pl.when(i < nu_ref[0])
    def _():
        @pl.when(first_ref[i] == 1)
        def _():
            @pl.when(i == 0)
            def _():
                for cp in fetch(be_ref[i]):
                    cp.start()

            for cp in fetch(be_ref[i]):
                cp.wait()

            def cast_rows(src, dst):
                def body(r, c):
                    rs = pl.multiple_of(r * cr, cr)
                    dst[pl.ds(rs, cr), :] = src[pl.ds(rs, cr), :].astype(BF16)
                    return c
                lax.fori_loop(0, src.shape[0] // cr, body, 0)

            cast_rows(stg_gu, wgu_bf)
            cast_rows(stg_d, wd_bf)

            @pl.when(nxt_ref[i] >= 0)
            def _():
                for cp in fetch(nxt_ref[i]):
                    cp.start()

        tm = xs_scr.shape[0]
        half = xs_scr.shape[1] // 2
        hid = wd_bf.shape[0]
        for sub in range(ROW_TILE):
            lo, hi = _unpack_bf16_pair(_load_token_rows(x_ref, sub, tm))
            xs_scr[:, sub * LANES:(sub + 1) * LANES] = lo.astype(BF16)
            xs_scr[:, half + sub * LANES:half + (sub + 1) * LANES] = hi.astype(BF16)
        xs = xs_scr[...]
        for c in range(hid // th):
            gt = jnp.dot(xs, wgu_bf[:, c * th:(c + 1) * th], preferred_element_type=F32)
            up = jnp.dot(xs, wgu_bf[:, hid + c * th:hid + (c + 1) * th], preferred_element_type=F32)
            act = (gt * jax.nn.sigmoid(gt) * up).astype(BF16)
            part = jnp.dot(act, wd_bf[c * th:(c + 1) * th, :], preferred_element_type=F32)
            if c == 0:
                acc_scr[...] = part
            else:
                acc_scr[...] += part
        y = acc_scr[...]
        _store_token_rows(y_ref, _pack_bf16_pair(y[:, :half], y[:, half:]))


def moe_experts(xin, sched, wgu_all, wd_all, layer, tm, th=256, cr=256):
    _, _, d, hid2 = wgu_all.shape
    hid = hid2 // 2
    nb = xin.shape[0] // (tm * ROW_TILE)
    block_e, first, nxt, n_used = sched

    def blk(i, be, fi, nx, nu):
        return (jnp.minimum(i, nu[0] - 1), 0)

    grid_spec = pltpu.PrefetchScalarGridSpec(
        num_scalar_prefetch=4,
        grid=(nb,),
        in_specs=[pl.BlockSpec((tm * ROW_TILE, LANES), blk),
                  pl.BlockSpec(memory_space=pl.ANY),
                  pl.BlockSpec(memory_space=pl.ANY)],
        out_specs=pl.BlockSpec((tm * ROW_TILE, LANES), blk),
        scratch_shapes=[pltpu.VMEM((d, hid2), BF16), pltpu.VMEM((hid, d), BF16),
                        pltpu.VMEM((d, hid2), F32), pltpu.VMEM((hid, d), F32),
                        pltpu.VMEM((tm, d), BF16), pltpu.VMEM((tm, d), F32),
                        pltpu.SemaphoreType.DMA((2,))],
    )
    return pl.pallas_call(
        functools.partial(_expert_kernel, layer=layer, th=th, cr=cr),
        grid_spec=grid_spec,
        out_shape=jax.ShapeDtypeStruct(xin.shape, U32),
        input_output_aliases={4: 0},
        compiler_params=_cparams(("arbitrary",)),
        name="moe_experts",
    )(block_e, first, nxt, n_used, xin, wgu_all, wd_all)


def _combine_kernel(dcur_ref, dnxt_ref, x_ref, rt_ref, mod_ref, fg_ref, y_ref, o_ref, ya, yb, sems,
                    *, tt, ntiles, final_norm):
    i = pl.program_id(0)
    slot = i % 2

    def issue(dref, sl):
        def body(r, c):
            dst = pl.ds(pl.multiple_of(r * ROW_TILE, ROW_TILE), ROW_TILE)
            for kk, buf in enumerate((ya, yb)):
                row = pl.multiple_of(dref[0, 0, 2 * r + kk] * ROW_TILE, ROW_TILE)
                pltpu.make_async_copy(y_ref.at[pl.ds(row, ROW_TILE)], buf.at[sl, dst], sems.at[sl]).start()
            return c
        lax.fori_loop(0, tt, body, 0, unroll=DMA_UNROLL)

    @pl.when(i == 0)
    def _():
        issue(dcur_ref, 0)

    @pl.when(i + 1 < ntiles)
    def _():
        issue(dnxt_ref, 1 - slot)

    _wait_rows(y_ref.at[pl.ds(0, tt * ROW_TILE)], ya.at[slot], sems.at[slot], 2)
    half = x_ref.shape[1] // 2
    w1 = rt_ref[:, 2:3]
    w2 = rt_ref[:, 3:4]
    ssq = jnp.zeros((tt, 1), F32)
    for sub in range(ROW_TILE):
        a_lo, a_hi = _unpack_bf16_pair(_load_token_rows(ya.at[slot], sub, tt))
        b_lo, b_hi = _unpack_bf16_pair(_load_token_rows(yb.at[slot], sub, tt))
        for base, a, b in ((sub * LANES, a_lo, b_lo), (half + sub * LANES, a_hi, b_hi)):
            cols = slice(base, base + LANES)
            out = x_ref[:, cols] + mod_ref[5:6, cols] * (a * w1 + b * w2)
            o_ref[:, cols] = out
            if final_norm:
                ssq = ssq + jnp.sum(out * out, axis=-1, keepdims=True)
    if final_norm:
        r = lax.rsqrt(ssq / (2 * half) + NORM_EPS)
        o_ref[...] = o_ref[...] * r * fg_ref[...]


def moe_combine(x, y, dest, route, mod, final_g, seq, final_norm, tt=256):
    n, d = x.shape
    tiles_per_seq = seq // tt
    ntiles = n // tt
    dest3 = dest.reshape(ntiles, 1, 2 * tt)
    return pl.pallas_call(
        functools.partial(_combine_kernel, tt=tt, ntiles=ntiles, final_norm=final_norm),
        grid=(ntiles,),
        in_specs=[pl.BlockSpec((1, 1, 2 * tt), lambda i: (i, 0, 0), memory_space=pltpu.SMEM),
                  pl.BlockSpec((1, 1, 2 * tt), lambda i: (jnp.minimum(i + 1, ntiles - 1), 0, 0),
                               memory_space=pltpu.SMEM),
                  pl.BlockSpec((tt, d), lambda i: (i, 0)),
                  pl.BlockSpec((tt, LANES), lambda i: (i, 0)),
                  pl.BlockSpec((None, 6, d), lambda i: (i // tiles_per_seq, 0, 0)),
                  pl.BlockSpec((1, d), lambda i: (0, 0)),
                  pl.BlockSpec(memory_space=pl.ANY)],
        out_specs=pl.BlockSpec((tt, d), lambda i: (i, 0)),
        out_shape=jax.ShapeDtypeStruct((n, d), F32),
        scratch_shapes=[pltpu.VMEM((2, tt * ROW_TILE, LANES), U32), pltpu.VMEM((2, tt * ROW_TILE, LANES), U32),
                        pltpu.SemaphoreType.DMA((2,))],
        compiler_params=_cparams(("arbitrary",)),
        name="moe_combine",
    )(dest3, dest3, x, route, mod, final_g.reshape(1, d), y)


def _expert_schedule(counts, tm, nb):
    ne = counts.shape[0]
    padded = ((counts + tm - 1) // tm) * tm
    pad_end = jnp.cumsum(padded)
    pad_start = pad_end - padded
    n_used = (pad_end[-1] // tm).astype(I32)
    blk0 = jnp.arange(nb, dtype=I32) * tm
    block_e = jnp.sum((pad_end[None, :] <= blk0[:, None]).astype(I32), axis=1)
    block_e = jnp.minimum(block_e, ne - 1)
    first = (blk0 == pad_start[block_e]).astype(I32)
    ids = jnp.where(counts > 0, jnp.arange(ne, dtype=I32), ne)
    later = jnp.concatenate([lax.cummin(ids[::-1])[::-1][1:], jnp.full((1,), ne, I32)])
    nxt = jnp.where(later < ne, later, -1).astype(I32)[block_e]
    has_padding = (blk0 + tm == pad_end[block_e]) & (counts[block_e] % tm != 0)
    zero_blocks = ((blk0 >= pad_end[-1]) | has_padding).astype(I32)
    return pad_start, zero_blocks, (block_e, first, nxt, n_used.reshape(1))


def hier_moe_layer(x, g, mod, w_group, b_group, w_expert, b_expert, wgu_all, wd_all, layer, final_g, seq,
                   final_norm, tm=256):
    n, d = x.shape
    ng, ne = w_group.shape[1], w_expert.shape[1]
    w_r = jnp.zeros((d, LANES), F32).at[:, :ng].set(w_group).at[:, ng:ng + ne].set(w_expert)
    b_r = jnp.zeros((1, LANES), F32).at[0, :ng].set(b_group).at[0, ng:ng + ne].set(b_expert)
    hpack, route, cnt = moe_router(x, g, mod, w_r, b_r, seq)
    counts = cnt[0, ng:ng + ne].astype(I32)
    rows = 2 * n + ne * tm
    pad_start, zero_blocks, sched = _expert_schedule(counts, tm, rows // tm)
    ps_lanes = jnp.zeros((1, LANES), F32).at[0, ng:ng + ne].set(pad_start.astype(F32))
    dest = moe_dest(route, ps_lanes, tm=min(2048, n))[:, :2].reshape(-1)
    xin = moe_dispatch(hpack, dest, zero_blocks, rows, tm)
    y = moe_experts(xin, sched, wgu_all, wd_all, layer, tm)
    return moe_combine(x, y, dest, route, mod, final_g, seq, final_norm)


def _rope_tables(positions):
    half = ROPE_DIM // 2
    inv_freq = ROPE_THETA ** (-jnp.arange(half, dtype=F32) * 2.0 / ROPE_DIM)
    ang = positions.astype(F32).reshape(-1, 1) * inv_freq
    cos, sin = jnp.cos(ang), jnp.sin(ang)
    n = ang.shape[0]
    gap = LANES // 2 - half
    c = jnp.concatenate([cos, jnp.ones((n, gap), F32), cos, jnp.ones((n, gap), F32)], axis=1)
    s = jnp.concatenate([-sin, jnp.zeros((n, gap), F32), sin, jnp.zeros((n, gap), F32)], axis=1)
    return c, s


def _rope_pair_layout(w, cols):
    half = ROPE_DIM // 2
    mid = LANES // 2
    maps = w[:, :cols].reshape(w.shape[0], cols // LANES, LANES)
    maps = jnp.concatenate([maps[..., :half], maps[..., 2 * half:mid + half], maps[..., half:2 * half],
                            maps[..., mid + half:]], axis=-1)
    return jnp.concatenate([maps.reshape(w.shape[0], cols), w[:, cols:]], axis=1)


def kernel(x, c, positions, ada_w, ada_b, norm_mix_g, norm_ffn_g, final_norm_g, attn_w_in, attn_w_out, attn_lambda_q1, attn_lambda_k1, attn_lambda_q2, attn_lambda_k2, attn_head_norm_g, mlstm_w_in, mlstm_conv_w, mlstm_conv_b, mlstm_gate_b, mlstm_head_norm_g, mlstm_w_out, moe_w_group, moe_b_group, moe_w_expert, moe_b_expert, moe_w_gu, moe_w_down):
    bsz, seq, d = x.shape
    n = bsz * seq
    depth = ada_w.shape[0]
    mod = adaln(c, ada_w, ada_b)
    xf = x.reshape(n, d)
    for i in range(depth):
        jm = i // 2
        if i % 2 == 0:
            qk_cols = 2 * DA_HEADS * 2 * DA_HEAD_DIM
            tabs = _rope_tables(positions)
            w_in = _rope_pair_layout(attn_w_in[jm], qk_cols).astype(BF16)
            qkv = norm_matmul(xf, norm_mix_g[i], mod[i], w_in, seq, rope=(*tabs, qk_cols, qk_cols // 2))
            lam_init = 0.8 - 0.6 * math.exp(-0.3 * i)
            mixed = diff_attention(qkv.reshape(bsz, seq, -1), attn_lambda_q1[jm], attn_lambda_k1[jm],
                                   attn_lambda_q2[jm], attn_lambda_k2[jm], attn_head_norm_g[jm], lam_init)
            w_out = attn_w_out[jm]
        else:
            qk_cols = 2 * ML_HEADS * ML_QK_DIM
            main_cols = qk_cols + 2 * ML_HEADS * ML_V_DIM
            w_in = mlstm_w_in[jm]
            ngate = 2 * ML_HEADS
            w_gate = jnp.zeros((d, LANES), F32).at[:, :ngate].set(w_in[:, main_cols:])
            b_gate = jnp.zeros((1, LANES), F32).at[0, :ngate].set(mlstm_gate_b[jm])
            proj, gates = norm_matmul(xf, norm_mix_g[i], mod[i], w_in[:, :main_cols].astype(BF16), seq,
                                      extra=(w_gate, b_gate))
            proj = proj.reshape(bsz, seq, main_cols)
            qk = conv_silu(proj, mlstm_conv_w[jm], mlstm_conv_b[jm])
            gates_c = gates.reshape(bsz, seq, LANES)
            gates_r = jnp.swapaxes(gates_c[:, :, :ngate], 1, 2)
            mixed = mlstm(qk, proj, gates_c, gates_r, mlstm_head_norm_g[jm])
            w_out = mlstm_w_out[jm]
        xf = matmul_res(mixed.reshape(n, -1), w_out.astype(BF16), xf, mod[i], seq, gate_row=2)
        xf = hier_moe_layer(xf, norm_ffn_g[i], mod[i], moe_w_group[i], moe_b_group[i], moe_w_expert[i],
                            moe_b_expert[i], moe_w_gu, moe_w_down, i, final_norm_g, seq,
                            final_norm=(i == depth - 1))
    return xf.reshape(bsz, seq, d)
```

```python
import functools
import math

import jax
import jax.numpy as jnp
from jax import lax
from jax.experimental import pallas as pl
from jax.experimental.pallas import tpu as pltpu

F32 = jnp.float32
BF16 = jnp.bfloat16
U32 = jnp.uint32
I32 = jnp.int32

NORM_EPS = 1e-6
ROPE_THETA = 500000.0
ATTN_CHUNK = 64
DA_HEADS = 8
DA_HEAD_DIM = 128
ROPE_DIM = 32
ML_HEADS = 8
ML_QK_DIM = 128
ML_V_DIM = 256
ML_CONV = 4
MOE_GROUPS = 4
MOE_PER_GROUP = 8
MOE_EXPERTS = 32
LANES = 128
NEG_BIG = -1e30
LOG2_E = math.log2(math.e)

VMEM_LIMIT = 56 * 1024 * 1024


def _cparams(sem):
    return pltpu.CompilerParams(dimension_semantics=sem, vmem_limit_bytes=VMEM_LIMIT)


def _split_hi_lo(a):
    hi = a.astype(BF16)
    lo = (a - hi.astype(F32)).astype(BF16)
    return hi, lo


def _dot3(a, w):
    ah, al = _split_hi_lo(a)
    wh, wl = _split_hi_lo(w)
    d = functools.partial(jnp.dot, preferred_element_type=F32)
    return d(ah, wh) + (d(ah, wl) + d(al, wh))


def _pack_bf16_pair(lo_f32, hi_f32):
    lo_bits = lax.bitcast_convert_type(lo_f32.astype(BF16).astype(F32), U32)
    hi_bits = lax.bitcast_convert_type(hi_f32.astype(BF16).astype(F32), U32)
    return hi_bits | (lo_bits >> 16)


def _unpack_bf16_pair(word):
    lo = lax.bitcast_convert_type(word << 16, F32)
    hi = lax.bitcast_convert_type(word & jnp.uint32(0xFFFF0000), F32)
    return lo, hi


ROW_TILE = 8


def _store_token_rows(ref, words):
    t = words.shape[0]
    for sub in range(ROW_TILE):
        ref[pl.ds(sub, t, stride=ROW_TILE), :] = words[:, sub * LANES:(sub + 1) * LANES]


def _load_token_rows(ref, sub, t):
    return ref[pl.ds(sub, t, stride=ROW_TILE), :]


def _rms_modulate(x, g, shift, scale):
    ms = jnp.mean(x * x, axis=-1, keepdims=True)
    y = x * lax.rsqrt(ms + NORM_EPS) * g
    return y * (1.0 + scale) + shift


def _adaln_kernel(c_ref, w_ref, b_ref, o_ref):
    c = c_ref[...]
    cond = c * jax.nn.sigmoid(c)
    acc = jnp.dot(cond.astype(BF16), w_ref[...].astype(BF16), preferred_element_type=F32)
    o_ref[...] = acc + b_ref[...]


def adaln(c, ada_w, ada_b, tn=1024):
    depth, d, n6 = ada_w.shape
    bsz = c.shape[0]
    rows = 8
    cp = jnp.zeros((rows, d), F32).at[:bsz].set(c)
    out = pl.pallas_call(
        _adaln_kernel,
        grid=(depth, n6 // tn),
        in_specs=[pl.BlockSpec((rows, d), lambda l, j: (0, 0)),
                  pl.BlockSpec((None, d, tn), lambda l, j: (l, 0, j)),
                  pl.BlockSpec((None, 1, tn), lambda l, j: (l, 0, j))],
        out_specs=pl.BlockSpec((None, rows, tn), lambda l, j: (l, 0, j)),
        out_shape=jax.ShapeDtypeStruct((depth, rows, n6), F32),
        compiler_params=_cparams(("arbitrary", "arbitrary")),
        name="adaln",
    )(cp, ada_w, ada_b.reshape(depth, 1, n6))
    return out[:, :bsz].reshape(depth, bsz, 6, d)


ROPE_ROWS = 64

def _norm_matmul_kernel(*refs, rope_tiles, q_tiles, has_extra, tn):
    if has_extra:
        x_ref, g_ref, mod_ref, w_ref, c_ref, s_ref, we_ref, be_ref, o_ref, oe_ref, h_scr, acc_scr = refs
    else:
        x_ref, g_ref, mod_ref, w_ref, c_ref, s_ref, o_ref, h_scr, acc_scr = refs
    j = pl.program_id(1)

    @pl.when(j == 0)
    def _():
        h = _rms_modulate(x_ref[...], g_ref[...], mod_ref[0:1, :], mod_ref[1:2, :])
        h_scr[...] = h.astype(BF16)
        if has_extra:
            oe_ref[...] = _dot3(h, we_ref[...]) + be_ref[...]

    acc = jnp.dot(h_scr[...], w_ref[...], preferred_element_type=F32)
    if rope_tiles == 0:
        o_ref[...] = acc.astype(o_ref.dtype)
    else:
        @pl.when(j < rope_tiles)
        def _():
            acc_scr[...] = acc
            scale = jnp.where(j < q_tiles, DA_HEAD_DIM ** -0.5 * LOG2_E, 1.0).astype(F32)

            def rotate_rows(r, c):
                rows = pl.ds(pl.multiple_of(r * ROPE_ROWS, ROPE_ROWS), ROPE_ROWS)
                cs, sn = c_ref[rows, :] * scale, s_ref[rows, :] * scale
                for m in range(tn // LANES):
                    t = acc_scr[rows, m * LANES:(m + 1) * LANES]
                    rot = t * cs + pltpu.roll(t, LANES // 2, 1) * sn
                    o_ref[rows, m * LANES:(m + 1) * LANES] = rot.astype(o_ref.dtype)
                return c

            lax.fori_loop(0, acc_scr.shape[0] // ROPE_ROWS, rotate_rows, 0, unroll=2)

        @pl.when(j >= rope_tiles)
        def _():
            o_ref[...] = acc.astype(o_ref.dtype)


def norm_matmul(x, g, mod, w, seq, rope=None, extra=None, tm=1024, tn=512):
    n, k = x.shape
    m = w.shape[1]
    tiles_per_seq = seq // tm
    has_extra = extra is not None
    if rope is None:
        dummy = jnp.zeros((8, LANES), F32)
        tabs = (dummy, dummy)
        tab_spec = pl.BlockSpec((8, LANES), lambda i, j: (0, 0))
        rope_tiles = q_tiles = 0
    else:
        tabs = rope[:2]
        tab_spec = pl.BlockSpec((tm, LANES), lambda i, j: (i, 0))
        rope_tiles, q_tiles = rope[2] // tn, rope[3] // tn
    in_specs = [pl.BlockSpec((tm, k), lambda i, j: (i, 0)),
                pl.BlockSpec((1, k), lambda i, j: (0, 0)),
                pl.BlockSpec((None, 6, k), lambda i, j: (i // tiles_per_seq, 0, 0)),
                pl.BlockSpec((k, tn), lambda i, j: (0, j)),
                tab_spec, tab_spec]
    args = [x, g.reshape(1, k), mod, w, *tabs]
    out_specs = pl.BlockSpec((tm, tn), lambda i, j: (i, j))
    out_shape = jax.ShapeDtypeStruct((n, m), BF16)
    if has_extra:
        we, be = extra
        in_specs += [pl.BlockSpec((k, LANES), lambda i, j: (0, 0)),
                     pl.BlockSpec((1, LANES), lambda i, j: (0, 0))]
        args += [we, be]
        out_specs = [out_specs, pl.BlockSpec((tm, LANES), lambda i, j: (i, 0))]
        out_shape = [out_shape, jax.ShapeDtypeStruct((n, LANES), F32)]
    return pl.pallas_call(
        functools.partial(_norm_matmul_kernel, rope_tiles=rope_tiles, q_tiles=q_tiles,
                          has_extra=has_extra, tn=tn),
        grid=(n // tm, m // tn),
        in_specs=in_specs,
        out_specs=out_specs,
        out_shape=out_shape,
        scratch_shapes=[pltpu.VMEM((tm, k), BF16),
                        pltpu.VMEM((tm, tn) if rope_tiles else (8, LANES), F32)],
        compiler_params=_cparams(("arbitrary", "arbitrary")),
        name="norm_matmul",
    )(*args)


def _matmul_res_kernel(a_ref, w_ref, res_ref, mod_ref, o_ref, *, gate_row):
    acc = jnp.dot(a_ref[...], w_ref[...], preferred_element_type=F32)
    o_ref[...] = res_ref[...] + mod_ref[gate_row:gate_row + 1, :] * acc


def matmul_res(a, w, res, mod, seq, gate_row, tm=512, tn=2048):
    n, k = a.shape
    m = w.shape[1]
    tiles_per_seq = seq // tm
    return pl.pallas_call(
        functools.partial(_matmul_res_kernel, gate_row=gate_row),
        grid=(n // tm, m // tn),
        in_specs=[pl.BlockSpec((tm, k), lambda i, j: (i, 0)),
                  pl.BlockSpec((k, tn), lambda i, j: (0, j)),
                  pl.BlockSpec((tm, tn), lambda i, j: (i, j)),
                  pl.BlockSpec((None, 6, tn), lambda i, j: (i // tiles_per_seq, 0, j))],
        out_specs=pl.BlockSpec((tm, tn), lambda i, j: (i, j)),
        out_shape=jax.ShapeDtypeStruct((n, m), F32),
        compiler_params=_cparams(("arbitrary", "arbitrary")),
        name="matmul_res",
    )(a, w, res, mod)


ATTN_GROUPS = (4, 2)


def _grouped_loop(n, body, groups):
    done = 0
    for group in groups:
        trips = (n - done) // group

        def grouped(gi, c, group=group, done=done):
            for t in range(group):
                c = body(done + gi * group + t, c)
            return c

        lax.fori_loop(0, trips, grouped, 0)
        done = done + trips * group
    lax.fori_loop(done, n, body, 0)


def _attn_kernel(q_ref, k_ref, v_ref, lq1_ref, lk1_ref, lq2_ref, lk2_ref, g_ref, o_ref,
                 s_scr, m_scr, l_scr, acc_scr, *, lam_init, tq, tk, seq):
    d = DA_HEAD_DIM
    nlane = tk // LANES
    ndiag = tq // tk
    lam = (jnp.exp(jnp.sum(lq1_ref[...] * lk1_ref[...], axis=-1, keepdims=True))
           - jnp.exp(jnp.sum(lq2_ref[...] * lk2_ref[...], axis=-1, keepdims=True)) + lam_init)
    shift = ATTN_CHUNK.bit_length() - 1
    row_chunk = jnp.right_shift(lax.broadcasted_iota(I32, (tq, tk), 0), shift)
    col_chunk = jnp.right_shift(lax.broadcasted_iota(I32, (tq, tk), 1), shift)
    nt = (((1,), (1,)), ((), ()))

    def lane_fold(a, op):
        part = a[:, 0:LANES]
        for cb in range(1, nlane):
            part = op(part, a[:, cb * LANES:(cb + 1) * LANES])
        return part

    def q_body(qi, _):
        qs = pl.multiple_of(qi * tq, tq)
        qm = (q_ref[pl.ds(qs, tq), 0:d], q_ref[pl.ds(qs, tq), d:2 * d])
        m_scr[...] = jnp.full_like(m_scr, NEG_BIG)

        def score(j, mask):
            ks = pl.multiple_of(j * tk, tk)
            for mp in range(2):
                k = k_ref[pl.ds(ks, tk), mp * d:(mp + 1) * d]
                s = lax.dot_general(qm[mp], k, nt, preferred_element_type=F32)
                if mask is not None:
                    s = jnp.where(mask, s, NEG_BIG)
                s_scr[mp, j] = s
                m_scr[mp] = jnp.maximum(m_scr[mp], lane_fold(s, jnp.maximum))

        def score_body(j, c):
            score(j, None)
            return c

        nfull = qi * ndiag
        _grouped_loop(nfull, score_body, ATTN_GROUPS)
        for t in range(ndiag):
            score(nfull + t, col_chunk + t * (tk // ATTN_CHUNK) <= row_chunk)
        m = [jnp.max(m_scr[mp], axis=-1, keepdims=True) for mp in range(2)]
        l_scr[...] = jnp.zeros_like(l_scr)
        acc_scr[...] = jnp.zeros_like(acc_scr)

        def pv_body(j, c):
            ks = pl.multiple_of(j * tk, tk)
            v = v_ref[pl.ds(ks, tk), :]
            for mp in range(2):
                p = jnp.exp2(s_scr[mp, j] - m[mp])
                l_scr[mp] += lane_fold(p, jnp.add)
                acc_scr[mp] += jnp.dot(p.astype(BF16), v, preferred_element_type=F32)
            return c

        _grouped_loop(nfull + ndiag, pv_body, ATTN_GROUPS)
        l = [jnp.sum(l_scr[mp], axis=-1, keepdims=True) for mp in range(2)]
        o = acc_scr[0] / l[0] - lam * (acc_scr[1] / l[1])
        ms = jnp.mean(o * o, axis=-1, keepdims=True)
        o = o * lax.rsqrt(ms + NORM_EPS) * g_ref[...] * (1.0 - lam_init)
        o_ref[pl.ds(qs, tq), :] = o.astype(o_ref.dtype)
        return 0

    lax.fori_loop(0, seq // tq, q_body, 0)


def diff_attention(qkv, lq1, lk1, lq2, lk2, head_g, lam_init, tq=512, tk=256):
    bsz, seq, _ = qkv.shape
    h, dv = DA_HEADS, 2 * DA_HEAD_DIM
    vec = lambda a: a.reshape(1, -1).astype(F32)
    small = lambda n: pl.BlockSpec((1, n), lambda b, hh: (0, 0))
    return pl.pallas_call(
        functools.partial(_attn_kernel, lam_init=lam_init, tq=tq, tk=tk, seq=seq),
        grid=(bsz, h),
        in_specs=[pl.BlockSpec((None, seq, dv), lambda b, hh: (b, 0, hh)),
                  pl.BlockSpec((None, seq, dv), lambda b, hh: (b, 0, h + hh)),
                  pl.BlockSpec((None, seq, dv), lambda b, hh: (b, 0, 2 * h + hh)),
                  small(DA_HEAD_DIM), small(DA_HEAD_DIM), small(DA_HEAD_DIM), small(DA_HEAD_DIM),
                  small(dv)],
        out_specs=pl.BlockSpec((None, seq, dv), lambda b, hh: (b, 0, hh)),
        out_shape=jax.ShapeDtypeStruct((bsz, seq, h * dv), BF16),
        scratch_shapes=[pltpu.VMEM((2, seq // tk, tq, tk), F32), pltpu.VMEM((2, tq, LANES), F32),
                        pltpu.VMEM((2, tq, LANES), F32), pltpu.VMEM((2, tq, dv), F32)],
        compiler_params=_cparams(("arbitrary", "arbitrary")),
        name="diff_attention",
    )(qkv, qkv, qkv, vec(lq1), vec(lk1), vec(lq2), vec(lk2), vec(head_g))


def _conv_silu_kernel(x_ref, w_ref, b_ref, o_ref, *, k_tile0):
    j = pl.program_id(1)
    x = x_ref[...].astype(F32)
    row = lax.broadcasted_iota(I32, x.shape, 0)
    y = x * w_ref[ML_CONV - 1:ML_CONV, :] + b_ref[...]
    for s in range(1, ML_CONV):
        shifted = jnp.where(row >= s, pltpu.roll(x, s, 0), 0.0)
        y = y + shifted * w_ref[ML_CONV - 1 - s:ML_CONV - s, :]
    y = y * jax.nn.sigmoid(y)
    scale = jnp.where(j >= k_tile0, ML_QK_DIM ** -0.5, 1.0).astype(F32)
    o_ref[...] = (y * scale).astype(o_ref.dtype)


def conv_silu(proj, conv_w, conv_b, tc=128):
    bsz, seq, _ = proj.shape
    cols = conv_w.shape[1]
    return pl.pallas_call(
        functools.partial(_conv_silu_kernel, k_tile0=(cols // 2) // tc),
        grid=(bsz, cols // tc),
        in_specs=[pl.BlockSpec((None, seq, tc), lambda b, j: (b, 0, j)),
                  pl.BlockSpec((ML_CONV, tc), lambda b, j: (0, j)),
                  pl.BlockSpec((1, tc), lambda b, j: (0, j))],
        out_specs=pl.BlockSpec((None, seq, tc), lambda b, j: (b, 0, j)),
        out_shape=jax.ShapeDtypeStruct((bsz, seq, cols), BF16),
        compiler_params=_cparams(("arbitrary", "arbitrary")),
        name="conv_silu",
    )(proj, conv_w, conv_b.reshape(1, cols))


def _mlstm_kernel(q_ref, k_ref, v_ref, op_ref, gc_ref, gr_ref, hg_ref, o_ref,
                  cx_scr, m_scr, *, chunk):
    c = pl.program_id(1)
    nh, dqk, dv = ML_HEADS, ML_QK_DIM, ML_V_DIM

    @pl.when(c == 0)
    def _():
        cx_scr[...] = jnp.zeros_like(cx_scr)
        m_scr[...] = jnp.zeros_like(m_scr)

    gc = gc_ref[...]
    gr = gr_ref[...]
    lf_c = jax.nn.log_sigmoid(gc)
    lf_r = jax.nn.log_sigmoid(gr)
    r_i = lax.broadcasted_iota(I32, (chunk, chunk), 0)
    c_i = lax.broadcasted_iota(I32, (chunk, chunk), 1)
    causal = c_i <= r_i
    tril = causal.astype(F32)
    triu = (r_i <= c_i).astype(F32)
    b_c = _dot3(tril, lf_c)
    b_r = _dot3(lf_r, triu)
    nt = (((1,), (1,)), ((), ()))
    tn_ = (((0,), (0,)), ((), ()))
    ones_l = jnp.ones((chunk, LANES), BF16)
    ones_v = jnp.ones((dv, LANES), BF16)
    wide = lambda a: jnp.concatenate([a] * (dv // LANES), axis=1)

    for h in range(nh):
        q = q_ref[:, h * dqk:(h + 1) * dqk]
        k = k_ref[:, h * dqk:(h + 1) * dqk]
        v_ext = jnp.concatenate([v_ref[:, h * dv:(h + 1) * dv], ones_l], axis=1)
        bc = b_c[:, nh + h:nh + h + 1]
        br = b_r[nh + h:nh + h + 1, :]
        ig_c = gc[:, h:h + 1]
        ig_r = gr[h:h + 1, :]
        m_prev = m_scr[h:h + 1, :]
        dmat = jnp.where(causal, bc - br + ig_r, NEG_BIG)
        inter = bc + m_prev
        m_t = jnp.maximum(inter, jnp.max(dmat, axis=-1, keepdims=True))
        w = jnp.exp(dmat - m_t)
        s = lax.dot_general(q, k, nt, preferred_element_type=F32) * w
        decay = jnp.exp(inter - m_t)
        cx = cx_scr[h]
        tot = (jnp.dot(s.astype(BF16), v_ext, preferred_element_type=F32)
               + decay * jnp.dot(q, cx.astype(BF16), preferred_element_type=F32))
        den = jnp.maximum(jnp.abs(tot[:, dv:]), jnp.exp(-m_t))
        hh = tot[:, :dv] / wide(den)
        b_last = bc[chunk - 1:chunk, :]
        g = b_last - bc + ig_c
        m_new = jnp.maximum(b_last + m_prev, jnp.max(g, axis=0, keepdims=True))
        carry_decay = jnp.exp(b_last + m_prev - m_new)
        wg = jnp.exp(g - m_new)
        wv = (wg * v_ext.astype(F32)).astype(BF16)
        cx_scr[h] = carry_decay * cx + lax.dot_general(k, wv, tn_, preferred_element_type=F32)
        m_scr[h:h + 1, :] = m_new
        ms = jnp.dot((hh * hh).astype(BF16), ones_v, preferred_element_type=F32) * (1.0 / dv)
        hn = hh * wide(lax.rsqrt(ms + NORM_EPS)) * hg_ref[:, h * dv:(h + 1) * dv]
        og = jax.nn.sigmoid(op_ref[:, h * dv:(h + 1) * dv].astype(F32))
        o_ref[:, h * dv:(h + 1) * dv] = (og * hn).astype(o_ref.dtype)


def mlstm(qk, proj, gates_c, gates_r, head_g, chunk=128):
    bsz, seq, _ = qk.shape
    nh = ML_HEADS
    qw, vw = nh * ML_QK_DIM, nh * ML_V_DIM
    v_blk = (2 * qw) // vw
    return pl.pallas_call(
        functools.partial(_mlstm_kernel, chunk=chunk),
        grid=(bsz, seq // chunk),
        in_specs=[pl.BlockSpec((None, chunk, qw), lambda b, c: (b, c, 0)),
                  pl.BlockSpec((None, chunk, qw), lambda b, c: (b, c, 1)),
                  pl.BlockSpec((None, chunk, vw), lambda b, c: (b, c, v_blk)),
                  pl.BlockSpec((None, chunk, vw), lambda b, c: (b, c, v_blk + 1)),
                  pl.BlockSpec((None, chunk, LANES), lambda b, c: (b, c, 0)),
                  pl.BlockSpec((None, 2 * nh, chunk), lambda b, c: (b, 0, c)),
                  pl.BlockSpec((1, vw), lambda b, c: (0, 0))],
        out_specs=pl.BlockSpec((None, chunk, vw), lambda b, c: (b, c, 0)),
        out_shape=jax.ShapeDtypeStruct((bsz, seq, vw), BF16),
        scratch_shapes=[pltpu.VMEM((nh, ML_QK_DIM, ML_V_DIM + LANES), F32),
                        pltpu.VMEM((nh, 1), F32)],
        compiler_params=_cparams(("arbitrary", "arbitrary")),
        name="mlstm",
    )(qk, qk, proj, proj, gates_c, gates_r, head_g.reshape(1, vw))


def _router_kernel(x_ref, g_ref, mod_ref, w_ref, b_ref, hp_ref, rt_ref, cnt_ref, run_scr):
    i = pl.program_id(0)

    @pl.when(i == 0)
    def _():
        run_scr[...] = jnp.zeros_like(run_scr)

    h = _rms_modulate(x_ref[...], g_ref[...], mod_ref[3:4, :], mod_ref[4:5, :])
    half = h.shape[1] // 2
    _store_token_rows(hp_ref, _pack_bf16_pair(h[:, :half], h[:, half:]))
    logits = _dot3(h, w_ref[...]) + b_ref[...]
    tm = logits.shape[0]
    lane =lax.broadcasted_iota(I32, logits.shape, 1).astype(F32)
    ng = float(MOE_GROUPS)
    is_g = lane < ng
    gl = jnp.where(is_g, logits, NEG_BIG)
    gmax = jnp.max(gl, axis=-1, keepdims=True)
    grp = jnp.min(jnp.where(gl == gmax, lane, float(LANES)), axis=-1, keepdims=True)
    p_group = 1.0 / jnp.sum(jnp.where(is_g, jnp.exp(gl - gmax), 0.0), axis=-1, keepdims=True)
    lo = ng + float(MOE_PER_GROUP) * grp
    el = jnp.where((lane >= lo) & (lane < lo + float(MOE_PER_GROUP)), logits, NEG_BIG)
    v1 = jnp.max(el, axis=-1, keepdims=True)
    i1 = jnp.min(jnp.where(el == v1, lane, float(LANES)), axis=-1, keepdims=True)
    el2 = jnp.where(lane == i1, NEG_BIG, el)
    v2 = jnp.max(el2, axis=-1, keepdims=True)
    i2 = jnp.min(jnp.where(el2 == v2, lane, float(LANES)), axis=-1, keepdims=True)
    ex = jnp.exp(v2 - v1)
    w1 = p_group / (1.0 + ex)
    w2 = p_group * (ex / (1.0 + ex))
    oh1 = (lane == i1).astype(F32)
    oh2 = (lane == i2).astype(F32)
    oh = oh1 + oh2
    r_i = lax.broadcasted_iota(I32, (tm, tm), 0)
    c_i = lax.broadcasted_iota(I32, (tm, tm), 1)
    before = (c_i < r_i).astype(BF16)
    prior = jnp.dot(before, oh.astype(BF16), preferred_element_type=F32) + run_scr[...]
    rank1 = jnp.sum(oh1 * prior, axis=-1, keepdims=True)
    rank2 = jnp.sum(oh2 * prior, axis=-1, keepdims=True)
    run_scr[...] += jnp.sum(oh, axis=0, keepdims=True)
    cnt_ref[...] = run_scr[...]
    rt_ref[...] = jnp.where(lane == 0.0, i1 - ng,
                  jnp.where(lane == 1.0, i2 - ng,
                  jnp.where(lane == 2.0, w1,
                  jnp.where(lane == 3.0, w2,
                  jnp.where(lane == 4.0, rank1,
                  jnp.where(lane == 5.0, rank2, 0.0))))))


def moe_router(x, g, mod, w_r, b_r, seq, tm=512):
    n, d = x.shape
    tiles_per_seq = seq // tm
    return pl.pallas_call(
        _router_kernel,
        grid=(n // tm,),
        in_specs=[pl.BlockSpec((tm, d), lambda i: (i, 0)),
                  pl.BlockSpec((1, d), lambda i: (0, 0)),
                  pl.BlockSpec((None, 6, d), lambda i: (i // tiles_per_seq, 0, 0)),
                  pl.BlockSpec((d, LANES), lambda i: (0, 0)),
                  pl.BlockSpec((1, LANES), lambda i: (0, 0))],
        out_specs=[pl.BlockSpec((tm * ROW_TILE, LANES), lambda i: (i, 0)),
                   pl.BlockSpec((tm, LANES), lambda i: (i, 0)),
                   pl.BlockSpec((1, LANES), lambda i: (0, 0))],
        out_shape=[jax.ShapeDtypeStruct((n * ROW_TILE, LANES), U32),
                   jax.ShapeDtypeStruct((n, LANES), F32),
                   jax.ShapeDtypeStruct((1, LANES), F32)],
        scratch_shapes=[pltpu.VMEM((1, LANES), F32)],
        compiler_params=_cparams(("arbitrary",)),
        name="moe_router",
    )(x, g.reshape(1, d), mod, w_r, b_r)


def _dest_kernel(rt_ref, ps_ref, o_ref):
    rt = rt_ref[...]
    lane = lax.broadcasted_iota(I32, rt.shape, 1).astype(F32)
    ng = float(MOE_GROUPS)
    ps = ps_ref[...]
    d1 = jnp.sum(jnp.where(lane == rt[:, 0:1] + ng, ps, 0.0), axis=-1, keepdims=True) + rt[:, 4:5]
    d2 = jnp.sum(jnp.where(lane == rt[:, 1:2] + ng, ps, 0.0), axis=-1, keepdims=True) + rt[:, 5:6]
    o_ref[...] = jnp.where(lane == 0.0, d1, jnp.where(lane == 1.0, d2, 0.0)).astype(I32)


def moe_dest(route, pad_start_lanes, tm=2048):
    n = route.shape[0]
    return pl.pallas_call(
        _dest_kernel,
        grid=(n // tm,),
        in_specs=[pl.BlockSpec((tm, LANES), lambda i: (i, 0)),
                  pl.BlockSpec((1, LANES), lambda i: (0, 0))],
        out_specs=pl.BlockSpec((tm, LANES), lambda i: (i, 0)),
        out_shape=jax.ShapeDtypeStruct((n, LANES), I32),
        compiler_params=_cparams(("arbitrary",)),
        name="moe_dest",
    )(route, pad_start_lanes)


DMA_UNROLL = 8


def _wait_rows(src_rows, dst_rows, sem, copies):
    for _ in range(copies):
        pltpu.make_async_copy(src_rows, dst_rows, sem).wait()


def _dispatch_kernel(zb_ref, dest_ref, hp_ref, xin_ref, zero_scr, ring, sems, *, tt, tm, nb, ntiles):
    i = pl.program_id(0)
    slot = i % 2
    blk_rows = tm * ROW_TILE
    tile_rows = tt * ROW_TILE

    @pl.when(i == 0)
    def _():
        zero_scr[...] = jnp.zeros_like(zero_scr)

        def zero_copy(blk):
            return pltpu.make_async_copy(zero_scr, xin_ref.at[pl.ds(pl.multiple_of(blk * blk_rows, blk_rows), blk_rows)],
                                         sems.at[2])

        def start(blk, c):
            @pl.when(zb_ref[blk] == 1)
            def _():
                zero_copy(blk).start()
            return c

        def wait(blk, c):
            @pl.when(zb_ref[blk] == 1)
            def _():
                zero_copy(blk).wait()
            return c

        lax.fori_loop(0, nb, start, 0)
        lax.fori_loop(0, nb, wait, 0)

    ring[slot] = hp_ref[...]

    def start_rows(r, c):
        src = ring.at[slot, pl.ds(pl.multiple_of(r * ROW_TILE, ROW_TILE), ROW_TILE)]
        for kk in range(2):
            row = pl.multiple_of(dest_ref[0, 0, 2 * r + kk] * ROW_TILE, ROW_TILE)
            pltpu.make_async_copy(src, xin_ref.at[pl.ds(row, ROW_TILE)], sems.at[slot]).start()
        return c

    lax.fori_loop(0, tt, start_rows, 0, unroll=DMA_UNROLL)
    whole = xin_ref.at[pl.ds(0, tile_rows)]

    @pl.when(i > 0)
    def _():
        _wait_rows(ring.at[1 - slot], whole, sems.at[1 - slot], 2)

    @pl.when(i == ntiles - 1)
    def _():
        _wait_rows(ring.at[slot], whole, sems.at[slot], 2)


def moe_dispatch(hpack, dest, zero_blocks, rows, tm, tt=256):
    n = hpack.shape[0] // ROW_TILE
    grid_spec = pltpu.PrefetchScalarGridSpec(
        num_scalar_prefetch=1,
        grid=(n // tt,),
        in_specs=[pl.BlockSpec((1, 1, 2 * tt), lambda i, zb: (i, 0, 0), memory_space=pltpu.SMEM),
                  pl.BlockSpec((tt * ROW_TILE, LANES), lambda i, zb: (i, 0))],
        out_specs=pl.BlockSpec(memory_space=pl.ANY),
        scratch_shapes=[pltpu.VMEM((tm * ROW_TILE, LANES), U32), pltpu.VMEM((2, tt * ROW_TILE, LANES), U32),
                        pltpu.SemaphoreType.DMA((3,))],
    )
    return pl.pallas_call(
        functools.partial(_dispatch_kernel, tt=tt, tm=tm, nb=rows // tm, ntiles=n // tt),
        grid_spec=grid_spec,
        out_shape=jax.ShapeDtypeStruct((rows * ROW_TILE, LANES), U32),
        compiler_params=_cparams(("arbitrary",)),
        name="moe_dispatch",
    )(zero_blocks, dest.reshape(n // tt, 1, 2 * tt), hpack)


def _expert_kernel(be_ref, first_ref, nxt_ref, nu_ref, x_ref, wgu_hbm, wd_hbm, y_ref,
                   wgu_bf, wd_bf, stg_gu, stg_d, xs_scr, acc_scr, sems, *, layer, th, cr):
    i = pl.program_id(0)

    def fetch(e):
        return (pltpu.make_async_copy(wgu_hbm.at[layer, e], stg_gu, sems.at[0]),
                pltpu.make_async_copy(wd_hbm.at[layer, e], stg_d, sems.at[1]))

    @pl.when(i < nu_ref[0])
    def _():
        @pl.when(first_ref[i] == 1)
        def _():
            @pl.when(i == 0)
            def _():
                for cp in fetch(be_ref[i]):
                    cp.start()

            for cp in fetch(be_ref[i]):
                cp.wait()

            def cast_rows(src, dst):
                def body(r, c):
                    rs = pl.multiple_of(r * cr, cr)
                    dst[pl.ds(rs, cr), :] = src[pl.ds(rs, cr), :].astype(BF16)
                    return c
                lax.fori_loop(0, src.shape[0] // cr, body, 0)

            cast_rows(stg_gu, wgu_bf)
            cast_rows(stg_d, wd_bf)

            @pl.when(nxt_ref[i] >= 0)
            def _():
                for cp in fetch(nxt_ref[i]):
                    cp.start()

        tm = xs_scr.shape[0]
        half = xs_scr.shape[1] // 2
        hid = wd_bf.shape[0]
        for sub in range(ROW_TILE):
            lo, hi = _unpack_bf16_pair(_load_token_rows(x_ref, sub, tm))
            xs_scr[:, sub * LANES:(sub + 1) * LANES] = lo.astype(BF16)
            xs_scr[:, half + sub * LANES:half + (sub + 1) * LANES] = hi.astype(BF16)
        xs = xs_scr[...]
        for c in range(hid // th):
            gt = jnp.dot(xs, wgu_bf[:, c * th:(c + 1) * th], preferred_element_type=F32)
            up = jnp.dot(xs, wgu_bf[:, hid + c * th:hid + (c + 1) * th], preferred_element_type=F32)
            act = (gt * jax.nn.sigmoid(gt) * up).astype(BF16)
            part = jnp.dot(act, wd_bf[c * th:(c + 1) * th, :], preferred_element_type=F32)
            if c == 0:
                acc_scr[...] = part
            else:
                acc_scr[...] += part
        y = acc_scr[...]
        _store_token_rows(y_ref, _pack_bf16_pair(y[:, :half], y[:, half:]))


def moe_experts(xin, sched, wgu_all, wd_all, layer, tm, th=256, cr=256):
    _, _, d, hid2 = wgu_all.shape
    hid = hid2 // 2
    nb = xin.shape[0] // (tm * ROW_TILE)
    block_e, first, nxt, n_used = sched

    def blk(i, be, fi, nx, nu):
        return (jnp.minimum(i, nu[0] - 1), 0)

    grid_spec = pltpu.PrefetchScalarGridSpec(
        num_scalar_prefetch=4,
        grid=(nb,),
        in_specs=[pl.BlockSpec((tm * ROW_TILE, LANES), blk),
                  pl.BlockSpec(memory_space=pl.ANY),
                  pl.BlockSpec(memory_space=pl.ANY)],
        out_specs=pl.BlockSpec((tm * ROW_TILE, LANES), blk),
        scratch_shapes=[pltpu.VMEM((d, hid2), BF16), pltpu.VMEM((hid, d), BF16),
                        pltpu.VMEM((d, hid2), F32), pltpu.VMEM((hid, d), F32),
                        pltpu.VMEM((tm, d), BF16), pltpu.VMEM((tm, d), F32),
                        pltpu.SemaphoreType.DMA((2,))],
    )
    return pl.pallas_call(
        functools.partial(_expert_kernel, layer=layer, th=th, cr=cr),
        grid_spec=grid_spec,
        out_shape=jax.ShapeDtypeStruct(xin.shape, U32),
        input_output_aliases={4: 0},
        compiler_params=_cparams(("arbitrary",)),
        name="moe_experts",
    )(block_e, first, nxt, n_used, xin, wgu_all, wd_all)


def _combine_kernel(dcur_ref, dnxt_ref, x_ref, rt_ref, mod_ref, fg_ref, y_ref, o_ref, ya, yb, sems,
                    *, tt, ntiles, final_norm):
    i = pl.program_id(0)
    slot = i % 2

    def issue(dref, sl):
        def body(r, c):
            dst = pl.ds(pl.multiple_of(r * ROW_TILE, ROW_TILE), ROW_TILE)
            for kk, buf in enumerate((ya, yb)):
                row = pl.multiple_of(dref[0, 0, 2 * r + kk] * ROW_TILE, ROW_TILE)
                pltpu.make_async_copy(y_ref.at[pl.ds(row, ROW_TILE)], buf.at[sl, dst], sems.at[sl]).start()
            return c
        lax.fori_loop(0, tt, body, 0, unroll=DMA_UNROLL)

    @pl.when(i == 0)
    def _():
        issue(dcur_ref, 0)

    @pl.when(i + 1 < ntiles)
    def _():
        issue(dnxt_ref, 1 - slot)

    _wait_rows(y_ref.at[pl.ds(0, tt * ROW_TILE)], ya.at[slot], sems.at[slot], 2)
    half = x_ref.shape[1] // 2
    w1 = rt_ref[:, 2:3]
    w2 = rt_ref[:, 3:4]
    ssq = jnp.zeros((tt, 1), F32)
    for sub in range(ROW_TILE):
        a_lo, a_hi = _unpack_bf16_pair(_load_token_rows(ya.at[slot], sub, tt))
        b_lo, b_hi = _unpack_bf16_pair(_load_token_rows(yb.at[slot], sub, tt))
        for base, a, b in ((sub * LANES, a_lo, b_lo), (half + sub * LANES, a_hi, b_hi)):
            cols = slice(base, base + LANES)
            out = x_ref[:, cols] + mod_ref[5:6, cols] * (a * w1 + b * w2)
            o_ref[:, cols] = out
            if final_norm:
                ssq = ssq + jnp.sum(out * out, axis=-1, keepdims=True)
    if final_norm:
        r = lax.rsqrt(ssq / (2 * half) + NORM_EPS)
        o_ref[...] = o_ref[...] * r * fg_ref[...]


def moe_combine(x, y, dest, route, mod, final_g, seq, final_norm, tt=256):
    n, d = x.shape
    tiles_per_seq = seq // tt
    ntiles = n // tt
    dest3 = dest.reshape(ntiles, 1, 2 * tt)
    return pl.pallas_call(
        functools.partial(_combine_kernel, tt=tt, ntiles=ntiles, final_norm=final_norm),
        grid=(ntiles,),
        in_specs=[pl.BlockSpec((1, 1, 2 * tt), lambda i: (i, 0, 0), memory_space=pltpu.SMEM),
                  pl.BlockSpec((1, 1, 2 * tt), lambda i: (jnp.minimum(i + 1, ntiles - 1), 0, 0),
                               memory_space=pltpu.SMEM),
                  pl.BlockSpec((tt, d), lambda i: (i, 0)),
                  pl.BlockSpec((tt, LANES), lambda i: (i, 0)),
                  pl.BlockSpec((None, 6, d), lambda i: (i // tiles_per_seq, 0, 0)),
                  pl.BlockSpec((1, d), lambda i: (0, 0)),
                  pl.BlockSpec(memory_space=pl.ANY)],
        out_specs=pl.BlockSpec((tt, d), lambda i: (i, 0)),
        out_shape=jax.ShapeDtypeStruct((n, d), F32),
        scratch_shapes=[pltpu.VMEM((2, tt * ROW_TILE, LANES), U32), pltpu.VMEM((2, tt * ROW_TILE, LANES), U32),
                        pltpu.SemaphoreType.DMA((2,))],
        compiler_params=_cparams(("arbitrary",)),
        name="moe_combine",
    )(dest3, dest3, x, route, mod, final_g.reshape(1, d), y)


def _expert_schedule(counts, tm, nb):
    ne = counts.shape[0]
    ids = jnp.arange(ne, dtype=I32)
    padded = ((counts + tm - 1) // tm) * tm
    pad_end = jnp.sum(jnp.where(ids[None, :] <= ids[:, None], padded[None, :], 0), axis=1)
    pad_start = pad_end - padded
    total = jnp.sum(padded)
    n_used = total // tm
    blk0 = jnp.arange(nb, dtype=I32) * tm
    block_e = jnp.minimum(jnp.sum((pad_end[None, :] <= blk0[:, None]).astype(I32), axis=1), ne - 1)
    onehot = block_e[:, None] == ids[None, :]
    pick = lambda v: jnp.sum(jnp.where(onehot, v[None, :], 0), axis=1)
    first = (blk0 == pick(pad_start)).astype(I32)
    later = jnp.min(jnp.where((ids[None, :] > ids[:, None]) & (counts[None, :] > 0), ids[None, :], ne), axis=1)
    nxt = pick(jnp.where(later < ne, later, -1))
    has_padding = (blk0 + tm == pick(pad_end)) & (pick(counts % tm) != 0)
    zero_blocks = ((blk0 >= total) | has_padding).astype(I32)
    return pad_start, zero_blocks, (block_e, first, nxt.astype(I32), n_used.astype(I32).reshape(1))


def hier_moe_layer(x, g, mod, w_group, b_group, w_expert, b_expert, wgu_all, wd_all, layer, final_g, seq,
                   final_norm, tm=256):
    n, d = x.shape
    ng, ne = w_group.shape[1], w_expert.shape[1]
    w_r = jnp.concatenate([w_group, w_expert, jnp.zeros((d, LANES - ng - ne), F32)], axis=1)
    b_r = jnp.concatenate([b_group, b_expert, jnp.zeros((LANES - ng - ne,), F32)]).reshape(1, LANES)
    hpack, route, cnt = moe_router(x, g, mod, w_r, b_r, seq)
    counts = cnt[0, ng:ng + ne].astype(I32)
    rows = 2 * n + ne * tm
    pad_start, zero_blocks, sched = _expert_schedule(counts, tm, rows // tm)
    ps_lanes = jnp.concatenate([jnp.zeros((ng,), F32), pad_start.astype(F32),
                                jnp.zeros((LANES - ng - ne,), F32)]).reshape(1, LANES)
    dest = moe_dest(route, ps_lanes, tm=min(2048, n))[:, :2].reshape(-1)
    xin = moe_dispatch(hpack, dest, zero_blocks, rows, tm)
    y = moe_experts(xin, sched, wgu_all, wd_all, layer, tm)
    return moe_combine(x, y, dest, route, mod, final_g, seq, final_norm)


def _rope_tables(positions):
    half = ROPE_DIM // 2
    inv_freq = ROPE_THETA ** (-jnp.arange(half, dtype=F32) * 2.0 / ROPE_DIM)
    ang = positions.astype(F32).reshape(-1, 1) * inv_freq
    cos, sin = jnp.cos(ang), jnp.sin(ang)
    n = ang.shape[0]
    gap = LANES // 2 - half
    c = jnp.concatenate([cos, jnp.ones((n, gap), F32), cos, jnp.ones((n, gap), F32)], axis=1)
    s = jnp.concatenate([-sin, jnp.zeros((n, gap), F32), sin, jnp.zeros((n, gap), F32)], axis=1)
    return c, s


def _rope_pair_layout(w, cols):
    half = ROPE_DIM // 2
    mid = LANES // 2
    maps = w[:, :cols].reshape(w.shape[0], cols // LANES, LANES)
    maps = jnp.concatenate([maps[..., :half], maps[..., 2 * half:mid + half], maps[..., half:2 * half],
                            maps[..., mid + half:]], axis=-1)
    return jnp.concatenate([maps.reshape(w.shape[0], cols), w[:, cols:]], axis=1)


def kernel(x, c, positions, ada_w, ada_b, norm_mix_g, norm_ffn_g, final_norm_g, attn_w_in, attn_w_out, attn_lambda_q1, attn_lambda_k1, attn_lambda_q2, attn_lambda_k2, attn_head_norm_g, mlstm_w_in, mlstm_conv_w, mlstm_conv_b, mlstm_gate_b, mlstm_head_norm_g, mlstm_w_out, moe_w_group, moe_b_group, moe_w_expert, moe_b_expert, moe_w_gu, moe_w_down):
    bsz, seq, d = x.shape
    n = bsz * seq
    depth = ada_w.shape[0]
    mod = adaln(c, ada_w, ada_b)
    xf = x.reshape(n, d)
    for i in range(depth):
        jm = i // 2
        if i % 2 == 0:
            qk_cols = 2 * DA_HEADS * 2 * DA_HEAD_DIM
            tabs = _rope_tables(positions)
            w_in = _rope_pair_layout(attn_w_in[jm], qk_cols).astype(BF16)
            qkv = norm_matmul(xf, norm_mix_g[i], mod[i], w_in, seq, rope=(*tabs, qk_cols, qk_cols // 2))
            lam_init = 0.8 - 0.6 * math.exp(-0.3 * i)
            mixed = diff_attention(qkv.reshape(bsz, seq, -1), attn_lambda_q1[jm], attn_lambda_k1[jm],
                                   attn_lambda_q2[jm], attn_lambda_k2[jm], attn_head_norm_g[jm], lam_init)
            w_out = attn_w_out[jm]
        else:
            qk_cols = 2 * ML_HEADS * ML_QK_DIM
            main_cols = qk_cols + 2 * ML_HEADS * ML_V_DIM
            w_in = mlstm_w_in[jm]
            ngate = 2 * ML_HEADS
            w_gate = jnp.concatenate([w_in[:, main_cols:], jnp.zeros((d, LANES - ngate), F32)], axis=1)
            b_gate = jnp.concatenate([mlstm_gate_b[jm], jnp.zeros((LANES - ngate,), F32)]).reshape(1, LANES)
            proj, gates = norm_matmul(xf, norm_mix_g[i], mod[i], w_in[:, :main_cols].astype(BF16), seq,
                                      extra=(w_gate, b_gate))
            proj = proj.reshape(bsz, seq, main_cols)
            qk = conv_silu(proj, mlstm_conv_w[jm], mlstm_conv_b[jm])
            gates_c = gates.reshape(bsz, seq, LANES)
            gates_r = jnp.swapaxes(gates_c[:, :, :ngate], 1, 2)
            mixed = mlstm(qk, proj, gates_c, gates_r, mlstm_head_norm_g[jm])
            w_out = mlstm_w_out[jm]
        xf = matmul_res(mixed.reshape(n, -1), w_out.astype(BF16), xf, mod[i], seq, gate_row=2)
        xf = hier_moe_layer(xf, norm_ffn_g[i], mod[i], moe_w_group[i], moe_b_group[i], moe_w_expert[i],
                            moe_b_expert[i], moe_w_gu, moe_w_down, i, final_norm_g, seq,
                            final_norm=(i == depth - 1))
    return xf.reshape(bsz, seq, d)
```

```python
import functools
import math

import jax
import jax.numpy as jnp
from jax import lax
from jax.experimental import pallas as pl
from jax.experimental.pallas import tpu as pltpu

F32 = jnp.float32
BF16 = jnp.bfloat16
U32 = jnp.uint32
I32 = jnp.int32

NORM_EPS = 1e-6
ROPE_THETA = 500000.0
ATTN_CHUNK = 64
DA_HEADS = 8
DA_HEAD_DIM = 128
ROPE_DIM = 32
ML_HEADS = 8
ML_QK_DIM = 128
ML_V_DIM = 256
ML_CONV = 4
MOE_GROUPS = 4
MOE_PER_GROUP = 8
MOE_EXPERTS = 32
LANES = 128
NEG_BIG = -1e30
LOG2_E = math.log2(math.e)

VMEM_LIMIT = 56 * 1024 * 1024


def _cparams(sem):
    return pltpu.CompilerParams(dimension_semantics=sem, vmem_limit_bytes=VMEM_LIMIT)


def _split_hi_lo(a):
    hi = a.astype(BF16)
    lo = (a - hi.astype(F32)).astype(BF16)
    return hi, lo


def _dot3(a, w):
    ah, al = _split_hi_lo(a)
    wh, wl = _split_hi_lo(w)
    d = functools.partial(jnp.dot, preferred_element_type=F32)
    return d(ah, wh) + (d(ah, wl) + d(al, wh))


def _pack_bf16_pair(lo_f32, hi_f32):
    lo_bits = lax.bitcast_convert_type(lo_f32.astype(BF16).astype(F32), U32)
    hi_bits = lax.bitcast_convert_type(hi_f32.astype(BF16).astype(F32), U32)
    return hi_bits | (lo_bits >> 16)


def _unpack_bf16_pair(word):
    lo = lax.bitcast_convert_type(word << 16, F32)
    hi = lax.bitcast_convert_type(word & jnp.uint32(0xFFFF0000), F32)
    return lo, hi


ROW_TILE = 8


def _store_token_rows(ref, words):
    t = words.shape[0]
    for sub in range(ROW_TILE):
        ref[pl.ds(sub, t, stride=ROW_TILE), :] = words[:, sub * LANES:(sub + 1) * LANES]


def _load_token_rows(ref, sub, t):
    return ref[pl.ds(sub, t, stride=ROW_TILE), :]


def _rms_modulate(x, g, shift, scale):
    ms = jnp.mean(x * x, axis=-1, keepdims=True)
    y = x * lax.rsqrt(ms + NORM_EPS) * g
    return y * (1.0 + scale) + shift


def _adaln_kernel(c_ref, w_ref, b_ref, o_ref):
    c = c_ref[...]
    cond = c * jax.nn.sigmoid(c)
    acc = jnp.dot(cond.astype(BF16), w_ref[...].astype(BF16), preferred_element_type=F32)
    o_ref[...] = acc + b_ref[...]


def adaln(c, ada_w, ada_b, tn=1024):
    depth, d, n6 = ada_w.shape
    bsz = c.shape[0]
    rows = 8
    cp = jnp.zeros((rows, d), F32).at[:bsz].set(c)
    out = pl.pallas_call(
        _adaln_kernel,
        grid=(depth, n6 // tn),
        in_specs=[pl.BlockSpec((rows, d), lambda l, j: (0, 0)),
                  pl.BlockSpec((None, d, tn), lambda l, j: (l, 0, j)),
                  pl.BlockSpec((None, 1, tn), lambda l, j: (l, 0, j))],
        out_specs=pl.BlockSpec((None, rows, tn), lambda l, j: (l, 0, j)),
        out_shape=jax.ShapeDtypeStruct((depth, rows, n6), F32),
        compiler_params=_cparams(("arbitrary", "arbitrary")),
        name="adaln",
    )(cp, ada_w, ada_b.reshape(depth, 1, n6))
    return out[:, :bsz].reshape(depth, bsz, 6, d)


ROPE_ROWS = 64

def _norm_matmul_kernel(*refs, rope_tiles, q_tiles, has_extra, tn):
    if has_extra:
        x_ref, g_ref, mod_ref, w_ref, c_ref, s_ref, we_ref, be_ref, o_ref, oe_ref, h_scr, acc_scr = refs
    else:
        x_ref, g_ref, mod_ref, w_ref, c_ref, s_ref, o_ref, h_scr, acc_scr = refs
    j = pl.program_id(1)

    @pl.when(j == 0)
    def _():
        h = _rms_modulate(x_ref[...], g_ref[...], mod_ref[0:1, :], mod_ref[1:2, :])
        h_scr[...] = h.astype(BF16)
        if has_extra:
            oe_ref[...] = _dot3(h, we_ref[...]) + be_ref[...]

    acc = jnp.dot(h_scr[...], w_ref[...], preferred_element_type=F32)
    if rope_tiles == 0:
        o_ref[...] = acc.astype(o_ref.dtype)
    else:
        @pl.when(j < rope_tiles)
        def _():
            acc_scr[...] = acc
            scale = jnp.where(j < q_tiles, DA_HEAD_DIM ** -0.5 * LOG2_E, 1.0).astype(F32)

            def rotate_rows(r, c):
                rows = pl.ds(pl.multiple_of(r * ROPE_ROWS, ROPE_ROWS), ROPE_ROWS)
                cs, sn = c_ref[rows, :] * scale, s_ref[rows, :] * scale
                for m in range(tn // LANES):
                    t = acc_scr[rows, m * LANES:(m + 1) * LANES]
                    rot = t * cs + pltpu.roll(t, LANES // 2, 1) * sn
                    o_ref[rows, m * LANES:(m + 1) * LANES] = rot.astype(o_ref.dtype)
                return c

            lax.fori_loop(0, acc_scr.shape[0] // ROPE_ROWS, rotate_rows, 0, unroll=2)

        @pl.when(j >= rope_tiles)
        def _():
            o_ref[...] = acc.astype(o_ref.dtype)


def norm_matmul(x, g, mod, w, seq, rope=None, extra=None, tm=1024, tn=512):
    n, k = x.shape
    m = w.shape[1]
    tiles_per_seq = seq // tm
    has_extra = extra is not None
    if rope is None:
        dummy = jnp.zeros((8, LANES), F32)
        tabs = (dummy, dummy)
        tab_spec = pl.BlockSpec((8, LANES), lambda i, j: (0, 0))
        rope_tiles = q_tiles = 0
    else:
        tabs = rope[:2]
        tab_spec = pl.BlockSpec((tm, LANES), lambda i, j: (i, 0))
        rope_tiles, q_tiles = rope[2] // tn, rope[3] // tn
    in_specs = [pl.BlockSpec((tm, k), lambda i, j: (i, 0)),
                pl.BlockSpec((1, k), lambda i, j: (0, 0)),
                pl.BlockSpec((None, 6, k), lambda i, j: (i // tiles_per_seq, 0, 0)),
                pl.BlockSpec((k, tn), lambda i, j: (0, j)),
                tab_spec, tab_spec]
    args = [x, g.reshape(1, k), mod, w, *tabs]
    out_specs = pl.BlockSpec((tm, tn), lambda i, j: (i, j))
    out_shape = jax.ShapeDtypeStruct((n, m), BF16)
    if has_extra:
        we, be = extra
        in_specs += [pl.BlockSpec((k, LANES), lambda i, j: (0, 0)),
                     pl.BlockSpec((1, LANES), lambda i, j: (0, 0))]
        args += [we, be]
        out_specs = [out_specs, pl.BlockSpec((tm, LANES), lambda i, j: (i, 0))]
        out_shape = [out_shape, jax.ShapeDtypeStruct((n, LANES), F32)]
    return pl.pallas_call(
        functools.partial(_norm_matmul_kernel, rope_tiles=rope_tiles, q_tiles=q_tiles,
                          has_extra=has_extra, tn=tn),
        grid=(n // tm, m // tn),
        in_specs=in_specs,
        out_specs=out_specs,
        out_shape=out_shape,
        scratch_shapes=[pltpu.VMEM((tm, k), BF16),
                        pltpu.VMEM((tm, tn) if rope_tiles else (8, LANES), F32)],
        compiler_params=_cparams(("arbitrary", "arbitrary")),
        name="norm_matmul",
    )(*args)


def _matmul_res_kernel(a_ref, w_ref, res_ref, mod_ref, o_ref, *, gate_row):
    acc = jnp.dot(a_ref[...], w_ref[...], preferred_element_type=F32)
    o_ref[...] = res_ref[...] + mod_ref[gate_row:gate_row + 1, :] * acc


def matmul_res(a, w, res, mod, seq, gate_row, tm=512, tn=2048):
    n, k = a.shape
    m = w.shape[1]
    tiles_per_seq = seq // tm
    return pl.pallas_call(
        functools.partial(_matmul_res_kernel, gate_row=gate_row),
        grid=(n // tm, m // tn),
        in_specs=[pl.BlockSpec((tm, k), lambda i, j: (i, 0)),
                  pl.BlockSpec((k, tn), lambda i, j: (0, j)),
                  pl.BlockSpec((tm, tn), lambda i, j: (i, j)),
                  pl.BlockSpec((None, 6, tn), lambda i, j: (i // tiles_per_seq, 0, j))],
        out_specs=pl.BlockSpec((tm, tn), lambda i, j: (i, j)),
        out_shape=jax.ShapeDtypeStruct((n, m), F32),
        compiler_params=_cparams(("arbitrary", "arbitrary")),
        name="matmul_res",
    )(a, w, res, mod)


ATTN_GROUPS = (4, 2)


def _grouped_loop(n, body, groups):
    done = 0
    for group in groups:
        trips = (n - done) // group

        def grouped(gi, c, group=group, done=done):
            for t in range(group):
                c = body(done + gi * group + t, c)
            return c

        lax.fori_loop(0, trips, grouped, 0)
        done = done + trips * group
    lax.fori_loop(done, n, body, 0)


def _attn_kernel(q_ref, k_ref, v_ref, lq1_ref, lk1_ref, lq2_ref, lk2_ref, g_ref, o_ref,
                 s_scr, m_scr, l_scr, acc_scr, *, lam_init, tq, tk, seq):
    d = DA_HEAD_DIM
    nlane = tk // LANES
    ndiag = tq // tk
    lam = (jnp.exp(jnp.sum(lq1_ref[...] * lk1_ref[...], axis=-1, keepdims=True))
           - jnp.exp(jnp.sum(lq2_ref[...] * lk2_ref[...], axis=-1, keepdims=True)) + lam_init)
    shift = ATTN_CHUNK.bit_length() - 1
    row_chunk = jnp.right_shift(lax.broadcasted_iota(I32, (tq, tk), 0), shift)
    col_chunk = jnp.right_shift(lax.broadcasted_iota(I32, (tq, tk), 1), shift)
    nt = (((1,), (1,)), ((), ()))

    def lane_fold(a, op):
        part = a[:, 0:LANES]
        for cb in range(1, nlane):
            part = op(part, a[:, cb * LANES:(cb + 1) * LANES])
        return part

    def q_body(qi, _):
        qs = pl.multiple_of(qi * tq, tq)
        qm = (q_ref[pl.ds(qs, tq), 0:d], q_ref[pl.ds(qs, tq), d:2 * d])
        m_scr[...] = jnp.full_like(m_scr, NEG_BIG)

        def score(j, mask):
            ks = pl.multiple_of(j * tk, tk)
            for mp in range(2):
                k = k_ref[pl.ds(ks, tk), mp * d:(mp + 1) * d]
                s = lax.dot_general(qm[mp], k, nt, preferred_element_type=F32)
                if mask is not None:
                    s = jnp.where(mask, s, NEG_BIG)
                s_scr[mp, j] = s
                m_scr[mp] = jnp.maximum(m_scr[mp], lane_fold(s, jnp.maximum))

        def score_body(j, c):
            score(j, None)
            return c

        nfull = qi * ndiag
        _grouped_loop(nfull, score_body, ATTN_GROUPS)
        for t in range(ndiag):
            score(nfull + t, col_chunk + t * (tk // ATTN_CHUNK) <= row_chunk)
        m = [jnp.max(m_scr[mp], axis=-1, keepdims=True) for mp in range(2)]
        l_scr[...] = jnp.zeros_like(l_scr)
        acc_scr[...] = jnp.zeros_like(acc_scr)

        def pv_body(j, c):
            ks = pl.multiple_of(j * tk, tk)
            v = v_ref[pl.ds(ks, tk), :]
            for mp in range(2):
                p = jnp.exp2(s_scr[mp, j] - m[mp])
                l_scr[mp] += lane_fold(p, jnp.add)
                acc_scr[mp] += jnp.dot(p.astype(BF16), v, preferred_element_type=F32)
            return c

        _grouped_loop(nfull + ndiag, pv_body, ATTN_GROUPS)
        l = [jnp.sum(l_scr[mp], axis=-1, keepdims=True) for mp in range(2)]
        o = acc_scr[0] / l[0] - lam * (acc_scr[1] / l[1])
        ms = jnp.mean(o * o, axis=-1, keepdims=True)
        o = o * lax.rsqrt(ms + NORM_EPS) * g_ref[...] * (1.0 - lam_init)
        o_ref[pl.ds(qs, tq), :] = o.astype(o_ref.dtype)
        return 0

    lax.fori_loop(0, seq // tq, q_body, 0)


def diff_attention(qkv, lq1, lk1, lq2, lk2, head_g, lam_init, tq=512, tk=256):
    bsz, seq, _ = qkv.shape
    h, dv = DA_HEADS, 2 * DA_HEAD_DIM
    vec = lambda a: a.reshape(1, -1).astype(F32)
    small = lambda n: pl.BlockSpec((1, n), lambda b, hh: (0, 0))
    return pl.pallas_call(
        functools.partial(_attn_kernel, lam_init=lam_init, tq=tq, tk=tk, seq=seq),
        grid=(bsz, h),
        in_specs=[pl.BlockSpec((None, seq, dv), lambda b, hh: (b, 0, hh)),
                  pl.BlockSpec((None, seq, dv), lambda b, hh: (b, 0, h + hh)),
                  pl.BlockSpec((None, seq, dv), lambda b, hh: (b, 0, 2 * h + hh)),
                  small(DA_HEAD_DIM), small(DA_HEAD_DIM), small(DA_HEAD_DIM), small(DA_HEAD_DIM),
                  small(dv)],
        out_specs=pl.BlockSpec((None, seq, dv), lambda b, hh: (b, 0, hh)),
        out_shape=jax.ShapeDtypeStruct((bsz, seq, h * dv), BF16),
        scratch_shapes=[pltpu.VMEM((2, seq // tk, tq, tk), F32), pltpu.VMEM((2, tq, LANES), F32),
                        pltpu.VMEM((2, tq, LANES), F32), pltpu.VMEM((2, tq, dv), F32)],
        compiler_params=_cparams(("arbitrary", "arbitrary")),
        name="diff_attention",
    )(qkv, qkv, qkv, vec(lq1), vec(lk1), vec(lq2), vec(lk2), vec(head_g))


def _conv_silu_kernel(x_ref, w_ref, b_ref, o_ref, *, k_tile0):
    j = pl.program_id(1)
    scale = jnp.where(j >= k_tile0, ML_QK_DIM ** -0.5, 1.0).astype(F32)
    taps = [w_ref[ML_CONV - 1 - s:ML_CONV - s, :] for s in range(ML_CONV)]

    def conv(x, shifted):
        y = x * taps[0] + b_ref[...]
        for s in range(1, ML_CONV):
            y = y + shifted(x, s) * taps[s]
        return (y * jax.nn.sigmoid(y) * scale).astype(o_ref.dtype)

    o_ref[...] = conv(x_ref[...].astype(F32), lambda x, s: pltpu.roll(x, s, 0))
    head = ROW_TILE
    row = lax.broadcasted_iota(I32, (head, x_ref.shape[1]), 0)
    o_ref[0:head, :] = conv(x_ref[0:head, :].astype(F32),
                            lambda x, s: jnp.where(row >= s, pltpu.roll(x, s, 0), 0.0))


def conv_silu(proj, conv_w, conv_b, tc=128):
    bsz, seq, _ = proj.shape
    cols = conv_w.shape[1]
    return pl.pallas_call(
        functools.partial(_conv_silu_kernel, k_tile0=(cols // 2) // tc),
        grid=(bsz, cols // tc),
        in_specs=[pl.BlockSpec((None, seq, tc), lambda b, j: (b, 0, j)),
                  pl.BlockSpec((ML_CONV, tc), lambda b, j: (0, j)),
                  pl.BlockSpec((1, tc), lambda b, j: (0, j))],
        out_specs=pl.BlockSpec((None, seq, tc), lambda b, j: (b, 0, j)),
        out_shape=jax.ShapeDtypeStruct((bsz, seq, cols), BF16),
        compiler_params=_cparams(("arbitrary", "arbitrary")),
        name="conv_silu",
    )(proj, conv_w, conv_b.reshape(1, cols))


def _mlstm_kernel(q_ref, k_ref, v_ref, op_ref, gc_ref, gr_ref, hg_ref, o_ref,
                  cx_scr, m_scr, *, chunk):
    c = pl.program_id(1)
    nh, dqk, dv = ML_HEADS, ML_QK_DIM, ML_V_DIM

    @pl.when(c == 0)
    def _():
        cx_scr[...] = jnp.zeros_like(cx_scr)
        m_scr[...] = jnp.zeros_like(m_scr)

    gc = gc_ref[...]
    gr = gr_ref[...]
    lf_c = jax.nn.log_sigmoid(gc)
    lf_r = jax.nn.log_sigmoid(gr)
    r_i = lax.broadcasted_iota(I32, (chunk, chunk), 0)
    c_i = lax.broadcasted_iota(I32, (chunk, chunk), 1)
    causal = c_i <= r_i
    tril = causal.astype(F32)
    triu = (r_i <= c_i).astype(F32)
    b_c = _dot3(tril, lf_c)
    b_r = _dot3(lf_r, triu)
    nt = (((1,), (1,)), ((), ()))
    tn_ = (((0,), (0,)), ((), ()))
    ones_l = jnp.ones((chunk, LANES), BF16)
    ones_v = jnp.ones((dv, LANES), BF16)
    wide = lambda a: jnp.concatenate([a] * (dv // LANES), axis=1)

    for h in range(nh):
        q = q_ref[:, h * dqk:(h + 1) * dqk]
        k = k_ref[:, h * dqk:(h + 1) * dqk]
        v_ext = jnp.concatenate([v_ref[:, h * dv:(h + 1) * dv], ones_l], axis=1)
        bc = b_c[:, nh + h:nh + h + 1]
        br = b_r[nh + h:nh + h + 1, :]
        ig_c = gc[:, h:h + 1]
        ig_r = gr[h:h + 1, :]
        m_prev = m_scr[h:h + 1, :]
        dmat = jnp.where(causal, bc - br + ig_r, NEG_BIG)
        inter = bc + m_prev
        m_t = jnp.maximum(inter, jnp.max(dmat, axis=-1, keepdims=True))
        w = jnp.exp(dmat - m_t)
        s = lax.dot_general(q, k, nt, preferred_element_type=F32) * w
        decay = jnp.exp(inter - m_t)
        cx = cx_scr[h]
        tot = (jnp.dot(s.astype(BF16), v_ext, preferred_element_type=F32)
               + decay * jnp.dot(q, cx.astype(BF16), preferred_element_type=F32))
        den = jnp.maximum(jnp.abs(tot[:, dv:]), jnp.exp(-m_t))
        hh = tot[:, :dv] / wide(den)
        b_last = bc[chunk - 1:chunk, :]
        g = b_last - bc + ig_c
        m_new = jnp.maximum(b_last + m_prev, jnp.max(g, axis=0, keepdims=True))
        carry_decay = jnp.exp(b_last + m_prev - m_new)
        wg = jnp.exp(g - m_new)
        wv = (wg * v_ext.astype(F32)).astype(BF16)
        cx_scr[h] = carry_decay * cx + lax.dot_general(k, wv, tn_, preferred_element_type=F32)
        m_scr[h:h + 1, :] = m_new
        ms = jnp.dot((hh * hh).astype(BF16), ones_v, preferred_element_type=F32) * (1.0 / dv)
        hn = hh * wide(lax.rsqrt(ms + NORM_EPS)) * hg_ref[:, h * dv:(h + 1) * dv]
        og = jax.nn.sigmoid(op_ref[:, h * dv:(h + 1) * dv].astype(F32))
        o_ref[:, h * dv:(h + 1) * dv] = (og * hn).astype(o_ref.dtype)


def mlstm(qk, proj, gates_c, gates_r, head_g, chunk=128):
    bsz, seq, _ = qk.shape
    nh = ML_HEADS
    qw, vw = nh * ML_QK_DIM, nh * ML_V_DIM
    v_blk = (2 * qw) // vw
    return pl.pallas_call(
        functools.partial(_mlstm_kernel, chunk=chunk),
        grid=(bsz, seq // chunk),
        in_specs=[pl.BlockSpec((None, chunk, qw), lambda b, c: (b, c, 0)),
                  pl.BlockSpec((None, chunk, qw), lambda b, c: (b, c, 1)),
                  pl.BlockSpec((None, chunk, vw), lambda b, c: (b, c, v_blk)),
                  pl.BlockSpec((None, chunk, vw), lambda b, c: (b, c, v_blk + 1)),
                  pl.BlockSpec((None, chunk, LANES), lambda b, c: (b, c, 0)),
                  pl.BlockSpec((None, 2 * nh, chunk), lambda b, c: (b, 0, c)),
                  pl.BlockSpec((1, vw), lambda b, c: (0, 0))],
        out_specs=pl.BlockSpec((None, chunk, vw), lambda b, c: (b, c, 0)),
        out_shape=jax.ShapeDtypeStruct((bsz, seq, vw), BF16),
        scratch_shapes=[pltpu.VMEM((nh, ML_QK_DIM, ML_V_DIM + LANES), F32),
                        pltpu.VMEM((nh, 1), F32)],
        compiler_params=_cparams(("arbitrary", "arbitrary")),
        name="mlstm",
    )(qk, qk, proj, proj, gates_c, gates_r, head_g.reshape(1, vw))


def _router_kernel(x_ref, g_ref, mod_ref, w_ref, b_ref, hp_ref, rt_ref, cnt_ref, run_scr):
    i = pl.program_id(0)

    @pl.when(i == 0)
    def _():
        run_scr[...] = jnp.zeros_like(run_scr)

    h = _rms_modulate(x_ref[...], g_ref[...], mod_ref[3:4, :], mod_ref[4:5, :])
    half = h.shape[1] // 2
    _store_token_rows(hp_ref, _pack_bf16_pair(h[:, :half], h[:, half:]))
    logits = _dot3(h, w_ref[...]) + b_ref[...]
    tm = logits.shape[0]
    lane =lax.broadcasted_iota(I32, logits.shape, 1).astype(F32)
    ng = float(MOE_GROUPS)
    is_g = lane < ng
    gl = jnp.where(is_g, logits, NEG_BIG)
    gmax = jnp.max(gl, axis=-1, keepdims=True)
    grp = jnp.min(jnp.where(gl == gmax, lane, float(LANES)), axis=-1, keepdims=True)
    p_group = 1.0 / jnp.sum(jnp.where(is_g, jnp.exp(gl - gmax), 0.0), axis=-1, keepdims=True)
    lo = ng + float(MOE_PER_GROUP) * grp
    el = jnp.where((lane >= lo) & (lane < lo + float(MOE_PER_GROUP)), logits, NEG_BIG)
    v1 = jnp.max(el, axis=-1, keepdims=True)
    i1 = jnp.min(jnp.where(el == v1, lane, float(LANES)), axis=-1, keepdims=True)
    el2 = jnp.where(lane == i1, NEG_BIG, el)
    v2 = jnp.max(el2, axis=-1, keepdims=True)
    i2 = jnp.min(jnp.where(el2 == v2, lane, float(LANES)), axis=-1, keepdims=True)
    ex = jnp.exp(v2 - v1)
    w1 = p_group / (1.0 + ex)
    w2 = p_group * (ex / (1.0 + ex))
    oh1 = (lane == i1).astype(F32)
    oh2 = (lane == i2).astype(F32)
    oh = oh1 + oh2
    r_i = lax.broadcasted_iota(I32, (tm, tm), 0)
    c_i = lax.broadcasted_iota(I32, (tm, tm), 1)
    before = (c_i < r_i).astype(BF16)
    prior = jnp.dot(before, oh.astype(BF16), preferred_element_type=F32) + run_scr[...]
    rank1 = jnp.sum(oh1 * prior, axis=-1, keepdims=True)
    rank2 = jnp.sum(oh2 * prior, axis=-1, keepdims=True)
    run_scr[...] += jnp.sum(oh, axis=0, keepdims=True)
    cnt_ref[...] = run_scr[...]
    rt_ref[...] = jnp.where(lane == 0.0, i1 - ng,
                  jnp.where(lane == 1.0, i2 - ng,
                  jnp.where(lane == 2.0, w1,
                  jnp.where(lane == 3.0, w2,
                  jnp.where(lane == 4.0, rank1,
                  jnp.where(lane == 5.0, rank2, 0.0))))))


def moe_router(x, g, mod, w_r, b_r, seq, tm=512):
    n, d = x.shape
    tiles_per_seq = seq // tm
    return pl.pallas_call(
        _router_kernel,
        grid=(n // tm,),
        in_specs=[pl.BlockSpec((tm, d), lambda i: (i, 0)),
                  pl.BlockSpec((1, d), lambda i: (0, 0)),
                  pl.BlockSpec((None, 6, d), lambda i: (i // tiles_per_seq, 0, 0)),
                  pl.BlockSpec((d, LANES), lambda i: (0, 0)),
                  pl.BlockSpec((1, LANES), lambda i: (0, 0))],
        out_specs=[pl.BlockSpec((tm * ROW_TILE, LANES), lambda i: (i, 0)),
                   pl.BlockSpec((tm, LANES), lambda i: (i, 0)),
                   pl.BlockSpec((1, LANES), lambda i: (0, 0))],
        out_shape=[jax.ShapeDtypeStruct((n * ROW_TILE, LANES), U32),
                   jax.ShapeDtypeStruct((n, LANES), F32),
                   jax.ShapeDtypeStruct((1, LANES), F32)],
        scratch_shapes=[pltpu.VMEM((1, LANES), F32)],
        compiler_params=_cparams(("arbitrary",)),
        name="moe_router",
    )(x, g.reshape(1, d), mod, w_r, b_r)


def _dest_kernel(rt_ref, ps_ref, o_ref):
    rt = rt_ref[...]
    lane = lax.broadcasted_iota(I32, rt.shape, 1).astype(F32)
    ng = float(MOE_GROUPS)
    ps = ps_ref[...]
    d1 = jnp.sum(jnp.where(lane == rt[:, 0:1] + ng, ps, 0.0), axis=-1, keepdims=True) + rt[:, 4:5]
    d2 = jnp.sum(jnp.where(lane == rt[:, 1:2] + ng, ps, 0.0), axis=-1, keepdims=True) + rt[:, 5:6]
    o_ref[...] = jnp.where(lane == 0.0, d1, jnp.where(lane == 1.0, d2, 0.0)).astype(I32)


def moe_dest(route, pad_start_lanes, tm=2048):
    n = route.shape[0]
    return pl.pallas_call(
        _dest_kernel,
        grid=(n // tm,),
        in_specs=[pl.BlockSpec((tm, LANES), lambda i: (i, 0)),
                  pl.BlockSpec((1, LANES), lambda i: (0, 0))],
        out_specs=pl.BlockSpec((tm, LANES), lambda i: (i, 0)),
        out_shape=jax.ShapeDtypeStruct((n, LANES), I32),
        compiler_params=_cparams(("arbitrary",)),
        name="moe_dest",
    )(route, pad_start_lanes)


DMA_UNROLL = 8


def _wait_rows(src_rows, dst_rows, sem, copies):
    for _ in range(copies):
        pltpu.make_async_copy(src_rows, dst_rows, sem).wait()


def _dispatch_kernel(zb_ref, dest_ref, hp_ref, xin_ref, zero_scr, ring, sems, *, tt, tm, nb, ntiles):
    i = pl.program_id(0)
    slot = i % 2
    blk_rows = tm * ROW_TILE
    tile_rows = tt * ROW_TILE

    @pl.when(i == 0)
    def _():
        zero_scr[...] = jnp.zeros_like(zero_scr)

        def zero_copy(blk):
            return pltpu.make_async_copy(zero_scr, xin_ref.at[pl.ds(pl.multiple_of(blk * blk_rows, blk_rows), blk_rows)],
                                         sems.at[2])

        def start(blk, c):
            @pl.when(zb_ref[blk] == 1)
            def _():
                zero_copy(blk).start()
            return c

        def wait(blk, c):
            @pl.when(zb_ref[blk] == 1)
            def _():
                zero_copy(blk).wait()
            return c

        lax.fori_loop(0, nb, start, 0)
        lax.fori_loop(0, nb, wait, 0)

    ring[slot] = hp_ref[...]

    def start_rows(r, c):
        src = ring.at[slot, pl.ds(pl.multiple_of(r * ROW_TILE, ROW_TILE), ROW_TILE)]
        for kk in range(2):
            row = pl.multiple_of(dest_ref[0, 0, 2 * r + kk] * ROW_TILE, ROW_TILE)
            pltpu.make_async_copy(src, xin_ref.at[pl.ds(row, ROW_TILE)], sems.at[slot]).start()
        return c

    lax.fori_loop(0, tt, start_rows, 0, unroll=DMA_UNROLL)
    whole = xin_ref.at[pl.ds(0, tile_rows)]

    @pl.when(i > 0)
    def _():
        _wait_rows(ring.at[1 - slot], whole, sems.at[1 - slot], 2)

    @pl.when(i == ntiles - 1)
    def _():
        _wait_rows(ring.at[slot], whole, sems.at[slot], 2)


def moe_dispatch(hpack, dest, zero_blocks, rows, tm, tt=256):
    n = hpack.shape[0] // ROW_TILE
    grid_spec = pltpu.PrefetchScalarGridSpec(
        num_scalar_prefetch=1,
        grid=(n // tt,),
        in_specs=[pl.BlockSpec((1, 1, 2 * tt), lambda i, zb: (i, 0, 0), memory_space=pltpu.SMEM),
                  pl.BlockSpec((tt * ROW_TILE, LANES), lambda i, zb: (i, 0))],
        out_specs=pl.BlockSpec(memory_space=pl.ANY),
        scratch_shapes=[pltpu.VMEM((tm * ROW_TILE, LANES), U32), pltpu.VMEM((2, tt * ROW_TILE, LANES), U32),
                        pltpu.SemaphoreType.DMA((3,))],
    )
    return pl.pallas_call(
        functools.partial(_dispatch_kernel, tt=tt, tm=tm, nb=rows // tm, ntiles=n // tt),
        grid_spec=grid_spec,
        out_shape=jax.ShapeDtypeStruct((rows * ROW_TILE, LANES), U32),
        compiler_params=_cparams(("arbitrary",)),
        name="moe_dispatch",
    )(zero_blocks, dest.reshape(n // tt, 1, 2 * tt), hpack)


def _expert_kernel(be_ref, first_ref, nxt_ref, nu_ref, x_ref, wgu_hbm, wd_hbm, y_ref,
                   wgu_bf, wd_bf, stg_gu, stg_d, xs_scr, acc_scr, sems, *, layer, th, cr):
    i = pl.program_id(0)

    def fetch(e):
        return (pltpu.make_async_copy(wgu_hbm.at[layer, e], stg_gu, sems.at[0]),
                pltpu.make_async_copy(wd_hbm.at[layer, e], stg_d, sems.at[1]))

    @pl.when(i < nu_ref[0])
    def _():
        @pl.when(first_ref[i] == 1)
        def _():
            @pl.when(i == 0)
            def _():
                for cp in fetch(be_ref[i]):
                    cp.start()

            for cp in fetch(be_ref[i]):
                cp.wait()

            def cast_rows(src, dst):
                def body(r, c):
                    rs = pl.multiple_of(r * cr, cr)
                    dst[pl.ds(rs, cr), :] = src[pl.ds(rs, cr), :].astype(BF16)
                    return c
                lax.fori_loop(0, src.shape[0] // cr, body, 0)

            cast_rows(stg_gu, wgu_bf)
            cast_rows(stg_d, wd_bf)

            @pl.when(nxt_ref[i] >= 0)
            def _():
                for cp in fetch(nxt_ref[i]):
                    cp.start()

        tm = xs_scr.shape[0]
        half = xs_scr.shape[1] // 2
        hid = wd_bf.shape[0]
        for sub in range(ROW_TILE):
            lo, hi = _unpack_bf16_pair(_load_token_rows(x_ref, sub, tm))
            xs_scr[:, sub * LANES:(sub + 1) * LANES] = lo.astype(BF16)
            xs_scr[:, half + sub * LANES:half + (sub + 1) * LANES] = hi.astype(BF16)
        xs = xs_scr[...]
        for c in range(hid // th):
            gt = jnp.dot(xs, wgu_bf[:, c * th:(c + 1) * th], preferred_element_type=F32)
            up = jnp.dot(xs, wgu_bf[:, hid + c * th:hid + (c + 1) * th], preferred_element_type=F32)
            act = (gt * jax.nn.sigmoid(gt) * up).astype(BF16)
            part = jnp.dot(act, wd_bf[c * th:(c + 1) * th, :], preferred_element_type=F32)
            if c == 0:
                acc_scr[...] = part
            else:
                acc_scr[...] += part
        y = acc_scr[...]
        _store_token_rows(y_ref, _pack_bf16_pair(y[:, :half], y[:, half:]))


def moe_experts(xin, sched, wgu_all, wd_all, layer, tm, th=256, cr=256):
    _, _, d, hid2 = wgu_all.shape
    hid = hid2 // 2
    nb = xin.shape[0] // (tm * ROW_TILE)
    block_e, first, nxt, n_used = sched

    def blk(i, be, fi, nx, nu):
        return (jnp.minimum(i, nu[0] - 1), 0)

    grid_spec = pltpu.PrefetchScalarGridSpec(
        num_scalar_prefetch=4,
        grid=(nb,),
        in_specs=[pl.BlockSpec((tm * ROW_TILE, LANES), blk),
                  pl.BlockSpec(memory_space=pl.ANY),
                  pl.BlockSpec(memory_space=pl.ANY)],
        out_specs=pl.BlockSpec((tm * ROW_TILE, LANES), blk),
        scratch_shapes=[pltpu.VMEM((d, hid2), BF16), pltpu.VMEM((hid, d), BF16),
                        pltpu.VMEM((d, hid2), F32), pltpu.VMEM((hid, d), F32),
                        pltpu.VMEM((tm, d), BF16), pltpu.VMEM((tm, d), F32),
                        pltpu.SemaphoreType.DMA((2,))],
    )
    return pl.pallas_call(
        functools.partial(_expert_kernel, layer=layer, th=th, cr=cr),
        grid_spec=grid_spec,
        out_shape=jax.ShapeDtypeStruct(xin.shape, U32),
        input_output_aliases={4: 0},
        compiler_params=_cparams(("arbitrary",)),
        name="moe_experts",
    )(block_e, first, nxt, n_used, xin, wgu_all, wd_all)


def _combine_kernel(dcur_ref, dnxt_ref, x_ref, rt_ref, mod_ref, fg_ref, y_ref, o_ref, ya, yb, sems,
                    *, tt, ntiles, final_norm):
    i = pl.program_id(0)
    slot = i % 2

    def issue(dref, sl):
        def body(r, c):
            dst = pl.ds(pl.multiple_of(r * ROW_TILE, ROW_TILE), ROW_TILE)
            for kk, buf in enumerate((ya, yb)):
                row = pl.multiple_of(dref[0, 0, 2 * r + kk] * ROW_TILE, ROW_TILE)
                pltpu.make_async_copy(y_ref.at[pl.ds(row, ROW_TILE)], buf.at[sl, dst], sems.at[sl]).start()
            return c
        lax.fori_loop(0, tt, body, 0, unroll=DMA_UNROLL)

    @pl.when(i == 0)
    def _():
        issue(dcur_ref, 0)

    @pl.when(i + 1 < ntiles)
    def _():
        issue(dnxt_ref, 1 - slot)

    _wait_rows(y_ref.at[pl.ds(0, tt * ROW_TILE)], ya.at[slot], sems.at[slot], 2)
    half = x_ref.shape[1] // 2
    w1 = rt_ref[:, 2:3]
    w2 = rt_ref[:, 3:4]
    ssq = jnp.zeros((tt, 1), F32)
    for sub in range(ROW_TILE):
        a_lo, a_hi = _unpack_bf16_pair(_load_token_rows(ya.at[slot], sub, tt))
        b_lo, b_hi = _unpack_bf16_pair(_load_token_rows(yb.at[slot], sub, tt))
        for base, a, b in ((sub * LANES, a_lo, b_lo), (half + sub * LANES, a_hi, b_hi)):
            cols = slice(base, base + LANES)
            out = x_ref[:, cols] + mod_ref[5:6, cols] * (a * w1 + b * w2)
            o_ref[:, cols] = out
            if final_norm:
                ssq = ssq + jnp.sum(out * out, axis=-1, keepdims=True)
    if final_norm:
        r = lax.rsqrt(ssq / (2 * half) + NORM_EPS)
        o_ref[...] = o_ref[...] * r * fg_ref[...]


def moe_combine(x, y, dest, route, mod, final_g, seq, final_norm, tt=256):
    n, d = x.shape
    tiles_per_seq = seq // tt
    ntiles = n // tt
    dest3 = dest.reshape(ntiles, 1, 2 * tt)
    return pl.pallas_call(
        functools.partial(_combine_kernel, tt=tt, ntiles=ntiles, final_norm=final_norm),
        grid=(ntiles,),
        in_specs=[pl.BlockSpec((1, 1, 2 * tt), lambda i: (i, 0, 0), memory_space=pltpu.SMEM),
                  pl.BlockSpec((1, 1, 2 * tt), lambda i: (jnp.minimum(i + 1, ntiles - 1), 0, 0),
                               memory_space=pltpu.SMEM),
                  pl.BlockSpec((tt, d), lambda i: (i, 0)),
                  pl.BlockSpec((tt, LANES), lambda i: (i, 0)),
                  pl.BlockSpec((None, 6, d), lambda i: (i // tiles_per_seq, 0, 0)),
                  pl.BlockSpec((1, d), lambda i: (0, 0)),
                  pl.BlockSpec(memory_space=pl.ANY)],
        out_specs=pl.BlockSpec((tt, d), lambda i: (i, 0)),
        out_shape=jax.ShapeDtypeStruct((n, d), F32),
        scratch_shapes=[pltpu.VMEM((2, tt * ROW_TILE, LANES), U32), pltpu.VMEM((2, tt * ROW_TILE, LANES), U32),
                        pltpu.SemaphoreType.DMA((2,))],
        compiler_params=_cparams(("arbitrary",)),
        name="moe_combine",
    )(dest3, dest3, x, route, mod, final_g.reshape(1, d), y)


def _expert_schedule(counts, tm, nb):
    ne = counts.shape[0]
    ids = jnp.arange(ne, dtype=I32)
    padded = ((counts + tm - 1) // tm) * tm
    pad_end = jnp.sum(jnp.where(ids[None, :] <= ids[:, None], padded[None, :], 0), axis=1)
    pad_start = pad_end - padded
    total = jnp.sum(padded)
    n_used = total // tm
    blk0 = jnp.arange(nb, dtype=I32) * tm
    block_e = jnp.minimum(jnp.sum((pad_end[None, :] <= blk0[:, None]).astype(I32), axis=1), ne - 1)
    onehot = block_e[:, None] == ids[None, :]
    pick = lambda v: jnp.sum(jnp.where(onehot, v[None, :], 0), axis=1)
    first = (blk0 == pick(pad_start)).astype(I32)
    later = jnp.min(jnp.where((ids[None, :] > ids[:, None]) & (counts[None, :] > 0), ids[None, :], ne), axis=1)
    nxt = pick(jnp.where(later < ne, later, -1))
    has_padding = (blk0 + tm == pick(pad_end)) & (pick(counts % tm) != 0)
    zero_blocks = ((blk0 >= total) | has_padding).astype(I32)
    return pad_start, zero_blocks, (block_e, first, nxt.astype(I32), n_used.astype(I32).reshape(1))


def hier_moe_layer(x, g, mod, w_group, b_group, w_expert, b_expert, wgu_all, wd_all, layer, final_g, seq,
                   final_norm, tm=256):
    n, d = x.shape
    ng, ne = w_group.shape[1], w_expert.shape[1]
    w_r = jnp.concatenate([w_group, w_expert, jnp.zeros((d, LANES - ng - ne), F32)], axis=1)
    b_r = jnp.concatenate([b_group, b_expert, jnp.zeros((LANES - ng - ne,), F32)]).reshape(1, LANES)
    hpack, route, cnt = moe_router(x, g, mod, w_r, b_r, seq)
    counts = cnt[0, ng:ng + ne].astype(I32)
    rows = 2 * n + ne * tm
    pad_start, zero_blocks, sched = _expert_schedule(counts, tm, rows // tm)
    ps_lanes = jnp.concatenate([jnp.zeros((ng,), F32), pad_start.astype(F32),
                                jnp.zeros((LANES - ng - ne,), F32)]).reshape(1, LANES)
    dest = moe_dest(route, ps_lanes, tm=min(2048, n))[:, :2].reshape(-1)
    xin = moe_dispatch(hpack, dest, zero_blocks, rows, tm)
    y = moe_experts(xin, sched, wgu_all, wd_all, layer, tm)
    return moe_combine(x, y, dest, route, mod, final_g, seq, final_norm)


def _rope_tables(positions):
    half = ROPE_DIM // 2
    inv_freq = ROPE_THETA ** (-jnp.arange(half, dtype=F32) * 2.0 / ROPE_DIM)
    gap = jnp.zeros((LANES // 2 - half,), F32)
    freq = jnp.concatenate([-inv_freq, gap, inv_freq, gap])
    ang = positions.astype(F32).reshape(-1, 1) * freq[None, :]
    return jnp.cos(ang), jnp.sin(ang)


def _weight_prep_kernel(w_ref, o_ref, *, pair_tiles, tn):
    j = pl.program_id(0)
    half = ROPE_DIM // 2
    mid = LANES // 2

    @pl.when(j >= pair_tiles)
    def _():
        o_ref[...] = w_ref[...].astype(BF16)

    if pair_tiles:
        @pl.when(j < pair_tiles)
        def _():
            lane = lax.broadcasted_iota(I32, (w_ref.shape[0], LANES), 1)
            for m in range(tn // LANES):
                t = w_ref[:, m * LANES:(m + 1) * LANES]
                up = pltpu.roll(t, LANES - half, 1)
                down = pltpu.roll(t, mid - half, 1)
                new = jnp.where(lane < half, t, jnp.where(lane < mid, up, jnp.where(lane < mid + half, down, t)))
                o_ref[:, m * LANES:(m + 1) * LANES] = new.astype(BF16)


def weight_prep(w, cols, pair_cols=0, tn=512):
    k = w.shape[0]
    return pl.pallas_call(
        functools.partial(_weight_prep_kernel, pair_tiles=pair_cols // tn, tn=tn),
        grid=(cols // tn,),
        in_specs=[pl.BlockSpec((k, tn), lambda j: (0, j))],
        out_specs=pl.BlockSpec((k, tn), lambda j: (0, j)),
        out_shape=jax.ShapeDtypeStruct((k, cols), BF16),
        compiler_params=_cparams(("arbitrary",)),
        name="weight_prep",
    )(w)


def kernel(x, c, positions, ada_w, ada_b, norm_mix_g, norm_ffn_g, final_norm_g, attn_w_in, attn_w_out, attn_lambda_q1, attn_lambda_k1, attn_lambda_q2, attn_lambda_k2, attn_head_norm_g, mlstm_w_in, mlstm_conv_w, mlstm_conv_b, mlstm_gate_b, mlstm_head_norm_g, mlstm_w_out, moe_w_group, moe_b_group, moe_w_expert, moe_b_expert, moe_w_gu, moe_w_down):
    bsz, seq, d = x.shape
    n = bsz * seq
    depth = ada_w.shape[0]
    mod = adaln(c, ada_w, ada_b)
    xf = x.reshape(n, d)
    for i in range(depth):
        jm = i // 2
        if i % 2 == 0:
            qk_cols = 2 * DA_HEADS * 2 * DA_HEAD_DIM
            tabs = _rope_tables(positions)
            w_in = weight_prep(attn_w_in[jm], attn_w_in.shape[2], pair_cols=qk_cols)
            qkv = norm_matmul(xf, norm_mix_g[i], mod[i], w_in, seq, rope=(*tabs, qk_cols, qk_cols // 2))
            lam_init = 0.8 - 0.6 * math.exp(-0.3 * i)
            mixed = diff_attention(qkv.reshape(bsz, seq, -1), attn_lambda_q1[jm], attn_lambda_k1[jm],
                                   attn_lambda_q2[jm], attn_lambda_k2[jm], attn_head_norm_g[jm], lam_init)
            w_out = attn_w_out[jm]
        else:
            qk_cols = 2 * ML_HEADS * ML_QK_DIM
            main_cols = qk_cols + 2 * ML_HEADS * ML_V_DIM
            w_in = mlstm_w_in[jm]
            ngate = 2 * ML_HEADS
            w_gate = jnp.concatenate([w_in[:, main_cols:], jnp.zeros((d, LANES - ngate), F32)], axis=1)
            b_gate = jnp.concatenate([mlstm_gate_b[jm], jnp.zeros((LANES - ngate,), F32)]).reshape(1, LANES)
            proj, gates = norm_matmul(xf, norm_mix_g[i], mod[i], weight_prep(w_in, main_cols), seq,
                                      extra=(w_gate, b_gate))
            proj = proj.reshape(bsz, seq, main_cols)
            qk = conv_silu(proj, mlstm_conv_w[jm], mlstm_conv_b[jm])
            gates_c = gates.reshape(bsz, seq, LANES)
            gates_r = jnp.swapaxes(gates_c[:, :, :ngate], 1, 2)
            mixed = mlstm(qk, proj, gates_c, gates_r, mlstm_head_norm_g[jm])
            w_out = mlstm_w_out[jm]
        xf = matmul_res(mixed.reshape(n, -1), w_out.astype(BF16), xf, mod[i], seq, gate_row=2)
        xf = hier_moe_layer(xf, norm_ffn_g[i], mod[i], moe_w_group[i], moe_b_group[i], moe_w_expert[i],
                            moe_b_expert[i], moe_w_gu, moe_w_down, i, final_norm_g, seq,
                            final_norm=(i == depth - 1))
    return xf.reshape(bsz, seq, d)
```

```python
import functools
import math

import jax
import jax.numpy as jnp
from jax import lax
from jax.experimental import pallas as pl
from jax.experimental.pallas import tpu as pltpu

F32 = jnp.float32
BF16 = jnp.bfloat16
U32 = jnp.uint32
I32 = jnp.int32

NORM_EPS = 1e-6
ROPE_THETA = 500000.0
ATTN_CHUNK = 64
DA_HEADS = 8
DA_HEAD_DIM = 128
ROPE_DIM = 32
ML_HEADS = 8
ML_QK_DIM = 128
ML_V_DIM = 256
ML_CONV = 4
MOE_GROUPS = 4
MOE_PER_GROUP = 8
MOE_EXPERTS = 32
LANES = 128
NEG_BIG = -1e30
LOG2_E = math.log2(math.e)

VMEM_LIMIT = 56 * 1024 * 1024


def _cparams(sem):
    return pltpu.CompilerParams(dimension_semantics=sem, vmem_limit_bytes=VMEM_LIMIT)


def _split_hi_lo(a):
    hi = a.astype(BF16)
    lo = (a - hi.astype(F32)).astype(BF16)
    return hi, lo


def _dot3(a, w):
    ah, al = _split_hi_lo(a)
    wh, wl = _split_hi_lo(w)
    d = functools.partial(jnp.dot, preferred_element_type=F32)
    return d(ah, wh) + (d(ah, wl) + d(al, wh))


def _pack_bf16_pair(lo_f32, hi_f32):
    lo_bits = lax.bitcast_convert_type(lo_f32.astype(BF16).astype(F32), U32)
    hi_bits = lax.bitcast_convert_type(hi_f32.astype(BF16).astype(F32), U32)
    return hi_bits | (lo_bits >> 16)


def _unpack_bf16_pair(word):
    lo = lax.bitcast_convert_type(word << 16, F32)
    hi = lax.bitcast_convert_type(word & jnp.uint32(0xFFFF0000), F32)
    return lo, hi


ROW_TILE = 8


def _store_token_rows(ref, words):
    t = words.shape[0]
    for sub in range(ROW_TILE):
        ref[pl.ds(sub, t, stride=ROW_TILE), :] = words[:, sub * LANES:(sub + 1) * LANES]


def _load_token_rows(ref, sub, t):
    return ref[pl.ds(sub, t, stride=ROW_TILE), :]


def _rms_modulate(x, g, shift, scale):
    ms = jnp.mean(x * x, axis=-1, keepdims=True)
    y = x * lax.rsqrt(ms + NORM_EPS) * g
    return y * (1.0 + scale) + shift


def _adaln_kernel(c_ref, w_ref, b_ref, o_ref):
    c = c_ref[...]
    cond = c * jax.nn.sigmoid(c)
    acc = jnp.dot(cond.astype(BF16), w_ref[...].astype(BF16), preferred_element_type=F32)
    o_ref[...] = acc + b_ref[...]


def adaln(c, ada_w, ada_b, tn=1024):
    depth, d, n6 = ada_w.shape
    bsz = c.shape[0]
    rows = 8
    cp = jnp.zeros((rows, d), F32).at[:bsz].set(c)
    out = pl.pallas_call(
        _adaln_kernel,
        grid=(depth, n6 // tn),
        in_specs=[pl.BlockSpec((rows, d), lambda l, j: (0, 0)),
                  pl.BlockSpec((None, d, tn), lambda l, j: (l, 0, j)),
                  pl.BlockSpec((None, 1, tn), lambda l, j: (l, 0, j))],
        out_specs=pl.BlockSpec((None, rows, tn), lambda l, j: (l, 0, j)),
        out_shape=jax.ShapeDtypeStruct((depth, rows, n6), F32),
        compiler_params=_cparams(("arbitrary", "arbitrary")),
        name="adaln",
    )(cp, ada_w, ada_b.reshape(depth, 1, n6))
    return out[:, :bsz].reshape(depth, bsz, 6, d)


ROPE_ROWS = 64


def _norm_matmul_kernel(*refs, rope_tiles, q_tiles, has_extra, tn, nj, nsteps):
    if has_extra:
        x_ref, g_ref, mod_ref, w_ref, c_ref, s_ref, we_ref, be_ref, o_ref, oe_ref, oet_ref, h_scr, acc_scr = refs
    else:
        x_ref, g_ref, mod_ref, w_ref, c_ref, s_ref, o_ref, h_scr, acc_scr = refs
    t = pl.program_id(0)
    j = t % nj

    @pl.when((j == 0) & (t < nsteps))
    def _():
        h = _rms_modulate(x_ref[...], g_ref[...], mod_ref[0:1, :], mod_ref[1:2, :])
        h_scr[...] = h.astype(BF16)
        if has_extra:
            extra = _dot3(h, we_ref[...]) + be_ref[...]
            oe_ref[...] = extra
            oet_ref[...] = extra.T

    @pl.when(t == 0)
    def _():
        acc_scr[...] = jnp.zeros_like(acc_scr)

    acc = jnp.dot(h_scr[...], w_ref[...], preferred_element_type=F32)
    tm = acc_scr.shape[0]
    if rope_tiles:
        jp = (t + nj - 1) % nj
        rope_on = jp < rope_tiles
        scale = jnp.where(jp < q_tiles, DA_HEAD_DIM ** -0.5 * LOG2_E, 1.0).astype(F32)
    for r in range(tm // ROPE_ROWS):
        rows = slice(r * ROPE_ROWS, (r + 1) * ROPE_ROWS)
        if rope_tiles:
            cs = jnp.where(rope_on, c_ref[rows, :] * scale, 1.0)
            sn = jnp.where(rope_on, s_ref[rows, :] * scale, 0.0)
        for m in range(tn // LANES):
            cols = slice(m * LANES, (m + 1) * LANES)
            prev = acc_scr[rows, cols]
            if rope_tiles:
                prev = prev * cs + pltpu.roll(prev, LANES // 2, 1) * sn
            o_ref[rows, cols] = prev.astype(o_ref.dtype)
    acc_scr[...] = acc


def norm_matmul(x, g, mod, w, seq, rope=None, extra=None, tm=1024, tn=512):
    n, k = x.shape
    m = w.shape[1]
    tiles_per_seq = seq // tm
    nj = m // tn
    nsteps = (n // tm) * nj
    has_extra = extra is not None
    cur = lambda t: jnp.minimum(t, nsteps - 1)
    prv = lambda t: jnp.maximum(t - 1, 0)
    if rope is None:
        dummy = jnp.zeros((8, LANES), F32)
        tabs = (dummy, dummy)
        tab_spec = pl.BlockSpec((8, LANES), lambda t: (0, 0))
        rope_tiles = q_tiles = 0
    else:
        tabs = rope[:2]
        tab_spec = pl.BlockSpec((tm, LANES), lambda t: (prv(t) // nj, 0))
        rope_tiles, q_tiles = rope[2] // tn, rope[3] // tn
    in_specs = [pl.BlockSpec((tm, k), lambda t: (cur(t) // nj, 0)),
                pl.BlockSpec((1, k), lambda t: (0, 0)),
                pl.BlockSpec((None, 6, k), lambda t: (cur(t) // nj // tiles_per_seq, 0, 0)),
                pl.BlockSpec((k, tn), lambda t: (0, cur(t) % nj)),
                tab_spec, tab_spec]
    args = [x, g.reshape(1, k), mod, w, *tabs]
    out_specs = pl.BlockSpec((tm, tn), lambda t: (prv(t) // nj, prv(t) % nj))
    out_shape = jax.ShapeDtypeStruct((n, m), BF16)
    if has_extra:
        we, be = extra
        in_specs += [pl.BlockSpec((k, LANES), lambda t: (0, 0)),
                     pl.BlockSpec((1, LANES), lambda t: (0, 0))]
        args += [we, be]
        out_specs = [out_specs, pl.BlockSpec((tm, LANES), lambda t: (cur(t) // nj, 0)),
                     pl.BlockSpec((LANES, tm), lambda t: (0, cur(t) // nj))]
        out_shape = [out_shape, jax.ShapeDtypeStruct((n, LANES), F32), jax.ShapeDtypeStruct((LANES, n), F32)]
    return pl.pallas_call(
        functools.partial(_norm_matmul_kernel, rope_tiles=rope_tiles, q_tiles=q_tiles,
                          has_extra=has_extra, tn=tn, nj=nj, nsteps=nsteps),
        grid=(nsteps + 1,),
        in_specs=in_specs,
        out_specs=out_specs,
        out_shape=out_shape,
        scratch_shapes=[pltpu.VMEM((tm, k), BF16), pltpu.VMEM((tm, tn), F32)],
        compiler_params=_cparams(("arbitrary",)),
        name="norm_matmul",
    )(*args)


def _matmul_res_kernel(a_ref, w_ref, res_ref, mod_ref, o_ref, *, gate_row):
    acc = jnp.dot(a_ref[...], w_ref[...], preferred_element_type=F32)
    o_ref[...] = res_ref[...] + mod_ref[gate_row:gate_row + 1, :] * acc


def matmul_res(a, w, res, mod, seq, gate_row, tm=512, tn=2048):
    n, k = a.shape
    m = w.shape[1]
    tiles_per_seq = seq // tm
    return pl.pallas_call(
        functools.partial(_matmul_res_kernel, gate_row=gate_row),
        grid=(n // tm, m // tn),
        in_specs=[pl.BlockSpec((tm, k), lambda i, j: (i, 0)),
                  pl.BlockSpec((k, tn), lambda i, j: (0, j)),
                  pl.BlockSpec((tm, tn), lambda i, j: (i, j)),
                  pl.BlockSpec((None, 6, tn), lambda i, j: (i // tiles_per_seq, 0, j))],
        out_specs=pl.BlockSpec((tm, tn), lambda i, j: (i, j)),
        out_shape=jax.ShapeDtypeStruct((n, m), F32),
        compiler_params=_cparams(("arbitrary", "arbitrary")),
        name="matmul_res",
    )(a, w, res, mod)


ATTN_GROUPS = (4, 2)


def _grouped_loop(n, body, groups):
    done = 0
    for group in groups:
        trips = (n - done) // group

        def grouped(gi, c, group=group, done=done):
            for t in range(group):
                c = body(done + gi * group + t, c)
            return c

        lax.fori_loop(0, trips, grouped, 0)
        done = done + trips * group
    lax.fori_loop(done, n, body, 0)


def _attn_kernel(q_ref, k_ref, v_ref, lq1_ref, lk1_ref, lq2_ref, lk2_ref, g_ref, o_ref,
                 s_scr, m_scr, l_scr, acc_scr, *, lam_init, tq, tk, seq):
    d = DA_HEAD_DIM
    nlane = tk // LANES
    ndiag = tq // tk
    lam = (jnp.exp(jnp.sum(lq1_ref[...] * lk1_ref[...], axis=-1, keepdims=True))
           - jnp.exp(jnp.sum(lq2_ref[...] * lk2_ref[...], axis=-1, keepdims=True)) + lam_init)
    shift = ATTN_CHUNK.bit_length() - 1
    row_chunk = jnp.right_shift(lax.broadcasted_iota(I32, (tq, tk), 0), shift)
    col_chunk = jnp.right_shift(lax.broadcasted_iota(I32, (tq, tk), 1), shift)
    nt = (((1,), (1,)), ((), ()))

    def lane_fold(a, op):
        part = a[:, 0:LANES]
        for cb in range(1, nlane):
            part = op(part, a[:, cb * LANES:(cb + 1) * LANES])
        return part

    def q_body(qi, _):
        qs = pl.multiple_of(qi * tq, tq)
        qm = (q_ref[pl.ds(qs, tq), 0:d], q_ref[pl.ds(qs, tq), d:2 * d])
        m_scr[...] = jnp.full_like(m_scr, NEG_BIG)

        def score(j, mask):
            ks = pl.multiple_of(j * tk, tk)
            for mp in range(2):
                k = k_ref[pl.ds(ks, tk), mp * d:(mp + 1) * d]
                s = lax.dot_general(qm[mp], k, nt, preferred_element_type=F32)
                if mask is not None:
                    s = jnp.where(mask, s, NEG_BIG)
                s_scr[mp, j] = s
                m_scr[mp] = jnp.maximum(m_scr[mp], lane_fold(s, jnp.maximum))

        def score_body(j, c):
            score(j, None)
            return c

        nfull = qi * ndiag
        _grouped_loop(nfull, score_body, ATTN_GROUPS)
        for t in range(ndiag):
            score(nfull + t, col_chunk + t * (tk // ATTN_CHUNK) <= row_chunk)
        m = [jnp.max(m_scr[mp], axis=-1, keepdims=True) for mp in range(2)]
        l_scr[...] = jnp.zeros_like(l_scr)
        acc_scr[...] = jnp.zeros_like(acc_scr)

        def pv_body(j, c):
            ks = pl.multiple_of(j * tk, tk)
            v = v_ref[pl.ds(ks, tk), :]
            for mp in range(2):
                p = jnp.exp2(s_scr[mp, j] - m[mp])
                l_scr[mp] += lane_fold(p, jnp.add)
                acc_scr[mp] += jnp.dot(p.astype(BF16), v, preferred_element_type=F32)
            return c

        _grouped_loop(nfull + ndiag, pv_body, ATTN_GROUPS)
        l = [jnp.sum(l_scr[mp], axis=-1, keepdims=True) for mp in range(2)]
        o = acc_scr[0] / l[0] - lam * (acc_scr[1] / l[1])
        ms = jnp.mean(o * o, axis=-1, keepdims=True)
        o = o * lax.rsqrt(ms + NORM_EPS) * g_ref[...] * (1.0 - lam_init)
        o_ref[pl.ds(qs, tq), :] = o.astype(o_ref.dtype)
        return 0

    lax.fori_loop(0, seq // tq, q_body, 0)


def diff_attention(qkv, lq1, lk1, lq2, lk2, head_g, lam_init, tq=512, tk=256):
    bsz, seq, _ = qkv.shape
    h, dv = DA_HEADS, 2 * DA_HEAD_DIM
    vec = lambda a: a.reshape(1, -1).astype(F32)
    small = lambda n: pl.BlockSpec((1, n), lambda b, hh: (0, 0))
    return pl.pallas_call(
        functools.partial(_attn_kernel, lam_init=lam_init, tq=tq, tk=tk, seq=seq),
        grid=(bsz, h),
        in_specs=[pl.BlockSpec((None, seq, dv), lambda b, hh: (b, 0, hh)),
                  pl.BlockSpec((None, seq, dv), lambda b, hh: (b, 0, h + hh)),
                  pl.BlockSpec((None, seq, dv), lambda b, hh: (b, 0, 2 * h + hh)),
                  small(DA_HEAD_DIM), small(DA_HEAD_DIM), small(DA_HEAD_DIM), small(DA_HEAD_DIM),
                  small(dv)],
        out_specs=pl.BlockSpec((None, seq, dv), lambda b, hh: (b, 0, hh)),
        out_shape=jax.ShapeDtypeStruct((bsz, seq, h * dv), BF16),
        scratch_shapes=[pltpu.VMEM((2, seq // tk, tq, tk), F32), pltpu.VMEM((2, tq, LANES), F32),
                        pltpu.VMEM((2, tq, LANES), F32), pltpu.VMEM((2, tq, dv), F32)],
        compiler_params=_cparams(("arbitrary", "arbitrary")),
        name="diff_attention",
    )(qkv, qkv, qkv, vec(lq1), vec(lk1), vec(lq2), vec(lk2), vec(head_g))


def _conv_silu_kernel(x_ref, w_ref, b_ref, o_ref, *, k_tile0):
    j = pl.program_id(1)
    scale = jnp.where(j >= k_tile0, ML_QK_DIM ** -0.5, 1.0).astype(F32)
    taps = [w_ref[ML_CONV - 1 - s:ML_CONV - s, :] for s in range(ML_CONV)]

    def conv(x, shifted):
        y = x * taps[0] + b_ref[...]
        for s in range(1, ML_CONV):
            y = y + shifted(x, s) * taps[s]
        return (y * jax.nn.sigmoid(y) * scale).astype(o_ref.dtype)

    o_ref[...] = conv(x_ref[...].astype(F32), lambda x, s: pltpu.roll(x, s, 0))
    head = ROW_TILE
    row = lax.broadcasted_iota(I32, (head, x_ref.shape[1]), 0)
    o_ref[0:head, :] = conv(x_ref[0:head, :].astype(F32),
                            lambda x, s: jnp.where(row >= s, pltpu.roll(x, s, 0), 0.0))


def conv_silu(proj, conv_w, conv_b, tc=128):
    bsz, seq, _ = proj.shape
    cols = conv_w.shape[1]
    return pl.pallas_call(
        functools.partial(_conv_silu_kernel, k_tile0=(cols // 2) // tc),
        grid=(bsz, cols // tc),
        in_specs=[pl.BlockSpec((None, seq, tc), lambda b, j: (b, 0, j)),
                  pl.BlockSpec((ML_CONV, tc), lambda b, j: (0, j)),
                  pl.BlockSpec((1, tc), lambda b, j: (0, j))],
        out_specs=pl.BlockSpec((None, seq, tc), lambda b, j: (b, 0, j)),
        out_shape=jax.ShapeDtypeStruct((bsz, seq, cols), BF16),
        compiler_params=_cparams(("arbitrary", "arbitrary")),
        name="conv_silu",
    )(proj, conv_w, conv_b.reshape(1, cols))


def _mlstm_kernel(q_ref, k_ref, v_ref, op_ref, gc_ref, gr_ref, hg_ref, o_ref,
                  cx_scr, m_scr, *, chunk):
    c = pl.program_id(1)
    nh, dqk, dv = ML_HEADS, ML_QK_DIM, ML_V_DIM

    @pl.when(c == 0)
    def _():
        cx_scr[...] = jnp.zeros_like(cx_scr)
        m_scr[...] = jnp.zeros_like(m_scr)

    gc = gc_ref[...]
    gr = gr_ref[...]
    lf_c = jax.nn.log_sigmoid(gc)
    lf_r = jax.nn.log_sigmoid(gr)
    r_i = lax.broadcasted_iota(I32, (chunk, chunk), 0)
    c_i = lax.broadcasted_iota(I32, (chunk, chunk), 1)
    causal = c_i <= r_i
    tril = causal.astype(F32)
    triu = (r_i <= c_i).astype(F32)
    b_c = _dot3(tril, lf_c)
    b_r = _dot3(lf_r, triu)
    nt = (((1,), (1,)), ((), ()))
    tn_ = (((0,), (0,)), ((), ()))
    ones_l = jnp.ones((chunk, LANES), BF16)
    ones_v = jnp.ones((dv, LANES), BF16)
    wide = lambda a: jnp.concatenate([a] * (dv // LANES), axis=1)

    for h in range(nh):
        q = q_ref[:, h * dqk:(h + 1) * dqk]
        k = k_ref[:, h * dqk:(h + 1) * dqk]
        v_ext = jnp.concatenate([v_ref[:, h * dv:(h + 1) * dv], ones_l], axis=1)
        bc = b_c[:, nh + h:nh + h + 1]
        br = b_r[nh + h:nh + h + 1, :]
        ig_c = gc[:, h:h + 1]
        ig_r = gr[h:h + 1, :]
        m_prev = m_scr[h:h + 1, :]
        dmat = jnp.where(causal, bc - br + ig_r, NEG_BIG)
        inter = bc + m_prev
        m_t = jnp.maximum(inter, jnp.max(dmat, axis=-1, keepdims=True))
        w = jnp.exp(dmat - m_t)
        s = lax.dot_general(q, k, nt, preferred_element_type=F32) * w
        decay = jnp.exp(inter - m_t)
        cx = cx_scr[h]
        tot = (jnp.dot(s.astype(BF16), v_ext, preferred_element_type=F32)
               + decay * jnp.dot(q, cx.astype(BF16), preferred_element_type=F32))
        den = jnp.maximum(jnp.abs(tot[:, dv:]), jnp.exp(-m_t))
        hh = tot[:, :dv] / wide(den)
        b_last = bc[chunk - 1:chunk, :]
        g = b_last - bc + ig_c
        m_new = jnp.maximum(b_last + m_prev, jnp.max(g, axis=0, keepdims=True))
        carry_decay = jnp.exp(b_last + m_prev - m_new)
        wg = jnp.exp(g - m_new)
        wv = (wg * v_ext.astype(F32)).astype(BF16)
        cx_scr[h] = carry_decay * cx + lax.dot_general(k, wv, tn_, preferred_element_type=F32)
        m_scr[h:h + 1, :] = m_new
        ms = jnp.dot((hh * hh).astype(BF16), ones_v, preferred_element_type=F32) * (1.0 / dv)
        hn = hh * wide(lax.rsqrt(ms + NORM_EPS)) * hg_ref[:, h * dv:(h + 1) * dv]
        og = jax.nn.sigmoid(op_ref[:, h * dv:(h + 1) * dv].astype(F32))
        o_ref[:, h * dv:(h + 1) * dv] = (og * hn).astype(o_ref.dtype)


def mlstm(qk, proj, gates_c, gates_r, head_g, chunk=128):
    bsz, seq, _ = qk.shape
    nh = ML_HEADS
    qw, vw = nh * ML_QK_DIM, nh * ML_V_DIM
    v_blk = (2 * qw) // vw
    return pl.pallas_call(
        functools.partial(_mlstm_kernel, chunk=chunk),
        grid=(bsz, seq // chunk),
        in_specs=[pl.BlockSpec((None, chunk, qw), lambda b, c: (b, c, 0)),
                  pl.BlockSpec((None, chunk, qw), lambda b, c: (b, c, 1)),
                  pl.BlockSpec((None, chunk, vw), lambda b, c: (b, c, v_blk)),
                  pl.BlockSpec((None, chunk, vw), lambda b, c: (b, c, v_blk + 1)),
                  pl.BlockSpec((None, chunk, LANES), lambda b, c: (b, c, 0)),
                  pl.BlockSpec((2 * nh, chunk), lambda b, c: (0, b * (seq // chunk) + c)),
                  pl.BlockSpec((1, vw), lambda b, c: (0, 0))],
        out_specs=pl.BlockSpec((None, chunk, vw), lambda b, c: (b, c, 0)),
        out_shape=jax.ShapeDtypeStruct((bsz, seq, vw), BF16),
        scratch_shapes=[pltpu.VMEM((nh, ML_QK_DIM, ML_V_DIM + LANES), F32),
                        pltpu.VMEM((nh, 1), F32)],
        compiler_params=_cparams(("arbitrary", "arbitrary")),
        name="mlstm",
    )(qk, qk, proj, proj, gates_c, gates_r, head_g.reshape(1, vw))


def _router_kernel(x_ref, g_ref, mod_ref, w_ref, b_ref, hp_ref, rt_ref, cnt_ref, run_scr):
    i = pl.program_id(0)

    @pl.when(i == 0)
    def _():
        run_scr[...] = jnp.zeros_like(run_scr)

    h = _rms_modulate(x_ref[...], g_ref[...], mod_ref[3:4, :], mod_ref[4:5, :])
    half = h.shape[1] // 2
    _store_token_rows(hp_ref, _pack_bf16_pair(h[:, :half], h[:, half:]))
    logits = _dot3(h, w_ref[...]) + b_ref[...]
    tm = logits.shape[0]
    lane =lax.broadcasted_iota(I32, logits.shape, 1).astype(F32)
    ng = float(MOE_GROUPS)
    is_g = lane < ng
    gl = jnp.where(is_g, logits, NEG_BIG)
    gmax = jnp.max(gl, axis=-1, keepdims=True)
    grp = jnp.min(jnp.where(gl == gmax, lane, float(LANES)), axis=-1, keepdims=True)
    p_group = 1.0 / jnp.sum(jnp.where(is_g, jnp.exp(gl - gmax), 0.0), axis=-1, keepdims=True)
    lo = ng + float(MOE_PER_GROUP) * grp
    el = jnp.where((lane >= lo) & (lane < lo + float(MOE_PER_GROUP)), logits, NEG_BIG)
    v1 = jnp.max(el, axis=-1, keepdims=True)
    i1 = jnp.min(jnp.where(el == v1, lane, float(LANES)), axis=-1, keepdims=True)
    el2 = jnp.where(lane == i1, NEG_BIG, el)
    v2 = jnp.max(el2, axis=-1, keepdims=True)
    i2 = jnp.min(jnp.where(el2 == v2, lane, float(LANES)), axis=-1, keepdims=True)
    ex = jnp.exp(v2 - v1)
    w1 = p_group / (1.0 + ex)
    w2 = p_group * (ex / (1.0 + ex))
    oh1 = (lane == i1).astype(F32)
    oh2 = (lane == i2).astype(F32)
    oh = oh1 + oh2
    r_i = lax.broadcasted_iota(I32, (tm, tm), 0)
    c_i = lax.broadcasted_iota(I32, (tm, tm), 1)
    before = (c_i < r_i).astype(BF16)
    prior = jnp.dot(before, oh.astype(BF16), preferred_element_type=F32) + run_scr[...]
    rank1 = jnp.sum(oh1 * prior, axis=-1, keepdims=True)
    rank2 = jnp.sum(oh2 * prior, axis=-1, keepdims=True)
    run_scr[...] += jnp.sum(oh, axis=0, keepdims=True)
    cnt_ref[...] = run_scr[...]
    rt_ref[...] = jnp.where(lane == 0.0, i1 - ng,
                  jnp.where(lane == 1.0, i2 - ng,
                  jnp.where(lane == 2.0, w1,
                  jnp.where(lane == 3.0, w2,
                  jnp.where(lane == 4.0, rank1,
                  jnp.where(lane == 5.0, rank2, 0.0))))))


def moe_router(x, g, mod, w_r, b_r, seq, tm=512):
    n, d = x.shape
    tiles_per_seq = seq // tm
    return pl.pallas_call(
        _router_kernel,
        grid=(n // tm,),
        in_specs=[pl.BlockSpec((tm, d), lambda i: (i, 0)),
                  pl.BlockSpec((1, d), lambda i: (0, 0)),
                  pl.BlockSpec((None, 6, d), lambda i: (i // tiles_per_seq, 0, 0)),
                  pl.BlockSpec((d, LANES), lambda i: (0, 0)),
                  pl.BlockSpec((1, LANES), lambda i: (0, 0))],
        out_specs=[pl.BlockSpec((tm * ROW_TILE, LANES), lambda i: (i, 0)),
                   pl.BlockSpec((tm, LANES), lambda i: (i, 0)),
                   pl.BlockSpec((1, LANES), lambda i: (0, 0))],
        out_shape=[jax.ShapeDtypeStruct((n * ROW_TILE, LANES), U32),
                   jax.ShapeDtypeStruct((n, LANES), F32),
                   jax.ShapeDtypeStruct((1, LANES), F32)],
        scratch_shapes=[pltpu.VMEM((1, LANES), F32)],
        compiler_params=_cparams(("arbitrary",)),
        name="moe_router",
    )(x, g.reshape(1, d), mod, w_r, b_r)


def _dest_kernel(rt_ref, ps_ref, o_ref):
    rt = rt_ref[...]
    lane = lax.broadcasted_iota(I32, rt.shape, 1).astype(F32)
    ng = float(MOE_GROUPS)
    ps = ps_ref[...]
    d1 = jnp.sum(jnp.where(lane == rt[:, 0:1] + ng, ps, 0.0), axis=-1, keepdims=True) + rt[:, 4:5]
    d2 = jnp.sum(jnp.where(lane == rt[:, 1:2] + ng, ps, 0.0), axis=-1, keepdims=True) + rt[:, 5:6]
    o_ref[...] = jnp.where(lane == 0.0, d1, jnp.where(lane == 1.0, d2, 0.0)).astype(I32)


def moe_dest(route, pad_start_lanes, tm=2048):
    n = route.shape[0]
    return pl.pallas_call(
        _dest_kernel,
        grid=(n // tm,),
        in_specs=[pl.BlockSpec((tm, LANES), lambda i: (i, 0)),
                  pl.BlockSpec((1, LANES), lambda i: (0, 0))],
        out_specs=pl.BlockSpec((tm, LANES), lambda i: (i, 0)),
        out_shape=jax.ShapeDtypeStruct((n, LANES), I32),
        compiler_params=_cparams(("arbitrary",)),
        name="moe_dest",
    )(route, pad_start_lanes)


DMA_UNROLL = 8


def _wait_rows(src_rows, dst_rows, sem, copies):
    for _ in range(copies):
        pltpu.make_async_copy(src_rows, dst_rows, sem).wait()


def _dispatch_kernel(zb_ref, dest_ref, hp_ref, xin_ref, zero_scr, ring, sems, *, tt, tm, nb, ntiles):
    i = pl.program_id(0)
    slot = i % 2
    blk_rows = tm * ROW_TILE
    tile_rows = tt * ROW_TILE

    @pl.when(i == 0)
    def _():
        zero_scr[...] = jnp.zeros_like(zero_scr)

        def zero_copy(blk):
            return pltpu.make_async_copy(zero_scr, xin_ref.at[pl.ds(pl.multiple_of(blk * blk_rows, blk_rows), blk_rows)],
                                         sems.at[2])

        def start(blk, c):
            @pl.when(zb_ref[blk] == 1)
            def _():
                zero_copy(blk).start()
            return c

        def wait(blk, c):
            @pl.when(zb_ref[blk] == 1)
            def _():
                zero_copy(blk).wait()
            return c

        lax.fori_loop(0, nb, start, 0)
        lax.fori_loop(0, nb, wait, 0)

    ring[slot] = hp_ref[...]

    def start_rows(r, c):
        src = ring.at[slot, pl.ds(pl.multiple_of(r * ROW_TILE, ROW_TILE), ROW_TILE)]
        for kk in range(2):
            row = pl.multiple_of(dest_ref[0, 0, 2 * r + kk] * ROW_TILE, ROW_TILE)
            pltpu.make_async_copy(src, xin_ref.at[pl.ds(row, ROW_TILE)], sems.at[slot]).start()
        return c

    lax.fori_loop(0, tt, start_rows, 0, unroll=DMA_UNROLL)
    whole = xin_ref.at[pl.ds(0, tile_rows)]

    @pl.when(i > 0)
    def _():
        _wait_rows(ring.at[1 - slot], whole, sems.at[1 - slot], 2)

    @pl.when(i == ntiles - 1)
    def _():
        _wait_rows(ring.at[slot], whole, sems.at[slot], 2)


def moe_dispatch(hpack, dest, zero_blocks, rows, tm, tt=256):
    n = hpack.shape[0] // ROW_TILE
    grid_spec = pltpu.PrefetchScalarGridSpec(
        num_scalar_prefetch=1,
        grid=(n // tt,),
        in_specs=[pl.BlockSpec((1, 1, 2 * tt), lambda i, zb: (i, 0, 0), memory_space=pltpu.SMEM),
                  pl.BlockSpec((tt * ROW_TILE, LANES), lambda i, zb: (i, 0))],
        out_specs=pl.BlockSpec(memory_space=pl.ANY),
        scratch_shapes=[pltpu.VMEM((tm * ROW_TILE, LANES), U32), pltpu.VMEM((2, tt * ROW_TILE, LANES), U32),
                        pltpu.SemaphoreType.DMA((3,))],
    )
    return pl.pallas_call(
        functools.partial(_dispatch_kernel, tt=tt, tm=tm, nb=rows // tm, ntiles=n // tt),
        grid_spec=grid_spec,
        out_shape=jax.ShapeDtypeStruct((rows * ROW_TILE, LANES), U32),
        compiler_params=_cparams(("arbitrary",)),
        name="moe_dispatch",
    )(zero_blocks, dest.reshape(n // tt, 1, 2 * tt), hpack)


def _expert_kernel(be_ref, first_ref, nxt_ref, nu_ref, x_ref, wgu_hbm, wd_hbm, y_ref,
                   wgu_bf, wd_bf, stg_gu, stg_d, xs_scr, acc_scr, sems, *, layer, th, cr):
    i = pl.program_id(0)

    def fetch(e):
        return (pltpu.make_async_copy(wgu_hbm.at[layer, e], stg_gu, sems.at[0]),
                pltpu.make_async_copy(wd_hbm.at[layer, e], stg_d, sems.at[1]))

    @pl.when(i < nu_ref[0])
    def _():
        @pl.when(first_ref[i] == 1)
        def _():
            @pl.when(i == 0)
            def _():
                for cp in fetch(be_ref[i]):
                    cp.start()

            for cp in fetch(be_ref[i]):
                cp.wait()

            def cast_rows(src, dst):
                def body(r, c):
                    rs = pl.multiple_of(r * cr, cr)
                    dst[pl.ds(rs, cr), :] = src[pl.ds(rs, cr), :].astype(BF16)
                    return c
                lax.fori_loop(0, src.shape[0] // cr, body, 0)

            cast_rows(stg_gu, wgu_bf)
            cast_rows(stg_d, wd_bf)

            @pl.when(nxt_ref[i] >= 0)
            def _():
                for cp in fetch(nxt_ref[i]):
                    cp.start()

        tm = xs_scr.shape[0]
        half = xs_scr.shape[1] // 2
        hid = wd_bf.shape[0]
        for sub in range(ROW_TILE):
            lo, hi = _unpack_bf16_pair(_load_token_rows(x_ref, sub, tm))
            xs_scr[:, sub * LANES:(sub + 1) * LANES] = lo.astype(BF16)
            xs_scr[:, half + sub * LANES:half + (sub + 1) * LANES] = hi.astype(BF16)
        xs = xs_scr[...]
        for c in range(hid // th):
            gt = jnp.dot(xs, wgu_bf[:, c * th:(c + 1) * th], preferred_element_type=F32)
            up = jnp.dot(xs, wgu_bf[:, hid + c * th:hid + (c + 1) * th], preferred_element_type=F32)
            act = (gt * jax.nn.sigmoid(gt) * up).astype(BF16)
            part = jnp.dot(act, wd_bf[c * th:(c + 1) * th, :], preferred_element_type=F32)
            if c == 0:
                acc_scr[...] = part
            else:
                acc_scr[...] += part
        y = acc_scr[...]
        _store_token_rows(y_ref, _pack_bf16_pair(y[:, :half], y[:, half:]))


def moe_experts(xin, sched, wgu_all, wd_all, layer, tm, th=256, cr=256):
    _, _, d, hid2 = wgu_all.shape
    hid = hid2 // 2
    nb = xin.shape[0] // (tm * ROW_TILE)
    block_e, first, nxt, n_used = sched

    def blk(i, be, fi, nx, nu):
        return (jnp.minimum(i, nu[0] - 1), 0)

    grid_spec = pltpu.PrefetchScalarGridSpec(
        num_scalar_prefetch=4,
        grid=(nb,),
        in_specs=[pl.BlockSpec((tm * ROW_TILE, LANES), blk),
                  pl.BlockSpec(memory_space=pl.ANY),
                  pl.BlockSpec(memory_space=pl.ANY)],
        out_specs=pl.BlockSpec((tm * ROW_TILE, LANES), blk),
        scratch_shapes=[pltpu.VMEM((d, hid2), BF16), pltpu.VMEM((hid, d), BF16),
                        pltpu.VMEM((d, hid2), F32), pltpu.VMEM((hid, d), F32),
                        pltpu.VMEM((tm, d), BF16), pltpu.VMEM((tm, d), F32),
                        pltpu.SemaphoreType.DMA((2,))],
    )
    return pl.pallas_call(
        functools.partial(_expert_kernel, layer=layer, th=th, cr=cr),
        grid_spec=grid_spec,
        out_shape=jax.ShapeDtypeStruct(xin.shape, U32),
        input_output_aliases={4: 0},
        compiler_params=_cparams(("arbitrary",)),
        name="moe_experts",
    )(block_e, first, nxt, n_used, xin, wgu_all, wd_all)


def _combine_kernel(dcur_ref, dnxt_ref, x_ref, rt_ref, mod_ref, fg_ref, y_ref, o_ref, ya, yb, sems,
                    *, tt, ntiles, final_norm):
    i = pl.program_id(0)
    slot = i % 2

    def issue(dref, sl):
        def body(r, c):
            dst = pl.ds(pl.multiple_of(r * ROW_TILE, ROW_TILE), ROW_TILE)
            for kk, buf in enumerate((ya, yb)):
                row = pl.multiple_of(dref[0, 0, 2 * r + kk] * ROW_TILE, ROW_TILE)
                pltpu.make_async_copy(y_ref.at[pl.ds(row, ROW_TILE)], buf.at[sl, dst], sems.at[sl]).start()
            return c
        lax.fori_loop(0, tt, body, 0, unroll=DMA_UNROLL)

    @pl.when(i == 0)
    def _():
        issue(dcur_ref, 0)

    @pl.when(i + 1 < ntiles)
    def _():
        issue(dnxt_ref, 1 - slot)

    _wait_rows(y_ref.at[pl.ds(0, tt * ROW_TILE)], ya.at[slot], sems.at[slot], 2)
    half = x_ref.shape[1] // 2
    w1 = rt_ref[:, 2:3]
    w2 = rt_ref[:, 3:4]
    ssq = jnp.zeros((tt, 1), F32)
    for sub in range(ROW_TILE):
        a_lo, a_hi = _unpack_bf16_pair(_load_token_rows(ya.at[slot], sub, tt))
        b_lo, b_hi = _unpack_bf16_pair(_load_token_rows(yb.at[slot], sub, tt))
        for base, a, b in ((sub * LANES, a_lo, b_lo), (half + sub * LANES, a_hi, b_hi)):
            cols = slice(base, base + LANES)
            out = x_ref[:, cols] + mod_ref[5:6, cols] * (a * w1 + b * w2)
            o_ref[:, cols] = out
            if final_norm:
                ssq = ssq + jnp.sum(out * out, axis=-1, keepdims=True)
    if final_norm:
        r = lax.rsqrt(ssq / (2 * half) + NORM_EPS)
        o_ref[...] = o_ref[...] * r * fg_ref[...]


def moe_combine(x, y, dest, route, mod, final_g, seq, final_norm, tt=256):
    n, d = x.shape
    tiles_per_seq = seq // tt
    ntiles = n // tt
    dest3 = dest.reshape(ntiles, 1, 2 * tt)
    return pl.pallas_call(
        functools.partial(_combine_kernel, tt=tt, ntiles=ntiles, final_norm=final_norm),
        grid=(ntiles,),
        in_specs=[pl.BlockSpec((1, 1, 2 * tt), lambda i: (i, 0, 0), memory_space=pltpu.SMEM),
                  pl.BlockSpec((1, 1, 2 * tt), lambda i: (jnp.minimum(i + 1, ntiles - 1), 0, 0),
                               memory_space=pltpu.SMEM),
                  pl.BlockSpec((tt, d), lambda i: (i, 0)),
                  pl.BlockSpec((tt, LANES), lambda i: (i, 0)),
                  pl.BlockSpec((None, 6, d), lambda i: (i // tiles_per_seq, 0, 0)),
                  pl.BlockSpec((1, d), lambda i: (0, 0)),
                  pl.BlockSpec(memory_space=pl.ANY)],
        out_specs=pl.BlockSpec((tt, d), lambda i: (i, 0)),
        out_shape=jax.ShapeDtypeStruct((n, d), F32),
        scratch_shapes=[pltpu.VMEM((2, tt * ROW_TILE, LANES), U32), pltpu.VMEM((2, tt * ROW_TILE, LANES), U32),
                        pltpu.SemaphoreType.DMA((2,))],
        compiler_params=_cparams(("arbitrary",)),
        name="moe_combine",
    )(dest3, dest3, x, route, mod, final_g.reshape(1, d), y)


def _expert_schedule(counts, tm, nb):
    ne = counts.shape[0]
    ids = jnp.arange(ne, dtype=I32)
    padded = ((counts + tm - 1) // tm) * tm
    pad_end = jnp.sum(jnp.where(ids[None, :] <= ids[:, None], padded[None, :], 0), axis=1)
    pad_start = pad_end - padded
    total = jnp.sum(padded)
    n_used = total // tm
    blk0 = jnp.arange(nb, dtype=I32) * tm
    block_e = jnp.minimum(jnp.sum((pad_end[None, :] <= blk0[:, None]).astype(I32), axis=1), ne - 1)
    onehot = block_e[:, None] == ids[None, :]
    pick = lambda v: jnp.sum(jnp.where(onehot, v[None, :], 0), axis=1)
    first = (blk0 == pick(pad_start)).astype(I32)
    later = jnp.min(jnp.where((ids[None, :] > ids[:, None]) & (counts[None, :] > 0), ids[None, :], ne), axis=1)
    nxt = pick(jnp.where(later < ne, later, -1))
    has_padding = (blk0 + tm == pick(pad_end)) & (pick(counts % tm) != 0)
    zero_blocks = ((blk0 >= total) | has_padding).astype(I32)
    return pad_start, zero_blocks, (block_e, first, nxt.astype(I32), n_used.astype(I32).reshape(1))


def hier_moe_layer(x, g, mod, w_group, b_group, w_expert, b_expert, wgu_all, wd_all, layer, final_g, seq,
                   final_norm, tm=256):
    n, d = x.shape
    ng, ne = w_group.shape[1], w_expert.shape[1]
    w_r = jnp.concatenate([w_group, w_expert, jnp.zeros((d, LANES - ng - ne), F32)], axis=1)
    b_r = jnp.concatenate([b_group, b_expert, jnp.zeros((LANES - ng - ne,), F32)]).reshape(1, LANES)
    hpack, route, cnt = moe_router(x, g, mod, w_r, b_r, seq)
    counts = cnt[0, ng:ng + ne].astype(I32)
    rows = 2 * n + ne * tm
    pad_start, zero_blocks, sched = _expert_schedule(counts, tm, rows // tm)
    ps_lanes = jnp.concatenate([jnp.zeros((ng,), F32), pad_start.astype(F32),
                                jnp.zeros((LANES - ng - ne,), F32)]).reshape(1, LANES)
    dest = moe_dest(route, ps_lanes, tm=min(2048, n))[:, :2].reshape(-1)
    xin = moe_dispatch(hpack, dest, zero_blocks, rows, tm)
    y = moe_experts(xin, sched, wgu_all, wd_all, layer, tm)
    return moe_combine(x, y, dest, route, mod, final_g, seq, final_norm)


def _rope_tables(positions):
    half = ROPE_DIM // 2
    inv_freq = ROPE_THETA ** (-jnp.arange(half, dtype=F32) * 2.0 / ROPE_DIM)
    gap = jnp.zeros((LANES // 2 - half,), F32)
    freq = jnp.concatenate([-inv_freq, gap, inv_freq, gap])
    ang = positions.astype(F32).reshape(-1, 1) * freq[None, :]
    return jnp.cos(ang), jnp.sin(ang)


def _weight_prep_kernel(w_ref, o_ref, *, pair_tiles, tn):
    j = pl.program_id(0)
    half = ROPE_DIM // 2
    mid = LANES // 2

    @pl.when(j >= pair_tiles)
    def _():
        o_ref[...] = w_ref[...].astype(BF16)

    if pair_tiles:
        @pl.when(j < pair_tiles)
        def _():
            lane = lax.broadcasted_iota(I32, (w_ref.shape[0], LANES), 1)
            for m in range(tn // LANES):
                t = w_ref[:, m * LANES:(m + 1) * LANES]
                up = pltpu.roll(t, LANES - half, 1)
                down = pltpu.roll(t, mid - half, 1)
                new = jnp.where(lane < half, t, jnp.where(lane < mid, up, jnp.where(lane < mid + half, down, t)))
                o_ref[:, m * LANES:(m + 1) * LANES] = new.astype(BF16)


def weight_prep(w, cols, pair_cols=0, tn=512):
    k = w.shape[0]
    return pl.pallas_call(
        functools.partial(_weight_prep_kernel, pair_tiles=pair_cols // tn, tn=tn),
        grid=(cols // tn,),
        in_specs=[pl.BlockSpec((k, tn), lambda j: (0, j))],
        out_specs=pl.BlockSpec((k, tn), lambda j: (0, j)),
        out_shape=jax.ShapeDtypeStruct((k, cols), BF16),
        compiler_params=_cparams(("arbitrary",)),
        name="weight_prep",
    )(w)


def kernel(x, c, positions, ada_w, ada_b, norm_mix_g, norm_ffn_g, final_norm_g, attn_w_in, attn_w_out, attn_lambda_q1, attn_lambda_k1, attn_lambda_q2, attn_lambda_k2, attn_head_norm_g, mlstm_w_in, mlstm_conv_w, mlstm_conv_b, mlstm_gate_b, mlstm_head_norm_g, mlstm_w_out, moe_w_group, moe_b_group, moe_w_expert, moe_b_expert, moe_w_gu, moe_w_down):
    bsz, seq, d = x.shape
    n = bsz * seq
    depth = ada_w.shape[0]
    mod = adaln(c, ada_w, ada_b)
    xf = x.reshape(n, d)
    for i in range(depth):
        jm = i // 2
        if i % 2 == 0:
            qk_cols = 2 * DA_HEADS * 2 * DA_HEAD_DIM
            tabs = _rope_tables(positions)
            w_in = weight_prep(attn_w_in[jm], attn_w_in.shape[2], pair_cols=qk_cols)
            qkv = norm_matmul(xf, norm_mix_g[i], mod[i], w_in, seq, rope=(*tabs, qk_cols, qk_cols // 2))
            lam_init = 0.8 - 0.6 * math.exp(-0.3 * i)
            mixed = diff_attention(qkv.reshape(bsz, seq, -1), attn_lambda_q1[jm], attn_lambda_k1[jm],
                                   attn_lambda_q2[jm], attn_lambda_k2[jm], attn_head_norm_g[jm], lam_init)
            w_out = attn_w_out[jm]
        else:
            qk_cols = 2 * ML_HEADS * ML_QK_DIM
            main_cols = qk_cols + 2 * ML_HEADS * ML_V_DIM
            w_in = mlstm_w_in[jm]
            ngate = 2 * ML_HEADS
            w_gate = jnp.concatenate([w_in[:, main_cols:], jnp.zeros((d, LANES - ngate), F32)], axis=1)
            b_gate = jnp.concatenate([mlstm_gate_b[jm], jnp.zeros((LANES - ngate,), F32)]).reshape(1, LANES)
            proj, gates, gates_t = norm_matmul(xf, norm_mix_g[i], mod[i], weight_prep(w_in, main_cols), seq,
                                               extra=(w_gate, b_gate))
            proj = proj.reshape(bsz, seq, main_cols)
            qk = conv_silu(proj, mlstm_conv_w[jm], mlstm_conv_b[jm])
            mixed = mlstm(qk, proj, gates.reshape(bsz, seq, LANES), gates_t, mlstm_head_norm_g[jm])
            w_out = mlstm_w_out[jm]
        xf = matmul_res(mixed.reshape(n, -1), w_out.astype(BF16), xf, mod[i], seq, gate_row=2)
        xf = hier_moe_layer(xf, norm_ffn_g[i], mod[i], moe_w_group[i], moe_b_group[i], moe_w_expert[i],
                            moe_b_expert[i], moe_w_gu, moe_w_down, i, final_norm_g, seq,
                            final_norm=(i == depth - 1))
    return xf.reshape(bsz, seq, d)
```

```python
import functools
import math

import jax
import jax.numpy as jnp
from jax import lax
from jax.experimental import pallas as pl
from jax.experimental.pallas import tpu as pltpu

F32 = jnp.float32
BF16 = jnp.bfloat16
U32 = jnp.uint32
I32 = jnp.int32

NORM_EPS = 1e-6
ROPE_THETA = 500000.0
ATTN_CHUNK = 64
DA_HEADS = 8
DA_HEAD_DIM = 128
ROPE_DIM = 32
ML_HEADS = 8
ML_QK_DIM = 128
ML_V_DIM = 256
ML_CONV = 4
MOE_GROUPS = 4
MOE_PER_GROUP = 8
MOE_EXPERTS = 32
LANES = 128
NEG_BIG = -1e30
LOG2_E = math.log2(math.e)

VMEM_LIMIT = 56 * 1024 * 1024


def _cparams(sem):
    return pltpu.CompilerParams(dimension_semantics=sem, vmem_limit_bytes=VMEM_LIMIT)


def _split_hi_lo(a):
    hi = a.astype(BF16)
    lo = (a - hi.astype(F32)).astype(BF16)
    return hi, lo


def _dot3(a, w, w_rows=False):
    ah, al = _split_hi_lo(a)
    wh, wl = _split_hi_lo(w)
    contract = (((1,), (1 if w_rows else 0,)), ((), ()))
    d = functools.partial(lax.dot_general, dimension_numbers=contract, preferred_element_type=F32)
    return d(ah, wh) + (d(ah, wl) + d(al, wh))


def _pack_bf16_pair(lo_f32, hi_f32):
    lo_bits = lax.bitcast_convert_type(lo_f32.astype(BF16).astype(F32), U32)
    hi_bits = lax.bitcast_convert_type(hi_f32.astype(BF16).astype(F32), U32)
    return hi_bits | (lo_bits >> 16)


def _unpack_bf16_pair(word):
    lo = lax.bitcast_convert_type(word << 16, F32)
    hi = lax.bitcast_convert_type(word & jnp.uint32(0xFFFF0000), F32)
    return lo, hi


ROW_TILE = 8


def _store_token_rows(ref, words):
    t = words.shape[0]
    for sub in range(ROW_TILE):
        ref[pl.ds(sub, t, stride=ROW_TILE), :] = words[:, sub * LANES:(sub + 1) * LANES]


def _load_token_rows(ref, sub, t):
    return ref[pl.ds(sub, t, stride=ROW_TILE), :]


def _rms_modulate(x, g, shift, scale):
    ms = jnp.mean(x * x, axis=-1, keepdims=True)
    y = x * lax.rsqrt(ms + NORM_EPS) * g
    return y * (1.0 + scale) + shift


def _adaln_kernel(c_ref, w_ref, b_ref, o_ref):
    c = c_ref[...]
    cond = c * jax.nn.sigmoid(c)
    acc = jnp.dot(cond.astype(BF16), w_ref[...].astype(BF16), preferred_element_type=F32)
    o_ref[...] = acc + b_ref[...]


def adaln(c, ada_w, ada_b, tn=1024):
    depth, d, n6 = ada_w.shape
    bsz = c.shape[0]
    rows = 8
    cp = jnp.zeros((rows, d), F32).at[:bsz].set(c)
    out = pl.pallas_call(
        _adaln_kernel,
        grid=(depth, n6 // tn),
        in_specs=[pl.BlockSpec((rows, d), lambda l, j: (0, 0)),
                  pl.BlockSpec((None, d, tn), lambda l, j: (l, 0, j)),
                  pl.BlockSpec((None, 1, tn), lambda l, j: (l, 0, j))],
        out_specs=pl.BlockSpec((None, rows, tn), lambda l, j: (l, 0, j)),
        out_shape=jax.ShapeDtypeStruct((depth, rows, n6), F32),
        compiler_params=_cparams(("arbitrary", "arbitrary")),
        name="adaln",
    )(cp, ada_w, ada_b.reshape(depth, 1, n6))
    return out[:, :bsz].reshape(depth, bsz, 6, d)


ROPE_ROWS = 64


def _norm_matmul_kernel(*refs, rope_tiles, q_tiles, has_extra, tn, nj, nsteps, w_rows):
    if has_extra:
        x_ref, g_ref, mod_ref, w_ref, c_ref, s_ref, we_ref, be_ref, o_ref, oe_ref, oet_ref, h_scr, acc_scr = refs
    else:
        x_ref, g_ref, mod_ref, w_ref, c_ref, s_ref, o_ref, h_scr, acc_scr = refs
    t = pl.program_id(0)
    j = t % nj

    @pl.when((j == 0) & (t < nsteps))
    def _():
        h = _rms_modulate(x_ref[...], g_ref[...], mod_ref[0:1, :], mod_ref[1:2, :])
        h_scr[...] = h.astype(BF16)
        if has_extra:
            extra = _dot3(h, we_ref[...], w_rows) + be_ref[...]
            oe_ref[...] = extra
            oet_ref[...] = extra.T

    @pl.when(t == 0)
    def _():
        acc_scr[...] = jnp.zeros_like(acc_scr)

    contract = (((1,), (1 if w_rows else 0,)), ((), ()))
    acc = lax.dot_general(h_scr[...], w_ref[...], contract, preferred_element_type=F32)
    tm = acc_scr.shape[0]
    if rope_tiles:
        jp = (t + nj - 1) % nj
        rope_on = jp < rope_tiles
        scale = jnp.where(jp < q_tiles, DA_HEAD_DIM ** -0.5 * LOG2_E, 1.0).astype(F32)
    for r in range(tm // ROPE_ROWS):
        rows = slice(r * ROPE_ROWS, (r + 1) * ROPE_ROWS)
        if rope_tiles:
            cs = jnp.where(rope_on, c_ref[rows, :] * scale, 1.0)
            sn = jnp.where(rope_on, s_ref[rows, :] * scale, 0.0)
        for m in range(tn // LANES):
            cols = slice(m * LANES, (m + 1) * LANES)
            prev = acc_scr[rows, cols]
            if rope_tiles:
                prev = prev * cs + pltpu.roll(prev, LANES // 2, 1) * sn
            o_ref[rows, cols] = prev.astype(o_ref.dtype)
    acc_scr[...] = acc


def norm_matmul(x, g, mod, w, seq, rope=None, extra=None, w_rows=False, tm=1024, tn=512):
    n, k = x.shape
    m = w.shape[0 if w_rows else 1]
    tiles_per_seq = seq // tm
    nj = m // tn
    nsteps = (n // tm) * nj
    has_extra = extra is not None
    cur = lambda t: jnp.minimum(t, nsteps - 1)
    prv = lambda t: jnp.maximum(t - 1, 0)
    if rope is None:
        dummy = jnp.zeros((8, LANES), F32)
        tabs = (dummy, dummy)
        tab_spec = pl.BlockSpec((8, LANES), lambda t: (0, 0))
        rope_tiles = q_tiles = 0
    else:
        tabs = rope[:2]
        tab_spec = pl.BlockSpec((tm, LANES), lambda t: (prv(t) // nj, 0))
        rope_tiles, q_tiles = rope[2] // tn, rope[3] // tn
    in_specs = [pl.BlockSpec((tm, k), lambda t: (cur(t) // nj, 0)),
                pl.BlockSpec((1, k), lambda t: (0, 0)),
                pl.BlockSpec((None, 6, k), lambda t: (cur(t) // nj // tiles_per_seq, 0, 0)),
                (pl.BlockSpec((tn, k), lambda t: (cur(t) % nj, 0)) if w_rows
                 else pl.BlockSpec((k, tn), lambda t: (0, cur(t) % nj))),
                tab_spec, tab_spec]
    args = [x, g.reshape(1, k), mod, w, *tabs]
    out_specs = pl.BlockSpec((tm, tn), lambda t: (prv(t) // nj, prv(t) % nj))
    out_shape = jax.ShapeDtypeStruct((n, m), BF16)
    if has_extra:
        we, be = extra
        in_specs += [pl.BlockSpec((LANES, k) if w_rows else (k, LANES), lambda t: (0, 0)),
                     pl.BlockSpec((1, LANES), lambda t: (0, 0))]
        args += [we, be]
        out_specs = [out_specs, pl.BlockSpec((tm, LANES), lambda t: (cur(t) // nj, 0)),
                     pl.BlockSpec((LANES, tm), lambda t: (0, cur(t) // nj))]
        out_shape = [out_shape, jax.ShapeDtypeStruct((n, LANES), F32), jax.ShapeDtypeStruct((LANES, n), F32)]
    return pl.pallas_call(
        functools.partial(_norm_matmul_kernel, rope_tiles=rope_tiles, q_tiles=q_tiles,
                          has_extra=has_extra, tn=tn, nj=nj, nsteps=nsteps, w_rows=w_rows),
        grid=(nsteps + 1,),
        in_specs=in_specs,
        out_specs=out_specs,
        out_shape=out_shape,
        scratch_shapes=[pltpu.VMEM((tm, k), BF16), pltpu.VMEM((tm, tn), F32)],
        compiler_params=_cparams(("arbitrary",)),
        name="norm_matmul",
    )(*args)


def _router_kernel(x_ref, g_ref, mod_ref, w_ref, b_ref, hp_ref, rt_ref, cnt_ref, run_scr):
    @pl.when(pl.program_id(0) == 0)
    def _():
        run_scr[...] = jnp.zeros_like(run_scr)

    h = _rms_modulate(x_ref[...], g_ref[...], mod_ref[3:4, :], mod_ref[4:5, :])
    half = h.shape[1] // 2
    _store_token_rows(hp_ref, _pack_bf16_pair(h[:, :half], h[:, half:]))
    logits = _dot3(h, w_ref[...]) + b_ref[...]
    tm = logits.shape[0]
    lane = lax.broadcasted_iota(I32, logits.shape, 1).astype(F32)
    ng = float(MOE_GROUPS)
    is_g = lane < ng
    gl = jnp.where(is_g, logits, NEG_BIG)
    gmax = jnp.max(gl, axis=-1, keepdims=True)
    grp = jnp.min(jnp.where(gl == gmax, lane, float(LANES)), axis=-1, keepdims=True)
    p_group = 1.0 / jnp.sum(jnp.where(is_g, jnp.exp(gl - gmax), 0.0), axis=-1, keepdims=True)
    lo = ng + float(MOE_PER_GROUP) * grp
    el = jnp.where((lane >= lo) & (lane < lo + float(MOE_PER_GROUP)), logits, NEG_BIG)
    v1 = jnp.max(el, axis=-1, keepdims=True)
    i1 = jnp.min(jnp.where(el == v1, lane, float(LANES)), axis=-1, keepdims=True)
    el2 = jnp.where(lane == i1, NEG_BIG, el)
    v2 = jnp.max(el2, axis=-1, keepdims=True)
    i2 = jnp.min(jnp.where(el2 == v2, lane, float(LANES)), axis=-1, keepdims=True)
    ex = jnp.exp(v2 - v1)
    w1 = p_group / (1.0 + ex)
    w2 = p_group * (ex / (1.0 + ex))
    oh1 = (lane == i1).astype(F32)
    oh2 = (lane == i2).astype(F32)
    oh = oh1 + oh2
    r_i = lax.broadcasted_iota(I32, (tm, tm), 0)
    c_i = lax.broadcasted_iota(I32, (tm, tm), 1)
    before = (c_i < r_i).astype(BF16)
    prior = jnp.dot(before, oh.astype(BF16), preferred_element_type=F32) + run_scr[...]
    rank1 = jnp.sum(oh1 * prior, axis=-1, keepdims=True)
    rank2 = jnp.sum(oh2 * prior, axis=-1, keepdims=True)
    run_scr[...] += jnp.sum(oh, axis=0, keepdims=True)
    cnt_ref[...] = run_scr[...]
    rt_ref[...] = jnp.where(lane == 0.0, i1 - ng,
                  jnp.where(lane == 1.0, i2 - ng,
                  jnp.where(lane == 2.0, w1,
                  jnp.where(lane == 3.0, w2,
                  jnp.where(lane == 4.0, rank1,
                  jnp.where(lane == 5.0, rank2, 0.0))))))


def moe_router(x, g, mod, w_r, b_r, seq, tm=512):
    n, d = x.shape
    tiles_per_seq = seq // tm
    return pl.pallas_call(
        _router_kernel,
        grid=(n // tm,),
        in_specs=[pl.BlockSpec((tm, d), lambda i: (i, 0)),
                  pl.BlockSpec((1, d), lambda i: (0, 0)),
                  pl.BlockSpec((None, 6, d), lambda i: (i // tiles_per_seq, 0, 0)),
                  pl.BlockSpec((d, LANES), lambda i: (0, 0)),
                  pl.BlockSpec((1, LANES), lambda i: (0, 0))],
        out_specs=[pl.BlockSpec((tm * ROW_TILE, LANES), lambda i: (i, 0)),
                   pl.BlockSpec((tm, LANES), lambda i: (i, 0)),
                   pl.BlockSpec((1, LANES), lambda i: (0, 0))],
        out_shape=[jax.ShapeDtypeStruct((n * ROW_TILE, LANES), U32),
                   jax.ShapeDtypeStruct((n, LANES), F32),
                   jax.ShapeDtypeStruct((1, LANES), F32)],
        scratch_shapes=[pltpu.VMEM((1, LANES), F32)],
        compiler_params=_cparams(("arbitrary",)),
        name="moe_router",
    )(x, g.reshape(1, d), mod, w_r, b_r)


def _matmul_res_kernel(a_ref, w_ref, res_ref, mod_ref, o_ref, *, gate_row):
    acc = jnp.dot(a_ref[...], w_ref[...], preferred_element_type=F32)
    o_ref[...] = res_ref[...] + mod_ref[gate_row:gate_row + 1, :] * acc


def matmul_res(a, w, res, mod, seq, gate_row, tm=512, tn=2048):
    n, k = a.shape
    m = w.shape[1]
    tiles_per_seq = seq // tm
    return pl.pallas_call(
        functools.partial(_matmul_res_kernel, gate_row=gate_row),
        grid=(n // tm, m // tn),
        in_specs=[pl.BlockSpec((tm, k), lambda i, j: (i, 0)),
                  pl.BlockSpec((k, tn), lambda i, j: (0, j)),
                  pl.BlockSpec((tm, tn), lambda i, j: (i, j)),
                  pl.BlockSpec((None, 6, tn), lambda i, j: (i // tiles_per_seq, 0, j))],
        out_specs=pl.BlockSpec((tm, tn), lambda i, j: (i, j)),
        out_shape=jax.ShapeDtypeStruct((n, m), F32),
        compiler_params=_cparams(("arbitrary", "arbitrary")),
        name="matmul_res",
    )(a, w, res, mod)


ATTN_GROUPS = (8, 4, 2)


def _grouped_loop(n, body, groups):
    done = 0
    for group in groups:
        trips = (n - done) // group

        def grouped(gi, c, group=group, done=done):
            for t in range(group):
                c = body(done + gi * group + t, c)
            return c

        lax.fori_loop(0, trips, grouped, 0)
        done = done + trips * group
    lax.fori_loop(done, n, body, 0)


def _attn_kernel(q_ref, k_ref, v_ref, lq1_ref, lk1_ref, lq2_ref, lk2_ref, g_ref, o_ref,
                 s_scr, m_scr, l_scr, acc_scr, *, lam_init, tq, tk, seq):
    d = DA_HEAD_DIM
    nlane = tk // LANES
    ndiag = tq // tk
    lam = (jnp.exp(jnp.sum(lq1_ref[...] * lk1_ref[...], axis=-1, keepdims=True))
           - jnp.exp(jnp.sum(lq2_ref[...] * lk2_ref[...], axis=-1, keepdims=True)) + lam_init)
    shift = ATTN_CHUNK.bit_length() - 1
    row_chunk = jnp.right_shift(lax.broadcasted_iota(I32, (tq, tk), 0), shift)
    col_chunk = jnp.right_shift(lax.broadcasted_iota(I32, (tq, tk), 1), shift)
    nt = (((1,), (1,)), ((), ()))

    def lane_fold(a, op):
        part = a[:, 0:LANES]
        for cb in range(1, nlane):
            part = op(part, a[:, cb * LANES:(cb + 1) * LANES])
        return part

    def q_body(qi, _):
        qs = pl.multiple_of(qi * tq, tq)
        qm = (q_ref[pl.ds(qs, tq), 0:d], q_ref[pl.ds(qs, tq), d:2 * d])
        m_scr[...] = jnp.full_like(m_scr, NEG_BIG)

        def score(j, mask):
            ks = pl.multiple_of(j * tk, tk)
            for mp in range(2):
                k = k_ref[pl.ds(ks, tk), mp * d:(mp + 1) * d]
                s = lax.dot_general(qm[mp], k, nt, preferred_element_type=F32)
                if mask is not None:
                    s = jnp.where(mask, s, NEG_BIG)
                s_scr[mp, j] = s
                m_scr[mp] = jnp.maximum(m_scr[mp], lane_fold(s, jnp.maximum))

        def score_body(j, c):
            score(j, None)
            return c

        nfull = qi * ndiag
        _grouped_loop(nfull, score_body, ATTN_GROUPS)
        for t in range(ndiag):
            score(nfull + t, col_chunk + t * (tk // ATTN_CHUNK) <= row_chunk)
        m = [jnp.max(m_scr[mp], axis=-1, keepdims=True) for mp in range(2)]
        l_scr[...] = jnp.zeros_like(l_scr)
        acc_scr[...] = jnp.zeros_like(acc_scr)

        def pv_body(j, c):
            ks = pl.multiple_of(j * tk, tk)
            v = v_ref[pl.ds(ks, tk), :]
            for mp in range(2):
                p = jnp.exp2(s_scr[mp, j] - m[mp])
                l_scr[mp] += lane_fold(p, jnp.add)
                acc_scr[mp] += jnp.dot(p.astype(BF16), v, preferred_element_type=F32)
            return c

        _grouped_loop(nfull + ndiag, pv_body, ATTN_GROUPS)
        l = [jnp.sum(l_scr[mp], axis=-1, keepdims=True) for mp in range(2)]
        o = acc_scr[0] / l[0] - lam * (acc_scr[1] / l[1])
        ms = jnp.mean(o * o, axis=-1, keepdims=True)
        o = o * lax.rsqrt(ms + NORM_EPS) * g_ref[...] * (1.0 - lam_init)
        o_ref[pl.ds(qs, tq), :] = o.astype(o_ref.dtype)
        return 0

    lax.fori_loop(0, seq // tq, q_body, 0)


def diff_attention(qkv, lq1, lk1, lq2, lk2, head_g, lam_init, tq=512, tk=256):
    bsz, seq, _ = qkv.shape
    h, dv = DA_HEADS, 2 * DA_HEAD_DIM
    vec = lambda a: a.reshape(1, -1).astype(F32)
    small = lambda n: pl.BlockSpec((1, n), lambda b, hh: (0, 0))
    return pl.pallas_call(
        functools.partial(_attn_kernel, lam_init=lam_init, tq=tq, tk=tk, seq=seq),
        grid=(bsz, h),
        in_specs=[pl.BlockSpec((None, seq, dv), lambda b, hh: (b, 0, hh)),
                  pl.BlockSpec((None, seq, dv), lambda b, hh: (b, 0, h + hh)),
                  pl.BlockSpec((None, seq, dv), lambda b, hh: (b, 0, 2 * h + hh)),
                  small(DA_HEAD_DIM), small(DA_HEAD_DIM), small(DA_HEAD_DIM), small(DA_HEAD_DIM),
                  small(dv)],
        out_specs=pl.BlockSpec((None, seq, dv), lambda b, hh: (b, 0, hh)),
        out_shape=jax.ShapeDtypeStruct((bsz, seq, h * dv), BF16),
        scratch_shapes=[pltpu.VMEM((2, seq // tk, tq, tk), F32), pltpu.VMEM((2, tq, LANES), F32),
                        pltpu.VMEM((2, tq, LANES), F32), pltpu.VMEM((2, tq, dv), F32)],
        compiler_params=_cparams(("arbitrary", "arbitrary")),
        name="diff_attention",
    )(qkv, qkv, qkv, vec(lq1), vec(lk1), vec(lq2), vec(lk2), vec(head_g))


def _conv_silu_kernel(x_ref, w_ref, b_ref, o_ref, *, k_tile0):
    j = pl.program_id(1)
    scale = jnp.where(j >= k_tile0, ML_QK_DIM ** -0.5, 1.0).astype(F32)
    taps = [w_ref[ML_CONV - 1 - s:ML_CONV - s, :] for s in range(ML_CONV)]

    def conv(x, shifted):
        y = x * taps[0] + b_ref[...]
        for s in range(1, ML_CONV):
            y = y + shifted(x, s) * taps[s]
        return (y * jax.nn.sigmoid(y) * scale).astype(o_ref.dtype)

    o_ref[...] = conv(x_ref[...].astype(F32), lambda x, s: pltpu.roll(x, s, 0))
    head = ROW_TILE
    row = lax.broadcasted_iota(I32, (head, x_ref.shape[1]), 0)
    o_ref[0:head, :] = conv(x_ref[0:head, :].astype(F32),
                            lambda x, s: jnp.where(row >= s, pltpu.roll(x, s, 0), 0.0))


def conv_silu(proj, conv_w, conv_b, tc=128):
    bsz, seq, _ = proj.shape
    cols = conv_w.shape[1]
    return pl.pallas_call(
        functools.partial(_conv_silu_kernel, k_tile0=(cols // 2) // tc),
        grid=(bsz, cols // tc),
        in_specs=[pl.BlockSpec((None, seq, tc), lambda b, j: (b, 0, j)),
                  pl.BlockSpec((ML_CONV, tc), lambda b, j: (0, j)),
                  pl.BlockSpec((1, tc), lambda b, j: (0, j))],
        out_specs=pl.BlockSpec((None, seq, tc), lambda b, j: (b, 0, j)),
        out_shape=jax.ShapeDtypeStruct((bsz, seq, cols), BF16),
        compiler_params=_cparams(("arbitrary", "arbitrary")),
        name="conv_silu",
    )(proj, conv_w, conv_b.reshape(1, cols))


def _mlstm_kernel(q_ref, k_ref, v_ref, op_ref, gc_ref, gr_ref, hg_ref, o_ref,
                  cx_scr, m_scr, *, chunk):
    c = pl.program_id(1)
    nh, dqk, dv = ML_HEADS, ML_QK_DIM, ML_V_DIM

    @pl.when(c == 0)
    def _():
        cx_scr[...] = jnp.zeros_like(cx_scr)
        m_scr[...] = jnp.zeros_like(m_scr)

    gc = gc_ref[...]
    gr = gr_ref[...]
    lf_c = jax.nn.log_sigmoid(gc)
    lf_r = jax.nn.log_sigmoid(gr)
    r_i = lax.broadcasted_iota(I32, (chunk, chunk), 0)
    c_i = lax.broadcasted_iota(I32, (chunk, chunk), 1)
    causal = c_i <= r_i
    tril = causal.astype(F32)
    triu = (r_i <= c_i).astype(F32)
    b_c = _dot3(tril, lf_c)
    b_r = _dot3(lf_r, triu)
    nt = (((1,), (1,)), ((), ()))
    tn_ = (((0,), (0,)), ((), ()))
    ones_l = jnp.ones((chunk, LANES), BF16)
    ones_v = jnp.ones((dv, LANES), BF16)
    wide = lambda a: jnp.concatenate([a] * (dv // LANES), axis=1)

    for h in range(nh):
        q = q_ref[:, h * dqk:(h + 1) * dqk]
        k = k_ref[:, h * dqk:(h + 1) * dqk]
        v_ext = jnp.concatenate([v_ref[:, h * dv:(h + 1) * dv], ones_l], axis=1)
        bc = b_c[:, nh + h:nh + h + 1]
        br = b_r[nh + h:nh + h + 1, :]
        ig_c = gc[:, h:h + 1]
        ig_r = gr[h:h + 1, :]
        m_prev = m_scr[h:h + 1, :]
        dmat = jnp.where(causal, bc - br + ig_r, NEG_BIG)
        inter = bc + m_prev
        m_t = jnp.maximum(inter, jnp.max(dmat, axis=-1, keepdims=True))
        w = jnp.exp(dmat - m_t)
        s = lax.dot_general(q, k, nt, preferred_element_type=F32) * w
        decay = jnp.exp(inter - m_t)
        cx = cx_scr[h]
        tot = (jnp.dot(s.astype(BF16), v_ext, preferred_element_type=F32)
               + decay * jnp.dot(q, cx.astype(BF16), preferred_element_type=F32))
        den = jnp.maximum(jnp.abs(tot[:, dv:]), jnp.exp(-m_t))
        hh = tot[:, :dv] / wide(den)
        b_last = bc[chunk - 1:chunk, :]
        g = b_last - bc + ig_c
        m_new = jnp.maximum(b_last + m_prev, jnp.max(g, axis=0, keepdims=True))
        carry_decay = jnp.exp(b_last + m_prev - m_new)
        wg = jnp.exp(g - m_new)
        wv = (wg * v_ext.astype(F32)).astype(BF16)
        cx_scr[h] = carry_decay * cx + lax.dot_general(k, wv, tn_, preferred_element_type=F32)
        m_scr[h:h + 1, :] = m_new
        ms = jnp.dot((hh * hh).astype(BF16), ones_v, preferred_element_type=F32) * (1.0 / dv)
        hn = hh * wide(lax.rsqrt(ms + NORM_EPS)) * hg_ref[:, h * dv:(h + 1) * dv]
        og = jax.nn.sigmoid(op_ref[:, h * dv:(h + 1) * dv].astype(F32))
        o_ref[:, h * dv:(h + 1) * dv] = (og * hn).astype(o_ref.dtype)


def mlstm(qk, proj, gates_c, gates_r, head_g, chunk=128):
    bsz, seq, _ = qk.shape
    nh = ML_HEADS
    qw, vw = nh * ML_QK_DIM, nh * ML_V_DIM
    v_blk = (2 * qw) // vw
    return pl.pallas_call(
        functools.partial(_mlstm_kernel, chunk=chunk),
        grid=(bsz, seq // chunk),
        in_specs=[pl.BlockSpec((None, chunk, qw), lambda b, c: (b, c, 0)),
                  pl.BlockSpec((None, chunk, qw), lambda b, c: (b, c, 1)),
                  pl.BlockSpec((None, chunk, vw), lambda b, c: (b, c, v_blk)),
                  pl.BlockSpec((None, chunk, vw), lambda b, c: (b, c, v_blk + 1)),
                  pl.BlockSpec((None, chunk, LANES), lambda b, c: (b, c, 0)),
                  pl.BlockSpec((2 * nh, chunk), lambda b, c: (0, b * (seq // chunk) + c)),
                  pl.BlockSpec((1, vw), lambda b, c: (0, 0))],
        out_specs=pl.BlockSpec((None, chunk, vw), lambda b, c: (b, c, 0)),
        out_shape=jax.ShapeDtypeStruct((bsz, seq, vw), BF16),
        scratch_shapes=[pltpu.VMEM((nh, ML_QK_DIM, ML_V_DIM + LANES), F32),
                        pltpu.VMEM((nh, 1), F32)],
        compiler_params=_cparams(("arbitrary", "arbitrary")),
        name="mlstm",
    )(qk, qk, proj, proj, gates_c, gates_r, head_g.reshape(1, vw))


def _dest_kernel(rt_ref, ps_ref, o_ref):
    rt = rt_ref[...]
    lane = lax.broadcasted_iota(I32, rt.shape, 1).astype(F32)
    ng = float(MOE_GROUPS)
    ps = ps_ref[...]
    d1 = jnp.sum(jnp.where(lane == rt[:, 0:1] + ng, ps, 0.0), axis=-1, keepdims=True) + rt[:, 4:5]
    d2 = jnp.sum(jnp.where(lane == rt[:, 1:2] + ng, ps, 0.0), axis=-1, keepdims=True) + rt[:, 5:6]
    o_ref[...] = jnp.where(lane == 0.0, d1, jnp.where(lane == 1.0, d2, 0.0)).astype(I32)


def moe_dest(route, pad_start_lanes, tm=2048):
    n = route.shape[0]
    return pl.pallas_call(
        _dest_kernel,
        grid=(n // tm,),
        in_specs=[pl.BlockSpec((tm, LANES), lambda i: (i, 0)),
                  pl.BlockSpec((1, LANES), lambda i: (0, 0))],
        out_specs=pl.BlockSpec((tm, LANES), lambda i: (i, 0)),
        out_shape=jax.ShapeDtypeStruct((n, LANES), I32),
        compiler_params=_cparams(("arbitrary",)),
        name="moe_dest",
    )(route, pad_start_lanes)


DMA_UNROLL = 8


def _wait_rows(src_rows, dst_rows, sem, copies):
    for _ in range(copies):
        pltpu.make_async_copy(src_rows, dst_rows, sem).wait()


def _dispatch_kernel(zb_ref, dest_ref, hp_ref, xin_ref, zero_scr, ring, sems, *, tt, tm, nb, ntiles):
    i = pl.program_id(0)
    slot = i % 2
    blk_rows = tm * ROW_TILE
    tile_rows = tt * ROW_TILE

    @pl.when(i == 0)
    def _():
        zero_scr[...] = jnp.zeros_like(zero_scr)

        def zero_copy(blk):
            return pltpu.make_async_copy(zero_scr, xin_ref.at[pl.ds(pl.multiple_of(blk * blk_rows, blk_rows), blk_rows)],
                                         sems.at[2])

        def start(blk, c):
            @pl.when(zb_ref[blk] == 1)
            def _():
                zero_copy(blk).start()
            return c

        def wait(blk, c):
            @pl.when(zb_ref[blk] == 1)
            def _():
                zero_copy(blk).wait()
            return c

        lax.fori_loop(0, nb, start, 0)
        lax.fori_loop(0, nb, wait, 0)

    ring[slot] = hp_ref[...]

    def start_rows(r, c):
        src = ring.at[slot, pl.ds(pl.multiple_of(r * ROW_TILE, ROW_TILE), ROW_TILE)]
        for kk in range(2):
            row = pl.multiple_of(dest_ref[0, 0, 2 * r + kk] * ROW_TILE, ROW_TILE)
            pltpu.make_async_copy(src, xin_ref.at[pl.ds(row, ROW_TILE)], sems.at[slot]).start()
        return c

    lax.fori_loop(0, tt, start_rows, 0, unroll=DMA_UNROLL)
    whole = xin_ref.at[pl.ds(0, tile_rows)]

    @pl.when(i > 0)
    def _():
        _wait_rows(ring.at[1 - slot], whole, sems.at[1 - slot], 2)

    @pl.when(i == ntiles - 1)
    def _():
        _wait_rows(ring.at[slot], whole, sems.at[slot], 2)


def moe_dispatch(hpack, dest, zero_blocks, rows, tm, tt=256):
    n = hpack.shape[0] // ROW_TILE
    grid_spec = pltpu.PrefetchScalarGridSpec(
        num_scalar_prefetch=1,
        grid=(n // tt,),
        in_specs=[pl.BlockSpec((1, 1, 2 * tt), lambda i, zb: (i, 0, 0), memory_space=pltpu.SMEM),
                  pl.BlockSpec((tt * ROW_TILE, LANES), lambda i, zb: (i, 0))],
        out_specs=pl.BlockSpec(memory_space=pl.ANY),
        scratch_shapes=[pltpu.VMEM((tm * ROW_TILE, LANES), U32), pltpu.VMEM((2, tt * ROW_TILE, LANES), U32),
                        pltpu.SemaphoreType.DMA((3,))],
    )
    return pl.pallas_call(
        functools.partial(_dispatch_kernel, tt=tt, tm=tm, nb=rows // tm, ntiles=n // tt),
        grid_spec=grid_spec,
        out_shape=jax.ShapeDtypeStruct((rows * ROW_TILE, LANES), U32),
        compiler_params=_cparams(("arbitrary",)),
        name="moe_dispatch",
    )(zero_blocks, dest.reshape(n // tt, 1, 2 * tt), hpack)


def _expert_kernel(be_ref, first_ref, nxt_ref, nv_ref, nu_ref, x_ref, wgu_hbm, wd_hbm, y_ref,
                   wgu_bf, wd_bf, stg_gu, stg_d, xs_scr, acc_scr, sems, *, layer, th, cr):
    i = pl.program_id(0)

    def fetch(e):
        return (pltpu.make_async_copy(wgu_hbm.at[layer, e], stg_gu, sems.at[0]),
                pltpu.make_async_copy(wd_hbm.at[layer, e], stg_d, sems.at[1]))

    @pl.when(i < nu_ref[0])
    def _():
        @pl.when(first_ref[i] == 1)
        def _():
            @pl.when(i == 0)
            def _():
                for cp in fetch(be_ref[i]):
                    cp.start()

            for cp in fetch(be_ref[i]):
                cp.wait()

            def cast_rows(src, dst):
                def body(r, c):
                    rs = pl.multiple_of(r * cr, cr)
                    dst[pl.ds(rs, cr), :] = src[pl.ds(rs, cr), :].astype(BF16)
                    return c
                lax.fori_loop(0, src.shape[0] // cr, body, 0)

            cast_rows(stg_gu, wgu_bf)
            cast_rows(stg_d, wd_bf)

            @pl.when(nxt_ref[i] >= 0)
            def _():
                for cp in fetch(nxt_ref[i]):
                    cp.start()

        tm = xs_scr.shape[0]
        half = xs_scr.shape[1] // 2
        hid = wd_bf.shape[0]

        def mlp(rows):
            for sub in range(ROW_TILE):
                lo, hi = _unpack_bf16_pair(_load_token_rows(x_ref, sub, rows))
                xs_scr[0:rows, sub * LANES:(sub + 1) * LANES] = lo.astype(BF16)
                xs_scr[0:rows, half + sub * LANES:half + (sub + 1) * LANES] = hi.astype(BF16)
            xs = xs_scr[0:rows, :]
            for c in range(hid // th):
                gt = jnp.dot(xs, wgu_bf[:, c * th:(c + 1) * th], preferred_element_type=F32)
                up = jnp.dot(xs, wgu_bf[:, hid + c * th:hid + (c + 1) * th], preferred_element_type=F32)
                act = (gt * jax.nn.sigmoid(gt) * up).astype(BF16)
                part = jnp.dot(act, wd_bf[c * th:(c + 1) * th, :], preferred_element_type=F32)
                if c == 0:
                    acc_scr[0:rows, :] = part
                else:
                    acc_scr[0:rows, :] += part
            y = acc_scr[0:rows, :]
            _store_token_rows(y_ref, _pack_bf16_pair(y[:, :half], y[:, half:]))
            if rows < tm:
                y_ref[rows * ROW_TILE:, :] = jnp.zeros(((tm - rows) * ROW_TILE, LANES), U32)

        @pl.when(nv_ref[i] > tm // 2)
        def _():
            mlp(tm)

        @pl.when(nv_ref[i] <= tm // 2)
        def _():
            mlp(tm // 2)


def moe_experts(xin, sched, wgu_all, wd_all, layer, tm, th=256, cr=256):
    _, _, d, hid2 = wgu_all.shape
    hid = hid2 // 2
    nb = xin.shape[0] // (tm * ROW_TILE)
    block_e, first, nxt, valid, n_used = sched

    def blk(i, be, fi, nx, nv, nu):
        return (jnp.minimum(i, nu[0] - 1), 0)

    grid_spec = pltpu.PrefetchScalarGridSpec(
        num_scalar_prefetch=5,
        grid=(nb,),
        in_specs=[pl.BlockSpec((tm * ROW_TILE, LANES), blk),
                  pl.BlockSpec(memory_space=pl.ANY),
                  pl.BlockSpec(memory_space=pl.ANY)],
        out_specs=pl.BlockSpec((tm * ROW_TILE, LANES), blk),
        scratch_shapes=[pltpu.VMEM((d, hid2), BF16), pltpu.VMEM((hid, d), BF16),
                        pltpu.VMEM((d, hid2), F32), pltpu.VMEM((hid, d), F32),
                        pltpu.VMEM((tm, d), BF16), pltpu.VMEM((tm, d), F32),
                        pltpu.SemaphoreType.DMA((2,))],
    )
    return pl.pallas_call(
        functools.partial(_expert_kernel, layer=layer, th=th, cr=cr),
        grid_spec=grid_spec,
        out_shape=jax.ShapeDtypeStruct(xin.shape, U32),
        input_output_aliases={5: 0},
        compiler_params=_cparams(("arbitrary",)),
        name="moe_experts",
    )(block_e, first, nxt, valid, n_used, xin, wgu_all, wd_all)


def _combine_kernel(dcur_ref, dnxt_ref, x_ref, rt_ref, mod_ref, fg_ref, y_ref, o_ref, ya, yb, sems,
                    *, tt, ntiles, final_norm):
    i = pl.program_id(0)
    slot = i % 2

    def issue(dref, sl):
        def body(r, c):
            dst = pl.ds(pl.multiple_of(r * ROW_TILE, ROW_TILE), ROW_TILE)
            for kk, buf in enumerate((ya, yb)):
                row = pl.multiple_of(dref[0, 0, 2 * r + kk] * ROW_TILE, ROW_TILE)
                pltpu.make_async_copy(y_ref.at[pl.ds(row, ROW_TILE)], buf.at[sl, dst], sems.at[sl]).start()
            return c
        lax.fori_loop(0, tt, body, 0, unroll=DMA_UNROLL)

    @pl.when(i == 0)
    def _():
        issue(dcur_ref, 0)

    @pl.when(i + 1 < ntiles)
    def _():
        issue(dnxt_ref, 1 - slot)

    _wait_rows(y_ref.at[pl.ds(0, tt * ROW_TILE)], ya.at[slot], sems.at[slot], 2)
    half = x_ref.shape[1] // 2
    w1 = rt_ref[:, 2:3]
    w2 = rt_ref[:, 3:4]
    ssq = jnp.zeros((tt, 1), F32)
    for sub in range(ROW_TILE):
        a_lo, a_hi = _unpack_bf16_pair(_load_token_rows(ya.at[slot], sub, tt))
        b_lo, b_hi = _unpack_bf16_pair(_load_token_rows(yb.at[slot], sub, tt))
        for base, a, b in ((sub * LANES, a_lo, b_lo), (half + sub * LANES, a_hi, b_hi)):
            cols = slice(base, base + LANES)
            out = x_ref[:, cols] + mod_ref[5:6, cols] * (a * w1 + b * w2)
            o_ref[:, cols] = out
            if final_norm:
                ssq = ssq + jnp.sum(out * out, axis=-1, keepdims=True)
    if final_norm:
        r = lax.rsqrt(ssq / (2 * half) + NORM_EPS)
        o_ref[...] = o_ref[...] * r * fg_ref[...]


def moe_combine(x, y, dest, route, mod, final_g, seq, final_norm, tt=256):
    n, d = x.shape
    tiles_per_seq = seq // tt
    ntiles = n // tt
    dest3 = dest.reshape(ntiles, 1, 2 * tt)
    return pl.pallas_call(
        functools.partial(_combine_kernel, tt=tt, ntiles=ntiles, final_norm=final_norm),
        grid=(ntiles,),
        in_specs=[pl.BlockSpec((1, 1, 2 * tt), lambda i: (i, 0, 0), memory_space=pltpu.SMEM),
                  pl.BlockSpec((1, 1, 2 * tt), lambda i: (jnp.minimum(i + 1, ntiles - 1), 0, 0),
                               memory_space=pltpu.SMEM),
                  pl.BlockSpec((tt, d), lambda i: (i, 0)),
                  pl.BlockSpec((tt, LANES), lambda i: (i, 0)),
                  pl.BlockSpec((None, 6, d), lambda i: (i // tiles_per_seq, 0, 0)),
                  pl.BlockSpec((1, d), lambda i: (0, 0)),
                  pl.BlockSpec(memory_space=pl.ANY)],
        out_specs=pl.BlockSpec((tt, d), lambda i: (i, 0)),
        out_shape=jax.ShapeDtypeStruct((n, d), F32),
        scratch_shapes=[pltpu.VMEM((2, tt * ROW_TILE, LANES), U32), pltpu.VMEM((2, tt * ROW_TILE, LANES), U32),
                        pltpu.SemaphoreType.DMA((2,))],
        compiler_params=_cparams(("arbitrary",)),
        name="moe_combine",
    )(dest3, dest3, x, route, mod, final_g.reshape(1, d), y)


def _expert_schedule(counts, tm, nb):
    ne = counts.shape[0]
    ids = jnp.arange(ne, dtype=I32)
    padded = ((counts + tm - 1) // tm) * tm
    pad_end = jnp.sum(jnp.where(ids[None, :] <= ids[:, None], padded[None, :], 0), axis=1)
    pad_start = pad_end - padded
    total = jnp.sum(padded)
    n_used = total // tm
    blk0 = jnp.arange(nb, dtype=I32) * tm
    block_e = jnp.minimum(jnp.sum((pad_end[None, :] <= blk0[:, None]).astype(I32), axis=1), ne - 1)
    onehot = block_e[:, None] == ids[None, :]
    pick = lambda v: jnp.sum(jnp.where(onehot, v[None, :], 0), axis=1)
    first = (blk0 == pick(pad_start)).astype(I32)
    later = jnp.min(jnp.where((ids[None, :] > ids[:, None]) & (counts[None, :] > 0), ids[None, :], ne), axis=1)
    nxt = pick(jnp.where(later < ne, later, -1))
    has_padding = (blk0 + tm == pick(pad_end)) & (pick(counts % tm) != 0)
    zero_blocks = ((blk0 >= total) | has_padding).astype(I32)
    valid = jnp.clip(pick(pad_start + counts) - blk0, 0, tm)
    return pad_start, zero_blocks, (block_e, first, nxt.astype(I32), valid.astype(I32),
                                    n_used.astype(I32).reshape(1))


def moe_router_weights(w_group, b_group, w_expert, b_expert):
    d = w_group.shape[0]
    ng, ne = w_group.shape[1], w_expert.shape[1]
    w_r = jnp.concatenate([w_group, w_expert, jnp.zeros((d, LANES - ng - ne), F32)], axis=1)
    b_r = jnp.concatenate([b_group, b_expert, jnp.zeros((LANES - ng - ne,), F32)]).reshape(1, LANES)
    return w_r, b_r


def hier_moe_layer(x, g, mod, w_r, b_r, wgu_all, wd_all, layer, final_g, seq, final_norm, tm=256):
    n, d = x.shape
    ng, ne = MOE_GROUPS, MOE_EXPERTS
    hpack, route, cnt = moe_router(x, g, mod, w_r, b_r, seq)
    counts = cnt[0, ng:ng + ne].astype(I32)
    rows = 2 * n + ne * tm
    pad_start, zero_blocks, sched = _expert_schedule(counts, tm, rows // tm)
    ps_lanes = jnp.concatenate([jnp.zeros((ng,), F32), pad_start.astype(F32),
                                jnp.zeros((LANES - ng - ne,), F32)]).reshape(1, LANES)
    dest = moe_dest(route, ps_lanes, tm=min(2048, n))[:, :2].reshape(-1)
    xin = moe_dispatch(hpack, dest, zero_blocks, rows, tm)
    y = moe_experts(xin, sched, wgu_all, wd_all, layer, tm)
    return moe_combine(x, y, dest, route, mod, final_g, seq, final_norm)


def _rope_tables(positions):
    half = ROPE_DIM // 2
    inv_freq = ROPE_THETA ** (-jnp.arange(half, dtype=F32) * 2.0 / ROPE_DIM)
    gap = jnp.zeros((LANES // 2 - half,), F32)
    freq = jnp.concatenate([-inv_freq, gap, inv_freq, gap])
    ang = positions.astype(F32).reshape(-1, 1) * freq[None, :]
    return jnp.cos(ang), jnp.sin(ang)


def _weight_prep_kernel(w_ref, o_ref, *, pair_tiles, tn):
    j = pl.program_id(0)
    half = ROPE_DIM // 2
    mid = LANES // 2

    @pl.when(j >= pair_tiles)
    def _():
        o_ref[...] = w_ref[...].astype(BF16)

    if pair_tiles:
        @pl.when(j < pair_tiles)
        def _():
            lane = lax.broadcasted_iota(I32, (w_ref.shape[0], LANES), 1)
            for m in range(tn // LANES):
                t = w_ref[:, m * LANES:(m + 1) * LANES]
                up = pltpu.roll(t, LANES - half, 1)
                down = pltpu.roll(t, mid - half, 1)
                new = jnp.where(lane < half, t, jnp.where(lane < mid, up, jnp.where(lane < mid + half, down, t)))
                o_ref[:, m * LANES:(m + 1) * LANES] = new.astype(BF16)


def weight_prep(w, cols, pair_cols=0, tn=512, w_rows=False):
    if w_rows:
        k = w.shape[1]
        return pl.pallas_call(
            functools.partial(_weight_prep_kernel, pair_tiles=0, tn=tn),
            grid=(cols // tn,),
            in_specs=[pl.BlockSpec((tn, k), lambda j: (j, 0))],
            out_specs=pl.BlockSpec((tn, k), lambda j: (j, 0)),
            out_shape=jax.ShapeDtypeStruct((cols, k), BF16),
            compiler_params=_cparams(("arbitrary",)),
            name="weight_prep",
        )(w)
    k = w.shape[0]
    return pl.pallas_call(
        functools.partial(_weight_prep_kernel, pair_tiles=pair_cols // tn, tn=tn),
        grid=(cols // tn,),
        in_specs=[pl.BlockSpec((k, tn), lambda j: (0, j))],
        out_specs=pl.BlockSpec((k, tn), lambda j: (0, j)),
        out_shape=jax.ShapeDtypeStruct((k, cols), BF16),
        compiler_params=_cparams(("arbitrary",)),
        name="weight_prep",
    )(w)


def kernel(x, c, positions, ada_w, ada_b, norm_mix_g, norm_ffn_g, final_norm_g, attn_w_in, attn_w_out, attn_lambda_q1, attn_lambda_k1, attn_lambda_q2, attn_lambda_k2, attn_head_norm_g, mlstm_w_in, mlstm_conv_w, mlstm_conv_b, mlstm_gate_b, mlstm_head_norm_g, mlstm_w_out, moe_w_group, moe_b_group, moe_w_expert, moe_b_expert, moe_w_gu, moe_w_down):
    bsz, seq, d = x.shape
    n = bsz * seq
    depth = ada_w.shape[0]
    mod = adaln(c, ada_w, ada_b)
    xf = x.reshape(n, d)
    for i in range(depth):
        jm = i // 2
        if i % 2 == 0:
            qk_cols = 2 * DA_HEADS * 2 * DA_HEAD_DIM
            tabs = _rope_tables(positions)
            w_in = weight_prep(attn_w_in[jm], attn_w_in.shape[2], pair_cols=qk_cols)
            qkv = norm_matmul(xf, norm_mix_g[i], mod[i], w_in, seq, rope=(*tabs, qk_cols, qk_cols // 2))
            lam_init = 0.8 - 0.6 * math.exp(-0.3 * i)
            mixed = diff_attention(qkv.reshape(bsz, seq, -1), attn_lambda_q1[jm], attn_lambda_k1[jm],
                                   attn_lambda_q2[jm], attn_lambda_k2[jm], attn_head_norm_g[jm], lam_init)
            w_out = attn_w_out[jm]
        else:
            qk_cols = 2 * ML_HEADS * ML_QK_DIM
            main_cols = qk_cols + 2 * ML_HEADS * ML_V_DIM
            w_in_t = jnp.swapaxes(mlstm_w_in[jm], 0, 1)
            ngate = 2 * ML_HEADS
            w_gate = jnp.concatenate([w_in_t[main_cols:], jnp.zeros((LANES - ngate, d), F32)], axis=0)
            b_gate = jnp.concatenate([mlstm_gate_b[jm], jnp.zeros((LANES - ngate,), F32)]).reshape(1, LANES)
            proj, gates, gates_t = norm_matmul(xf, norm_mix_g[i], mod[i],
                                               weight_prep(w_in_t, main_cols, w_rows=True), seq,
                                               extra=(w_gate, b_gate), w_rows=True)
            proj = proj.reshape(bsz, seq, main_cols)
            qk = conv_silu(proj, mlstm_conv_w[jm], mlstm_conv_b[jm])
            mixed = mlstm(qk, proj, gates.reshape(bsz, seq, LANES), gates_t, mlstm_head_norm_g[jm])
            w_out = mlstm_w_out[jm]
        w_r, b_r = moe_router_weights(moe_w_group[i], moe_b_group[i], moe_w_expert[i], moe_b_expert[i])
        xf = matmul_res(mixed.reshape(n, -1), w_out.astype(BF16), xf, mod[i], seq, gate_row=2)
        xf = hier_moe_layer(xf, norm_ffn_g[i], mod[i], w_r, b_r, moe_w_gu, moe_w_down, i, final_norm_g, seq,
                            final_norm=(i == depth - 1))
    return xf.reshape(bsz, seq, d)
```

```python
import functools
import math

import jax
import jax.numpy as jnp
from jax import lax
from jax.experimental import pallas as pl
from jax.experimental.pallas import tpu as pltpu

F32 = jnp.float32
BF16 = jnp.bfloat16
U32 = jnp.uint32
I32 = jnp.int32

NORM_EPS = 1e-6
ROPE_THETA = 500000.0
ATTN_CHUNK = 64
DA_HEADS = 8
DA_HEAD_DIM = 128
ROPE_DIM = 32
ML_HEADS = 8
ML_QK_DIM = 128
ML_V_DIM = 256
ML_CONV = 4
MOE_GROUPS = 4
MOE_PER_GROUP = 8
MOE_EXPERTS = 32
LANES = 128
NEG_BIG = -1e30
LOG2_E = math.log2(math.e)

VMEM_LIMIT = 56 * 1024 * 1024


def _cparams(sem):
    return pltpu.CompilerParams(dimension_semantics=sem, vmem_limit_bytes=VMEM_LIMIT)


def _split_hi_lo(a):
    hi = a.astype(BF16)
    lo = (a - hi.astype(F32)).astype(BF16)
    return hi, lo


def _dot3(a, w, w_rows=False):
    ah, al = _split_hi_lo(a)
    wh, wl = _split_hi_lo(w)
    contract = (((1,), (1 if w_rows else 0,)), ((), ()))
    d = functools.partial(lax.dot_general, dimension_numbers=contract, preferred_element_type=F32)
    return d(ah, wh) + (d(ah, wl) + d(al, wh))


def _pack_bf16_pair(lo_f32, hi_f32):
    lo_bits = lax.bitcast_convert_type(lo_f32.astype(BF16).astype(F32), U32)
    hi_bits = lax.bitcast_convert_type(hi_f32.astype(BF16).astype(F32), U32)
    return hi_bits | (lo_bits >> 16)


def _unpack_bf16_pair(word):
    lo = lax.bitcast_convert_type(word << 16, F32)
    hi = lax.bitcast_convert_type(word & jnp.uint32(0xFFFF0000), F32)
    return lo, hi


ROW_TILE = 8


def _store_token_rows(ref, words):
    t = words.shape[0]
    for sub in range(ROW_TILE):
        ref[pl.ds(sub, t, stride=ROW_TILE), :] = words[:, sub * LANES:(sub + 1) * LANES]


def _load_token_rows(ref, sub, t):
    return ref[pl.ds(sub, t, stride=ROW_TILE), :]


def _rms_modulate(x, g, shift, scale):
    ms = jnp.mean(x * x, axis=-1, keepdims=True)
    y = x * lax.rsqrt(ms + NORM_EPS) * g
    return y * (1.0 + scale) + shift


def _adaln_kernel(c_ref, w_ref, b_ref, o_ref):
    c = c_ref[...]
    cond = c * jax.nn.sigmoid(c)
    acc = jnp.dot(cond.astype(BF16), w_ref[...].astype(BF16), preferred_element_type=F32)
    o_ref[...] = acc + b_ref[...]


def adaln(c, ada_w, ada_b, tn=1024):
    depth, d, n6 = ada_w.shape
    bsz = c.shape[0]
    rows = 8
    cp = jnp.zeros((rows, d), F32).at[:bsz].set(c)
    out = pl.pallas_call(
        _adaln_kernel,
        grid=(depth, n6 // tn),
        in_specs=[pl.BlockSpec((rows, d), lambda l, j: (0, 0)),
                  pl.BlockSpec((None, d, tn), lambda l, j: (l, 0, j)),
                  pl.BlockSpec((None, 1, tn), lambda l, j: (l, 0, j))],
        out_specs=pl.BlockSpec((None, rows, tn), lambda l, j: (l, 0, j)),
        out_shape=jax.ShapeDtypeStruct((depth, rows, n6), F32),
        compiler_params=_cparams(("arbitrary", "arbitrary")),
        name="adaln",
    )(cp, ada_w, ada_b.reshape(depth, 1, n6))
    return out[:, :bsz].reshape(depth, bsz, 6, d)


ROPE_ROWS = 64


def _norm_matmul_kernel(*refs, rope_tiles, q_tiles, has_extra, tn, nj, nsteps, w_rows):
    if has_extra:
        x_ref, g_ref, mod_ref, w_ref, c_ref, s_ref, we_ref, be_ref, o_ref, oe_ref, oet_ref, h_scr, acc_scr = refs
    else:
        x_ref, g_ref, mod_ref, w_ref, c_ref, s_ref, o_ref, h_scr, acc_scr = refs
    t = pl.program_id(0)
    j = t % nj

    @pl.when((j == 0) & (t < nsteps))
    def _():
        h = _rms_modulate(x_ref[...], g_ref[...], mod_ref[0:1, :], mod_ref[1:2, :])
        h_scr[...] = h.astype(BF16)
        if has_extra:
            extra = _dot3(h, we_ref[...], w_rows) + be_ref[...]
            oe_ref[...] = extra
            oet_ref[...] = extra.T

    @pl.when(t == 0)
    def _():
        acc_scr[...] = jnp.zeros_like(acc_scr)

    contract = (((1,), (1 if w_rows else 0,)), ((), ()))
    acc = lax.dot_general(h_scr[...], w_ref[...], contract, preferred_element_type=F32)
    tm = acc_scr.shape[0]
    if rope_tiles:
        jp = (t + nj - 1) % nj
        rope_on = jp < rope_tiles
        scale = jnp.where(jp < q_tiles, DA_HEAD_DIM ** -0.5 * LOG2_E, 1.0).astype(F32)
    for r in range(tm // ROPE_ROWS):
        rows = slice(r * ROPE_ROWS, (r + 1) * ROPE_ROWS)
        if rope_tiles:
            cs = jnp.where(rope_on, c_ref[rows, :] * scale, 1.0)
            sn = jnp.where(rope_on, s_ref[rows, :] * scale, 0.0)
        for m in range(tn // LANES):
            cols = slice(m * LANES, (m + 1) * LANES)
            prev = acc_scr[rows, cols]
            if rope_tiles:
                prev = prev * cs + pltpu.roll(prev, LANES // 2, 1) * sn
            o_ref[rows, cols] = prev.astype(o_ref.dtype)
    acc_scr[...] = acc


def norm_matmul(x, g, mod, w, seq, rope=None, extra=None, w_rows=False, tm=1024, tn=512):
    n, k = x.shape
    m = w.shape[0 if w_rows else 1]
    tiles_per_seq = seq // tm
    nj = m // tn
    nsteps = (n // tm) * nj
    has_extra = extra is not None
    cur = lambda t: jnp.minimum(t, nsteps - 1)
    prv = lambda t: jnp.maximum(t - 1, 0)
    if rope is None:
        dummy = jnp.zeros((8, LANES), F32)
        tabs = (dummy, dummy)
        tab_spec = pl.BlockSpec((8, LANES), lambda t: (0, 0))
        rope_tiles = q_tiles = 0
    else:
        tabs = rope[:2]
        tab_spec = pl.BlockSpec((tm, LANES), lambda t: (prv(t) // nj, 0))
        rope_tiles, q_tiles = rope[2] // tn, rope[3] // tn
    in_specs = [pl.BlockSpec((tm, k), lambda t: (cur(t) // nj, 0)),
                pl.BlockSpec((1, k), lambda t: (0, 0)),
                pl.BlockSpec((None, 6, k), lambda t: (cur(t) // nj // tiles_per_seq, 0, 0)),
                (pl.BlockSpec((tn, k), lambda t: (cur(t) % nj, 0)) if w_rows
                 else pl.BlockSpec((k, tn), lambda t: (0, cur(t) % nj))),
                tab_spec, tab_spec]
    args = [x, g.reshape(1, k), mod, w, *tabs]
    out_specs = pl.BlockSpec((tm, tn), lambda t: (prv(t) // nj, prv(t) % nj))
    out_shape = jax.ShapeDtypeStruct((n, m), BF16)
    if has_extra:
        we, be = extra
        in_specs += [pl.BlockSpec((LANES, k) if w_rows else (k, LANES), lambda t: (0, 0)),
                     pl.BlockSpec((1, LANES), lambda t: (0, 0))]
        args += [we, be]
        out_specs = [out_specs, pl.BlockSpec((tm, LANES), lambda t: (cur(t) // nj, 0)),
                     pl.BlockSpec((LANES, tm), lambda t: (0, cur(t) // nj))]
        out_shape = [out_shape, jax.ShapeDtypeStruct((n, LANES), F32), jax.ShapeDtypeStruct((LANES, n), F32)]
    return pl.pallas_call(
        functools.partial(_norm_matmul_kernel, rope_tiles=rope_tiles, q_tiles=q_tiles,
                          has_extra=has_extra, tn=tn, nj=nj, nsteps=nsteps, w_rows=w_rows),
        grid=(nsteps + 1,),
        in_specs=in_specs,
        out_specs=out_specs,
        out_shape=out_shape,
        scratch_shapes=[pltpu.VMEM((tm, k), BF16), pltpu.VMEM((tm, tn), F32)],
        compiler_params=_cparams(("arbitrary",)),
        name="norm_matmul",
    )(*args)


def _router_kernel(x_ref, g_ref, mod_ref, w_ref, b_ref, hp_ref, rt_ref, cnt_ref, run_scr):
    @pl.when(pl.program_id(0) == 0)
    def _():
        run_scr[...] = jnp.zeros_like(run_scr)

    h = _rms_modulate(x_ref[...], g_ref[...], mod_ref[3:4, :], mod_ref[4:5, :])
    half = h.shape[1] // 2
    _store_token_rows(hp_ref, _pack_bf16_pair(h[:, :half], h[:, half:]))
    logits = _dot3(h, w_ref[...]) + b_ref[...]
    tm = logits.shape[0]
    lane = lax.broadcasted_iota(I32, logits.shape, 1).astype(F32)
    ng = float(MOE_GROUPS)
    is_g = lane < ng
    gl = jnp.where(is_g, logits, NEG_BIG)
    gmax = jnp.max(gl, axis=-1, keepdims=True)
    grp = jnp.min(jnp.where(gl == gmax, lane, float(LANES)), axis=-1, keepdims=True)
    p_group = 1.0 / jnp.sum(jnp.where(is_g, jnp.exp(gl - gmax), 0.0), axis=-1, keepdims=True)
    lo = ng + float(MOE_PER_GROUP) * grp
    el = jnp.where((lane >= lo) & (lane < lo + float(MOE_PER_GROUP)), logits, NEG_BIG)
    v1 = jnp.max(el, axis=-1, keepdims=True)
    i1 = jnp.min(jnp.where(el == v1, lane, float(LANES)), axis=-1, keepdims=True)
    el2 = jnp.where(lane == i1, NEG_BIG, el)
    v2 = jnp.max(el2, axis=-1, keepdims=True)
    i2 = jnp.min(jnp.where(el2 == v2, lane, float(LANES)), axis=-1, keepdims=True)
    ex = jnp.exp(v2 - v1)
    w1 = p_group / (1.0 + ex)
    w2 = p_group * (ex / (1.0 + ex))
    oh1 = (lane == i1).astype(F32)
    oh2 = (lane == i2).astype(F32)
    oh = oh1 + oh2
    r_i = lax.broadcasted_iota(I32, (tm, tm), 0)
    c_i = lax.broadcasted_iota(I32, (tm, tm), 1)
    before = (c_i < r_i).astype(BF16)
    prior = jnp.dot(before, oh.astype(BF16), preferred_element_type=F32) + run_scr[...]
    rank1 = jnp.sum(oh1 * prior, axis=-1, keepdims=True)
    rank2 = jnp.sum(oh2 * prior, axis=-1, keepdims=True)
    run_scr[...] += jnp.sum(oh, axis=0, keepdims=True)
    cnt_ref[...] = run_scr[...]
    rt_ref[...] = jnp.where(lane == 0.0, i1 - ng,
                  jnp.where(lane == 1.0, i2 - ng,
                  jnp.where(lane == 2.0, w1,
                  jnp.where(lane == 3.0, w2,
                  jnp.where(lane == 4.0, rank1,
                  jnp.where(lane == 5.0, rank2, 0.0))))))


def moe_router(x, g, mod, w_r, b_r, seq, tm=512):
    n, d = x.shape
    tiles_per_seq = seq // tm
    return pl.pallas_call(
        _router_kernel,
        grid=(n // tm,),
        in_specs=[pl.BlockSpec((tm, d), lambda i: (i, 0)),
                  pl.BlockSpec((1, d), lambda i: (0, 0)),
                  pl.BlockSpec((None, 6, d), lambda i: (i // tiles_per_seq, 0, 0)),
                  pl.BlockSpec((d, LANES), lambda i: (0, 0)),
                  pl.BlockSpec((1, LANES), lambda i: (0, 0))],
        out_specs=[pl.BlockSpec((tm * ROW_TILE, LANES), lambda i: (i, 0)),
                   pl.BlockSpec((tm, LANES), lambda i: (i, 0)),
                   pl.BlockSpec((1, LANES), lambda i: (0, 0))],
        out_shape=[jax.ShapeDtypeStruct((n * ROW_TILE, LANES), U32),
                   jax.ShapeDtypeStruct((n, LANES), F32),
                   jax.ShapeDtypeStruct((1, LANES), F32)],
        scratch_shapes=[pltpu.VMEM((1, LANES), F32)],
        compiler_params=_cparams(("arbitrary",)),
        name="moe_router",
    )(x, g.reshape(1, d), mod, w_r, b_r)


def _matmul_res_kernel(a_ref, w_ref, res_ref, mod_ref, o_ref, *, gate_row):
    acc = jnp.dot(a_ref[...], w_ref[...], preferred_element_type=F32)
    o_ref[...] = res_ref[...] + mod_ref[gate_row:gate_row + 1, :] * acc


def matmul_res(a, w, res, mod, seq, gate_row, tm=512, tn=2048):
    n, k = a.shape
    m = w.shape[1]
    tiles_per_seq = seq // tm
    return pl.pallas_call(
        functools.partial(_matmul_res_kernel, gate_row=gate_row),
        grid=(n // tm, m // tn),
        in_specs=[pl.BlockSpec((tm, k), lambda i, j: (i, 0)),
                  pl.BlockSpec((k, tn), lambda i, j: (0, j)),
                  pl.BlockSpec((tm, tn), lambda i, j: (i, j)),
                  pl.BlockSpec((None, 6, tn), lambda i, j: (i // tiles_per_seq, 0, j))],
        out_specs=pl.BlockSpec((tm, tn), lambda i, j: (i, j)),
        out_shape=jax.ShapeDtypeStruct((n, m), F32),
        compiler_params=_cparams(("arbitrary", "arbitrary")),
        name="matmul_res",
    )(a, w, res, mod)


ATTN_GROUPS = (8, 4, 2)


def _grouped_loop(n, body, groups):
    done = 0
    for group in groups:
        trips = (n - done) // group

        def grouped(gi, c, group=group, done=done):
            for t in range(group):
                c = body(done + gi * group + t, c)
            return c

        lax.fori_loop(0, trips, grouped, 0)
        done = done + trips * group
    lax.fori_loop(done, n, body, 0)


def _attn_kernel(q_ref, k_ref, v_ref, lq1_ref, lk1_ref, lq2_ref, lk2_ref, g_ref, o_ref,
                 s_scr, m_scr, l_scr, acc_scr, bias_scr, *, lam_init, tq, tk, seq):
    d = DA_HEAD_DIM
    nlane = tk // LANES
    ndiag = tq // tk
    lam = (jnp.exp(jnp.sum(lq1_ref[...] * lk1_ref[...], axis=-1, keepdims=True))
           - jnp.exp(jnp.sum(lq2_ref[...] * lk2_ref[...], axis=-1, keepdims=True)) + lam_init)
    shift = ATTN_CHUNK.bit_length() - 1
    row_chunk = jnp.right_shift(lax.broadcasted_iota(I32, (tq, tk), 0), shift)
    col_chunk = jnp.right_shift(lax.broadcasted_iota(I32, (tq, tk), 1), shift)
    bias_scr[0] = jnp.zeros((tq, tk), F32)
    for t in range(ndiag):
        bias_scr[t + 1] = jnp.where(col_chunk + t * (tk // ATTN_CHUNK) <= row_chunk, 0.0, NEG_BIG)
    nt = (((1,), (1,)), ((), ()))

    def lane_fold(a, op):
        part = a[:, 0:LANES]
        for cb in range(1, nlane):
            part = op(part, a[:, cb * LANES:(cb + 1) * LANES])
        return part

    def q_body(qi, _):
        qs = pl.multiple_of(qi * tq, tq)
        qm = (q_ref[pl.ds(qs, tq), 0:d], q_ref[pl.ds(qs, tq), d:2 * d])
        m_scr[...] = jnp.full_like(m_scr, NEG_BIG)

        nfull = qi * ndiag

        def score_body(j, c):
            ks = pl.multiple_of(j * tk, tk)
            bias = bias_scr[jnp.maximum(j - nfull + 1, 0)]
            for mp in range(2):
                k = k_ref[pl.ds(ks, tk), mp * d:(mp + 1) * d]
                s = lax.dot_general(qm[mp], k, nt, preferred_element_type=F32) + bias
                s_scr[mp, j] = s
                m_scr[mp] = jnp.maximum(m_scr[mp], lane_fold(s, jnp.maximum))
            return c

        _grouped_loop(nfull + ndiag, score_body, ATTN_GROUPS)
        m = [jnp.max(m_scr[mp], axis=-1, keepdims=True) for mp in range(2)]
        l_scr[...] = jnp.zeros_like(l_scr)
        acc_scr[...] = jnp.zeros_like(acc_scr)

        def pv_body(j, c):
            ks = pl.multiple_of(j * tk, tk)
            v = v_ref[pl.ds(ks, tk), :]
            for mp in range(2):
                p = jnp.exp2(s_scr[mp, j] - m[mp])
                l_scr[mp] += lane_fold(p, jnp.add)
                acc_scr[mp] += jnp.dot(p.astype(BF16), v, preferred_element_type=F32)
            return c

        _grouped_loop(nfull + ndiag, pv_body, ATTN_GROUPS)
        l = [jnp.sum(l_scr[mp], axis=-1, keepdims=True) for mp in range(2)]
        o = acc_scr[0] / l[0] - lam * (acc_scr[1] / l[1])
        ms = jnp.mean(o * o, axis=-1, keepdims=True)
        o = o * lax.rsqrt(ms + NORM_EPS) * g_ref[...] * (1.0 - lam_init)
        o_ref[pl.ds(qs, tq), :] = o.astype(o_ref.dtype)
        return 0

    lax.fori_loop(0, seq // tq, q_body, 0)


def diff_attention(qkv, lq1, lk1, lq2, lk2, head_g, lam_init, tq=512, tk=256):
    bsz, seq, _ = qkv.shape
    h, dv = DA_HEADS, 2 * DA_HEAD_DIM
    vec = lambda a: a.reshape(1, -1).astype(F32)
    small = lambda n: pl.BlockSpec((1, n), lambda b, hh: (0, 0))
    return pl.pallas_call(
        functools.partial(_attn_kernel, lam_init=lam_init, tq=tq, tk=tk, seq=seq),
        grid=(bsz, h),
        in_specs=[pl.BlockSpec((None, seq, dv), lambda b, hh: (b, 0, hh)),
                  pl.BlockSpec((None, seq, dv), lambda b, hh: (b, 0, h + hh)),
                  pl.BlockSpec((None, seq, dv), lambda b, hh: (b, 0, 2 * h + hh)),
                  small(DA_HEAD_DIM), small(DA_HEAD_DIM), small(DA_HEAD_DIM), small(DA_HEAD_DIM),
                  small(dv)],
        out_specs=pl.BlockSpec((None, seq, dv), lambda b, hh: (b, 0, hh)),
        out_shape=jax.ShapeDtypeStruct((bsz, seq, h * dv), BF16),
        scratch_shapes=[pltpu.VMEM((2, seq // tk, tq, tk), F32), pltpu.VMEM((2, tq, LANES), F32),
                        pltpu.VMEM((2, tq, LANES), F32), pltpu.VMEM((2, tq, dv), F32),
                        pltpu.VMEM((tq // tk + 1, tq, tk), F32)],
        compiler_params=_cparams(("arbitrary", "arbitrary")),
        name="diff_attention",
    )(qkv, qkv, qkv, vec(lq1), vec(lk1), vec(lq2), vec(lk2), vec(head_g))


def _conv_silu_kernel(x_ref, w_ref, b_ref, o_ref, *, k_tile0):
    j = pl.program_id(1)
    scale = jnp.where(j >= k_tile0, ML_QK_DIM ** -0.5, 1.0).astype(F32)
    taps = [w_ref[ML_CONV - 1 - s:ML_CONV - s, :] for s in range(ML_CONV)]

    def conv(x, shifted):
        y = x * taps[0] + b_ref[...]
        for s in range(1, ML_CONV):
            y = y + shifted(x, s) * taps[s]
        return (y * jax.nn.sigmoid(y) * scale).astype(o_ref.dtype)

    o_ref[...] = conv(x_ref[...].astype(F32), lambda x, s: pltpu.roll(x, s, 0))
    head = ROW_TILE
    row = lax.broadcasted_iota(I32, (head, x_ref.shape[1]), 0)
    o_ref[0:head, :] = conv(x_ref[0:head, :].astype(F32),
                            lambda x, s: jnp.where(row >= s, pltpu.roll(x, s, 0), 0.0))


def conv_silu(proj, conv_w, conv_b, tc=128):
    bsz, seq, _ = proj.shape
    cols = conv_w.shape[1]
    return pl.pallas_call(
        functools.partial(_conv_silu_kernel, k_tile0=(cols // 2) // tc),
        grid=(bsz, cols // tc),
        in_specs=[pl.BlockSpec((None, seq, tc), lambda b, j: (b, 0, j)),
                  pl.BlockSpec((ML_CONV, tc), lambda b, j: (0, j)),
                  pl.BlockSpec((1, tc), lambda b, j: (0, j))],
        out_specs=pl.BlockSpec((None, seq, tc), lambda b, j: (b, 0, j)),
        out_shape=jax.ShapeDtypeStruct((bsz, seq, cols), BF16),
        compiler_params=_cparams(("arbitrary", "arbitrary")),
        name="conv_silu",
    )(proj, conv_w, conv_b.reshape(1, cols))


def _mlstm_kernel(q_ref, k_ref, v_ref, op_ref, gc_ref, gr_ref, hg_ref, o_ref,
                  cx_scr, m_scr, *, chunk):
    c = pl.program_id(1)
    nh, dqk, dv = ML_HEADS, ML_QK_DIM, ML_V_DIM

    @pl.when(c == 0)
    def _():
        cx_scr[...] = jnp.zeros_like(cx_scr)
        m_scr[...] = jnp.zeros_like(m_scr)

    gc = gc_ref[...]
    gr = gr_ref[...]
    lf_c = jax.nn.log_sigmoid(gc)
    lf_r = jax.nn.log_sigmoid(gr)
    r_i = lax.broadcasted_iota(I32, (chunk, chunk), 0)
    c_i = lax.broadcasted_iota(I32, (chunk, chunk), 1)
    causal = c_i <= r_i
    tril = causal.astype(F32)
    triu = (r_i <= c_i).astype(F32)
    b_c = _dot3(tril, lf_c)
    b_r = _dot3(lf_r, triu)
    nt = (((1,), (1,)), ((), ()))
    tn_ = (((0,), (0,)), ((), ()))
    ones_l = jnp.ones((chunk, LANES), BF16)
    ones_v = jnp.ones((dv, LANES), BF16)
    wide = lambda a: jnp.concatenate([a] * (dv // LANES), axis=1)

    for h in range(nh):
        q = q_ref[:, h * dqk:(h + 1) * dqk]
        k = k_ref[:, h * dqk:(h + 1) * dqk]
        v_ext = jnp.concatenate([v_ref[:, h * dv:(h + 1) * dv], ones_l], axis=1)
        bc = b_c[:, nh + h:nh + h + 1]
        br = b_r[nh + h:nh + h + 1, :]
        ig_c = gc[:, h:h + 1]
        ig_r = gr[h:h + 1, :]
        m_prev = m_scr[h:h + 1, :]
        dmat = jnp.where(causal, bc - br + ig_r, NEG_BIG)
        inter = bc + m_prev
        m_t = jnp.maximum(inter, jnp.max(dmat, axis=-1, keepdims=True))
        w = jnp.exp(dmat - m_t)
        s = lax.dot_general(q, k, nt, preferred_element_type=F32) * w
        decay = jnp.exp(inter - m_t)
        cx = cx_scr[h]
        tot = (jnp.dot(s.astype(BF16), v_ext, preferred_element_type=F32)
               + decay * jnp.dot(q, cx.astype(BF16), preferred_element_type=F32))
        den = jnp.maximum(jnp.abs(tot[:, dv:]), jnp.exp(-m_t))
        hh = tot[:, :dv] / wide(den)
        b_last = bc[chunk - 1:chunk, :]
        g = b_last - bc + ig_c
        m_new = jnp.maximum(b_last + m_prev, jnp.max(g, axis=0, keepdims=True))
        carry_decay = jnp.exp(b_last + m_prev - m_new)
        wg = jnp.exp(g - m_new)
        wv = (wg * v_ext.astype(F32)).astype(BF16)
        cx_scr[h] = carry_decay * cx + lax.dot_general(k, wv, tn_, preferred_element_type=F32)
        m_scr[h:h + 1, :] = m_new
        ms = jnp.dot((hh * hh).astype(BF16), ones_v, preferred_element_type=F32) * (1.0 / dv)
        hn = hh * wide(lax.rsqrt(ms + NORM_EPS)) * hg_ref[:, h * dv:(h + 1) * dv]
        og = jax.nn.sigmoid(op_ref[:, h * dv:(h + 1) * dv].astype(F32))
        o_ref[:, h * dv:(h + 1) * dv] = (og * hn).astype(o_ref.dtype)


def mlstm(qk, proj, gates_c, gates_r, head_g, chunk=256):
    bsz, seq, _ = qk.shape
    nh = ML_HEADS
    qw, vw = nh * ML_QK_DIM, nh * ML_V_DIM
    v_blk = (2 * qw) // vw
    return pl.pallas_call(
        functools.partial(_mlstm_kernel, chunk=chunk),
        grid=(bsz, seq // chunk),
        in_specs=[pl.BlockSpec((None, chunk, qw), lambda b, c: (b, c, 0)),
                  pl.BlockSpec((None, chunk, qw), lambda b, c: (b, c, 1)),
                  pl.BlockSpec((None, chunk, vw), lambda b, c: (b, c, v_blk)),
                  pl.BlockSpec((None, chunk, vw), lambda b, c: (b, c, v_blk + 1)),
                  pl.BlockSpec((None, chunk, LANES), lambda b, c: (b, c, 0)),
                  pl.BlockSpec((2 * nh, chunk), lambda b, c: (0, b * (seq // chunk) + c)),
                  pl.BlockSpec((1, vw), lambda b, c: (0, 0))],
        out_specs=pl.BlockSpec((None, chunk, vw), lambda b, c: (b, c, 0)),
        out_shape=jax.ShapeDtypeStruct((bsz, seq, vw), BF16),
        scratch_shapes=[pltpu.VMEM((nh, ML_QK_DIM, ML_V_DIM + LANES), F32),
                        pltpu.VMEM((nh, 1), F32)],
        compiler_params=_cparams(("arbitrary", "arbitrary")),
        name="mlstm",
    )(qk, qk, proj, proj, gates_c, gates_r, head_g.reshape(1, vw))


def _dest_kernel(rt_ref, ps_ref, o_ref):
    rt = rt_ref[...]
    lane = lax.broadcasted_iota(I32, rt.shape, 1).astype(F32)
    ng = float(MOE_GROUPS)
    ps = ps_ref[...]
    d1 = jnp.sum(jnp.where(lane == rt[:, 0:1] + ng, ps, 0.0), axis=-1, keepdims=True) + rt[:, 4:5]
    d2 = jnp.sum(jnp.where(lane == rt[:, 1:2] + ng, ps, 0.0), axis=-1, keepdims=True) + rt[:, 5:6]
    o_ref[...] = jnp.where(lane == 0.0, d1, jnp.where(lane == 1.0, d2, 0.0)).astype(I32)


def moe_dest(route, pad_start_lanes, tm=2048):
    n = route.shape[0]
    return pl.pallas_call(
        _dest_kernel,
        grid=(n // tm,),
        in_specs=[pl.BlockSpec((tm, LANES), lambda i: (i, 0)),
                  pl.BlockSpec((1, LANES), lambda i: (0, 0))],
        out_specs=pl.BlockSpec((tm, LANES), lambda i: (i, 0)),
        out_shape=jax.ShapeDtypeStruct((n, LANES), I32),
        compiler_params=_cparams(("arbitrary",)),
        name="moe_dest",
    )(route, pad_start_lanes)


DMA_UNROLL = 8


def _wait_rows(src_rows, dst_rows, sem, copies):
    for _ in range(copies):
        pltpu.make_async_copy(src_rows, dst_rows, sem).wait()


def _dispatch_kernel(zb_ref, dest_ref, hp_ref, xin_ref, zero_scr, ring, sems, *, tt, tm, nb, ntiles):
    i = pl.program_id(0)
    slot = i % 2
    blk_rows = tm * ROW_TILE
    tile_rows = tt * ROW_TILE

    @pl.when(i == 0)
    def _():
        zero_scr[...] = jnp.zeros_like(zero_scr)

        def zero_copy(blk):
            return pltpu.make_async_copy(zero_scr, xin_ref.at[pl.ds(pl.multiple_of(blk * blk_rows, blk_rows), blk_rows)],
                                         sems.at[2])

        def start(blk, c):
            @pl.when(zb_ref[blk] == 1)
            def _():
                zero_copy(blk).start()
            return c

        def wait(blk, c):
            @pl.when(zb_ref[blk] == 1)
            def _():
                zero_copy(blk).wait()
            return c

        lax.fori_loop(0, nb, start, 0)
        lax.fori_loop(0, nb, wait, 0)

    ring[slot] = hp_ref[...]

    def start_rows(r, c):
        src = ring.at[slot, pl.ds(pl.multiple_of(r * ROW_TILE, ROW_TILE), ROW_TILE)]
        for kk in range(2):
            row = pl.multiple_of(dest_ref[0, 0, 2 * r + kk] * ROW_TILE, ROW_TILE)
            pltpu.make_async_copy(src, xin_ref.at[pl.ds(row, ROW_TILE)], sems.at[slot]).start()
        return c

    lax.fori_loop(0, tt, start_rows, 0, unroll=DMA_UNROLL)
    whole = xin_ref.at[pl.ds(0, tile_rows)]

    @pl.when(i > 0)
    def _():
        _wait_rows(ring.at[1 - slot], whole, sems.at[1 - slot], 2)

    @pl.when(i == ntiles - 1)
    def _():
        _wait_rows(ring.at[slot], whole, sems.at[slot], 2)


def moe_dispatch(hpack, dest, zero_blocks, rows, tm, tt=256):
    n = hpack.shape[0] // ROW_TILE
    grid_spec = pltpu.PrefetchScalarGridSpec(
        num_scalar_prefetch=1,
        grid=(n // tt,),
        in_specs=[pl.BlockSpec((1, 1, 2 * tt), lambda i, zb: (i, 0, 0), memory_space=pltpu.SMEM),
                  pl.BlockSpec((tt * ROW_TILE, LANES), lambda i, zb: (i, 0))],
        out_specs=pl.BlockSpec(memory_space=pl.ANY),
        scratch_shapes=[pltpu.VMEM((tm * ROW_TILE, LANES), U32), pltpu.VMEM((2, tt * ROW_TILE, LANES), U32),
                        pltpu.SemaphoreType.DMA((3,))],
    )
    return pl.pallas_call(
        functools.partial(_dispatch_kernel, tt=tt, tm=tm, nb=rows // tm, ntiles=n // tt),
        grid_spec=grid_spec,
        out_shape=jax.ShapeDtypeStruct((rows * ROW_TILE, LANES), U32),
        compiler_params=_cparams(("arbitrary",)),
        name="moe_dispatch",
    )(zero_blocks, dest.reshape(n // tt, 1, 2 * tt), hpack)


def _expert_kernel(be_ref, first_ref, nxt_ref, nv_ref, nu_ref, x_ref, wgu_hbm, wd_hbm, y_ref,
                   wgu_bf, wd_bf, stg_gu, stg_d, xs_scr, acc_scr, sems, *, layer, th, cr):
    i = pl.program_id(0)

    def fetch(e):
        return (pltpu.make_async_copy(wgu_hbm.at[layer, e], stg_gu, sems.at[0]),
                pltpu.make_async_copy(wd_hbm.at[layer, e], stg_d, sems.at[1]))

    @pl.when(i < nu_ref[0])
    def _():
        @pl.when(first_ref[i] == 1)
        def _():
            @pl.when(i == 0)
            def _():
                for cp in fetch(be_ref[i]):
                    cp.start()

            for cp in fetch(be_ref[i]):
                cp.wait()

            def cast_rows(src, dst):
                def body(r, c):
                    rs = pl.multiple_of(r * cr, cr)
                    dst[pl.ds(rs, cr), :] = src[pl.ds(rs, cr), :].astype(BF16)
                    return c
                lax.fori_loop(0, src.shape[0] // cr, body, 0)

            cast_rows(stg_gu, wgu_bf)
            cast_rows(stg_d, wd_bf)

            @pl.when(nxt_ref[i] >= 0)
            def _():
                for cp in fetch(nxt_ref[i]):
                    cp.start()

        tm = xs_scr.shape[0]
        half = xs_scr.shape[1] // 2
        hid = wd_bf.shape[0]

        def mlp(rows):
            for sub in range(ROW_TILE):
                lo, hi = _unpack_bf16_pair(_load_token_rows(x_ref, sub, rows))
                xs_scr[0:rows, sub * LANES:(sub + 1) * LANES] = lo.astype(BF16)
                xs_scr[0:rows, half + sub * LANES:half + (sub + 1) * LANES] = hi.astype(BF16)
            xs = xs_scr[0:rows, :]
            for c in range(hid // th):
                gt = jnp.dot(xs, wgu_bf[:, c * th:(c + 1) * th], preferred_element_type=F32)
                up = jnp.dot(xs, wgu_bf[:, hid + c * th:hid + (c + 1) * th], preferred_element_type=F32)
                act = (gt * jax.nn.sigmoid(gt) * up).astype(BF16)
                part = jnp.dot(act, wd_bf[c * th:(c + 1) * th, :], preferred_element_type=F32)
                if c == 0:
                    acc_scr[0:rows, :] = part
                else:
                    acc_scr[0:rows, :] += part
            y = acc_scr[0:rows, :]
            _store_token_rows(y_ref, _pack_bf16_pair(y[:, :half], y[:, half:]))
            if rows < tm:
                y_ref[rows * ROW_TILE:, :] = jnp.zeros(((tm - rows) * ROW_TILE, LANES), U32)

        @pl.when(nv_ref[i] > tm // 2)
        def _():
            mlp(tm)

        @pl.when(nv_ref[i] <= tm // 2)
        def _():
            mlp(tm // 2)


def moe_experts(xin, sched, wgu_all, wd_all, layer, tm, th=256, cr=256):
    _, _, d, hid2 = wgu_all.shape
    hid = hid2 // 2
    nb = xin.shape[0] // (tm * ROW_TILE)
    block_e, first, nxt, valid, n_used = sched

    def blk(i, be, fi, nx, nv, nu):
        return (jnp.minimum(i, nu[0] - 1), 0)

    grid_spec = pltpu.PrefetchScalarGridSpec(
        num_scalar_prefetch=5,
        grid=(nb,),
        in_specs=[pl.BlockSpec((tm * ROW_TILE, LANES), blk),
                  pl.BlockSpec(memory_space=pl.ANY),
                  pl.BlockSpec(memory_space=pl.ANY)],
        out_specs=pl.BlockSpec((tm * ROW_TILE, LANES), blk),
        scratch_shapes=[pltpu.VMEM((d, hid2), BF16), pltpu.VMEM((hid, d), BF16),
                        pltpu.VMEM((d, hid2), F32), pltpu.VMEM((hid, d), F32),
                        pltpu.VMEM((tm, d), BF16), pltpu.VMEM((tm, d), F32),
                        pltpu.SemaphoreType.DMA((2,))],
    )
    return pl.pallas_call(
        functools.partial(_expert_kernel, layer=layer, th=th, cr=cr),
        grid_spec=grid_spec,
        out_shape=jax.ShapeDtypeStruct(xin.shape, U32),
        input_output_aliases={5: 0},
        compiler_params=_cparams(("arbitrary",)),
        name="moe_experts",
    )(block_e, first, nxt, valid, n_used, xin, wgu_all, wd_all)


def _combine_kernel(dcur_ref, dnxt_ref, x_ref, rt_ref, mod_ref, fg_ref, y_ref, o_ref, ya, yb, sems,
                    *, tt, ntiles, final_norm):
    i = pl.program_id(0)
    slot = i % 2

    def issue(dref, sl):
        def body(r, c):
            dst = pl.ds(pl.multiple_of(r * ROW_TILE, ROW_TILE), ROW_TILE)
            for kk, buf in enumerate((ya, yb)):
                row = pl.multiple_of(dref[0, 0, 2 * r + kk] * ROW_TILE, ROW_TILE)
                pltpu.make_async_copy(y_ref.at[pl.ds(row, ROW_TILE)], buf.at[sl, dst], sems.at[sl]).start()
            return c
        lax.fori_loop(0, tt, body, 0, unroll=DMA_UNROLL)

    @pl.when(i == 0)
    def _():
        issue(dcur_ref, 0)

    @pl.when(i + 1 < ntiles)
    def _():
        issue(dnxt_ref, 1 - slot)

    _wait_rows(y_ref.at[pl.ds(0, tt * ROW_TILE)], ya.at[slot], sems.at[slot], 2)
    half = x_ref.shape[1] // 2
    w1 = rt_ref[:, 2:3]
    w2 = rt_ref[:, 3:4]
    ssq = jnp.zeros((tt, 1), F32)
    for sub in range(ROW_TILE):
        a_lo, a_hi = _unpack_bf16_pair(_load_token_rows(ya.at[slot], sub, tt))
        b_lo, b_hi = _unpack_bf16_pair(_load_token_rows(yb.at[slot], sub, tt))
        for base, a, b in ((sub * LANES, a_lo, b_lo), (half + sub * LANES, a_hi, b_hi)):
            cols = slice(base, base + LANES)
            out = x_ref[:, cols] + mod_ref[5:6, cols] * (a * w1 + b * w2)
            o_ref[:, cols] = out
            if final_norm:
                ssq = ssq + jnp.sum(out * out, axis=-1, keepdims=True)
    if final_norm:
        r = lax.rsqrt(ssq / (2 * half) + NORM_EPS)
        o_ref[...] = o_ref[...] * r * fg_ref[...]


def moe_combine(x, y, dest, route, mod, final_g, seq, final_norm, tt=256):
    n, d = x.shape
    tiles_per_seq = seq // tt
    ntiles = n // tt
    dest3 = dest.reshape(ntiles, 1, 2 * tt)
    return pl.pallas_call(
        functools.partial(_combine_kernel, tt=tt, ntiles=ntiles, final_norm=final_norm),
        grid=(ntiles,),
        in_specs=[pl.BlockSpec((1, 1, 2 * tt), lambda i: (i, 0, 0), memory_space=pltpu.SMEM),
                  pl.BlockSpec((1, 1, 2 * tt), lambda i: (jnp.minimum(i + 1, ntiles - 1), 0, 0),
                               memory_space=pltpu.SMEM),
                  pl.BlockSpec((tt, d), lambda i: (i, 0)),
                  pl.BlockSpec((tt, LANES), lambda i: (i, 0)),
                  pl.BlockSpec((None, 6, d), lambda i: (i // tiles_per_seq, 0, 0)),
                  pl.BlockSpec((1, d), lambda i: (0, 0)),
                  pl.BlockSpec(memory_space=pl.ANY)],
        out_specs=pl.BlockSpec((tt, d), lambda i: (i, 0)),
        out_shape=jax.ShapeDtypeStruct((n, d), F32),
        scratch_shapes=[pltpu.VMEM((2, tt * ROW_TILE, LANES), U32), pltpu.VMEM((2, tt * ROW_TILE, LANES), U32),
                        pltpu.SemaphoreType.DMA((2,))],
        compiler_params=_cparams(("arbitrary",)),
        name="moe_combine",
    )(dest3, dest3, x, route, mod, final_g.reshape(1, d), y)


def _expert_schedule(counts, tm, nb):
    ne = counts.shape[0]
    ids = jnp.arange(ne, dtype=I32)
    padded = ((counts + tm - 1) // tm) * tm
    pad_end = jnp.sum(jnp.where(ids[None, :] <= ids[:, None], padded[None, :], 0), axis=1)
    pad_start = pad_end - padded
    total = jnp.sum(padded)
    n_used = total // tm
    blk0 = jnp.arange(nb, dtype=I32) * tm
    block_e = jnp.minimum(jnp.sum((pad_end[None, :] <= blk0[:, None]).astype(I32), axis=1), ne - 1)
    onehot = block_e[:, None] == ids[None, :]
    pick = lambda v: jnp.sum(jnp.where(onehot, v[None, :], 0), axis=1)
    first = (blk0 == pick(pad_start)).astype(I32)
    later = jnp.min(jnp.where((ids[None, :] > ids[:, None]) & (counts[None, :] > 0), ids[None, :], ne), axis=1)
    nxt = pick(jnp.where(later < ne, later, -1))
    has_padding = (blk0 + tm == pick(pad_end)) & (pick(counts % tm) != 0)
    zero_blocks = ((blk0 >= total) | has_padding).astype(I32)
    valid = jnp.clip(pick(pad_start + counts) - blk0, 0, tm)
    return pad_start, zero_blocks, (block_e, first, nxt.astype(I32), valid.astype(I32),
                                    n_used.astype(I32).reshape(1))


def moe_router_weights(w_group, b_group, w_expert, b_expert):
    d = w_group.shape[0]
    ng, ne = w_group.shape[1], w_expert.shape[1]
    w_r = jnp.concatenate([w_group, w_expert, jnp.zeros((d, LANES - ng - ne), F32)], axis=1)
    b_r = jnp.concatenate([b_group, b_expert, jnp.zeros((LANES - ng - ne,), F32)]).reshape(1, LANES)
    return w_r, b_r


def hier_moe_layer(x, g, mod, w_r, b_r, wgu_all, wd_all, layer, final_g, seq, final_norm, tm=256):
    n, d = x.shape
    ng, ne = MOE_GROUPS, MOE_EXPERTS
    hpack, route, cnt = moe_router(x, g, mod, w_r, b_r, seq)
    counts = cnt[0, ng:ng + ne].astype(I32)
    rows = 2 * n + ne * tm
    pad_start, zero_blocks, sched = _expert_schedule(counts, tm, rows // tm)
    ps_lanes = jnp.concatenate([jnp.zeros((ng,), F32), pad_start.astype(F32),
                                jnp.zeros((LANES - ng - ne,), F32)]).reshape(1, LANES)
    dest = moe_dest(route, ps_lanes, tm=min(2048, n))[:, :2].reshape(-1)
    xin = moe_dispatch(hpack, dest, zero_blocks, rows, tm)
    y = moe_experts(xin, sched, wgu_all, wd_all, layer, tm)
    return moe_combine(x, y, dest, route, mod, final_g, seq, final_norm)


def _rope_tables(positions):
    half = ROPE_DIM // 2
    inv_freq = ROPE_THETA ** (-jnp.arange(half, dtype=F32) * 2.0 / ROPE_DIM)
    gap = jnp.zeros((LANES // 2 - half,), F32)
    freq = jnp.concatenate([-inv_freq, gap, inv_freq, gap])
    ang = positions.astype(F32).reshape(-1, 1) * freq[None, :]
    return jnp.cos(ang), jnp.sin(ang)


def _weight_prep_kernel(w_ref, o_ref, *, pair_tiles, tn):
    j = pl.program_id(0)
    half = ROPE_DIM // 2
    mid = LANES // 2

    @pl.when(j >= pair_tiles)
    def _():
        o_ref[...] = w_ref[...].astype(BF16)

    if pair_tiles:
        @pl.when(j < pair_tiles)
        def _():
            lane = lax.broadcasted_iota(I32, (w_ref.shape[0], LANES), 1)
            for m in range(tn // LANES):
                t = w_ref[:, m * LANES:(m + 1) * LANES]
                up = pltpu.roll(t, LANES - half, 1)
                down = pltpu.roll(t, mid - half, 1)
                new = jnp.where(lane < half, t, jnp.where(lane < mid, up, jnp.where(lane < mid + half, down, t)))
                o_ref[:, m * LANES:(m + 1) * LANES] = new.astype(BF16)


def weight_prep(w, cols, pair_cols=0, tn=512, w_rows=False):
    if w_rows:
        k = w.shape[1]
        return pl.pallas_call(
            functools.partial(_weight_prep_kernel, pair_tiles=0, tn=tn),
            grid=(cols // tn,),
            in_specs=[pl.BlockSpec((tn, k), lambda j: (j, 0))],
            out_specs=pl.BlockSpec((tn, k), lambda j: (j, 0)),
            out_shape=jax.ShapeDtypeStruct((cols, k), BF16),
            compiler_params=_cparams(("arbitrary",)),
            name="weight_prep",
        )(w)
    k = w.shape[0]
    return pl.pallas_call(
        functools.partial(_weight_prep_kernel, pair_tiles=pair_cols // tn, tn=tn),
        grid=(cols // tn,),
        in_specs=[pl.BlockSpec((k, tn), lambda j: (0, j))],
        out_specs=pl.BlockSpec((k, tn), lambda j: (0, j)),
        out_shape=jax.ShapeDtypeStruct((k, cols), BF16),
        compiler_params=_cparams(("arbitrary",)),
        name="weight_prep",
    )(w)


def kernel(x, c, positions, ada_w, ada_b, norm_mix_g, norm_ffn_g, final_norm_g, attn_w_in, attn_w_out, attn_lambda_q1, attn_lambda_k1, attn_lambda_q2, attn_lambda_k2, attn_head_norm_g, mlstm_w_in, mlstm_conv_w, mlstm_conv_b, mlstm_gate_b, mlstm_head_norm_g, mlstm_w_out, moe_w_group, moe_b_group, moe_w_expert, moe_b_expert, moe_w_gu, moe_w_down):
    bsz, seq, d = x.shape
    n = bsz * seq
    depth = ada_w.shape[0]
    mod = adaln(c, ada_w, ada_b)
    xf = x.reshape(n, d)
    for i in range(depth):
        jm = i // 2
        if i % 2 == 0:
            qk_cols = 2 * DA_HEADS * 2 * DA_HEAD_DIM
            tabs = _rope_tables(positions)
            w_in = weight_prep(attn_w_in[jm], attn_w_in.shape[2], pair_cols=qk_cols)
            qkv = norm_matmul(xf, norm_mix_g[i], mod[i], w_in, seq, rope=(*tabs, qk_cols, qk_cols // 2))
            lam_init = 0.8 - 0.6 * math.exp(-0.3 * i)
            mixed = diff_attention(qkv.reshape(bsz, seq, -1), attn_lambda_q1[jm], attn_lambda_k1[jm],
                                   attn_lambda_q2[jm], attn_lambda_k2[jm], attn_head_norm_g[jm], lam_init)
            w_out = attn_w_out[jm]
        else:
            qk_cols = 2 * ML_HEADS * ML_QK_DIM
            main_cols = qk_cols + 2 * ML_HEADS * ML_V_DIM
            w_in_t = jnp.swapaxes(mlstm_w_in[jm], 0, 1)
            ngate = 2 * ML_HEADS
            w_gate = jnp.concatenate([w_in_t[main_cols:], jnp.zeros((LANES - ngate, d), F32)], axis=0)
            b_gate = jnp.concatenate([mlstm_gate_b[jm], jnp.zeros((LANES - ngate,), F32)]).reshape(1, LANES)
            proj, gates, gates_t = norm_matmul(xf, norm_mix_g[i], mod[i],
                                               weight_prep(w_in_t, main_cols, w_rows=True), seq,
                                               extra=(w_gate, b_gate), w_rows=True)
            proj = proj.reshape(bsz, seq, main_cols)
            qk = conv_silu(proj, mlstm_conv_w[jm], mlstm_conv_b[jm])
            mixed = mlstm(qk, proj, gates.reshape(bsz, seq, LANES), gates_t, mlstm_head_norm_g[jm])
            w_out = mlstm_w_out[jm]
        w_r, b_r = moe_router_weights(moe_w_group[i], moe_b_group[i], moe_w_expert[i], moe_b_expert[i])
        xf = matmul_res(mixed.reshape(n, -1), w_out.astype(BF16), xf, mod[i], seq, gate_row=2)
        xf = hier_moe_layer(xf, norm_ffn_g[i], mod[i], w_r, b_r, moe_w_gu, moe_w_down, i, final_norm_g, seq,
                            final_norm=(i == depth - 1))
    return xf.reshape(bsz, seq, d)
```

```python
import functools
import math

import jax
import jax.numpy as jnp
from jax import lax
from jax.experimental import pallas as pl
from jax.experimental.pallas import tpu as pltpu

F32 = jnp.float32
BF16 = jnp.bfloat16
U32 = jnp.uint32
I32 = jnp.int32

NORM_EPS = 1e-6
ROPE_THETA = 500000.0
ATTN_CHUNK = 64
DA_HEADS = 8
DA_HEAD_DIM = 128
ROPE_DIM = 32
ML_HEADS = 8
ML_QK_DIM = 128
ML_V_DIM = 256
ML_CONV = 4
MOE_GROUPS = 4
MOE_PER_GROUP = 8
MOE_EXPERTS = 32
LANES = 128
NEG_BIG = -1e30
LOG2_E = math.log2(math.e)

VMEM_LIMIT = 56 * 1024 * 1024


def _cparams(sem):
    return pltpu.CompilerParams(dimension_semantics=sem, vmem_limit_bytes=VMEM_LIMIT)


def _split_hi_lo(a):
    hi = a.astype(BF16)
    lo = (a - hi.astype(F32)).astype(BF16)
    return hi, lo


def _dot3(a, w, w_rows=False):
    ah, al = _split_hi_lo(a)
    wh, wl = _split_hi_lo(w)
    contract = (((1,), (1 if w_rows else 0,)), ((), ()))
    d = functools.partial(lax.dot_general, dimension_numbers=contract, preferred_element_type=F32)
    return d(ah, wh) + (d(ah, wl) + d(al, wh))


def _pack_bf16_pair(lo_f32, hi_f32):
    lo_bits = lax.bitcast_convert_type(lo_f32.astype(BF16).astype(F32), U32)
    hi_bits = lax.bitcast_convert_type(hi_f32.astype(BF16).astype(F32), U32)
    return hi_bits | (lo_bits >> 16)


def _unpack_bf16_pair(word):
    lo = lax.bitcast_convert_type(word << 16, F32)
    hi = lax.bitcast_convert_type(word & jnp.uint32(0xFFFF0000), F32)
    return lo, hi


ROW_TILE = 8


def _store_token_rows(ref, words):
    t = words.shape[0]
    for sub in range(ROW_TILE):
        ref[pl.ds(sub, t, stride=ROW_TILE), :] = words[:, sub * LANES:(sub + 1) * LANES]


def _load_token_rows(ref, sub, t):
    return ref[pl.ds(sub, t, stride=ROW_TILE), :]


def _rms_modulate(x, g, shift, scale):
    ms = jnp.mean(x * x, axis=-1, keepdims=True)
    y = x * lax.rsqrt(ms + NORM_EPS) * g
    return y * (1.0 + scale) + shift


def _adaln_kernel(c_ref, w_ref, b_ref, o_ref):
    c = c_ref[...]
    cond = c * jax.nn.sigmoid(c)
    acc = jnp.dot(cond.astype(BF16), w_ref[...].astype(BF16), preferred_element_type=F32)
    o_ref[...] = acc + b_ref[...]


def adaln(c, ada_w, ada_b, tn=1024):
    depth, d, n6 = ada_w.shape
    bsz = c.shape[0]
    rows = 8
    cp = jnp.zeros((rows, d), F32).at[:bsz].set(c)
    out = pl.pallas_call(
        _adaln_kernel,
        grid=(depth, n6 // tn),
        in_specs=[pl.BlockSpec((rows, d), lambda l, j: (0, 0)),
                  pl.BlockSpec((None, d, tn), lambda l, j: (l, 0, j)),
                  pl.BlockSpec((None, 1, tn), lambda l, j: (l, 0, j))],
        out_specs=pl.BlockSpec((None, rows, tn), lambda l, j: (l, 0, j)),
        out_shape=jax.ShapeDtypeStruct((depth, rows, n6), F32),
        compiler_params=_cparams(("arbitrary", "arbitrary")),
        name="adaln",
    )(cp, ada_w, ada_b.reshape(depth, 1, n6))
    return out[:, :bsz].reshape(depth, bsz, 6, d)


ROPE_ROWS = 64


def _norm_matmul_kernel(*refs, rope_tiles, q_tiles, has_extra, tn, nj, nsteps, w_rows):
    if has_extra:
        x_ref, g_ref, mod_ref, w_ref, c_ref, s_ref, we_ref, be_ref, o_ref, oe_ref, oet_ref, h_scr, acc_scr = refs
    else:
        x_ref, g_ref, mod_ref, w_ref, c_ref, s_ref, o_ref, h_scr, acc_scr = refs
    t = pl.program_id(0)
    j = t % nj

    @pl.when((j == 0) & (t < nsteps))
    def _():
        h = _rms_modulate(x_ref[...], g_ref[...], mod_ref[0:1, :], mod_ref[1:2, :])
        h_scr[...] = h.astype(BF16)
        if has_extra:
            extra = _dot3(h, we_ref[...], w_rows) + be_ref[...]
            oe_ref[...] = extra
            oet_ref[...] = extra.T

    @pl.when(t == 0)
    def _():
        acc_scr[...] = jnp.zeros_like(acc_scr)

    contract = (((1,), (1 if w_rows else 0,)), ((), ()))
    acc = lax.dot_general(h_scr[...], w_ref[...], contract, preferred_element_type=F32)
    tm = acc_scr.shape[0]
    if rope_tiles:
        jp = (t + nj - 1) % nj
        rope_on = jp < rope_tiles
        scale = jnp.where(jp < q_tiles, DA_HEAD_DIM ** -0.5 * LOG2_E, 1.0).astype(F32)
    for r in range(tm // ROPE_ROWS):
        rows = slice(r * ROPE_ROWS, (r + 1) * ROPE_ROWS)
        if rope_tiles:
            cs = jnp.where(rope_on, c_ref[rows, :] * scale, 1.0)
            sn = jnp.where(rope_on, s_ref[rows, :] * scale, 0.0)
        for m in range(tn // LANES):
            cols = slice(m * LANES, (m + 1) * LANES)
            prev = acc_scr[rows, cols]
            if rope_tiles:
                prev = prev * cs + pltpu.roll(prev, LANES // 2, 1) * sn
            o_ref[rows, cols] = prev.astype(o_ref.dtype)
    acc_scr[...] = acc


def norm_matmul(x, g, mod, w, seq, rope=None, extra=None, w_rows=False, tm=1024, tn=512):
    n, k = x.shape
    m = w.shape[0 if w_rows else 1]
    tiles_per_seq = seq // tm
    nj = m // tn
    nsteps = (n // tm) * nj
    has_extra = extra is not None
    cur = lambda t: jnp.minimum(t, nsteps - 1)
    prv = lambda t: jnp.maximum(t - 1, 0)
    if rope is None:
        dummy = jnp.zeros((8, LANES), F32)
        tabs = (dummy, dummy)
        tab_spec = pl.BlockSpec((8, LANES), lambda t: (0, 0))
        rope_tiles = q_tiles = 0
    else:
        tabs = rope[:2]
        tab_spec = pl.BlockSpec((tm, LANES), lambda t: (prv(t) // nj, 0))
        rope_tiles, q_tiles = rope[2] // tn, rope[3] // tn
    in_specs = [pl.BlockSpec((tm, k), lambda t: (cur(t) // nj, 0)),
                pl.BlockSpec((1, k), lambda t: (0, 0)),
                pl.BlockSpec((None, 6, k), lambda t: (cur(t) // nj // tiles_per_seq, 0, 0)),
                (pl.BlockSpec((tn, k), lambda t: (cur(t) % nj, 0)) if w_rows
                 else pl.BlockSpec((k, tn), lambda t: (0, cur(t) % nj))),
                tab_spec, tab_spec]
    args = [x, g.reshape(1, k), mod, w, *tabs]
    out_specs = pl.BlockSpec((tm, tn), lambda t: (prv(t) // nj, prv(t) % nj))
    out_shape = jax.ShapeDtypeStruct((n, m), BF16)
    if has_extra:
        we, be = extra
        in_specs += [pl.BlockSpec((LANES, k) if w_rows else (k, LANES), lambda t: (0, 0)),
                     pl.BlockSpec((1, LANES), lambda t: (0, 0))]
        args += [we, be]
        out_specs = [out_specs, pl.BlockSpec((tm, LANES), lambda t: (cur(t) // nj, 0)),
                     pl.BlockSpec((LANES, tm), lambda t: (0, cur(t) // nj))]
        out_shape = [out_shape, jax.ShapeDtypeStruct((n, LANES), F32), jax.ShapeDtypeStruct((LANES, n), F32)]
    return pl.pallas_call(
        functools.partial(_norm_matmul_kernel, rope_tiles=rope_tiles, q_tiles=q_tiles,
                          has_extra=has_extra, tn=tn, nj=nj, nsteps=nsteps, w_rows=w_rows),
        grid=(nsteps + 1,),
        in_specs=in_specs,
        out_specs=out_specs,
        out_shape=out_shape,
        scratch_shapes=[pltpu.VMEM((tm, k), BF16), pltpu.VMEM((tm, tn), F32)],
        compiler_params=_cparams(("arbitrary",)),
        name="norm_matmul",
    )(*args)


def _router_kernel(x_ref, g_ref, mod_ref, w_ref, b_ref, hp_ref, rt_ref, cnt_ref, run_scr):
    @pl.when(pl.program_id(0) == 0)
    def _():
        run_scr[...] = jnp.zeros_like(run_scr)

    h = _rms_modulate(x_ref[...], g_ref[...], mod_ref[3:4, :], mod_ref[4:5, :])
    half = h.shape[1] // 2
    _store_token_rows(hp_ref, _pack_bf16_pair(h[:, :half], h[:, half:]))
    logits = _dot3(h, w_ref[...]) + b_ref[...]
    tm = logits.shape[0]
    lane = lax.broadcasted_iota(I32, logits.shape, 1).astype(F32)
    ng = float(MOE_GROUPS)
    is_g = lane < ng
    gl = jnp.where(is_g, logits, NEG_BIG)
    gmax = jnp.max(gl, axis=-1, keepdims=True)
    grp = jnp.min(jnp.where(gl == gmax, lane, float(LANES)), axis=-1, keepdims=True)
    p_group = 1.0 / jnp.sum(jnp.where(is_g, jnp.exp(gl - gmax), 0.0), axis=-1, keepdims=True)
    lo = ng + float(MOE_PER_GROUP) * grp
    el = jnp.where((lane >= lo) & (lane < lo + float(MOE_PER_GROUP)), logits, NEG_BIG)
    v1 = jnp.max(el, axis=-1, keepdims=True)
    i1 = jnp.min(jnp.where(el == v1, lane, float(LANES)), axis=-1, keepdims=True)
    el2 = jnp.where(lane == i1, NEG_BIG, el)
    v2 = jnp.max(el2, axis=-1, keepdims=True)
    i2 = jnp.min(jnp.where(el2 == v2, lane, float(LANES)), axis=-1, keepdims=True)
    ex = jnp.exp(v2 - v1)
    w1 = p_group / (1.0 + ex)
    w2 = p_group * (ex / (1.0 + ex))
    oh1 = (lane == i1).astype(F32)
    oh2 = (lane == i2).astype(F32)
    oh = oh1 + oh2
    r_i = lax.broadcasted_iota(I32, (tm, tm), 0)
    c_i = lax.broadcasted_iota(I32, (tm, tm), 1)
    before = (c_i < r_i).astype(BF16)
    prior = jnp.dot(before, oh.astype(BF16), preferred_element_type=F32) + run_scr[...]
    rank1 = jnp.sum(oh1 * prior, axis=-1, keepdims=True)
    rank2 = jnp.sum(oh2 * prior, axis=-1, keepdims=True)
    run_scr[...] += jnp.sum(oh, axis=0, keepdims=True)
    cnt_ref[...] = run_scr[...]
    rt_ref[...] = jnp.where(lane == 0.0, i1 - ng,
                  jnp.where(lane == 1.0, i2 - ng,
                  jnp.where(lane == 2.0, w1,
                  jnp.where(lane == 3.0, w2,
                  jnp.where(lane == 4.0, rank1,
                  jnp.where(lane == 5.0, rank2, 0.0))))))


def moe_router(x, g, mod, w_r, b_r, seq, tm=512):
    n, d = x.shape
    tiles_per_seq = seq // tm
    return pl.pallas_call(
        _router_kernel,
        grid=(n // tm,),
        in_specs=[pl.BlockSpec((tm, d), lambda i: (i, 0)),
                  pl.BlockSpec((1, d), lambda i: (0, 0)),
                  pl.BlockSpec((None, 6, d), lambda i: (i // tiles_per_seq, 0, 0)),
                  pl.BlockSpec((d, LANES), lambda i: (0, 0)),
                  pl.BlockSpec((1, LANES), lambda i: (0, 0))],
        out_specs=[pl.BlockSpec((tm * ROW_TILE, LANES), lambda i: (i, 0)),
                   pl.BlockSpec((tm, LANES), lambda i: (i, 0)),
                   pl.BlockSpec((1, LANES), lambda i: (0, 0))],
        out_shape=[jax.ShapeDtypeStruct((n * ROW_TILE, LANES), U32),
                   jax.ShapeDtypeStruct((n, LANES), F32),
                   jax.ShapeDtypeStruct((1, LANES), F32)],
        scratch_shapes=[pltpu.VMEM((1, LANES), F32)],
        compiler_params=_cparams(("arbitrary",)),
        name="moe_router",
    )(x, g.reshape(1, d), mod, w_r, b_r)


def _matmul_res_kernel(a_ref, w_ref, res_ref, mod_ref, o_ref, *, gate_row):
    acc = jnp.dot(a_ref[...], w_ref[...], preferred_element_type=F32)
    o_ref[...] = res_ref[...] + mod_ref[gate_row:gate_row + 1, :] * acc


def matmul_res(a, w, res, mod, seq, gate_row, tm=512, tn=2048):
    n, k = a.shape
    m = w.shape[1]
    tiles_per_seq = seq // tm
    return pl.pallas_call(
        functools.partial(_matmul_res_kernel, gate_row=gate_row),
        grid=(n // tm, m // tn),
        in_specs=[pl.BlockSpec((tm, k), lambda i, j: (i, 0)),
                  pl.BlockSpec((k, tn), lambda i, j: (0, j)),
                  pl.BlockSpec((tm, tn), lambda i, j: (i, j)),
                  pl.BlockSpec((None, 6, tn), lambda i, j: (i // tiles_per_seq, 0, j))],
        out_specs=pl.BlockSpec((tm, tn), lambda i, j: (i, j)),
        out_shape=jax.ShapeDtypeStruct((n, m), F32),
        compiler_params=_cparams(("arbitrary", "arbitrary")),
        name="matmul_res",
    )(a, w, res, mod)


ATTN_GROUPS = (8, 4, 2)


def _grouped_loop(n, body, groups):
    done = 0
    for group in groups:
        trips = (n - done) // group

        def grouped(gi, c, group=group, done=done):
            for t in range(group):
                c = body(done + gi * group + t, c)
            return c

        lax.fori_loop(0, trips, grouped, 0)
        done = done + trips * group
    lax.fori_loop(done, n, body, 0)


def _attn_kernel(q_ref, k_ref, v_ref, lq1_ref, lk1_ref, lq2_ref, lk2_ref, g_ref, o_ref,
                 s_scr, m_scr, l_scr, acc_scr, bias_scr, *, lam_init, tq, tk, seq):
    d = DA_HEAD_DIM
    nlane = tk // LANES
    ndiag = tq // tk
    lam = (jnp.exp(jnp.sum(lq1_ref[...] * lk1_ref[...], axis=-1, keepdims=True))
           - jnp.exp(jnp.sum(lq2_ref[...] * lk2_ref[...], axis=-1, keepdims=True)) + lam_init)
    shift = ATTN_CHUNK.bit_length() - 1
    row_chunk = jnp.right_shift(lax.broadcasted_iota(I32, (tq, tk), 0), shift)
    col_chunk = jnp.right_shift(lax.broadcasted_iota(I32, (tq, tk), 1), shift)
    bias_scr[0] = jnp.zeros((tq, tk), F32)
    for t in range(ndiag):
        bias_scr[t + 1] = jnp.where(col_chunk + t * (tk // ATTN_CHUNK) <= row_chunk, 0.0, NEG_BIG)
    nt = (((1,), (1,)), ((), ()))

    def lane_fold(a, op):
        part = a[:, 0:LANES]
        for cb in range(1, nlane):
            part = op(part, a[:, cb * LANES:(cb + 1) * LANES])
        return part

    def q_body(qi, _):
        qs = pl.multiple_of(qi * tq, tq)
        qm = (q_ref[pl.ds(qs, tq), 0:d], q_ref[pl.ds(qs, tq), d:2 * d])
        m_scr[...] = jnp.full_like(m_scr, NEG_BIG)

        nfull = qi * ndiag

        def score_body(j, c):
            ks = pl.multiple_of(j * tk, tk)
            bias = bias_scr[jnp.maximum(j - nfull + 1, 0)]
            for mp in range(2):
                k = k_ref[pl.ds(ks, tk), mp * d:(mp + 1) * d]
                s = lax.dot_general(qm[mp], k, nt, preferred_element_type=F32) + bias
                s_scr[mp, j] = s
                m_scr[mp] = jnp.maximum(m_scr[mp], lane_fold(s, jnp.maximum))
            return c

        _grouped_loop(nfull + ndiag, score_body, ATTN_GROUPS)
        m = [jnp.max(m_scr[mp], axis=-1, keepdims=True) for mp in range(2)]
        l_scr[...] = jnp.zeros_like(l_scr)
        acc_scr[...] = jnp.zeros_like(acc_scr)

        def pv_body(j, c):
            ks = pl.multiple_of(j * tk, tk)
            v = v_ref[pl.ds(ks, tk), :]
            for mp in range(2):
                p = jnp.exp2(s_scr[mp, j] - m[mp])
                l_scr[mp] += lane_fold(p, jnp.add)
                acc_scr[mp] += jnp.dot(p.astype(BF16), v, preferred_element_type=F32)
            return c

        _grouped_loop(nfull + ndiag, pv_body, ATTN_GROUPS)
        l = [jnp.sum(l_scr[mp], axis=-1, keepdims=True) for mp in range(2)]
        o = acc_scr[0] / l[0] - lam * (acc_scr[1] / l[1])
        ms = jnp.mean(o * o, axis=-1, keepdims=True)
        o = o * lax.rsqrt(ms + NORM_EPS) * g_ref[...] * (1.0 - lam_init)
        o_ref[pl.ds(qs, tq), :] = o.astype(o_ref.dtype)
        return 0

    lax.fori_loop(0, seq // tq, q_body, 0)


def diff_attention(qkv, lq1, lk1, lq2, lk2, head_g, lam_init, tq=512, tk=256):
    bsz, seq, _ = qkv.shape
    h, dv = DA_HEADS, 2 * DA_HEAD_DIM
    vec = lambda a: a.reshape(1, -1).astype(F32)
    small = lambda n: pl.BlockSpec((1, n), lambda b, hh: (0, 0))
    return pl.pallas_call(
        functools.partial(_attn_kernel, lam_init=lam_init, tq=tq, tk=tk, seq=seq),
        grid=(bsz, h),
        in_specs=[pl.BlockSpec((None, seq, dv), lambda b, hh: (b, 0, hh)),
                  pl.BlockSpec((None, seq, dv), lambda b, hh: (b, 0, h + hh)),
                  pl.BlockSpec((None, seq, dv), lambda b, hh: (b, 0, 2 * h + hh)),
                  small(DA_HEAD_DIM), small(DA_HEAD_DIM), small(DA_HEAD_DIM), small(DA_HEAD_DIM),
                  small(dv)],
        out_specs=pl.BlockSpec((None, seq, dv), lambda b, hh: (b, 0, hh)),
        out_shape=jax.ShapeDtypeStruct((bsz, seq, h * dv), BF16),
        scratch_shapes=[pltpu.VMEM((2, seq // tk, tq, tk), F32), pltpu.VMEM((2, tq, LANES), F32),
                        pltpu.VMEM((2, tq, LANES), F32), pltpu.VMEM((2, tq, dv), F32),
                        pltpu.VMEM((tq // tk + 1, tq, tk), F32)],
        compiler_params=_cparams(("arbitrary", "arbitrary")),
        name="diff_attention",
    )(qkv, qkv, qkv, vec(lq1), vec(lk1), vec(lq2), vec(lk2), vec(head_g))


def _conv_silu_kernel(x_ref, w_ref, b_ref, o_ref, *, k_tile0):
    j = pl.program_id(1)
    scale = jnp.where(j >= k_tile0, ML_QK_DIM ** -0.5, 1.0).astype(F32)
    taps = [w_ref[ML_CONV - 1 - s:ML_CONV - s, :] for s in range(ML_CONV)]

    def conv(x, shifted):
        y = x * taps[0] + b_ref[...]
        for s in range(1, ML_CONV):
            y = y + shifted(x, s) * taps[s]
        return (y * jax.nn.sigmoid(y) * scale).astype(o_ref.dtype)

    o_ref[...] = conv(x_ref[...].astype(F32), lambda x, s: pltpu.roll(x, s, 0))
    head = ROW_TILE
    row = lax.broadcasted_iota(I32, (head, x_ref.shape[1]), 0)
    o_ref[0:head, :] = conv(x_ref[0:head, :].astype(F32),
                            lambda x, s: jnp.where(row >= s, pltpu.roll(x, s, 0), 0.0))


def conv_silu(proj, conv_w, conv_b, tc=128):
    bsz, seq, _ = proj.shape
    cols = conv_w.shape[1]
    return pl.pallas_call(
        functools.partial(_conv_silu_kernel, k_tile0=(cols // 2) // tc),
        grid=(bsz, cols // tc),
        in_specs=[pl.BlockSpec((None, seq, tc), lambda b, j: (b, 0, j)),
                  pl.BlockSpec((ML_CONV, tc), lambda b, j: (0, j)),
                  pl.BlockSpec((1, tc), lambda b, j: (0, j))],
        out_specs=pl.BlockSpec((None, seq, tc), lambda b, j: (b, 0, j)),
        out_shape=jax.ShapeDtypeStruct((bsz, seq, cols), BF16),
        compiler_params=_cparams(("arbitrary", "arbitrary")),
        name="conv_silu",
    )(proj, conv_w, conv_b.reshape(1, cols))


def _mlstm_kernel(q_ref, k_ref, v_ref, op_ref, gc_ref, gr_ref, hg_ref, o_ref,
                  cx_scr, m_scr, *, chunk):
    c = pl.program_id(1)
    nh, dqk, dv = ML_HEADS, ML_QK_DIM, ML_V_DIM

    @pl.when(c == 0)
    def _():
        cx_scr[...] = jnp.zeros_like(cx_scr)
        m_scr[...] = jnp.zeros_like(m_scr)

    gc = gc_ref[...]
    gr = gr_ref[...]
    lf_c = jax.nn.log_sigmoid(gc)
    lf_r = jax.nn.log_sigmoid(gr)
    r_i = lax.broadcasted_iota(I32, (chunk, chunk), 0)
    c_i = lax.broadcasted_iota(I32, (chunk, chunk), 1)
    causal = c_i <= r_i
    tril = causal.astype(F32)
    triu = (r_i <= c_i).astype(F32)
    b_c = _dot3(tril, lf_c)
    b_r = _dot3(lf_r, triu)
    nt = (((1,), (1,)), ((), ()))
    tn_ = (((0,), (0,)), ((), ()))
    ones_l = jnp.ones((chunk, LANES), BF16)
    ones_v = jnp.ones((dv, LANES), BF16)
    wide = lambda a: jnp.concatenate([a] * (dv // LANES), axis=1)

    for h in range(nh):
        q = q_ref[:, h * dqk:(h + 1) * dqk]
        k = k_ref[:, h * dqk:(h + 1) * dqk]
        v_ext = jnp.concatenate([v_ref[:, h * dv:(h + 1) * dv], ones_l], axis=1)
        bc = b_c[:, nh + h:nh + h + 1]
        br = b_r[nh + h:nh + h + 1, :]
        ig_c = gc[:, h:h + 1]
        ig_r = gr[h:h + 1, :]
        m_prev = m_scr[h:h + 1, :]
        dmat = jnp.where(causal, bc - br + ig_r, NEG_BIG)
        inter = bc + m_prev
        m_t = jnp.maximum(inter, jnp.max(dmat, axis=-1, keepdims=True))
        w = jnp.exp(dmat - m_t)
        s = lax.dot_general(q, k, nt, preferred_element_type=F32) * w
        decay = jnp.exp(inter - m_t)
        cx = cx_scr[h]
        tot = (jnp.dot(s.astype(BF16), v_ext, preferred_element_type=F32)
               + decay * jnp.dot(q, cx.astype(BF16), preferred_element_type=F32))
        den = jnp.maximum(jnp.abs(tot[:, dv:]), jnp.exp(-m_t))
        hh = tot[:, :dv] / wide(den)
        b_last = bc[chunk - 1:chunk, :]
        g = b_last - bc + ig_c
        m_new = jnp.maximum(b_last + m_prev, jnp.max(g, axis=0, keepdims=True))
        carry_decay = jnp.exp(b_last + m_prev - m_new)
        wg = jnp.exp(g - m_new)
        wv = (wg * v_ext.astype(F32)).astype(BF16)
        cx_scr[h] = carry_decay * cx + lax.dot_general(k, wv, tn_, preferred_element_type=F32)
        m_scr[h:h + 1, :] = m_new
        ms = jnp.dot((hh * hh).astype(BF16), ones_v, preferred_element_type=F32) * (1.0 / dv)
        hn = hh * wide(lax.rsqrt(ms + NORM_EPS)) * hg_ref[:, h * dv:(h + 1) * dv]
        og = jax.nn.sigmoid(op_ref[:, h * dv:(h + 1) * dv].astype(F32))
        o_ref[:, h * dv:(h + 1) * dv] = (og * hn).astype(o_ref.dtype)


def mlstm(qk, proj, gates_c, gates_r, head_g, chunk=256):
    bsz, seq, _ = qk.shape
    nh = ML_HEADS
    qw, vw = nh * ML_QK_DIM, nh * ML_V_DIM
    v_blk = (2 * qw) // vw
    return pl.pallas_call(
        functools.partial(_mlstm_kernel, chunk=chunk),
        grid=(bsz, seq // chunk),
        in_specs=[pl.BlockSpec((None, chunk, qw), lambda b, c: (b, c, 0)),
                  pl.BlockSpec((None, chunk, qw), lambda b, c: (b, c, 1)),
                  pl.BlockSpec((None, chunk, vw), lambda b, c: (b, c, v_blk)),
                  pl.BlockSpec((None, chunk, vw), lambda b, c: (b, c, v_blk + 1)),
                  pl.BlockSpec((None, chunk, LANES), lambda b, c: (b, c, 0)),
                  pl.BlockSpec((2 * nh, chunk), lambda b, c: (0, b * (seq // chunk) + c)),
                  pl.BlockSpec((1, vw), lambda b, c: (0, 0))],
        out_specs=pl.BlockSpec((None, chunk, vw), lambda b, c: (b, c, 0)),
        out_shape=jax.ShapeDtypeStruct((bsz, seq, vw), BF16),
        scratch_shapes=[pltpu.VMEM((nh, ML_QK_DIM, ML_V_DIM + LANES), F32),
                        pltpu.VMEM((nh, 1), F32)],
        compiler_params=_cparams(("arbitrary", "arbitrary")),
        name="mlstm",
    )(qk, qk, proj, proj, gates_c, gates_r, head_g.reshape(1, vw))


def _dest_kernel(rt_ref, ps_ref, o_ref):
    rt = rt_ref[...]
    lane = lax.broadcasted_iota(I32, rt.shape, 1).astype(F32)
    ng = float(MOE_GROUPS)
    ps = ps_ref[...]
    d1 = jnp.sum(jnp.where(lane == rt[:, 0:1] + ng, ps, 0.0), axis=-1, keepdims=True) + rt[:, 4:5]
    d2 = jnp.sum(jnp.where(lane == rt[:, 1:2] + ng, ps, 0.0), axis=-1, keepdims=True) + rt[:, 5:6]
    o_ref[...] = jnp.where(lane == 0.0, d1, jnp.where(lane == 1.0, d2, 0.0)).astype(I32)


def moe_dest(route, pad_start_lanes, tm=2048):
    n = route.shape[0]
    return pl.pallas_call(
        _dest_kernel,
        grid=(n // tm,),
        in_specs=[pl.BlockSpec((tm, LANES), lambda i: (i, 0)),
                  pl.BlockSpec((1, LANES), lambda i: (0, 0))],
        out_specs=pl.BlockSpec((tm, LANES), lambda i: (i, 0)),
        out_shape=jax.ShapeDtypeStruct((n, LANES), I32),
        compiler_params=_cparams(("arbitrary",)),
        name="moe_dest",
    )(route, pad_start_lanes)


DMA_UNROLL = 8


def _wait_rows(src_rows, dst_rows, sem, copies):
    for _ in range(copies):
        pltpu.make_async_copy(src_rows, dst_rows, sem).wait()


def _dispatch_kernel(zb_ref, dest_ref, hp_ref, xin_ref, zero_scr, ring, sems, *, tt, tm, nb, ntiles):
    i = pl.program_id(0)
    slot = i % 2
    blk_rows = tm * ROW_TILE
    tile_rows = tt * ROW_TILE

    @pl.when(i == 0)
    def _():
        zero_scr[...] = jnp.zeros_like(zero_scr)

        def zero_copy(blk):
            return pltpu.make_async_copy(zero_scr, xin_ref.at[pl.ds(pl.multiple_of(blk * blk_rows, blk_rows), blk_rows)],
                                         sems.at[2])

        def start(blk, c):
            @pl.when(zb_ref[blk] == 1)
            def _():
                zero_copy(blk).start()
            return c

        def wait(blk, c):
            @pl.when(zb_ref[blk] == 1)
            def _():
                zero_copy(blk).wait()
            return c

        lax.fori_loop(0, nb, start, 0)
        lax.fori_loop(0, nb, wait, 0)

    ring[slot] = hp_ref[...]

    def start_rows(r, c):
        src = ring.at[slot, pl.ds(pl.multiple_of(r * ROW_TILE, ROW_TILE), ROW_TILE)]
        for kk in range(2):
            row = pl.multiple_of(dest_ref[0, 0, 2 * r + kk] * ROW_TILE, ROW_TILE)
            pltpu.make_async_copy(src, xin_ref.at[pl.ds(row, ROW_TILE)], sems.at[slot]).start(priority=kk)
        return c

    lax.fori_loop(0, tt, start_rows, 0, unroll=DMA_UNROLL)
    whole = xin_ref.at[pl.ds(0, tile_rows)]

    @pl.when(i > 0)
    def _():
        _wait_rows(ring.at[1 - slot], whole, sems.at[1 - slot], 2)

    @pl.when(i == ntiles - 1)
    def _():
        _wait_rows(ring.at[slot], whole, sems.at[slot], 2)


def moe_dispatch(hpack, dest, zero_blocks, rows, tm, tt=256):
    n = hpack.shape[0] // ROW_TILE
    grid_spec = pltpu.PrefetchScalarGridSpec(
        num_scalar_prefetch=1,
        grid=(n // tt,),
        in_specs=[pl.BlockSpec((1, 1, 2 * tt), lambda i, zb: (i, 0, 0), memory_space=pltpu.SMEM),
                  pl.BlockSpec((tt * ROW_TILE, LANES), lambda i, zb: (i, 0))],
        out_specs=pl.BlockSpec(memory_space=pl.ANY),
        scratch_shapes=[pltpu.VMEM((tm * ROW_TILE, LANES), U32), pltpu.VMEM((2, tt * ROW_TILE, LANES), U32),
                        pltpu.SemaphoreType.DMA((3,))],
    )
    return pl.pallas_call(
        functools.partial(_dispatch_kernel, tt=tt, tm=tm, nb=rows // tm, ntiles=n // tt),
        grid_spec=grid_spec,
        out_shape=jax.ShapeDtypeStruct((rows * ROW_TILE, LANES), U32),
        compiler_params=_cparams(("arbitrary",)),
        name="moe_dispatch",
    )(zero_blocks, dest.reshape(n // tt, 1, 2 * tt), hpack)


def _expert_kernel(be_ref, first_ref, nxt_ref, nv_ref, nu_ref, x_ref, wgu_hbm, wd_hbm, y_ref,
                   wgu_bf, wd_bf, stg_gu, stg_d, xs_scr, acc_scr, sems, *, layer, th, cr):
    i = pl.program_id(0)

    def fetch(e):
        return (pltpu.make_async_copy(wgu_hbm.at[layer, e], stg_gu, sems.at[0]),
                pltpu.make_async_copy(wd_hbm.at[layer, e], stg_d, sems.at[1]))

    @pl.when(i < nu_ref[0])
    def _():
        @pl.when(first_ref[i] == 1)
        def _():
            @pl.when(i == 0)
            def _():
                for cp in fetch(be_ref[i]):
                    cp.start()

            for cp in fetch(be_ref[i]):
                cp.wait()

            def cast_rows(src, dst):
                def body(r, c):
                    rs = pl.multiple_of(r * cr, cr)
                    dst[pl.ds(rs, cr), :] = src[pl.ds(rs, cr), :].astype(BF16)
                    return c
                lax.fori_loop(0, src.shape[0] // cr, body, 0)

            cast_rows(stg_gu, wgu_bf)
            cast_rows(stg_d, wd_bf)

            @pl.when(nxt_ref[i] >= 0)
            def _():
                for cp in fetch(nxt_ref[i]):
                    cp.start()

        tm = xs_scr.shape[0]
        half = xs_scr.shape[1] // 2
        hid = wd_bf.shape[0]

        def mlp(rows):
            for sub in range(ROW_TILE):
                lo, hi = _unpack_bf16_pair(_load_token_rows(x_ref, sub, rows))
                xs_scr[0:rows, sub * LANES:(sub + 1) * LANES] = lo.astype(BF16)
                xs_scr[0:rows, half + sub * LANES:half + (sub + 1) * LANES] = hi.astype(BF16)
            xs = xs_scr[0:rows, :]
            for c in range(hid // th):
                gt = jnp.dot(xs, wgu_bf[:, c * th:(c + 1) * th], preferred_element_type=F32)
                up = jnp.dot(xs, wgu_bf[:, hid + c * th:hid + (c + 1) * th], preferred_element_type=F32)
                act = (gt * jax.nn.sigmoid(gt) * up).astype(BF16)
                part = jnp.dot(act, wd_bf[c * th:(c + 1) * th, :], preferred_element_type=F32)
                if c == 0:
                    acc_scr[0:rows, :] = part
                else:
                    acc_scr[0:rows, :] += part
            y = acc_scr[0:rows, :]
            _store_token_rows(y_ref, _pack_bf16_pair(y[:, :half], y[:, half:]))
            if rows < tm:
                y_ref[rows * ROW_TILE:, :] = jnp.zeros(((tm - rows) * ROW_TILE, LANES), U32)

        @pl.when(nv_ref[i] > tm // 2)
        def _():
            mlp(tm)

        @pl.when(nv_ref[i] <= tm // 2)
        def _():
            mlp(tm // 2)


def moe_experts(xin, sched, wgu_all, wd_all, layer, tm, th=256, cr=256):
    _, _, d, hid2 = wgu_all.shape
    hid = hid2 // 2
    nb = xin.shape[0] // (tm * ROW_TILE)
    block_e, first, nxt, valid, n_used = sched

    def blk(i, be, fi, nx, nv, nu):
        return (jnp.minimum(i, nu[0] - 1), 0)

    grid_spec = pltpu.PrefetchScalarGridSpec(
        num_scalar_prefetch=5,
        grid=(nb,),
        in_specs=[pl.BlockSpec((tm * ROW_TILE, LANES), blk),
                  pl.BlockSpec(memory_space=pl.ANY),
                  pl.BlockSpec(memory_space=pl.ANY)],
        out_specs=pl.BlockSpec((tm * ROW_TILE, LANES), blk),
        scratch_shapes=[pltpu.VMEM((d, hid2), BF16), pltpu.VMEM((hid, d), BF16),
                        pltpu.VMEM((d, hid2), F32), pltpu.VMEM((hid, d), F32),
                        pltpu.VMEM((tm, d), BF16), pltpu.VMEM((tm, d), F32),
                        pltpu.SemaphoreType.DMA((2,))],
    )
    return pl.pallas_call(
        functools.partial(_expert_kernel, layer=layer, th=th, cr=cr),
        grid_spec=grid_spec,
        out_shape=jax.ShapeDtypeStruct(xin.shape, U32),
        input_output_aliases={5: 0},
        compiler_params=_cparams(("arbitrary",)),
        name="moe_experts",
    )(block_e, first, nxt, valid, n_used, xin, wgu_all, wd_all)


def _combine_kernel(dcur_ref, dnxt_ref, x_ref, rt_ref, mod_ref, fg_ref, y_ref, o_ref, ya, yb, sems,
                    *, tt, ntiles, final_norm):
    i = pl.program_id(0)
    slot = i % 2

    def issue(dref, sl):
        def body(r, c):
            dst = pl.ds(pl.multiple_of(r * ROW_TILE, ROW_TILE), ROW_TILE)
            for kk, buf in enumerate((ya, yb)):
                row = pl.multiple_of(dref[0, 0, 2 * r + kk] * ROW_TILE, ROW_TILE)
                pltpu.make_async_copy(y_ref.at[pl.ds(row, ROW_TILE)], buf.at[sl, dst], sems.at[sl]).start(priority=kk)
            return c
        lax.fori_loop(0, tt, body, 0, unroll=DMA_UNROLL)

    @pl.when(i == 0)
    def _():
        issue(dcur_ref, 0)

    @pl.when(i + 1 < ntiles)
    def _():
        issue(dnxt_ref, 1 - slot)

    _wait_rows(y_ref.at[pl.ds(0, tt * ROW_TILE)], ya.at[slot], sems.at[slot], 2)
    half = x_ref.shape[1] // 2
    w1 = rt_ref[:, 2:3]
    w2 = rt_ref[:, 3:4]
    ssq = jnp.zeros((tt, 1), F32)
    for sub in range(ROW_TILE):
        a_lo, a_hi = _unpack_bf16_pair(_load_token_rows(ya.at[slot], sub, tt))
        b_lo, b_hi = _unpack_bf16_pair(_load_token_rows(yb.at[slot], sub, tt))
        for base, a, b in ((sub * LANES, a_lo, b_lo), (half + sub * LANES, a_hi, b_hi)):
            cols = slice(base, base + LANES)
            out = x_ref[:, cols] + mod_ref[5:6, cols] * (a * w1 + b * w2)
            o_ref[:, cols] = out
            if final_norm:
                ssq = ssq + jnp.sum(out * out, axis=-1, keepdims=True)
    if final_norm:
        r = lax.rsqrt(ssq / (2 * half) + NORM_EPS)
        o_ref[...] = o_ref[...] * r * fg_ref[...]


def moe_combine(x, y, dest, route, mod, final_g, seq, final_norm, tt=256):
    n, d = x.shape
    tiles_per_seq = seq // tt
    ntiles = n // tt
    dest3 = dest.reshape(ntiles, 1, 2 * tt)
    return pl.pallas_call(
        functools.partial(_combine_kernel, tt=tt, ntiles=ntiles, final_norm=final_norm),
        grid=(ntiles,),
        in_specs=[pl.BlockSpec((1, 1, 2 * tt), lambda i: (i, 0, 0), memory_space=pltpu.SMEM),
                  pl.BlockSpec((1, 1, 2 * tt), lambda i: (jnp.minimum(i + 1, ntiles - 1), 0, 0),
                               memory_space=pltpu.SMEM),
                  pl.BlockSpec((tt, d), lambda i: (i, 0)),
                  pl.BlockSpec((tt, LANES), lambda i: (i, 0)),
                  pl.BlockSpec((None, 6, d), lambda i: (i // tiles_per_seq, 0, 0)),
                  pl.BlockSpec((1, d), lambda i: (0, 0)),
                  pl.BlockSpec(memory_space=pl.ANY)],
        out_specs=pl.BlockSpec((tt, d), lambda i: (i, 0)),
        out_shape=jax.ShapeDtypeStruct((n, d), F32),
        scratch_shapes=[pltpu.VMEM((2, tt * ROW_TILE, LANES), U32), pltpu.VMEM((2, tt * ROW_TILE, LANES), U32),
                        pltpu.SemaphoreType.DMA((2,))],
        compiler_params=_cparams(("arbitrary",)),
        name="moe_combine",
    )(dest3, dest3, x, route, mod, final_g.reshape(1, d), y)


def _expert_schedule(counts, tm, nb):
    ne = counts.shape[0]
    ids = jnp.arange(ne, dtype=I32)
    padded = ((counts + tm - 1) // tm) * tm
    pad_end = jnp.sum(jnp.where(ids[None, :] <= ids[:, None], padded[None, :], 0), axis=1)
    pad_start = pad_end - padded
    total = jnp.sum(padded)
    n_used = total // tm
    blk0 = jnp.arange(nb, dtype=I32) * tm
    block_e = jnp.minimum(jnp.sum((pad_end[None, :] <= blk0[:, None]).astype(I32), axis=1), ne - 1)
    onehot = block_e[:, None] == ids[None, :]
    pick = lambda v: jnp.sum(jnp.where(onehot, v[None, :], 0), axis=1)
    first = (blk0 == pick(pad_start)).astype(I32)
    later = jnp.min(jnp.where((ids[None, :] > ids[:, None]) & (counts[None, :] > 0), ids[None, :], ne), axis=1)
    nxt = pick(jnp.where(later < ne, later, -1))
    has_padding = (blk0 + tm == pick(pad_end)) & (pick(counts % tm) != 0)
    zero_blocks = ((blk0 >= total) | has_padding).astype(I32)
    valid = jnp.clip(pick(pad_start + counts) - blk0, 0, tm)
    return pad_start, zero_blocks, (block_e, first, nxt.astype(I32), valid.astype(I32),
                                    n_used.astype(I32).reshape(1))


def moe_router_weights(w_group, b_group, w_expert, b_expert):
    d = w_group.shape[0]
    ng, ne = w_group.shape[1], w_expert.shape[1]
    w_r = jnp.concatenate([w_group, w_expert, jnp.zeros((d, LANES - ng - ne), F32)], axis=1)
    b_r = jnp.concatenate([b_group, b_expert, jnp.zeros((LANES - ng - ne,), F32)]).reshape(1, LANES)
    return w_r, b_r


def hier_moe_layer(x, g, mod, w_r, b_r, wgu_all, wd_all, layer, final_g, seq, final_norm, tm=256):
    n, d = x.shape
    ng, ne = MOE_GROUPS, MOE_EXPERTS
    hpack, route, cnt = moe_router(x, g, mod, w_r, b_r, seq)
    counts = cnt[0, ng:ng + ne].astype(I32)
    rows = 2 * n + ne * tm
    pad_start, zero_blocks, sched = _expert_schedule(counts, tm, rows // tm)
    ps_lanes = jnp.concatenate([jnp.zeros((ng,), F32), pad_start.astype(F32),
                                jnp.zeros((LANES - ng - ne,), F32)]).reshape(1, LANES)
    dest = moe_dest(route, ps_lanes, tm=min(2048, n))[:, :2].reshape(-1)
    xin = moe_dispatch(hpack, dest, zero_blocks, rows, tm)
    y = moe_experts(xin, sched, wgu_all, wd_all, layer, tm)
    return moe_combine(x, y, dest, route, mod, final_g, seq, final_norm)


def _rope_tables(positions):
    half = ROPE_DIM // 2
    inv_freq = ROPE_THETA ** (-jnp.arange(half, dtype=F32) * 2.0 / ROPE_DIM)
    gap = jnp.zeros((LANES // 2 - half,), F32)
    freq = jnp.concatenate([-inv_freq, gap, inv_freq, gap])
    ang = positions.astype(F32).reshape(-1, 1) * freq[None, :]
    return jnp.cos(ang), jnp.sin(ang)


def _weight_prep_kernel(w_ref, o_ref, *, pair_tiles, tn):
    j = pl.program_id(0)
    half = ROPE_DIM // 2
    mid = LANES // 2

    @pl.when(j >= pair_tiles)
    def _():
        o_ref[...] = w_ref[...].astype(BF16)

    if pair_tiles:
        @pl.when(j < pair_tiles)
        def _():
            lane = lax.broadcasted_iota(I32, (w_ref.shape[0], LANES), 1)
            for m in range(tn // LANES):
                t = w_ref[:, m * LANES:(m + 1) * LANES]
                up = pltpu.roll(t, LANES - half, 1)
                down = pltpu.roll(t, mid - half, 1)
                new = jnp.where(lane < half, t, jnp.where(lane < mid, up, jnp.where(lane < mid + half, down, t)))
                o_ref[:, m * LANES:(m + 1) * LANES] = new.astype(BF16)


def weight_prep(w, cols, pair_cols=0, tn=512, w_rows=False):
    if w_rows:
        k = w.shape[1]
        return pl.pallas_call(
            functools.partial(_weight_prep_kernel, pair_tiles=0, tn=tn),
            grid=(cols // tn,),
            in_specs=[pl.BlockSpec((tn, k), lambda j: (j, 0))],
            out_specs=pl.BlockSpec((tn, k), lambda j: (j, 0)),
            out_shape=jax.ShapeDtypeStruct((cols, k), BF16),
            compiler_params=_cparams(("arbitrary",)),
            name="weight_prep",
        )(w)
    k = w.shape[0]
    return pl.pallas_call(
        functools.partial(_weight_prep_kernel, pair_tiles=pair_cols // tn, tn=tn),
        grid=(cols // tn,),
        in_specs=[pl.BlockSpec((k, tn), lambda j: (0, j))],
        out_specs=pl.BlockSpec((k, tn), lambda j: (0, j)),
        out_shape=jax.ShapeDtypeStruct((k, cols), BF16),
        compiler_params=_cparams(("arbitrary",)),
        name="weight_prep",
    )(w)


def kernel(x, c, positions, ada_w, ada_b, norm_mix_g, norm_ffn_g, final_norm_g, attn_w_in, attn_w_out, attn_lambda_q1, attn_lambda_k1, attn_lambda_q2, attn_lambda_k2, attn_head_norm_g, mlstm_w_in, mlstm_conv_w, mlstm_conv_b, mlstm_gate_b, mlstm_head_norm_g, mlstm_w_out, moe_w_group, moe_b_group, moe_w_expert, moe_b_expert, moe_w_gu, moe_w_down):
    bsz, seq, d = x.shape
    n = bsz * seq
    depth = ada_w.shape[0]
    mod = adaln(c, ada_w, ada_b)
    xf = x.reshape(n, d)
    for i in range(depth):
        jm = i // 2
        if i % 2 == 0:
            qk_cols = 2 * DA_HEADS * 2 * DA_HEAD_DIM
            tabs = _rope_tables(positions)
            w_in = weight_prep(attn_w_in[jm], attn_w_in.shape[2], pair_cols=qk_cols)
            qkv = norm_matmul(xf, norm_mix_g[i], mod[i], w_in, seq, rope=(*tabs, qk_cols, qk_cols // 2))
            lam_init = 0.8 - 0.6 * math.exp(-0.3 * i)
            mixed = diff_attention(qkv.reshape(bsz, seq, -1), attn_lambda_q1[jm], attn_lambda_k1[jm],
                                   attn_lambda_q2[jm], attn_lambda_k2[jm], attn_head_norm_g[jm], lam_init)
            w_out = attn_w_out[jm]
        else:
            qk_cols = 2 * ML_HEADS * ML_QK_DIM
            main_cols = qk_cols + 2 * ML_HEADS * ML_V_DIM
            w_in_t = jnp.swapaxes(mlstm_w_in[jm], 0, 1)
            ngate = 2 * ML_HEADS
            w_gate = jnp.concatenate([w_in_t[main_cols:], jnp.zeros((LANES - ngate, d), F32)], axis=0)
            b_gate = jnp.concatenate([mlstm_gate_b[jm], jnp.zeros((LANES - ngate,), F32)]).reshape(1, LANES)
            proj, gates, gates_t = norm_matmul(xf, norm_mix_g[i], mod[i],
                                               weight_prep(w_in_t, main_cols, w_rows=True), seq,
                                               extra=(w_gate, b_gate), w_rows=True)
            proj = proj.reshape(bsz, seq, main_cols)
            qk = conv_silu(proj, mlstm_conv_w[jm], mlstm_conv_b[jm])
            mixed = mlstm(qk, proj, gates.reshape(bsz, seq, LANES), gates_t, mlstm_head_norm_g[jm])
            w_out = mlstm_w_out[jm]
        w_r, b_r = moe_router_weights(moe_w_group[i], moe_b_group[i], moe_w_expert[i], moe_b_expert[i])
        xf = matmul_res(mixed.reshape(n, -1), w_out.astype(BF16), xf, mod[i], seq, gate_row=2)
        xf = hier_moe_layer(xf, norm_ffn_g[i], mod[i], w_r, b_r, moe_w_gu, moe_w_down, i, final_norm_g, seq,
                            final_norm=(i == depth - 1))
    return xf.reshape(bsz, seq, d)
```

```python
import functools
import math

import jax
import jax.numpy as jnp
from jax import lax
from jax.experimental import pallas as pl
from jax.experimental.pallas import tpu as pltpu

F32 = jnp.float32
BF16 = jnp.bfloat16
U32 = jnp.uint32
I32 = jnp.int32

NORM_EPS = 1e-6
ROPE_THETA = 500000.0
ATTN_CHUNK = 64
DA_HEADS = 8
DA_HEAD_DIM = 128
ROPE_DIM = 32
ML_HEADS = 8
ML_QK_DIM = 128
ML_V_DIM = 256
ML_CONV = 4
MOE_GROUPS = 4
MOE_PER_GROUP = 8
MOE_EXPERTS = 32
LANES = 128
NEG_BIG = -1e30
LOG2_E = math.log2(math.e)

VMEM_LIMIT = 56 * 1024 * 1024


def _cparams(sem):
    return pltpu.CompilerParams(dimension_semantics=sem, vmem_limit_bytes=VMEM_LIMIT)


def _split_hi_lo(a):
    hi = a.astype(BF16)
    lo = (a - hi.astype(F32)).astype(BF16)
    return hi, lo


def _dot3(a, w, w_rows=False):
    ah, al = _split_hi_lo(a)
    wh, wl = _split_hi_lo(w)
    contract = (((1,), (1 if w_rows else 0,)), ((), ()))
    d = functools.partial(lax.dot_general, dimension_numbers=contract, preferred_element_type=F32)
    return d(ah, wh) + (d(ah, wl) + d(al, wh))


def _pack_bf16_pair(lo_f32, hi_f32):
    lo_bits = lax.bitcast_convert_type(lo_f32.astype(BF16).astype(F32), U32)
    hi_bits = lax.bitcast_convert_type(hi_f32.astype(BF16).astype(F32), U32)
    return hi_bits | (lo_bits >> 16)


def _unpack_bf16_pair(word):
    lo = lax.bitcast_convert_type(word << 16, F32)
    hi = lax.bitcast_convert_type(word & jnp.uint32(0xFFFF0000), F32)
    return lo, hi


ROW_TILE = 8


def _store_token_rows(ref, words):
    t = words.shape[0]
    for sub in range(ROW_TILE):
        ref[pl.ds(sub, t, stride=ROW_TILE), :] = words[:, sub * LANES:(sub + 1) * LANES]


def _load_token_rows(ref, sub, t):
    return ref[pl.ds(sub, t, stride=ROW_TILE), :]


def _rms_modulate(x, g, shift, scale):
    ms = jnp.mean(x * x, axis=-1, keepdims=True)
    y = x * lax.rsqrt(ms + NORM_EPS) * g
    return y * (1.0 + scale) + shift


def _adaln_kernel(c_ref, w_ref, b_ref, o_ref):
    c = c_ref[...]
    cond = c * jax.nn.sigmoid(c)
    acc = jnp.dot(cond.astype(BF16), w_ref[...].astype(BF16), preferred_element_type=F32)
    o_ref[...] = acc + b_ref[...]


def adaln(c, ada_w, ada_b, tn=1024):
    depth, d, n6 = ada_w.shape
    bsz = c.shape[0]
    rows = 8
    cp = jnp.zeros((rows, d), F32).at[:bsz].set(c)
    out = pl.pallas_call(
        _adaln_kernel,
        grid=(depth, n6 // tn),
        in_specs=[pl.BlockSpec((rows, d), lambda l, j: (0, 0)),
                  pl.BlockSpec((None, d, tn), lambda l, j: (l, 0, j)),
                  pl.BlockSpec((None, 1, tn), lambda l, j: (l, 0, j))],
        out_specs=pl.BlockSpec((None, rows, tn), lambda l, j: (l, 0, j)),
        out_shape=jax.ShapeDtypeStruct((depth, rows, n6), F32),
        compiler_params=_cparams(("arbitrary", "arbitrary")),
        name="adaln",
    )(cp, ada_w, ada_b.reshape(depth, 1, n6))
    return out[:, :bsz].reshape(depth, bsz, 6, d)


ROPE_ROWS = 64


def _norm_matmul_kernel(*refs, rope_tiles, q_tiles, has_extra, tn, nj, nsteps, w_rows):
    if has_extra:
        x_ref, g_ref, mod_ref, w_ref, c_ref, s_ref, we_ref, be_ref, o_ref, oe_ref, oet_ref, h_scr, acc_scr = refs
    else:
        x_ref, g_ref, mod_ref, w_ref, c_ref, s_ref, o_ref, h_scr, acc_scr = refs
    t = pl.program_id(0)
    j = t % nj

    @pl.when((j == 0) & (t < nsteps))
    def _():
        h = _rms_modulate(x_ref[...], g_ref[...], mod_ref[0:1, :], mod_ref[1:2, :])
        h_scr[...] = h.astype(BF16)
        if has_extra:
            extra = _dot3(h, we_ref[...], w_rows) + be_ref[...]
            oe_ref[...] = extra
            oet_ref[...] = extra.T

    @pl.when(t == 0)
    def _():
        acc_scr[...] = jnp.zeros_like(acc_scr)

    contract = (((1,), (1 if w_rows else 0,)), ((), ()))
    acc = lax.dot_general(h_scr[...], w_ref[...], contract, preferred_element_type=F32)
    tm = acc_scr.shape[0]
    if rope_tiles:
        jp = (t + nj - 1) % nj
        rope_on = jp < rope_tiles
        scale = jnp.where(jp < q_tiles, DA_HEAD_DIM ** -0.5 * LOG2_E, 1.0).astype(F32)
    for r in range(tm // ROPE_ROWS):
        rows = slice(r * ROPE_ROWS, (r + 1) * ROPE_ROWS)
        if rope_tiles:
            cs = jnp.where(rope_on, c_ref[rows, :] * scale, 1.0)
            sn = jnp.where(rope_on, s_ref[rows, :] * scale, 0.0)
        for m in range(tn // LANES):
            cols = slice(m * LANES, (m + 1) * LANES)
            prev = acc_scr[rows, cols]
            if rope_tiles:
                prev = prev * cs + pltpu.roll(prev, LANES // 2, 1) * sn
            o_ref[rows, cols] = prev.astype(o_ref.dtype)
    acc_scr[...] = acc


def norm_matmul(x, g, mod, w, seq, rope=None, extra=None, w_rows=False, tm=1024, tn=1024):
    n, k = x.shape
    m = w.shape[0 if w_rows else 1]
    tiles_per_seq = seq // tm
    nj = m // tn
    nsteps = (n // tm) * nj
    has_extra = extra is not None
    cur = lambda t: jnp.minimum(t, nsteps - 1)
    prv = lambda t: jnp.maximum(t - 1, 0)
    if rope is None:
        dummy = jnp.zeros((8, LANES), F32)
        tabs = (dummy, dummy)
        tab_spec = pl.BlockSpec((8, LANES), lambda t: (0, 0))
        rope_tiles = q_tiles = 0
    else:
        tabs = rope[:2]
        tab_spec = pl.BlockSpec((tm, LANES), lambda t: (prv(t) // nj, 0))
        rope_tiles, q_tiles = rope[2] // tn, rope[3] // tn
    in_specs = [pl.BlockSpec((tm, k), lambda t: (cur(t) // nj, 0)),
                pl.BlockSpec((1, k), lambda t: (0, 0)),
                pl.BlockSpec((None, 6, k), lambda t: (cur(t) // nj // tiles_per_seq, 0, 0)),
                (pl.BlockSpec((tn, k), lambda t: (cur(t) % nj, 0)) if w_rows
                 else pl.BlockSpec((k, tn), lambda t: (0, cur(t) % nj))),
                tab_spec, tab_spec]
    args = [x, g.reshape(1, k), mod, w, *tabs]
    out_specs = pl.BlockSpec((tm, tn), lambda t: (prv(t) // nj, prv(t) % nj))
    out_shape = jax.ShapeDtypeStruct((n, m), BF16)
    if has_extra:
        we, be = extra
        in_specs += [pl.BlockSpec((LANES, k) if w_rows else (k, LANES), lambda t: (0, 0)),
                     pl.BlockSpec((1, LANES), lambda t: (0, 0))]
        args += [we, be]
        out_specs = [out_specs, pl.BlockSpec((tm, LANES), lambda t: (cur(t) // nj, 0)),
                     pl.BlockSpec((LANES, tm), lambda t: (0, cur(t) // nj))]
        out_shape = [out_shape, jax.ShapeDtypeStruct((n, LANES), F32), jax.ShapeDtypeStruct((LANES, n), F32)]
    return pl.pallas_call(
        functools.partial(_norm_matmul_kernel, rope_tiles=rope_tiles, q_tiles=q_tiles,
                          has_extra=has_extra, tn=tn, nj=nj, nsteps=nsteps, w_rows=w_rows),
        grid=(nsteps + 1,),
        in_specs=in_specs,
        out_specs=out_specs,
        out_shape=out_shape,
        scratch_shapes=[pltpu.VMEM((tm, k), BF16), pltpu.VMEM((tm, tn), F32)],
        compiler_params=_cparams(("arbitrary",)),
        name="norm_matmul",
    )(*args)


def _router_kernel(x_ref, g_ref, mod_ref, w_ref, b_ref, hp_ref, rt_ref, cnt_ref, run_scr):
    @pl.when(pl.program_id(0) == 0)
    def _():
        run_scr[...] = jnp.zeros_like(run_scr)

    h = _rms_modulate(x_ref[...], g_ref[...], mod_ref[3:4, :], mod_ref[4:5, :])
    half = h.shape[1] // 2
    _store_token_rows(hp_ref, _pack_bf16_pair(h[:, :half], h[:, half:]))
    logits = _dot3(h, w_ref[...]) + b_ref[...]
    tm = logits.shape[0]
    lane = lax.broadcasted_iota(I32, logits.shape, 1).astype(F32)
    ng = float(MOE_GROUPS)
    is_g = lane < ng
    gl = jnp.where(is_g, logits, NEG_BIG)
    gmax = jnp.max(gl, axis=-1, keepdims=True)
    grp = jnp.min(jnp.where(gl == gmax, lane, float(LANES)), axis=-1, keepdims=True)
    p_group = 1.0 / jnp.sum(jnp.where(is_g, jnp.exp(gl - gmax), 0.0), axis=-1, keepdims=True)
    lo = ng + float(MOE_PER_GROUP) * grp
    el = jnp.where((lane >= lo) & (lane < lo + float(MOE_PER_GROUP)), logits, NEG_BIG)
    v1 = jnp.max(el, axis=-1, keepdims=True)
    i1 = jnp.min(jnp.where(el == v1, lane, float(LANES)), axis=-1, keepdims=True)
    el2 = jnp.where(lane == i1, NEG_BIG, el)
    v2 = jnp.max(el2, axis=-1, keepdims=True)
    i2 = jnp.min(jnp.where(el2 == v2, lane, float(LANES)), axis=-1, keepdims=True)
    ex = jnp.exp(v2 - v1)
    w1 = p_group / (1.0 + ex)
    w2 = p_group * (ex / (1.0 + ex))
    oh1 = (lane == i1).astype(F32)
    oh2 = (lane == i2).astype(F32)
    oh = oh1 + oh2
    r_i = lax.broadcasted_iota(I32, (tm, tm), 0)
    c_i = lax.broadcasted_iota(I32, (tm, tm), 1)
    before = (c_i < r_i).astype(BF16)
    prior = jnp.dot(before, oh.astype(BF16), preferred_element_type=F32) + run_scr[...]
    rank1 = jnp.sum(oh1 * prior, axis=-1, keepdims=True)
    rank2 = jnp.sum(oh2 * prior, axis=-1, keepdims=True)
    run_scr[...] += jnp.sum(oh, axis=0, keepdims=True)
    cnt_ref[...] = run_scr[...]
    rt_ref[...] = jnp.where(lane == 0.0, i1 - ng,
                  jnp.where(lane == 1.0, i2 - ng,
                  jnp.where(lane == 2.0, w1,
                  jnp.where(lane == 3.0, w2,
                  jnp.where(lane == 4.0, rank1,
                  jnp.where(lane == 5.0, rank2, 0.0))))))


def moe_router(x, g, mod, w_r, b_r, seq, tm=512):
    n, d = x.shape
    tiles_per_seq = seq // tm
    return pl.pallas_call(
        _router_kernel,
        grid=(n // tm,),
        in_specs=[pl.BlockSpec((tm, d), lambda i: (i, 0)),
                  pl.BlockSpec((1, d), lambda i: (0, 0)),
                  pl.BlockSpec((None, 6, d), lambda i: (i // tiles_per_seq, 0, 0)),
                  pl.BlockSpec((d, LANES), lambda i: (0, 0)),
                  pl.BlockSpec((1, LANES), lambda i: (0, 0))],
        out_specs=[pl.BlockSpec((tm * ROW_TILE, LANES), lambda i: (i, 0)),
                   pl.BlockSpec((tm, LANES), lambda i: (i, 0)),
                   pl.BlockSpec((1, LANES), lambda i: (0, 0))],
        out_shape=[jax.ShapeDtypeStruct((n * ROW_TILE, LANES), U32),
                   jax.ShapeDtypeStruct((n, LANES), F32),
                   jax.ShapeDtypeStruct((1, LANES), F32)],
        scratch_shapes=[pltpu.VMEM((1, LANES), F32)],
        compiler_params=_cparams(("arbitrary",)),
        name="moe_router",
    )(x, g.reshape(1, d), mod, w_r, b_r)


def _matmul_res_kernel(a_ref, w_ref, res_ref, mod_ref, o_ref, *, gate_row):
    acc = jnp.dot(a_ref[...], w_ref[...], preferred_element_type=F32)
    o_ref[...] = res_ref[...] + mod_ref[gate_row:gate_row + 1, :] * acc


def matmul_res(a, w, res, mod, seq, gate_row, tm=512, tn=2048):
    n, k = a.shape
    m = w.shape[1]
    tiles_per_seq = seq // tm
    return pl.pallas_call(
        functools.partial(_matmul_res_kernel, gate_row=gate_row),
        grid=(n // tm, m // tn),
        in_specs=[pl.BlockSpec((tm, k), lambda i, j: (i, 0)),
                  pl.BlockSpec((k, tn), lambda i, j: (0, j)),
                  pl.BlockSpec((tm, tn), lambda i, j: (i, j)),
                  pl.BlockSpec((None, 6, tn), lambda i, j: (i // tiles_per_seq, 0, j))],
        out_specs=pl.BlockSpec((tm, tn), lambda i, j: (i, j)),
        out_shape=jax.ShapeDtypeStruct((n, m), F32),
        compiler_params=_cparams(("arbitrary", "arbitrary")),
        name="matmul_res",
    )(a, w, res, mod)


ATTN_GROUPS = (8, 4, 2)


def _grouped_loop(n, body, groups):
    done = 0
    for group in groups:
        trips = (n - done) // group

        def grouped(gi, c, group=group, done=done):
            for t in range(group):
                c = body(done + gi * group + t, c)
            return c

        lax.fori_loop(0, trips, grouped, 0)
        done = done + trips * group
    lax.fori_loop(done, n, body, 0)


def _attn_kernel(q_ref, k_ref, v_ref, lq1_ref, lk1_ref, lq2_ref, lk2_ref, g_ref, o_ref,
                 s_scr, m_scr, l_scr, acc_scr, bias_scr, *, lam_init, tq, tk, seq):
    d = DA_HEAD_DIM
    nlane = tk // LANES
    ndiag = tq // tk
    lam = (jnp.exp(jnp.sum(lq1_ref[...] * lk1_ref[...], axis=-1, keepdims=True))
           - jnp.exp(jnp.sum(lq2_ref[...] * lk2_ref[...], axis=-1, keepdims=True)) + lam_init)
    shift = ATTN_CHUNK.bit_length() - 1
    row_chunk = jnp.right_shift(lax.broadcasted_iota(I32, (tq, tk), 0), shift)
    col_chunk = jnp.right_shift(lax.broadcasted_iota(I32, (tq, tk), 1), shift)
    bias_scr[0] = jnp.zeros((tq, tk), F32)
    for t in range(ndiag):
        bias_scr[t + 1] = jnp.where(col_chunk + t * (tk // ATTN_CHUNK) <= row_chunk, 0.0, NEG_BIG)
    nt = (((1,), (1,)), ((), ()))

    def lane_fold(a, op):
        part = a[:, 0:LANES]
        for cb in range(1, nlane):
            part = op(part, a[:, cb * LANES:(cb + 1) * LANES])
        return part

    def q_body(qi, _):
        qs = pl.multiple_of(qi * tq, tq)
        qm = (q_ref[pl.ds(qs, tq), 0:d], q_ref[pl.ds(qs, tq), d:2 * d])
        m_scr[...] = jnp.full_like(m_scr, NEG_BIG)

        nfull = qi * ndiag

        def score_body(j, c):
            ks = pl.multiple_of(j * tk, tk)
            bias = bias_scr[jnp.maximum(j - nfull + 1, 0)]
            for mp in range(2):
                k = k_ref[pl.ds(ks, tk), mp * d:(mp + 1) * d]
                s = lax.dot_general(qm[mp], k, nt, preferred_element_type=F32) + bias
                s_scr[mp, j] = s
                m_scr[mp] = jnp.maximum(m_scr[mp], lane_fold(s, jnp.maximum))
            return c

        _grouped_loop(nfull + ndiag, score_body, ATTN_GROUPS)
        m = [jnp.max(m_scr[mp], axis=-1, keepdims=True) for mp in range(2)]
        l_scr[...] = jnp.zeros_like(l_scr)
        acc_scr[...] = jnp.zeros_like(acc_scr)

        def pv_body(j, c):
            ks = pl.multiple_of(j * tk, tk)
            v = v_ref[pl.ds(ks, tk), :]
            for mp in range(2):
                p = jnp.exp2(s_scr[mp, j] - m[mp])
                l_scr[mp] += lane_fold(p, jnp.add)
                acc_scr[mp] += jnp.dot(p.astype(BF16), v, preferred_element_type=F32)
            return c

        _grouped_loop(nfull + ndiag, pv_body, ATTN_GROUPS)
        l = [jnp.sum(l_scr[mp], axis=-1, keepdims=True) for mp in range(2)]
        o = acc_scr[0] / l[0] - lam * (acc_scr[1] / l[1])
        ms = jnp.mean(o * o, axis=-1, keepdims=True)
        o = o * lax.rsqrt(ms + NORM_EPS) * g_ref[...] * (1.0 - lam_init)
        o_ref[pl.ds(qs, tq), :] = o.astype(o_ref.dtype)
        return 0

    lax.fori_loop(0, seq // tq, q_body, 0)


def diff_attention(qkv, lq1, lk1, lq2, lk2, head_g, lam_init, tq=512, tk=256):
    bsz, seq, _ = qkv.shape
    h, dv = DA_HEADS, 2 * DA_HEAD_DIM
    vec = lambda a: a.reshape(1, -1).astype(F32)
    small = lambda n: pl.BlockSpec((1, n), lambda b, hh: (0, 0))
    return pl.pallas_call(
        functools.partial(_attn_kernel, lam_init=lam_init, tq=tq, tk=tk, seq=seq),
        grid=(bsz, h),
        in_specs=[pl.BlockSpec((None, seq, dv), lambda b, hh: (b, 0, hh)),
                  pl.BlockSpec((None, seq, dv), lambda b, hh: (b, 0, h + hh)),
                  pl.BlockSpec((None, seq, dv), lambda b, hh: (b, 0, 2 * h + hh)),
                  small(DA_HEAD_DIM), small(DA_HEAD_DIM), small(DA_HEAD_DIM), small(DA_HEAD_DIM),
                  small(dv)],
        out_specs=pl.BlockSpec((None, seq, dv), lambda b, hh: (b, 0, hh)),
        out_shape=jax.ShapeDtypeStruct((bsz, seq, h * dv), BF16),
        scratch_shapes=[pltpu.VMEM((2, seq // tk, tq, tk), F32), pltpu.VMEM((2, tq, LANES), F32),
                        pltpu.VMEM((2, tq, LANES), F32), pltpu.VMEM((2, tq, dv), F32),
                        pltpu.VMEM((tq // tk + 1, tq, tk), F32)],
        compiler_params=_cparams(("arbitrary", "arbitrary")),
        name="diff_attention",
    )(qkv, qkv, qkv, vec(lq1), vec(lk1), vec(lq2), vec(lk2), vec(head_g))


def _conv_silu_kernel(x_ref, w_ref, b_ref, o_ref, *, k_tile0):
    j = pl.program_id(1)
    scale = jnp.where(j >= k_tile0, ML_QK_DIM ** -0.5, 1.0).astype(F32)
    taps = [w_ref[ML_CONV - 1 - s:ML_CONV - s, :] for s in range(ML_CONV)]

    def conv(x, shifted):
        y = x * taps[0] + b_ref[...]
        for s in range(1, ML_CONV):
            y = y + shifted(x, s) * taps[s]
        return (y * jax.nn.sigmoid(y) * scale).astype(o_ref.dtype)

    o_ref[...] = conv(x_ref[...].astype(F32), lambda x, s: pltpu.roll(x, s, 0))
    head = ROW_TILE
    row = lax.broadcasted_iota(I32, (head, x_ref.shape[1]), 0)
    o_ref[0:head, :] = conv(x_ref[0:head, :].astype(F32),
                            lambda x, s: jnp.where(row >= s, pltpu.roll(x, s, 0), 0.0))


def conv_silu(proj, conv_w, conv_b, tc=256):
    bsz, seq, _ = proj.shape
    cols = conv_w.shape[1]
    return pl.pallas_call(
        functools.partial(_conv_silu_kernel, k_tile0=(cols // 2) // tc),
        grid=(bsz, cols // tc),
        in_specs=[pl.BlockSpec((None, seq, tc), lambda b, j: (b, 0, j)),
                  pl.BlockSpec((ML_CONV, tc), lambda b, j: (0, j)),
                  pl.BlockSpec((1, tc), lambda b, j: (0, j))],
        out_specs=pl.BlockSpec((None, seq, tc), lambda b, j: (b, 0, j)),
        out_shape=jax.ShapeDtypeStruct((bsz, seq, cols), BF16),
        compiler_params=_cparams(("arbitrary", "arbitrary")),
        name="conv_silu",
    )(proj, conv_w, conv_b.reshape(1, cols))


def _mlstm_kernel(q_ref, k_ref, v_ref, op_ref, gc_ref, gr_ref, hg_ref, o_ref,
                  cx_scr, m_scr, *, chunk):
    c = pl.program_id(1)
    nh, dqk, dv = ML_HEADS, ML_QK_DIM, ML_V_DIM

    @pl.when(c == 0)
    def _():
        cx_scr[...] = jnp.zeros_like(cx_scr)
        m_scr[...] = jnp.zeros_like(m_scr)

    gc = gc_ref[...]
    gr = gr_ref[...]
    lf_c = jax.nn.log_sigmoid(gc)
    lf_r = jax.nn.log_sigmoid(gr)
    r_i = lax.broadcasted_iota(I32, (chunk, chunk), 0)
    c_i = lax.broadcasted_iota(I32, (chunk, chunk), 1)
    causal = c_i <= r_i
    tril = causal.astype(F32)
    triu = (r_i <= c_i).astype(F32)
    b_c = _dot3(tril, lf_c)
    b_r = _dot3(lf_r, triu)
    nt = (((1,), (1,)), ((), ()))
    tn_ = (((0,), (0,)), ((), ()))
    ones_l = jnp.ones((chunk, LANES), BF16)
    ones_v = jnp.ones((dv, LANES), BF16)
    wide = lambda a: jnp.concatenate([a] * (dv // LANES), axis=1)

    for h in range(nh):
        q = q_ref[:, h * dqk:(h + 1) * dqk]
        k = k_ref[:, h * dqk:(h + 1) * dqk]
        v_ext = jnp.concatenate([v_ref[:, h * dv:(h + 1) * dv], ones_l], axis=1)
        bc = b_c[:, nh + h:nh + h + 1]
        br = b_r[nh + h:nh + h + 1, :]
        ig_c = gc[:, h:h + 1]
        ig_r = gr[h:h + 1, :]
        m_prev = m_scr[h:h + 1, :]
        dmat = jnp.where(causal, bc - br + ig_r, NEG_BIG)
        inter = bc + m_prev
        m_t = jnp.maximum(inter, jnp.max(dmat, axis=-1, keepdims=True))
        w = jnp.exp(dmat - m_t)
        s = lax.dot_general(q, k, nt, preferred_element_type=F32) * w
        decay = jnp.exp(inter - m_t)
        cx = cx_scr[h]
        tot = (jnp.dot(s.astype(BF16), v_ext, preferred_element_type=F32)
               + decay * jnp.dot(q, cx.astype(BF16), preferred_element_type=F32))
        den = jnp.maximum(jnp.abs(tot[:, dv:]), jnp.exp(-m_t))
        hh = tot[:, :dv] / wide(den)
        b_last = bc[chunk - 1:chunk, :]
        g = b_last - bc + ig_c
        m_new = jnp.maximum(b_last + m_prev, jnp.max(g, axis=0, keepdims=True))
        carry_decay = jnp.exp(b_last + m_prev - m_new)
        wg = jnp.exp(g - m_new)
        wv = (wg * v_ext.astype(F32)).astype(BF16)
        cx_scr[h] = carry_decay * cx + lax.dot_general(k, wv, tn_, preferred_element_type=F32)
        m_scr[h:h + 1, :] = m_new
        ms = jnp.dot((hh * hh).astype(BF16), ones_v, preferred_element_type=F32) * (1.0 / dv)
        hn = hh * wide(lax.rsqrt(ms + NORM_EPS)) * hg_ref[:, h * dv:(h + 1) * dv]
        og = jax.nn.sigmoid(op_ref[:, h * dv:(h + 1) * dv].astype(F32))
        o_ref[:, h * dv:(h + 1) * dv] = (og * hn).astype(o_ref.dtype)


def mlstm(qk, proj, gates_c, gates_r, head_g, chunk=256):
    bsz, seq, _ = qk.shape
    nh = ML_HEADS
    qw, vw = nh * ML_QK_DIM, nh * ML_V_DIM
    v_blk = (2 * qw) // vw
    return pl.pallas_call(
        functools.partial(_mlstm_kernel, chunk=chunk),
        grid=(bsz, seq // chunk),
        in_specs=[pl.BlockSpec((None, chunk, qw), lambda b, c: (b, c, 0)),
                  pl.BlockSpec((None, chunk, qw), lambda b, c: (b, c, 1)),
                  pl.BlockSpec((None, chunk, vw), lambda b, c: (b, c, v_blk)),
                  pl.BlockSpec((None, chunk, vw), lambda b, c: (b, c, v_blk + 1)),
                  pl.BlockSpec((None, chunk, LANES), lambda b, c: (b, c, 0)),
                  pl.BlockSpec((2 * nh, chunk), lambda b, c: (0, b * (seq // chunk) + c)),
                  pl.BlockSpec((1, vw), lambda b, c: (0, 0))],
        out_specs=pl.BlockSpec((None, chunk, vw), lambda b, c: (b, c, 0)),
        out_shape=jax.ShapeDtypeStruct((bsz, seq, vw), BF16),
        scratch_shapes=[pltpu.VMEM((nh, ML_QK_DIM, ML_V_DIM + LANES), F32),
                        pltpu.VMEM((nh, 1), F32)],
        compiler_params=_cparams(("arbitrary", "arbitrary")),
        name="mlstm",
    )(qk, qk, proj, proj, gates_c, gates_r, head_g.reshape(1, vw))


def _dest_kernel(rt_ref, ps_ref, o_ref):
    rt = rt_ref[...]
    lane = lax.broadcasted_iota(I32, rt.shape, 1).astype(F32)
    ng = float(MOE_GROUPS)
    ps = ps_ref[...]
    d1 = jnp.sum(jnp.where(lane == rt[:, 0:1] + ng, ps, 0.0), axis=-1, keepdims=True) + rt[:, 4:5]
    d2 = jnp.sum(jnp.where(lane == rt[:, 1:2] + ng, ps, 0.0), axis=-1, keepdims=True) + rt[:, 5:6]
    o_ref[...] = jnp.where(lane == 0.0, d1, jnp.where(lane == 1.0, d2, 0.0)).astype(I32)


def moe_dest(route, pad_start_lanes, tm=2048):
    n = route.shape[0]
    return pl.pallas_call(
        _dest_kernel,
        grid=(n // tm,),
        in_specs=[pl.BlockSpec((tm, LANES), lambda i: (i, 0)),
                  pl.BlockSpec((1, LANES), lambda i: (0, 0))],
        out_specs=pl.BlockSpec((tm, LANES), lambda i: (i, 0)),
        out_shape=jax.ShapeDtypeStruct((n, LANES), I32),
        compiler_params=_cparams(("arbitrary",)),
        name="moe_dest",
    )(route, pad_start_lanes)


DMA_UNROLL = 8


def _wait_rows(src_rows, dst_rows, sem, copies):
    for _ in range(copies):
        pltpu.make_async_copy(src_rows, dst_rows, sem).wait()


def _dispatch_kernel(zb_ref, dest_ref, hp_ref, xin_ref, zero_scr, ring, sems, *, tt, tm, nb, ntiles):
    i = pl.program_id(0)
    slot = i % 2
    blk_rows = tm * ROW_TILE
    tile_rows = tt * ROW_TILE

    @pl.when(i == 0)
    def _():
        zero_scr[...] = jnp.zeros_like(zero_scr)

        def zero_copy(blk):
            return pltpu.make_async_copy(zero_scr, xin_ref.at[pl.ds(pl.multiple_of(blk * blk_rows, blk_rows), blk_rows)],
                                         sems.at[2])

        def start(blk, c):
            @pl.when(zb_ref[blk] == 1)
            def _():
                zero_copy(blk).start()
            return c

        def wait(blk, c):
            @pl.when(zb_ref[blk] == 1)
            def _():
                zero_copy(blk).wait()
            return c

        lax.fori_loop(0, nb, start, 0)
        lax.fori_loop(0, nb, wait, 0)

    ring[slot] = hp_ref[...]

    def start_rows(r, c):
        src = ring.at[slot, pl.ds(pl.multiple_of(r * ROW_TILE, ROW_TILE), ROW_TILE)]
        for kk in range(2):
            row = pl.multiple_of(dest_ref[0, 0, 2 * r + kk] * ROW_TILE, ROW_TILE)
            pltpu.make_async_copy(src, xin_ref.at[pl.ds(row, ROW_TILE)], sems.at[slot]).start(priority=kk)
        return c

    lax.fori_loop(0, tt, start_rows, 0, unroll=DMA_UNROLL)
    whole = xin_ref.at[pl.ds(0, tile_rows)]

    @pl.when(i > 0)
    def _():
        _wait_rows(ring.at[1 - slot], whole, sems.at[1 - slot], 2)

    @pl.when(i == ntiles - 1)
    def _():
        _wait_rows(ring.at[slot], whole, sems.at[slot], 2)


def moe_dispatch(hpack, dest, zero_blocks, rows, tm, tt=512):
    n = hpack.shape[0] // ROW_TILE
    grid_spec = pltpu.PrefetchScalarGridSpec(
        num_scalar_prefetch=1,
        grid=(n // tt,),
        in_specs=[pl.BlockSpec((1, 1, 2 * tt), lambda i, zb: (i, 0, 0), memory_space=pltpu.SMEM),
                  pl.BlockSpec((tt * ROW_TILE, LANES), lambda i, zb: (i, 0))],
        out_specs=pl.BlockSpec(memory_space=pl.ANY),
        scratch_shapes=[pltpu.VMEM((tm * ROW_TILE, LANES), U32), pltpu.VMEM((2, tt * ROW_TILE, LANES), U32),
                        pltpu.SemaphoreType.DMA((3,))],
    )
    return pl.pallas_call(
        functools.partial(_dispatch_kernel, tt=tt, tm=tm, nb=rows // tm, ntiles=n // tt),
        grid_spec=grid_spec,
        out_shape=jax.ShapeDtypeStruct((rows * ROW_TILE, LANES), U32),
        compiler_params=_cparams(("arbitrary",)),
        name="moe_dispatch",
    )(zero_blocks, dest.reshape(n // tt, 1, 2 * tt), hpack)


def _expert_kernel(be_ref, first_ref, nxt_ref, nv_ref, nu_ref, x_ref, wgu_hbm, wd_hbm, y_ref,
                   wgu_bf, wd_bf, stg_gu, stg_d, xs_scr, acc_scr, sems, *, layer, th, cr):
    i = pl.program_id(0)

    def fetch(e):
        return (pltpu.make_async_copy(wgu_hbm.at[layer, e], stg_gu, sems.at[0]),
                pltpu.make_async_copy(wd_hbm.at[layer, e], stg_d, sems.at[1]))

    @pl.when(i < nu_ref[0])
    def _():
        @pl.when(first_ref[i] == 1)
        def _():
            @pl.when(i == 0)
            def _():
                for cp in fetch(be_ref[i]):
                    cp.start()

            for cp in fetch(be_ref[i]):
                cp.wait()

            def cast_rows(src, dst):
                def body(r, c):
                    rs = pl.multiple_of(r * cr, cr)
                    dst[pl.ds(rs, cr), :] = src[pl.ds(rs, cr), :].astype(BF16)
                    return c
                lax.fori_loop(0, src.shape[0] // cr, body, 0)

            cast_rows(stg_gu, wgu_bf)
            cast_rows(stg_d, wd_bf)

            @pl.when(nxt_ref[i] >= 0)
            def _():
                for cp in fetch(nxt_ref[i]):
                    cp.start()

        tm = xs_scr.shape[0]
        half = xs_scr.shape[1] // 2
        hid = wd_bf.shape[0]

        def mlp(rows):
            for sub in range(ROW_TILE):
                lo, hi = _unpack_bf16_pair(_load_token_rows(x_ref, sub, rows))
                xs_scr[0:rows, sub * LANES:(sub + 1) * LANES] = lo.astype(BF16)
                xs_scr[0:rows, half + sub * LANES:half + (sub + 1) * LANES] = hi.astype(BF16)
            xs = xs_scr[0:rows, :]
            for c in range(hid // th):
                gt = jnp.dot(xs, wgu_bf[:, c * th:(c + 1) * th], preferred_element_type=F32)
                up = jnp.dot(xs, wgu_bf[:, hid + c * th:hid + (c + 1) * th], preferred_element_type=F32)
                act = (gt * jax.nn.sigmoid(gt) * up).astype(BF16)
                part = jnp.dot(act, wd_bf[c * th:(c + 1) * th, :], preferred_element_type=F32)
                if c == 0:
                    acc_scr[0:rows, :] = part
                else:
                    acc_scr[0:rows, :] += part
            y = acc_scr[0:rows, :]
            _store_token_rows(y_ref, _pack_bf16_pair(y[:, :half], y[:, half:]))
            if rows < tm:
                y_ref[rows * ROW_TILE:, :] = jnp.zeros(((tm - rows) * ROW_TILE, LANES), U32)

        @pl.when(nv_ref[i] > tm // 2)
        def _():
            mlp(tm)

        @pl.when(nv_ref[i] <= tm // 2)
        def _():
            mlp(tm // 2)


def moe_experts(xin, sched, wgu_all, wd_all, layer, tm, th=256, cr=256):
    _, _, d, hid2 = wgu_all.shape
    hid = hid2 // 2
    nb = xin.shape[0] // (tm * ROW_TILE)
    block_e, first, nxt, valid, n_used = sched

    def blk(i, be, fi, nx, nv, nu):
        return (jnp.minimum(i, nu[0] - 1), 0)

    grid_spec = pltpu.PrefetchScalarGridSpec(
        num_scalar_prefetch=5,
        grid=(nb,),
        in_specs=[pl.BlockSpec((tm * ROW_TILE, LANES), blk),
                  pl.BlockSpec(memory_space=pl.ANY),
                  pl.BlockSpec(memory_space=pl.ANY)],
        out_specs=pl.BlockSpec((tm * ROW_TILE, LANES), blk),
        scratch_shapes=[pltpu.VMEM((d, hid2), BF16), pltpu.VMEM((hid, d), BF16),
                        pltpu.VMEM((d, hid2), F32), pltpu.VMEM((hid, d), F32),
                        pltpu.VMEM((tm, d), BF16), pltpu.VMEM((tm, d), F32),
                        pltpu.SemaphoreType.DMA((2,))],
    )
    return pl.pallas_call(
        functools.partial(_expert_kernel, layer=layer, th=th, cr=cr),
        grid_spec=grid_spec,
        out_shape=jax.ShapeDtypeStruct(xin.shape, U32),
        input_output_aliases={5: 0},
        compiler_params=_cparams(("arbitrary",)),
        name="moe_experts",
    )(block_e, first, nxt, valid, n_used, xin, wgu_all, wd_all)


def _combine_kernel(dcur_ref, dnxt_ref, x_ref, rt_ref, mod_ref, fg_ref, y_ref, o_ref, ya, yb, sems,
                    *, tt, ntiles, final_norm):
    i = pl.program_id(0)
    slot = i % 2

    def issue(dref, sl):
        def body(r, c):
            dst = pl.ds(pl.multiple_of(r * ROW_TILE, ROW_TILE), ROW_TILE)
            for kk, buf in enumerate((ya, yb)):
                row = pl.multiple_of(dref[0, 0, 2 * r + kk] * ROW_TILE, ROW_TILE)
                pltpu.make_async_copy(y_ref.at[pl.ds(row, ROW_TILE)], buf.at[sl, dst], sems.at[sl]).start(priority=kk)
            return c
        lax.fori_loop(0, tt, body, 0, unroll=DMA_UNROLL)

    @pl.when(i == 0)
    def _():
        issue(dcur_ref, 0)

    @pl.when(i + 1 < ntiles)
    def _():
        issue(dnxt_ref, 1 - slot)

    _wait_rows(y_ref.at[pl.ds(0, tt * ROW_TILE)], ya.at[slot], sems.at[slot], 2)
    half = x_ref.shape[1] // 2
    w1 = rt_ref[:, 2:3]
    w2 = rt_ref[:, 3:4]
    ssq = jnp.zeros((tt, 1), F32)
    for sub in range(ROW_TILE):
        a_lo, a_hi = _unpack_bf16_pair(_load_token_rows(ya.at[slot], sub, tt))
        b_lo, b_hi = _unpack_bf16_pair(_load_token_rows(yb.at[slot], sub, tt))
        for base, a, b in ((sub * LANES, a_lo, b_lo), (half + sub * LANES, a_hi, b_hi)):
            cols = slice(base, base + LANES)
            out = x_ref[:, cols] + mod_ref[5:6, cols] * (a * w1 + b * w2)
            o_ref[:, cols] = out
            if final_norm:
                ssq = ssq + jnp.sum(out * out, axis=-1, keepdims=True)
    if final_norm:
        r = lax.rsqrt(ssq / (2 * half) + NORM_EPS)
        o_ref[...] = o_ref[...] * r * fg_ref[...]


def moe_combine(x, y, dest, route, mod, final_g, seq, final_norm, tt=512):
    n, d = x.shape
    tiles_per_seq = seq // tt
    ntiles = n // tt
    dest3 = dest.reshape(ntiles, 1, 2 * tt)
    return pl.pallas_call(
        functools.partial(_combine_kernel, tt=tt, ntiles=ntiles, final_norm=final_norm),
        grid=(ntiles,),
        in_specs=[pl.BlockSpec((1, 1, 2 * tt), lambda i: (i, 0, 0), memory_space=pltpu.SMEM),
                  pl.BlockSpec((1, 1, 2 * tt), lambda i: (jnp.minimum(i + 1, ntiles - 1), 0, 0),
                               memory_space=pltpu.SMEM),
                  pl.BlockSpec((tt, d), lambda i: (i, 0)),
                  pl.BlockSpec((tt, LANES), lambda i: (i, 0)),
                  pl.BlockSpec((None, 6, d), lambda i: (i // tiles_per_seq, 0, 0)),
                  pl.BlockSpec((1, d), lambda i: (0, 0)),
                  pl.BlockSpec(memory_space=pl.ANY)],
        out_specs=pl.BlockSpec((tt, d), lambda i: (i, 0)),
        out_shape=jax.ShapeDtypeStruct((n, d), F32),
        scratch_shapes=[pltpu.VMEM((2, tt * ROW_TILE, LANES), U32), pltpu.VMEM((2, tt * ROW_TILE, LANES), U32),
                        pltpu.SemaphoreType.DMA((2,))],
        compiler_params=_cparams(("arbitrary",)),
        name="moe_combine",
    )(dest3, dest3, x, route, mod, final_g.reshape(1, d), y)


def _expert_schedule(counts, tm, nb):
    ne = counts.shape[0]
    ids = jnp.arange(ne, dtype=I32)
    padded = ((counts + tm - 1) // tm) * tm
    pad_end = jnp.sum(jnp.where(ids[None, :] <= ids[:, None], padded[None, :], 0), axis=1)
    pad_start = pad_end - padded
    total = jnp.sum(padded)
    n_used = total // tm
    blk0 = jnp.arange(nb, dtype=I32) * tm
    block_e = jnp.minimum(jnp.sum((pad_end[None, :] <= blk0[:, None]).astype(I32), axis=1), ne - 1)
    onehot = block_e[:, None] == ids[None, :]
    pick = lambda v: jnp.sum(jnp.where(onehot, v[None, :], 0), axis=1)
    first = (blk0 == pick(pad_start)).astype(I32)
    later = jnp.min(jnp.where((ids[None, :] > ids[:, None]) & (counts[None, :] > 0), ids[None, :], ne), axis=1)
    nxt = pick(jnp.where(later < ne, later, -1))
    has_padding = (blk0 + tm == pick(pad_end)) & (pick(counts % tm) != 0)
    zero_blocks = ((blk0 >= total) | has_padding).astype(I32)
    valid = jnp.clip(pick(pad_start + counts) - blk0, 0, tm)
    return pad_start, zero_blocks, (block_e, first, nxt.astype(I32), valid.astype(I32),
                                    n_used.astype(I32).reshape(1))


def moe_router_weights(w_group, b_group, w_expert, b_expert):
    d = w_group.shape[0]
    ng, ne = w_group.shape[1], w_expert.shape[1]
    w_r = jnp.concatenate([w_group, w_expert, jnp.zeros((d, LANES - ng - ne), F32)], axis=1)
    b_r = jnp.concatenate([b_group, b_expert, jnp.zeros((LANES - ng - ne,), F32)]).reshape(1, LANES)
    return w_r, b_r


def hier_moe_layer(x, g, mod, w_r, b_r, wgu_all, wd_all, layer, final_g, seq, final_norm, tm=256):
    n, d = x.shape
    ng, ne = MOE_GROUPS, MOE_EXPERTS
    hpack, route, cnt = moe_router(x, g, mod, w_r, b_r, seq)
    counts = cnt[0, ng:ng + ne].astype(I32)
    rows = 2 * n + ne * tm
    pad_start, zero_blocks, sched = _expert_schedule(counts, tm, rows // tm)
    ps_lanes = jnp.concatenate([jnp.zeros((ng,), F32), pad_start.astype(F32),
                                jnp.zeros((LANES - ng - ne,), F32)]).reshape(1, LANES)
    dest = moe_dest(route, ps_lanes, tm=min(2048, n))[:, :2].reshape(-1)
    xin = moe_dispatch(hpack, dest, zero_blocks, rows, tm)
    y = moe_experts(xin, sched, wgu_all, wd_all, layer, tm)
    return moe_combine(x, y, dest, route, mod, final_g, seq, final_norm)


def _rope_tables(positions):
    half = ROPE_DIM // 2
    inv_freq = ROPE_THETA ** (-jnp.arange(half, dtype=F32) * 2.0 / ROPE_DIM)
    gap = jnp.zeros((LANES // 2 - half,), F32)
    freq = jnp.concatenate([-inv_freq, gap, inv_freq, gap])
    ang = positions.astype(F32).reshape(-1, 1) * freq[None, :]
    return jnp.cos(ang), jnp.sin(ang)


def _weight_prep_kernel(w_ref, o_ref, *, pair_tiles, tn):
    j = pl.program_id(0)
    half = ROPE_DIM // 2
    mid = LANES // 2

    @pl.when(j >= pair_tiles)
    def _():
        o_ref[...] = w_ref[...].astype(BF16)

    if pair_tiles:
        @pl.when(j < pair_tiles)
        def _():
            lane = lax.broadcasted_iota(I32, (w_ref.shape[0], LANES), 1)
            for m in range(tn // LANES):
                t = w_ref[:, m * LANES:(m + 1) * LANES]
                up = pltpu.roll(t, LANES - half, 1)
                down = pltpu.roll(t, mid - half, 1)
                new = jnp.where(lane < half, t, jnp.where(lane < mid, up, jnp.where(lane < mid + half, down, t)))
                o_ref[:, m * LANES:(m + 1) * LANES] = new.astype(BF16)


def weight_prep(w, cols, pair_cols=0, tn=512, w_rows=False):
    if w_rows:
        k = w.shape[1]
        return pl.pallas_call(
            functools.partial(_weight_prep_kernel, pair_tiles=0, tn=tn),
            grid=(cols // tn,),
            in_specs=[pl.BlockSpec((tn, k), lambda j: (j, 0))],
            out_specs=pl.BlockSpec((tn, k), lambda j: (j, 0)),
            out_shape=jax.ShapeDtypeStruct((cols, k), BF16),
            compiler_params=_cparams(("arbitrary",)),
            name="weight_prep",
        )(w)
    k = w.shape[0]
    return pl.pallas_call(
        functools.partial(_weight_prep_kernel, pair_tiles=pair_cols // tn, tn=tn),
        grid=(cols // tn,),
        in_specs=[pl.BlockSpec((k, tn), lambda j: (0, j))],
        out_specs=pl.BlockSpec((k, tn), lambda j: (0, j)),
        out_shape=jax.ShapeDtypeStruct((k, cols), BF16),
        compiler_params=_cparams(("arbitrary",)),
        name="weight_prep",
    )(w)


def kernel(x, c, positions, ada_w, ada_b, norm_mix_g, norm_ffn_g, final_norm_g, attn_w_in, attn_w_out, attn_lambda_q1, attn_lambda_k1, attn_lambda_q2, attn_lambda_k2, attn_head_norm_g, mlstm_w_in, mlstm_conv_w, mlstm_conv_b, mlstm_gate_b, mlstm_head_norm_g, mlstm_w_out, moe_w_group, moe_b_group, moe_w_expert, moe_b_expert, moe_w_gu, moe_w_down):
    bsz, seq, d = x.shape
    n = bsz * seq
    depth = ada_w.shape[0]
    mod = adaln(c, ada_w, ada_b)
    xf = x.reshape(n, d)
    for i in range(depth):
        jm = i // 2
        if i % 2 == 0:
            qk_cols = 2 * DA_HEADS * 2 * DA_HEAD_DIM
            tabs = _rope_tables(positions)
            w_in = weight_prep(attn_w_in[jm], attn_w_in.shape[2], pair_cols=qk_cols)
            qkv = norm_matmul(xf, norm_mix_g[i], mod[i], w_in, seq, rope=(*tabs, qk_cols, qk_cols // 2))
            lam_init = 0.8 - 0.6 * math.exp(-0.3 * i)
            mixed = diff_attention(qkv.reshape(bsz, seq, -1), attn_lambda_q1[jm], attn_lambda_k1[jm],
                                   attn_lambda_q2[jm], attn_lambda_k2[jm], attn_head_norm_g[jm], lam_init)
            w_out = attn_w_out[jm]
        else:
            qk_cols = 2 * ML_HEADS * ML_QK_DIM
            main_cols = qk_cols + 2 * ML_HEADS * ML_V_DIM
            w_in_t = jnp.swapaxes(mlstm_w_in[jm], 0, 1)
            ngate = 2 * ML_HEADS
            w_gate = jnp.concatenate([w_in_t[main_cols:], jnp.zeros((LANES - ngate, d), F32)], axis=0)
            b_gate = jnp.concatenate([mlstm_gate_b[jm], jnp.zeros((LANES - ngate,), F32)]).reshape(1, LANES)
            proj, gates, gates_t = norm_matmul(xf, norm_mix_g[i], mod[i],
                                               weight_prep(w_in_t, main_cols, w_rows=True), seq,
                                               extra=(w_gate, b_gate), w_rows=True)
            proj = proj.reshape(bsz, seq, main_cols)
            qk = conv_silu(proj, mlstm_conv_w[jm], mlstm_conv_b[jm])
            mixed = mlstm(qk, proj, gates.reshape(bsz, seq, LANES), gates_t, mlstm_head_norm_g[jm])
            w_out = mlstm_w_out[jm]
        w_r, b_r = moe_router_weights(moe_w_group[i], moe_b_group[i], moe_w_expert[i], moe_b_expert[i])
        xf = matmul_res(mixed.reshape(n, -1), w_out.astype(BF16), xf, mod[i], seq, gate_row=2)
        xf = hier_moe_layer(xf, norm_ffn_g[i], mod[i], w_r, b_r, moe_w_gu, moe_w_down, i, final_norm_g, seq,
                            final_norm=(i == depth - 1))
    return xf.reshape(bsz, seq, d)
```

```python
import functools
import math

import jax
import jax.numpy as jnp
from jax import lax
from jax.experimental import pallas as pl
from jax.experimental.pallas import tpu as pltpu

F32 = jnp.float32
BF16 = jnp.bfloat16
U32 = jnp.uint32
I32 = jnp.int32

NORM_EPS = 1e-6
ROPE_THETA = 500000.0
ATTN_CHUNK = 64
DA_HEADS = 8
DA_HEAD_DIM = 128
ROPE_DIM = 32
ML_HEADS = 8
ML_QK_DIM = 128
ML_V_DIM = 256
ML_CONV = 4
MOE_GROUPS = 4
MOE_PER_GROUP = 8
MOE_EXPERTS = 32
LANES = 128
NEG_BIG = -1e30
LOG2_E = math.log2(math.e)

VMEM_LIMIT = 56 * 1024 * 1024


def _cparams(sem):
    return pltpu.CompilerParams(dimension_semantics=sem, vmem_limit_bytes=VMEM_LIMIT)


def _split_hi_lo(a):
    hi = a.astype(BF16)
    lo = (a - hi.astype(F32)).astype(BF16)
    return hi, lo


def _dot3(a, w, w_rows=False):
    ah, al = _split_hi_lo(a)
    wh, wl = _split_hi_lo(w)
    contract = (((1,), (1 if w_rows else 0,)), ((), ()))
    d = functools.partial(lax.dot_general, dimension_numbers=contract, preferred_element_type=F32)
    return d(ah, wh) + (d(ah, wl) + d(al, wh))


def _pack_bf16_pair(lo_f32, hi_f32):
    lo_bits = lax.bitcast_convert_type(lo_f32.astype(BF16).astype(F32), U32)
    hi_bits = lax.bitcast_convert_type(hi_f32.astype(BF16).astype(F32), U32)
    return hi_bits | (lo_bits >> 16)


def _unpack_bf16_pair(word):
    lo = lax.bitcast_convert_type(word << 16, F32)
    hi = lax.bitcast_convert_type(word & jnp.uint32(0xFFFF0000), F32)
    return lo, hi


ROW_TILE = 8


def _store_token_rows(ref, words):
    t = words.shape[0]
    for sub in range(ROW_TILE):
        ref[pl.ds(sub, t, stride=ROW_TILE), :] = words[:, sub * LANES:(sub + 1) * LANES]


def _load_token_rows(ref, sub, t):
    return ref[pl.ds(sub, t, stride=ROW_TILE), :]


def _rms_modulate(x, g, shift, scale):
    ms = jnp.mean(x * x, axis=-1, keepdims=True)
    y = x * lax.rsqrt(ms + NORM_EPS) * g
    return y * (1.0 + scale) + shift


def _adaln_kernel(c_ref, w_ref, b_ref, o_ref):
    c = c_ref[...]
    cond = c * jax.nn.sigmoid(c)
    acc = jnp.dot(cond.astype(BF16), w_ref[...].astype(BF16), preferred_element_type=F32)
    o_ref[...] = acc + b_ref[...]


def adaln(c, ada_w, ada_b, tn=1024):
    depth, d, n6 = ada_w.shape
    bsz = c.shape[0]
    rows = 8
    cp = jnp.zeros((rows, d), F32).at[:bsz].set(c)
    out = pl.pallas_call(
        _adaln_kernel,
        grid=(depth, n6 // tn),
        in_specs=[pl.BlockSpec((rows, d), lambda l, j: (0, 0)),
                  pl.BlockSpec((None, d, tn), lambda l, j: (l, 0, j)),
                  pl.BlockSpec((None, 1, tn), lambda l, j: (l, 0, j))],
        out_specs=pl.BlockSpec((None, rows, tn), lambda l, j: (l, 0, j)),
        out_shape=jax.ShapeDtypeStruct((depth, rows, n6), F32),
        compiler_params=_cparams(("arbitrary", "arbitrary")),
        name="adaln",
    )(cp, ada_w, ada_b.reshape(depth, 1, n6))
    return out[:, :bsz].reshape(depth, bsz, 6, d)


ROPE_ROWS = 64


def _norm_matmul_kernel(*refs, rope_tiles, q_tiles, has_extra, tn, nj, nsteps, w_rows):
    if has_extra:
        x_ref, g_ref, mod_ref, w_ref, c_ref, s_ref, we_ref, be_ref, o_ref, oe_ref, oet_ref, h_scr, acc_scr = refs
    else:
        x_ref, g_ref, mod_ref, w_ref, c_ref, s_ref, o_ref, h_scr, acc_scr = refs
    t = pl.program_id(0)
    j = t % nj

    @pl.when((j == 0) & (t < nsteps))
    def _():
        h = _rms_modulate(x_ref[...], g_ref[...], mod_ref[0:1, :], mod_ref[1:2, :])
        h_scr[...] = h.astype(BF16)
        if has_extra:
            extra = _dot3(h, we_ref[...], w_rows) + be_ref[...]
            oe_ref[...] = extra
            oet_ref[...] = extra.T

    @pl.when(t == 0)
    def _():
        acc_scr[...] = jnp.zeros_like(acc_scr)

    contract = (((1,), (1 if w_rows else 0,)), ((), ()))
    acc = lax.dot_general(h_scr[...], w_ref[...], contract, preferred_element_type=F32)
    tm = acc_scr.shape[0]
    if rope_tiles:
        jp = (t + nj - 1) % nj
        rope_on = jp < rope_tiles
        scale = jnp.where(jp < q_tiles, DA_HEAD_DIM ** -0.5 * LOG2_E, 1.0).astype(F32)
    for r in range(tm // ROPE_ROWS):
        rows = slice(r * ROPE_ROWS, (r + 1) * ROPE_ROWS)
        if rope_tiles:
            cs = jnp.where(rope_on, c_ref[rows, :] * scale, 1.0)
            sn = jnp.where(rope_on, s_ref[rows, :] * scale, 0.0)
        for m in range(tn // LANES):
            cols = slice(m * LANES, (m + 1) * LANES)
            prev = acc_scr[rows, cols]
            if rope_tiles:
                prev = prev * cs + pltpu.roll(prev, LANES // 2, 1) * sn
            o_ref[rows, cols] = prev.astype(o_ref.dtype)
    acc_scr[...] = acc


def norm_matmul(x, g, mod, w, seq, rope=None, extra=None, w_rows=False, tm=1024, tn=1024):
    n, k = x.shape
    m = w.shape[0 if w_rows else 1]
    tiles_per_seq = seq // tm
    nj = m // tn
    nsteps = (n // tm) * nj
    has_extra = extra is not None
    cur = lambda t: jnp.minimum(t, nsteps - 1)
    prv = lambda t: jnp.maximum(t - 1, 0)
    if rope is None:
        dummy = jnp.zeros((8, LANES), F32)
        tabs = (dummy, dummy)
        tab_spec = pl.BlockSpec((8, LANES), lambda t: (0, 0))
        rope_tiles = q_tiles = 0
    else:
        tabs = rope[:2]
        tab_spec = pl.BlockSpec((tm, LANES), lambda t: (prv(t) // nj, 0))
        rope_tiles, q_tiles = rope[2] // tn, rope[3] // tn
    in_specs = [pl.BlockSpec((tm, k), lambda t: (cur(t) // nj, 0)),
                pl.BlockSpec((1, k), lambda t: (0, 0)),
                pl.BlockSpec((None, 6, k), lambda t: (cur(t) // nj // tiles_per_seq, 0, 0)),
                (pl.BlockSpec((tn, k), lambda t: (cur(t) % nj, 0)) if w_rows
                 else pl.BlockSpec((k, tn), lambda t: (0, cur(t) % nj))),
                tab_spec, tab_spec]
    args = [x, g.reshape(1, k), mod, w, *tabs]
    out_specs = pl.BlockSpec((tm, tn), lambda t: (prv(t) // nj, prv(t) % nj))
    out_shape = jax.ShapeDtypeStruct((n, m), BF16)
    if has_extra:
        we, be = extra
        in_specs += [pl.BlockSpec((LANES, k) if w_rows else (k, LANES), lambda t: (0, 0)),
                     pl.BlockSpec((1, LANES), lambda t: (0, 0))]
        args += [we, be]
        out_specs = [out_specs, pl.BlockSpec((tm, LANES), lambda t: (cur(t) // nj, 0)),
                     pl.BlockSpec((LANES, tm), lambda t: (0, cur(t) // nj))]
        out_shape = [out_shape, jax.ShapeDtypeStruct((n, LANES), F32), jax.ShapeDtypeStruct((LANES, n), F32)]
    return pl.pallas_call(
        functools.partial(_norm_matmul_kernel, rope_tiles=rope_tiles, q_tiles=q_tiles,
                          has_extra=has_extra, tn=tn, nj=nj, nsteps=nsteps, w_rows=w_rows),
        grid=(nsteps + 1,),
        in_specs=in_specs,
        out_specs=out_specs,
        out_shape=out_shape,
        scratch_shapes=[pltpu.VMEM((tm, k), BF16), pltpu.VMEM((tm, tn), F32)],
        compiler_params=_cparams(("arbitrary",)),
        name="norm_matmul",
    )(*args)


def _router_kernel(x_ref, g_ref, mod_ref, w_ref, b_ref, hp_ref, rt_ref, cnt_ref, run_scr):
    @pl.when(pl.program_id(0) == 0)
    def _():
        run_scr[...] = jnp.zeros_like(run_scr)

    h = _rms_modulate(x_ref[...], g_ref[...], mod_ref[3:4, :], mod_ref[4:5, :])
    half = h.shape[1] // 2
    _store_token_rows(hp_ref, _pack_bf16_pair(h[:, :half], h[:, half:]))
    logits = _dot3(h, w_ref[...]) + b_ref[...]
    tm = logits.shape[0]
    lane = lax.broadcasted_iota(I32, logits.shape, 1).astype(F32)
    ng = float(MOE_GROUPS)
    is_g = lane < ng
    gl = jnp.where(is_g, logits, NEG_BIG)
    gmax = jnp.max(gl, axis=-1, keepdims=True)
    grp = jnp.min(jnp.where(gl == gmax, lane, float(LANES)), axis=-1, keepdims=True)
    p_group = 1.0 / jnp.sum(jnp.where(is_g, jnp.exp(gl - gmax), 0.0), axis=-1, keepdims=True)
    lo = ng + float(MOE_PER_GROUP) * grp
    el = jnp.where((lane >= lo) & (lane < lo + float(MOE_PER_GROUP)), logits, NEG_BIG)
    v1 = jnp.max(el, axis=-1, keepdims=True)
    i1 = jnp.min(jnp.where(el == v1, lane, float(LANES)), axis=-1, keepdims=True)
    el2 = jnp.where(lane == i1, NEG_BIG, el)
    v2 = jnp.max(el2, axis=-1, keepdims=True)
    i2 = jnp.min(jnp.where(el2 == v2, lane, float(LANES)), axis=-1, keepdims=True)
    ex = jnp.exp(v2 - v1)
    w1 = p_group / (1.0 + ex)
    w2 = p_group * (ex / (1.0 + ex))
    oh1 = (lane == i1).astype(F32)
    oh2 = (lane == i2).astype(F32)
    oh = oh1 + oh2
    r_i = lax.broadcasted_iota(I32, (tm, tm), 0)
    c_i = lax.broadcasted_iota(I32, (tm, tm), 1)
    before = (c_i < r_i).astype(BF16)
    prior = jnp.dot(before, oh.astype(BF16), preferred_element_type=F32) + run_scr[...]
    rank1 = jnp.sum(oh1 * prior, axis=-1, keepdims=True)
    rank2 = jnp.sum(oh2 * prior, axis=-1, keepdims=True)
    run_scr[...] += jnp.sum(oh, axis=0, keepdims=True)
    cnt_ref[...] = run_scr[...]
    rt_ref[...] = jnp.where(lane == 0.0, i1 - ng,
                  jnp.where(lane == 1.0, i2 - ng,
                  jnp.where(lane == 2.0, w1,
                  jnp.where(lane == 3.0, w2,
                  jnp.where(lane == 4.0, rank1,
                  jnp.where(lane == 5.0, rank2, 0.0))))))


def moe_router(x, g, mod, w_r, b_r, seq, tm=512):
    n, d = x.shape
    tiles_per_seq = seq // tm
    return pl.pallas_call(
        _router_kernel,
        grid=(n // tm,),
        in_specs=[pl.BlockSpec((tm, d), lambda i: (i, 0)),
                  pl.BlockSpec((1, d), lambda i: (0, 0)),
                  pl.BlockSpec((None, 6, d), lambda i: (i // tiles_per_seq, 0, 0)),
                  pl.BlockSpec((d, LANES), lambda i: (0, 0)),
                  pl.BlockSpec((1, LANES), lambda i: (0, 0))],
        out_specs=[pl.BlockSpec((tm * ROW_TILE, LANES), lambda i: (i, 0)),
                   pl.BlockSpec((tm, LANES), lambda i: (i, 0)),
                   pl.BlockSpec((1, LANES), lambda i: (0, 0))],
        out_shape=[jax.ShapeDtypeStruct((n * ROW_TILE, LANES), U32),
                   jax.ShapeDtypeStruct((n, LANES), F32),
                   jax.ShapeDtypeStruct((1, LANES), F32)],
        scratch_shapes=[pltpu.VMEM((1, LANES), F32)],
        compiler_params=_cparams(("arbitrary",)),
        name="moe_router",
    )(x, g.reshape(1, d), mod, w_r, b_r)


def _matmul_res_kernel(a_ref, w_ref, res_ref, mod_ref, o_ref, *, gate_row):
    acc = jnp.dot(a_ref[...], w_ref[...], preferred_element_type=F32)
    o_ref[...] = res_ref[...] + mod_ref[gate_row:gate_row + 1, :] * acc


def matmul_res(a, w, res, mod, seq, gate_row, tm=512, tn=2048):
    n, k = a.shape
    m = w.shape[1]
    tiles_per_seq = seq // tm
    return pl.pallas_call(
        functools.partial(_matmul_res_kernel, gate_row=gate_row),
        grid=(n // tm, m // tn),
        in_specs=[pl.BlockSpec((tm, k), lambda i, j: (i, 0)),
                  pl.BlockSpec((k, tn), lambda i, j: (0, j)),
                  pl.BlockSpec((tm, tn), lambda i, j: (i, j)),
                  pl.BlockSpec((None, 6, tn), lambda i, j: (i // tiles_per_seq, 0, j))],
        out_specs=pl.BlockSpec((tm, tn), lambda i, j: (i, j)),
        out_shape=jax.ShapeDtypeStruct((n, m), F32),
        compiler_params=_cparams(("arbitrary", "arbitrary")),
        name="matmul_res",
    )(a, w, res, mod)


ATTN_GROUPS = (8, 4, 2)


def _grouped_loop(n, body, groups):
    done = 0
    for group in groups:
        trips = (n - done) // group

        def grouped(gi, c, group=group, done=done):
            for t in range(group):
                c = body(done + gi * group + t, c)
            return c

        lax.fori_loop(0, trips, grouped, 0)
        done = done + trips * group
    lax.fori_loop(done, n, body, 0)


def _attn_kernel(q_ref, k_ref, v_ref, lq1_ref, lk1_ref, lq2_ref, lk2_ref, g_ref, o_ref,
                 s_scr, m_scr, l_scr, acc_scr, bias_scr, *, lam_init, tq, tk, seq):
    d = DA_HEAD_DIM
    nlane = tk // LANES
    ndiag = tq // tk
    lam = (jnp.exp(jnp.sum(lq1_ref[...] * lk1_ref[...], axis=-1, keepdims=True))
           - jnp.exp(jnp.sum(lq2_ref[...] * lk2_ref[...], axis=-1, keepdims=True)) + lam_init)
    shift = ATTN_CHUNK.bit_length() - 1
    row_chunk = jnp.right_shift(lax.broadcasted_iota(I32, (tq, tk), 0), shift)
    col_chunk = jnp.right_shift(lax.broadcasted_iota(I32, (tq, tk), 1), shift)
    bias_scr[0] = jnp.zeros((tq, tk), F32)
    for t in range(ndiag):
        bias_scr[t + 1] = jnp.where(col_chunk + t * (tk // ATTN_CHUNK) <= row_chunk, 0.0, NEG_BIG)
    nt = (((1,), (1,)), ((), ()))

    def lane_fold(a, op):
        part = a[:, 0:LANES]
        for cb in range(1, nlane):
            part = op(part, a[:, cb * LANES:(cb + 1) * LANES])
        return part

    def q_body(qi, _):
        qs = pl.multiple_of(qi * tq, tq)
        qm = (q_ref[pl.ds(qs, tq), 0:d], q_ref[pl.ds(qs, tq), d:2 * d])
        m_scr[...] = jnp.full_like(m_scr, NEG_BIG)

        nfull = qi * ndiag

        def score_body(j, c):
            ks = pl.multiple_of(j * tk, tk)
            bias = bias_scr[jnp.maximum(j - nfull + 1, 0)]
            for mp in range(2):
                k = k_ref[pl.ds(ks, tk), mp * d:(mp + 1) * d]
                s = lax.dot_general(qm[mp], k, nt, preferred_element_type=F32) + bias
                s_scr[mp, j] = s
                m_scr[mp] = jnp.maximum(m_scr[mp], lane_fold(s, jnp.maximum))
            return c

        _grouped_loop(nfull + ndiag, score_body, ATTN_GROUPS)
        m = [jnp.max(m_scr[mp], axis=-1, keepdims=True) for mp in range(2)]
        l_scr[...] = jnp.zeros_like(l_scr)
        acc_scr[...] = jnp.zeros_like(acc_scr)

        def pv_body(j, c):
            ks = pl.multiple_of(j * tk, tk)
            v = v_ref[pl.ds(ks, tk), :]
            for mp in range(2):
                p = jnp.exp2(s_scr[mp, j] - m[mp])
                l_scr[mp] += lane_fold(p, jnp.add)
                acc_scr[mp] += jnp.dot(p.astype(BF16), v, preferred_element_type=F32)
            return c

        _grouped_loop(nfull + ndiag, pv_body, ATTN_GROUPS)
        l = [jnp.sum(l_scr[mp], axis=-1, keepdims=True) for mp in range(2)]
        o = acc_scr[0] / l[0] - lam * (acc_scr[1] / l[1])
        ms = jnp.mean(o * o, axis=-1, keepdims=True)
        o = o * lax.rsqrt(ms + NORM_EPS) * g_ref[...] * (1.0 - lam_init)
        o_ref[pl.ds(qs, tq), :] = o.astype(o_ref.dtype)
        return 0

    lax.fori_loop(0, seq // tq, q_body, 0)


def diff_attention(qkv, lq1, lk1, lq2, lk2, head_g, lam_init, tq=512, tk=256):
    bsz, seq, _ = qkv.shape
    h, dv = DA_HEADS, 2 * DA_HEAD_DIM
    vec = lambda a: a.reshape(1, -1).astype(F32)
    small = lambda n: pl.BlockSpec((1, n), lambda b, hh: (0, 0))
    return pl.pallas_call(
        functools.partial(_attn_kernel, lam_init=lam_init, tq=tq, tk=tk, seq=seq),
        grid=(bsz, h),
        in_specs=[pl.BlockSpec((None, seq, dv), lambda b, hh: (b, 0, hh)),
                  pl.BlockSpec((None, seq, dv), lambda b, hh: (b, 0, h + hh)),
                  pl.BlockSpec((None, seq, dv), lambda b, hh: (b, 0, 2 * h + hh)),
                  small(DA_HEAD_DIM), small(DA_HEAD_DIM), small(DA_HEAD_DIM), small(DA_HEAD_DIM),
                  small(dv)],
        out_specs=pl.BlockSpec((None, seq, dv), lambda b, hh: (b, 0, hh)),
        out_shape=jax.ShapeDtypeStruct((bsz, seq, h * dv), BF16),
        scratch_shapes=[pltpu.VMEM((2, seq // tk, tq, tk), F32), pltpu.VMEM((2, tq, LANES), F32),
                        pltpu.VMEM((2, tq, LANES), F32), pltpu.VMEM((2, tq, dv), F32),
                        pltpu.VMEM((tq // tk + 1, tq, tk), F32)],
        compiler_params=_cparams(("arbitrary", "arbitrary")),
        name="diff_attention",
    )(qkv, qkv, qkv, vec(lq1), vec(lk1), vec(lq2), vec(lk2), vec(head_g))


def _conv_silu_kernel(x_ref, w_ref, b_ref, o_ref, *, k_tile0):
    j = pl.program_id(1)
    scale = jnp.where(j >= k_tile0, ML_QK_DIM ** -0.5, 1.0).astype(F32)
    taps = [w_ref[ML_CONV - 1 - s:ML_CONV - s, :] for s in range(ML_CONV)]

    def conv(x, shifted):
        y = x * taps[0] + b_ref[...]
        for s in range(1, ML_CONV):
            y = y + shifted(x, s) * taps[s]
        return (y * jax.nn.sigmoid(y) * scale).astype(o_ref.dtype)

    o_ref[...] = conv(x_ref[...].astype(F32), lambda x, s: pltpu.roll(x, s, 0))
    head = ROW_TILE
    row = lax.broadcasted_iota(I32, (head, x_ref.shape[1]), 0)
    o_ref[0:head, :] = conv(x_ref[0:head, :].astype(F32),
                            lambda x, s: jnp.where(row >= s, pltpu.roll(x, s, 0), 0.0))


def conv_silu(proj, conv_w, conv_b, tc=256):
    bsz, seq, _ = proj.shape
    cols = conv_w.shape[1]
    return pl.pallas_call(
        functools.partial(_conv_silu_kernel, k_tile0=(cols // 2) // tc),
        grid=(bsz, cols // tc),
        in_specs=[pl.BlockSpec((None, seq, tc), lambda b, j: (b, 0, j)),
                  pl.BlockSpec((ML_CONV, tc), lambda b, j: (0, j)),
                  pl.BlockSpec((1, tc), lambda b, j: (0, j))],
        out_specs=pl.BlockSpec((None, seq, tc), lambda b, j: (b, 0, j)),
        out_shape=jax.ShapeDtypeStruct((bsz, seq, cols), BF16),
        compiler_params=_cparams(("arbitrary", "arbitrary")),
        name="conv_silu",
    )(proj, conv_w, conv_b.reshape(1, cols))


def _mlstm_kernel(q_ref, k_ref, v_ref, op_ref, gc_ref, gr_ref, hg_ref, o_ref,
                  cx_scr, m_scr, *, chunk):
    c = pl.program_id(1)
    nh, dqk, dv = ML_HEADS, ML_QK_DIM, ML_V_DIM

    @pl.when(c == 0)
    def _():
        cx_scr[...] = jnp.zeros_like(cx_scr)
        m_scr[...] = jnp.zeros_like(m_scr)

    gc = gc_ref[...]
    gr = gr_ref[...]
    lf_c = jax.nn.log_sigmoid(gc)
    lf_r = jax.nn.log_sigmoid(gr)
    r_i = lax.broadcasted_iota(I32, (chunk, chunk), 0)
    c_i = lax.broadcasted_iota(I32, (chunk, chunk), 1)
    causal = c_i <= r_i
    tril = causal.astype(F32)
    triu = (r_i <= c_i).astype(F32)
    b_c = _dot3(tril, lf_c)
    b_r = _dot3(lf_r, triu)
    nt = (((1,), (1,)), ((), ()))
    tn_ = (((0,), (0,)), ((), ()))
    ones_l = jnp.ones((chunk, LANES), BF16)
    ones_v = jnp.ones((dv, LANES), BF16)
    wide = lambda a: jnp.concatenate([a] * (dv // LANES), axis=1)

    for h in range(nh):
        q = q_ref[:, h * dqk:(h + 1) * dqk]
        k = k_ref[:, h * dqk:(h + 1) * dqk]
        v_ext = jnp.concatenate([v_ref[:, h * dv:(h + 1) * dv], ones_l], axis=1)
        bc = b_c[:, nh + h:nh + h + 1]
        br = b_r[nh + h:nh + h + 1, :]
        ig_c = gc[:, h:h + 1]
        ig_r = gr[h:h + 1, :]
        m_prev = m_scr[h:h + 1, :]
        dmat = jnp.where(causal, bc - br + ig_r, NEG_BIG)
        inter = bc + m_prev
        m_t = jnp.maximum(inter, jnp.max(dmat, axis=-1, keepdims=True))
        w = jnp.exp(dmat - m_t)
        s = lax.dot_general(q, k, nt, preferred_element_type=F32) * w
        decay = jnp.exp(inter - m_t)
        cx = cx_scr[h]
        tot = (jnp.dot(s.astype(BF16), v_ext, preferred_element_type=F32)
               + decay * jnp.dot(q, cx.astype(BF16), preferred_element_type=F32))
        den = jnp.maximum(jnp.abs(tot[:, dv:]), jnp.exp(-m_t))
        hh = tot[:, :dv] / wide(den)
        b_last = bc[chunk - 1:chunk, :]
        g = b_last - bc + ig_c
        m_new = jnp.maximum(b_last + m_prev, jnp.max(g, axis=0, keepdims=True))
        carry_decay = jnp.exp(b_last + m_prev - m_new)
        wg = jnp.exp(g - m_new)
        wv = (wg * v_ext.astype(F32)).astype(BF16)
        cx_scr[h] = carry_decay * cx + lax.dot_general(k, wv, tn_, preferred_element_type=F32)
        m_scr[h:h + 1, :] = m_new
        ms = jnp.dot((hh * hh).astype(BF16), ones_v, preferred_element_type=F32) * (1.0 / dv)
        hn = hh * wide(lax.rsqrt(ms + NORM_EPS)) * hg_ref[:, h * dv:(h + 1) * dv]
        og = jax.nn.sigmoid(op_ref[:, h * dv:(h + 1) * dv].astype(F32))
        o_ref[:, h * dv:(h + 1) * dv] = (og * hn).astype(o_ref.dtype)


def mlstm(qk, proj, gates_c, gates_r, head_g, chunk=256):
    bsz, seq, _ = qk.shape
    nh = ML_HEADS
    qw, vw = nh * ML_QK_DIM, nh * ML_V_DIM
    v_blk = (2 * qw) // vw
    return pl.pallas_call(
        functools.partial(_mlstm_kernel, chunk=chunk),
        grid=(bsz, seq // chunk),
        in_specs=[pl.BlockSpec((None, chunk, qw), lambda b, c: (b, c, 0)),
                  pl.BlockSpec((None, chunk, qw), lambda b, c: (b, c, 1)),
                  pl.BlockSpec((None, chunk, vw), lambda b, c: (b, c, v_blk)),
                  pl.BlockSpec((None, chunk, vw), lambda b, c: (b, c, v_blk + 1)),
                  pl.BlockSpec((None, chunk, LANES), lambda b, c: (b, c, 0)),
                  pl.BlockSpec((2 * nh, chunk), lambda b, c: (0, b * (seq // chunk) + c)),
                  pl.BlockSpec((1, vw), lambda b, c: (0, 0))],
        out_specs=pl.BlockSpec((None, chunk, vw), lambda b, c: (b, c, 0)),
        out_shape=jax.ShapeDtypeStruct((bsz, seq, vw), BF16),
        scratch_shapes=[pltpu.VMEM((nh, ML_QK_DIM, ML_V_DIM + LANES), F32),
                        pltpu.VMEM((nh, 1), F32)],
        compiler_params=_cparams(("arbitrary", "arbitrary")),
        name="mlstm",
    )(qk, qk, proj, proj, gates_c, gates_r, head_g.reshape(1, vw))


def _dest_kernel(rt_ref, ps_ref, o_ref):
    rt = rt_ref[...]
    lane = lax.broadcasted_iota(I32, rt.shape, 1).astype(F32)
    ng = float(MOE_GROUPS)
    ps = ps_ref[...]
    d1 = jnp.sum(jnp.where(lane == rt[:, 0:1] + ng, ps, 0.0), axis=-1, keepdims=True) + rt[:, 4:5]
    d2 = jnp.sum(jnp.where(lane == rt[:, 1:2] + ng, ps, 0.0), axis=-1, keepdims=True) + rt[:, 5:6]
    o_ref[...] = jnp.where(lane == 0.0, d1, jnp.where(lane == 1.0, d2, 0.0)).astype(I32)


def moe_dest(route, pad_start_lanes, tm=2048):
    n = route.shape[0]
    return pl.pallas_call(
        _dest_kernel,
        grid=(n // tm,),
        in_specs=[pl.BlockSpec((tm, LANES), lambda i: (i, 0)),
                  pl.BlockSpec((1, LANES), lambda i: (0, 0))],
        out_specs=pl.BlockSpec((tm, LANES), lambda i: (i, 0)),
        out_shape=jax.ShapeDtypeStruct((n, LANES), I32),
        compiler_params=_cparams(("arbitrary",)),
        name="moe_dest",
    )(route, pad_start_lanes)


DMA_UNROLL = 8


def _wait_rows(src_rows, dst_rows, sem, copies):
    for _ in range(copies):
        pltpu.make_async_copy(src_rows, dst_rows, sem).wait()


def _dispatch_kernel(zb_ref, dest_ref, hp_ref, xin_ref, zero_scr, ring, sems, *, tt, tm, nb, ntiles):
    i = pl.program_id(0)
    slot = i % 2
    blk_rows = tm * ROW_TILE
    tile_rows = tt * ROW_TILE

    @pl.when(i == 0)
    def _():
        zero_scr[...] = jnp.zeros_like(zero_scr)

        def zero_copy(blk):
            return pltpu.make_async_copy(zero_scr, xin_ref.at[pl.ds(pl.multiple_of(blk * blk_rows, blk_rows), blk_rows)],
                                         sems.at[2])

        def start(blk, c):
            @pl.when(zb_ref[blk] == 1)
            def _():
                zero_copy(blk).start()
            return c

        def wait(blk, c):
            @pl.when(zb_ref[blk] == 1)
            def _():
                zero_copy(blk).wait()
            return c

        lax.fori_loop(0, nb, start, 0)
        lax.fori_loop(0, nb, wait, 0)

    ring[slot] = hp_ref[...]

    def start_rows(r, c):
        src = ring.at[slot, pl.ds(pl.multiple_of(r * ROW_TILE, ROW_TILE), ROW_TILE)]
        for kk in range(2):
            row = pl.multiple_of(dest_ref[0, 0, 2 * r + kk] * ROW_TILE, ROW_TILE)
            pltpu.make_async_copy(src, xin_ref.at[pl.ds(row, ROW_TILE)], sems.at[slot]).start(priority=kk)
        return c

    lax.fori_loop(0, tt, start_rows, 0, unroll=DMA_UNROLL)
    whole = xin_ref.at[pl.ds(0, tile_rows)]

    @pl.when(i > 0)
    def _():
        _wait_rows(ring.at[1 - slot], whole, sems.at[1 - slot], 2)

    @pl.when(i == ntiles - 1)
    def _():
        _wait_rows(ring.at[slot], whole, sems.at[slot], 2)


def moe_dispatch(hpack, dest, zero_blocks, rows, tm, tt=512):
    n = hpack.shape[0] // ROW_TILE
    grid_spec = pltpu.PrefetchScalarGridSpec(
        num_scalar_prefetch=1,
        grid=(n // tt,),
        in_specs=[pl.BlockSpec((1, 1, 2 * tt), lambda i, zb: (i, 0, 0), memory_space=pltpu.SMEM),
                  pl.BlockSpec((tt * ROW_TILE, LANES), lambda i, zb: (i, 0))],
        out_specs=pl.BlockSpec(memory_space=pl.ANY),
        scratch_shapes=[pltpu.VMEM((tm * ROW_TILE, LANES), U32), pltpu.VMEM((2, tt * ROW_TILE, LANES), U32),
                        pltpu.SemaphoreType.DMA((3,))],
    )
    return pl.pallas_call(
        functools.partial(_dispatch_kernel, tt=tt, tm=tm, nb=rows // tm, ntiles=n // tt),
        grid_spec=grid_spec,
        out_shape=jax.ShapeDtypeStruct((rows * ROW_TILE, LANES), U32),
        compiler_params=_cparams(("arbitrary",)),
        name="moe_dispatch",
    )(zero_blocks, dest.reshape(n // tt, 1, 2 * tt), hpack)


def _expert_kernel(be_ref, first_ref, nxt_ref, nv_ref, nu_ref, x_ref, wgu_hbm, wd_hbm, y_ref,
                   wgu_bf, wd_bf, stg_gu, stg_d, xs_scr, acc_scr, sems, *, layer, th, cr):
    i = pl.program_id(0)

    def fetch(e):
        return (pltpu.make_async_copy(wgu_hbm.at[layer, e], stg_gu, sems.at[0]),
                pltpu.make_async_copy(wd_hbm.at[layer, e], stg_d, sems.at[1]))

    @pl.when(i < nu_ref[0])
    def _():
        @pl.when(first_ref[i] == 1)
        def _():
            @pl.when(i == 0)
            def _():
                for cp in fetch(be_ref[i]):
                    cp.start()

            for cp in fetch(be_ref[i]):
                cp.wait()

            def cast_rows(src, dst):
                def body(r, c):
                    rs = pl.multiple_of(r * cr, cr)
                    dst[pl.ds(rs, cr), :] = src[pl.ds(rs, cr), :].astype(BF16)
                    return c
                lax.fori_loop(0, src.shape[0] // cr, body, 0)

            cast_rows(stg_gu, wgu_bf)
            cast_rows(stg_d, wd_bf)

            @pl.when(nxt_ref[i] >= 0)
            def _():
                for cp in fetch(nxt_ref[i]):
                    cp.start()

        tm = xs_scr.shape[0]
        half = xs_scr.shape[1] // 2
        hid = wd_bf.shape[0]

        def mlp(rows):
            for sub in range(ROW_TILE):
                lo, hi = _unpack_bf16_pair(_load_token_rows(x_ref, sub, rows))
                xs_scr[0:rows, sub * LANES:(sub + 1) * LANES] = lo.astype(BF16)
                xs_scr[0:rows, half + sub * LANES:half + (sub + 1) * LANES] = hi.astype(BF16)
            xs = xs_scr[0:rows, :]
            for c in range(hid // th):
                gt = jnp.dot(xs, wgu_bf[:, c * th:(c + 1) * th], preferred_element_type=F32)
                up = jnp.dot(xs, wgu_bf[:, hid + c * th:hid + (c + 1) * th], preferred_element_type=F32)
                act = (gt * jax.nn.sigmoid(gt) * up).astype(BF16)
                part = jnp.dot(act, wd_bf[c * th:(c + 1) * th, :], preferred_element_type=F32)
                if c == 0:
                    acc_scr[0:rows, :] = part
                else:
                    acc_scr[0:rows, :] += part
            y = acc_scr[0:rows, :]
            _store_token_rows(y_ref, _pack_bf16_pair(y[:, :half], y[:, half:]))
            if rows < tm:
                y_ref[rows * ROW_TILE:, :] = jnp.zeros(((tm - rows) * ROW_TILE, LANES), U32)

        @pl.when(nv_ref[i] > tm // 2)
        def _():
            mlp(tm)

        @pl.when(nv_ref[i] <= tm // 2)
        def _():
            mlp(tm // 2)


def moe_experts(xin, sched, wgu_all, wd_all, layer, tm, th=256, cr=256):
    _, _, d, hid2 = wgu_all.shape
    hid = hid2 // 2
    nb = xin.shape[0] // (tm * ROW_TILE)
    block_e, first, nxt, valid, n_used = sched

    def blk(i, be, fi, nx, nv, nu):
        return (jnp.minimum(i, nu[0] - 1), 0)

    grid_spec = pltpu.PrefetchScalarGridSpec(
        num_scalar_prefetch=5,
        grid=(nb,),
        in_specs=[pl.BlockSpec((tm * ROW_TILE, LANES), blk),
                  pl.BlockSpec(memory_space=pl.ANY),
                  pl.BlockSpec(memory_space=pl.ANY)],
        out_specs=pl.BlockSpec((tm * ROW_TILE, LANES), blk),
        scratch_shapes=[pltpu.VMEM((d, hid2), BF16), pltpu.VMEM((hid, d), BF16),
                        pltpu.VMEM((d, hid2), F32), pltpu.VMEM((hid, d), F32),
                        pltpu.VMEM((tm, d), BF16), pltpu.VMEM((tm, d), F32),
                        pltpu.SemaphoreType.DMA((2,))],
    )
    return pl.pallas_call(
        functools.partial(_expert_kernel, layer=layer, th=th, cr=cr),
        grid_spec=grid_spec,
        out_shape=jax.ShapeDtypeStruct(xin.shape, U32),
        input_output_aliases={5: 0},
        compiler_params=_cparams(("arbitrary",)),
        name="moe_experts",
    )(block_e, first, nxt, valid, n_used, xin, wgu_all, wd_all)


def _combine_kernel(dcur_ref, dnxt_ref, x_ref, rt_ref, mod_ref, fg_ref, y_ref, o_ref, ya, yb, sems,
                    *, tt, ntiles, final_norm):
    i = pl.program_id(0)
    slot = i % 2

    def issue(dref, sl):
        def body(r, c):
            dst = pl.ds(pl.multiple_of(r * ROW_TILE, ROW_TILE), ROW_TILE)
            for kk, buf in enumerate((ya, yb)):
                row = pl.multiple_of(dref[0, 0, 2 * r + kk] * ROW_TILE, ROW_TILE)
                pltpu.make_async_copy(y_ref.at[pl.ds(row, ROW_TILE)], buf.at[sl, dst], sems.at[sl]).start(priority=kk)
            return c
        lax.fori_loop(0, tt, body, 0, unroll=DMA_UNROLL)

    @pl.when(i == 0)
    def _():
        issue(dcur_ref, 0)

    @pl.when(i + 1 < ntiles)
    def _():
        issue(dnxt_ref, 1 - slot)

    _wait_rows(y_ref.at[pl.ds(0, tt * ROW_TILE)], ya.at[slot], sems.at[slot], 2)
    half = x_ref.shape[1] // 2
    w1 = rt_ref[:, 2:3]
    w2 = rt_ref[:, 3:4]
    ssq = jnp.zeros((tt, 1), F32)
    for sub in range(ROW_TILE):
        a_lo, a_hi = _unpack_bf16_pair(_load_token_rows(ya.at[slot], sub, tt))
        b_lo, b_hi = _unpack_bf16_pair(_load_token_rows(yb.at[slot], sub, tt))
        for base, a, b in ((sub * LANES, a_lo, b_lo), (half + sub * LANES, a_hi, b_hi)):
            cols = slice(base, base + LANES)
            out = x_ref[:, cols] + mod_ref[5:6, cols] * (a * w1 + b * w2)
            o_ref[:, cols] = out
            if final_norm:
                ssq = ssq + jnp.sum(out * out, axis=-1, keepdims=True)
    if final_norm:
        r = lax.rsqrt(ssq / (2 * half) + NORM_EPS)
        o_ref[...] = o_ref[...] * r * fg_ref[...]


def moe_combine(x, y, dest, route, mod, final_g, seq, final_norm, tt=256):
    n, d = x.shape
    tiles_per_seq = seq // tt
    ntiles = n // tt
    dest3 = dest.reshape(ntiles, 1, 2 * tt)
    return pl.pallas_call(
        functools.partial(_combine_kernel, tt=tt, ntiles=ntiles, final_norm=final_norm),
        grid=(ntiles,),
        in_specs=[pl.BlockSpec((1, 1, 2 * tt), lambda i: (i, 0, 0), memory_space=pltpu.SMEM),
                  pl.BlockSpec((1, 1, 2 * tt), lambda i: (jnp.minimum(i + 1, ntiles - 1), 0, 0),
                               memory_space=pltpu.SMEM),
                  pl.BlockSpec((tt, d), lambda i: (i, 0)),
                  pl.BlockSpec((tt, LANES), lambda i: (i, 0)),
                  pl.BlockSpec((None, 6, d), lambda i: (i // tiles_per_seq, 0, 0)),
                  pl.BlockSpec((1, d), lambda i: (0, 0)),
                  pl.BlockSpec(memory_space=pl.ANY)],
        out_specs=pl.BlockSpec((tt, d), lambda i: (i, 0)),
        out_shape=jax.ShapeDtypeStruct((n, d), F32),
        scratch_shapes=[pltpu.VMEM((2, tt * ROW_TILE, LANES), U32), pltpu.VMEM((2, tt * ROW_TILE, LANES), U32),
                        pltpu.SemaphoreType.DMA((2,))],
        compiler_params=_cparams(("arbitrary",)),
        name="moe_combine",
    )(dest3, dest3, x, route, mod, final_g.reshape(1, d), y)


def _expert_schedule(counts, tm, nb):
    ne = counts.shape[0]
    ids = jnp.arange(ne, dtype=I32)
    padded = ((counts + tm - 1) // tm) * tm
    pad_end = jnp.sum(jnp.where(ids[None, :] <= ids[:, None], padded[None, :], 0), axis=1)
    pad_start = pad_end - padded
    total = jnp.sum(padded)
    n_used = total // tm
    blk0 = jnp.arange(nb, dtype=I32) * tm
    block_e = jnp.minimum(jnp.sum((pad_end[None, :] <= blk0[:, None]).astype(I32), axis=1), ne - 1)
    onehot = block_e[:, None] == ids[None, :]
    pick = lambda v: jnp.sum(jnp.where(onehot, v[None, :], 0), axis=1)
    first = (blk0 == pick(pad_start)).astype(I32)
    later = jnp.min(jnp.where((ids[None, :] > ids[:, None]) & (counts[None, :] > 0), ids[None, :], ne), axis=1)
    nxt = pick(jnp.where(later < ne, later, -1))
    has_padding = (blk0 + tm == pick(pad_end)) & (pick(counts % tm) != 0)
    zero_blocks = ((blk0 >= total) | has_padding).astype(I32)
    valid = jnp.clip(pick(pad_start + counts) - blk0, 0, tm)
    return pad_start, zero_blocks, (block_e, first, nxt.astype(I32), valid.astype(I32),
                                    n_used.astype(I32).reshape(1))


def moe_router_weights(w_group, b_group, w_expert, b_expert):
    d = w_group.shape[0]
    ng, ne = w_group.shape[1], w_expert.shape[1]
    w_r = jnp.concatenate([w_group, w_expert, jnp.zeros((d, LANES - ng - ne), F32)], axis=1)
    b_r = jnp.concatenate([b_group, b_expert, jnp.zeros((LANES - ng - ne,), F32)]).reshape(1, LANES)
    return w_r, b_r


def hier_moe_layer(x, g, mod, w_r, b_r, wgu_all, wd_all, layer, final_g, seq, final_norm, tm=256):
    n, d = x.shape
    ng, ne = MOE_GROUPS, MOE_EXPERTS
    hpack, route, cnt = moe_router(x, g, mod, w_r, b_r, seq)
    counts = cnt[0, ng:ng + ne].astype(I32)
    rows = 2 * n + ne * tm
    pad_start, zero_blocks, sched = _expert_schedule(counts, tm, rows // tm)
    ps_lanes = jnp.concatenate([jnp.zeros((ng,), F32), pad_start.astype(F32),
                                jnp.zeros((LANES - ng - ne,), F32)]).reshape(1, LANES)
    dest = moe_dest(route, ps_lanes, tm=min(2048, n))[:, :2].reshape(-1)
    xin = moe_dispatch(hpack, dest, zero_blocks, rows, tm)
    y = moe_experts(xin, sched, wgu_all, wd_all, layer, tm)
    return moe_combine(x, y, dest, route, mod, final_g, seq, final_norm)


def _rope_tables(positions):
    half = ROPE_DIM // 2
    inv_freq = ROPE_THETA ** (-jnp.arange(half, dtype=F32) * 2.0 / ROPE_DIM)
    gap = jnp.zeros((LANES // 2 - half,), F32)
    freq = jnp.concatenate([-inv_freq, gap, inv_freq, gap])
    ang = positions.astype(F32).reshape(-1, 1) * freq[None, :]
    return jnp.cos(ang), jnp.sin(ang)


def _weight_prep_kernel(w_ref, o_ref, *, pair_tiles, tn):
    j = pl.program_id(0)
    half = ROPE_DIM // 2
    mid = LANES // 2

    @pl.when(j >= pair_tiles)
    def _():
        o_ref[...] = w_ref[...].astype(BF16)

    if pair_tiles:
        @pl.when(j < pair_tiles)
        def _():
            lane = lax.broadcasted_iota(I32, (w_ref.shape[0], LANES), 1)
            for m in range(tn // LANES):
                t = w_ref[:, m * LANES:(m + 1) * LANES]
                up = pltpu.roll(t, LANES - half, 1)
                down = pltpu.roll(t, mid - half, 1)
                new = jnp.where(lane < half, t, jnp.where(lane < mid, up, jnp.where(lane < mid + half, down, t)))
                o_ref[:, m * LANES:(m + 1) * LANES] = new.astype(BF16)


def weight_prep(w, cols, pair_cols=0, tn=512, w_rows=False):
    if w_rows:
        k = w.shape[1]
        return pl.pallas_call(
            functools.partial(_weight_prep_kernel, pair_tiles=0, tn=tn),
            grid=(cols // tn,),
            in_specs=[pl.BlockSpec((tn, k), lambda j: (j, 0))],
            out_specs=pl.BlockSpec((tn, k), lambda j: (j, 0)),
            out_shape=jax.ShapeDtypeStruct((cols, k), BF16),
            compiler_params=_cparams(("arbitrary",)),
            name="weight_prep",
        )(w)
    k = w.shape[0]
    return pl.pallas_call(
        functools.partial(_weight_prep_kernel, pair_tiles=pair_cols // tn, tn=tn),
        grid=(cols // tn,),
        in_specs=[pl.BlockSpec((k, tn), lambda j: (0, j))],
        out_specs=pl.BlockSpec((k, tn), lambda j: (0, j)),
        out_shape=jax.ShapeDtypeStruct((k, cols), BF16),
        compiler_params=_cparams(("arbitrary",)),
        name="weight_prep",
    )(w)


def kernel(x, c, positions, ada_w, ada_b, norm_mix_g, norm_ffn_g, final_norm_g, attn_w_in, attn_w_out, attn_lambda_q1, attn_lambda_k1, attn_lambda_q2, attn_lambda_k2, attn_head_norm_g, mlstm_w_in, mlstm_conv_w, mlstm_conv_b, mlstm_gate_b, mlstm_head_norm_g, mlstm_w_out, moe_w_group, moe_b_group, moe_w_expert, moe_b_expert, moe_w_gu, moe_w_down):
    bsz, seq, d = x.shape
    n = bsz * seq
    depth = ada_w.shape[0]
    mod = adaln(c, ada_w, ada_b)
    xf = x.reshape(n, d)
    for i in range(depth):
        jm = i // 2
        if i % 2 == 0:
            qk_cols = 2 * DA_HEADS * 2 * DA_HEAD_DIM
            tabs = _rope_tables(positions)
            w_in = weight_prep(attn_w_in[jm], attn_w_in.shape[2], pair_cols=qk_cols)
            qkv = norm_matmul(xf, norm_mix_g[i], mod[i], w_in, seq, rope=(*tabs, qk_cols, qk_cols // 2))
            lam_init = 0.8 - 0.6 * math.exp(-0.3 * i)
            mixed = diff_attention(qkv.reshape(bsz, seq, -1), attn_lambda_q1[jm], attn_lambda_k1[jm],
                                   attn_lambda_q2[jm], attn_lambda_k2[jm], attn_head_norm_g[jm], lam_init)
            w_out = attn_w_out[jm]
        else:
            qk_cols = 2 * ML_HEADS * ML_QK_DIM
            main_cols = qk_cols + 2 * ML_HEADS * ML_V_DIM
            w_in_t = jnp.swapaxes(mlstm_w_in[jm], 0, 1)
            ngate = 2 * ML_HEADS
            w_gate = jnp.concatenate([w_in_t[main_cols:], jnp.zeros((LANES - ngate, d), F32)], axis=0)
            b_gate = jnp.concatenate([mlstm_gate_b[jm], jnp.zeros((LANES - ngate,), F32)]).reshape(1, LANES)
            proj, gates, gates_t = norm_matmul(xf, norm_mix_g[i], mod[i],
                                               weight_prep(w_in_t, main_cols, w_rows=True), seq,
                                               extra=(w_gate, b_gate), w_rows=True)
            proj = proj.reshape(bsz, seq, main_cols)
            qk = conv_silu(proj, mlstm_conv_w[jm], mlstm_conv_b[jm])
            mixed = mlstm(qk, proj, gates.reshape(bsz, seq, LANES), gates_t, mlstm_head_norm_g[jm])
            w_out = mlstm_w_out[jm]
        w_r, b_r = moe_router_weights(moe_w_group[i], moe_b_group[i], moe_w_expert[i], moe_b_expert[i])
        xf = matmul_res(mixed.reshape(n, -1), w_out.astype(BF16), xf, mod[i], seq, gate_row=2)
        xf = hier_moe_layer(xf, norm_ffn_g[i], mod[i], w_r, b_r, moe_w_gu, moe_w_down, i, final_norm_g, seq,
                            final_norm=(i == depth - 1))
    return xf.reshape(bsz, seq, d)
```

```python
import functools
import math

import jax
import jax.numpy as jnp
from jax import lax
from jax.experimental import pallas as pl
from jax.experimental.pallas import tpu as pltpu

F32 = jnp.float32
BF16 = jnp.bfloat16
U32 = jnp.uint32
I32 = jnp.int32

NORM_EPS = 1e-6
ROPE_THETA = 500000.0
ATTN_CHUNK = 64
DA_HEADS = 8
DA_HEAD_DIM = 128
ROPE_DIM = 32
ML_HEADS = 8
ML_QK_DIM = 128
ML_V_DIM = 256
ML_CONV = 4
MOE_GROUPS = 4
MOE_PER_GROUP = 8
MOE_EXPERTS = 32
LANES = 128
NEG_BIG = -1e30
LOG2_E = math.log2(math.e)

VMEM_LIMIT = 56 * 1024 * 1024


def _cparams(sem):
    return pltpu.CompilerParams(dimension_semantics=sem, vmem_limit_bytes=VMEM_LIMIT)


def _split_hi_lo(a):
    hi = a.astype(BF16)
    lo = (a - hi.astype(F32)).astype(BF16)
    return hi, lo


def _dot3(a, w, w_rows=False):
    ah, al = _split_hi_lo(a)
    wh, wl = _split_hi_lo(w)
    contract = (((1,), (1 if w_rows else 0,)), ((), ()))
    d = functools.partial(lax.dot_general, dimension_numbers=contract, preferred_element_type=F32)
    return d(ah, wh) + (d(ah, wl) + d(al, wh))


def _pack_bf16_pair(lo_f32, hi_f32):
    lo_bits = lax.bitcast_convert_type(lo_f32.astype(BF16).astype(F32), U32)
    hi_bits = lax.bitcast_convert_type(hi_f32.astype(BF16).astype(F32), U32)
    return hi_bits | (lo_bits >> 16)


def _unpack_bf16_pair(word):
    lo = lax.bitcast_convert_type(word << 16, F32)
    hi = lax.bitcast_convert_type(word & jnp.uint32(0xFFFF0000), F32)
    return lo, hi


ROW_TILE = 8


def _store_token_rows(ref, words):
    t = words.shape[0]
    for sub in range(ROW_TILE):
        ref[pl.ds(sub, t, stride=ROW_TILE), :] = words[:, sub * LANES:(sub + 1) * LANES]


def _load_token_rows(ref, sub, t):
    return ref[pl.ds(sub, t, stride=ROW_TILE), :]


def _rms_modulate(x, g, shift, scale):
    ms = jnp.mean(x * x, axis=-1, keepdims=True)
    y = x * lax.rsqrt(ms + NORM_EPS) * g
    return y * (1.0 + scale) + shift


def _adaln_kernel(c_ref, w_ref, b_ref, o_ref):
    c = c_ref[...]
    cond = c * jax.nn.sigmoid(c)
    acc = jnp.dot(cond.astype(BF16), w_ref[...].astype(BF16), preferred_element_type=F32)
    o_ref[...] = acc + b_ref[...]


def adaln(c, ada_w, ada_b, tn=1024):
    depth, d, n6 = ada_w.shape
    bsz = c.shape[0]
    rows = 8
    cp = jnp.zeros((rows, d), F32).at[:bsz].set(c)
    out = pl.pallas_call(
        _adaln_kernel,
        grid=(depth, n6 // tn),
        in_specs=[pl.BlockSpec((rows, d), lambda l, j: (0, 0)),
                  pl.BlockSpec((None, d, tn), lambda l, j: (l, 0, j)),
                  pl.BlockSpec((None, 1, tn), lambda l, j: (l, 0, j))],
        out_specs=pl.BlockSpec((None, rows, tn), lambda l, j: (l, 0, j)),
        out_shape=jax.ShapeDtypeStruct((depth, rows, n6), F32),
        compiler_params=_cparams(("arbitrary", "arbitrary")),
        name="adaln",
    )(cp, ada_w, ada_b.reshape(depth, 1, n6))
    return out[:, :bsz].reshape(depth, bsz, 6, d)


ROPE_ROWS = 64


def _norm_matmul_kernel(*refs, rope_tiles, q_tiles, has_extra, tn, nj, nsteps, w_rows):
    if has_extra:
        x_ref, g_ref, mod_ref, w_ref, c_ref, s_ref, we_ref, be_ref, o_ref, oe_ref, oet_ref, h_scr, acc_scr = refs
    else:
        x_ref, g_ref, mod_ref, w_ref, c_ref, s_ref, o_ref, h_scr, acc_scr = refs
    t = pl.program_id(0)
    j = t % nj

    @pl.when((j == 0) & (t < nsteps))
    def _():
        h = _rms_modulate(x_ref[...], g_ref[...], mod_ref[0:1, :], mod_ref[1:2, :])
        h_scr[...] = h.astype(BF16)
        if has_extra:
            extra = _dot3(h, we_ref[...], w_rows) + be_ref[...]
            oe_ref[...] = extra
            oet_ref[...] = extra.T

    @pl.when(t == 0)
    def _():
        acc_scr[...] = jnp.zeros_like(acc_scr)

    contract = (((1,), (1 if w_rows else 0,)), ((), ()))
    acc = lax.dot_general(h_scr[...], w_ref[...], contract, preferred_element_type=F32)
    tm = acc_scr.shape[0]
    if rope_tiles:
        jp = (t + nj - 1) % nj
        rope_on = jp < rope_tiles
        scale = jnp.where(jp < q_tiles, DA_HEAD_DIM ** -0.5 * LOG2_E, 1.0).astype(F32)
    for r in range(tm // ROPE_ROWS):
        rows = slice(r * ROPE_ROWS, (r + 1) * ROPE_ROWS)
        if rope_tiles:
            cs = jnp.where(rope_on, c_ref[rows, :] * scale, 1.0)
            sn = jnp.where(rope_on, s_ref[rows, :] * scale, 0.0)
        for m in range(tn // LANES):
            cols = slice(m * LANES, (m + 1) * LANES)
            prev = acc_scr[rows, cols]
            if rope_tiles:
                prev = prev * cs + pltpu.roll(prev, LANES // 2, 1) * sn
            o_ref[rows, cols] = prev.astype(o_ref.dtype)
    acc_scr[...] = acc


def norm_matmul(x, g, mod, w, seq, rope=None, extra=None, w_rows=False, tm=1024, tn=1024):
    n, k = x.shape
    m = w.shape[0 if w_rows else 1]
    tiles_per_seq = seq // tm
    nj = m // tn
    nsteps = (n // tm) * nj
    has_extra = extra is not None
    cur = lambda t: jnp.minimum(t, nsteps - 1)
    prv = lambda t: jnp.maximum(t - 1, 0)
    if rope is None:
        dummy = jnp.zeros((8, LANES), F32)
        tabs = (dummy, dummy)
        tab_spec = pl.BlockSpec((8, LANES), lambda t: (0, 0))
        rope_tiles = q_tiles = 0
    else:
        tabs = rope[:2]
        tab_spec = pl.BlockSpec((tm, LANES), lambda t: (prv(t) // nj, 0))
        rope_tiles, q_tiles = rope[2] // tn, rope[3] // tn
    in_specs = [pl.BlockSpec((tm, k), lambda t: (cur(t) // nj, 0)),
                pl.BlockSpec((1, k), lambda t: (0, 0)),
                pl.BlockSpec((None, 6, k), lambda t: (cur(t) // nj // tiles_per_seq, 0, 0)),
                (pl.BlockSpec((tn, k), lambda t: (cur(t) % nj, 0)) if w_rows
                 else pl.BlockSpec((k, tn), lambda t: (0, cur(t) % nj))),
                tab_spec, tab_spec]
    args = [x, g.reshape(1, k), mod, w, *tabs]
    out_specs = pl.BlockSpec((tm, tn), lambda t: (prv(t) // nj, prv(t) % nj))
    out_shape = jax.ShapeDtypeStruct((n, m), BF16)
    if has_extra:
        we, be = extra
        in_specs += [pl.BlockSpec((LANES, k) if w_rows else (k, LANES), lambda t: (0, 0)),
                     pl.BlockSpec((1, LANES), lambda t: (0, 0))]
        args += [we, be]
        out_specs = [out_specs, pl.BlockSpec((tm, LANES), lambda t: (cur(t) // nj, 0)),
                     pl.BlockSpec((LANES, tm), lambda t: (0, cur(t) // nj))]
        out_shape = [out_shape, jax.ShapeDtypeStruct((n, LANES), F32), jax.ShapeDtypeStruct((LANES, n), F32)]
    return pl.pallas_call(
        functools.partial(_norm_matmul_kernel, rope_tiles=rope_tiles, q_tiles=q_tiles,
                          has_extra=has_extra, tn=tn, nj=nj, nsteps=nsteps, w_rows=w_rows),
        grid=(nsteps + 1,),
        in_specs=in_specs,
        out_specs=out_specs,
        out_shape=out_shape,
        scratch_shapes=[pltpu.VMEM((tm, k), BF16), pltpu.VMEM((tm, tn), F32)],
        compiler_params=_cparams(("arbitrary",)),
        name="norm_matmul",
    )(*args)


def _router_kernel(x_ref, g_ref, mod_ref, w_ref, b_ref, hp_ref, rt_ref, cnt_ref, run_scr):
    @pl.when(pl.program_id(0) == 0)
    def _():
        run_scr[...] = jnp.zeros_like(run_scr)

    h = _rms_modulate(x_ref[...], g_ref[...], mod_ref[3:4, :], mod_ref[4:5, :])
    half = h.shape[1] // 2
    _store_token_rows(hp_ref, _pack_bf16_pair(h[:, :half], h[:, half:]))
    tm = h.shape[0]
    hh, hl = _split_hi_lo(h)
    wh, wl = _split_hi_lo(w_ref[...])
    quad = jnp.dot(jnp.concatenate([hh, hl], axis=0), jnp.concatenate([wh, wl], axis=1),
                   preferred_element_type=F32)
    logits = (quad[:tm, :LANES] + (quad[:tm, LANES:] + quad[tm:, :LANES]) + quad[tm:, LANES:]
              + b_ref[...])
    lane = lax.broadcasted_iota(I32, logits.shape, 1).astype(F32)
    ng = float(MOE_GROUPS)
    is_g = lane < ng
    gl = jnp.where(is_g, logits, NEG_BIG)
    gmax = jnp.max(gl, axis=-1, keepdims=True)
    grp = jnp.min(jnp.where(gl == gmax, lane, float(LANES)), axis=-1, keepdims=True)
    p_group = 1.0 / jnp.sum(jnp.where(is_g, jnp.exp(gl - gmax), 0.0), axis=-1, keepdims=True)
    lo = ng + float(MOE_PER_GROUP) * grp
    el = jnp.where((lane >= lo) & (lane < lo + float(MOE_PER_GROUP)), logits, NEG_BIG)
    v1 = jnp.max(el, axis=-1, keepdims=True)
    i1 = jnp.min(jnp.where(el == v1, lane, float(LANES)), axis=-1, keepdims=True)
    el2 = jnp.where(lane == i1, NEG_BIG, el)
    v2 = jnp.max(el2, axis=-1, keepdims=True)
    i2 = jnp.min(jnp.where(el2 == v2, lane, float(LANES)), axis=-1, keepdims=True)
    ex = jnp.exp(v2 - v1)
    w1 = p_group / (1.0 + ex)
    w2 = p_group * (ex / (1.0 + ex))
    oh1 = (lane == i1).astype(F32)
    oh2 = (lane == i2).astype(F32)
    oh = oh1 + oh2
    r_i = lax.broadcasted_iota(I32, (tm, tm), 0)
    c_i = lax.broadcasted_iota(I32, (tm, tm), 1)
    before = (c_i < r_i).astype(BF16)
    prior = jnp.dot(before, oh.astype(BF16), preferred_element_type=F32) + run_scr[...]
    rank1 = jnp.sum(oh1 * prior, axis=-1, keepdims=True)
    rank2 = jnp.sum(oh2 * prior, axis=-1, keepdims=True)
    run_scr[...] += jnp.sum(oh, axis=0, keepdims=True)
    cnt_ref[...] = run_scr[...]
    rt_ref[...] = jnp.where(lane == 0.0, i1 - ng,
                  jnp.where(lane == 1.0, i2 - ng,
                  jnp.where(lane == 2.0, w1,
                  jnp.where(lane == 3.0, w2,
                  jnp.where(lane == 4.0, rank1,
                  jnp.where(lane == 5.0, rank2, 0.0))))))


def moe_router(x, g, mod, w_r, b_r, seq, tm=512):
    n, d = x.shape
    tiles_per_seq = seq // tm
    return pl.pallas_call(
        _router_kernel,
        grid=(n // tm,),
        in_specs=[pl.BlockSpec((tm, d), lambda i: (i, 0)),
                  pl.BlockSpec((1, d), lambda i: (0, 0)),
                  pl.BlockSpec((None, 6, d), lambda i: (i // tiles_per_seq, 0, 0)),
                  pl.BlockSpec((d, LANES), lambda i: (0, 0)),
                  pl.BlockSpec((1, LANES), lambda i: (0, 0))],
        out_specs=[pl.BlockSpec((tm * ROW_TILE, LANES), lambda i: (i, 0)),
                   pl.BlockSpec((tm, LANES), lambda i: (i, 0)),
                   pl.BlockSpec((1, LANES), lambda i: (0, 0))],
        out_shape=[jax.ShapeDtypeStruct((n * ROW_TILE, LANES), U32),
                   jax.ShapeDtypeStruct((n, LANES), F32),
                   jax.ShapeDtypeStruct((1, LANES), F32)],
        scratch_shapes=[pltpu.VMEM((1, LANES), F32)],
        compiler_params=_cparams(("arbitrary",)),
        name="moe_router",
    )(x, g.reshape(1, d), mod, w_r, b_r)


def _matmul_res_kernel(a_ref, w_ref, res_ref, mod_ref, o_ref, *, gate_row):
    acc = jnp.dot(a_ref[...], w_ref[...], preferred_element_type=F32)
    o_ref[...] = res_ref[...] + mod_ref[gate_row:gate_row + 1, :] * acc


def matmul_res(a, w, res, mod, seq, gate_row, tm=512, tn=2048):
    n, k = a.shape
    m = w.shape[1]
    tiles_per_seq = seq // tm
    return pl.pallas_call(
        functools.partial(_matmul_res_kernel, gate_row=gate_row),
        grid=(n // tm, m // tn),
        in_specs=[pl.BlockSpec((tm, k), lambda i, j: (i, 0)),
                  pl.BlockSpec((k, tn), lambda i, j: (0, j)),
                  pl.BlockSpec((tm, tn), lambda i, j: (i, j)),
                  pl.BlockSpec((None, 6, tn), lambda i, j: (i // tiles_per_seq, 0, j))],
        out_specs=pl.BlockSpec((tm, tn), lambda i, j: (i, j)),
        out_shape=jax.ShapeDtypeStruct((n, m), F32),
        compiler_params=_cparams(("arbitrary", "arbitrary")),
        name="matmul_res",
    )(a, w, res, mod)


ATTN_GROUPS = (8, 4, 2)


def _grouped_loop(n, body, groups):
    done = 0
    for group in groups:
        trips = (n - done) // group

        def grouped(gi, c, group=group, done=done):
            for t in range(group):
                c = body(done + gi * group + t, c)
            return c

        lax.fori_loop(0, trips, grouped, 0)
        done = done + trips * group
    lax.fori_loop(done, n, body, 0)


def _attn_kernel(q_ref, k_ref, v_ref, lq1_ref, lk1_ref, lq2_ref, lk2_ref, g_ref, o_ref,
                 s_scr, m_scr, l_scr, acc_scr, bias_scr, *, lam_init, tq, tk, seq):
    d = DA_HEAD_DIM
    nlane = tk // LANES
    ndiag = tq // tk
    lam = (jnp.exp(jnp.sum(lq1_ref[...] * lk1_ref[...], axis=-1, keepdims=True))
           - jnp.exp(jnp.sum(lq2_ref[...] * lk2_ref[...], axis=-1, keepdims=True)) + lam_init)
    shift = ATTN_CHUNK.bit_length() - 1
    row_chunk = jnp.right_shift(lax.broadcasted_iota(I32, (tq, tk), 0), shift)
    col_chunk = jnp.right_shift(lax.broadcasted_iota(I32, (tq, tk), 1), shift)
    bias_scr[0] = jnp.zeros((tq, tk), F32)
    for t in range(ndiag):
        bias_scr[t + 1] = jnp.where(col_chunk + t * (tk // ATTN_CHUNK) <= row_chunk, 0.0, NEG_BIG)
    nt = (((1,), (1,)), ((), ()))

    def lane_fold(a, op):
        part = a[:, 0:LANES]
        for cb in range(1, nlane):
            part = op(part, a[:, cb * LANES:(cb + 1) * LANES])
        return part

    def q_body(qi, _):
        qs = pl.multiple_of(qi * tq, tq)
        qm = (q_ref[pl.ds(qs, tq), 0:d], q_ref[pl.ds(qs, tq), d:2 * d])
        m_scr[...] = jnp.full_like(m_scr, NEG_BIG)

        nfull = qi * ndiag

        def score_body(j, c):
            ks = pl.multiple_of(j * tk, tk)
            bias = bias_scr[jnp.maximum(j - nfull + 1, 0)]
            for mp in range(2):
                k = k_ref[pl.ds(ks, tk), mp * d:(mp + 1) * d]
                s = lax.dot_general(qm[mp], k, nt, preferred_element_type=F32) + bias
                s_scr[mp, j] = s
                m_scr[mp] = jnp.maximum(m_scr[mp], lane_fold(s, jnp.maximum))
            return c

        _grouped_loop(nfull + ndiag, score_body, ATTN_GROUPS)
        m = [jnp.max(m_scr[mp], axis=-1, keepdims=True) for mp in range(2)]
        l_scr[...] = jnp.zeros_like(l_scr)
        acc_scr[...] = jnp.zeros_like(acc_scr)

        def pv_body(j, c):
            ks = pl.multiple_of(j * tk, tk)
            v = v_ref[pl.ds(ks, tk), :]
            for mp in range(2):
                p = jnp.exp2(s_scr[mp, j] - m[mp])
                l_scr[mp] += lane_fold(p, jnp.add)
                acc_scr[mp] += jnp.dot(p.astype(BF16), v, preferred_element_type=F32)
            return c

        _grouped_loop(nfull + ndiag, pv_body, ATTN_GROUPS)
        l = [jnp.sum(l_scr[mp], axis=-1, keepdims=True) for mp in range(2)]
        o = acc_scr[0] / l[0] - lam * (acc_scr[1] / l[1])
        ms = jnp.mean(o * o, axis=-1, keepdims=True)
        o = o * lax.rsqrt(ms + NORM_EPS) * g_ref[...] * (1.0 - lam_init)
        o_ref[pl.ds(qs, tq), :] = o.astype(o_ref.dtype)
        return 0

    lax.fori_loop(0, seq // tq, q_body, 0)


def diff_attention(qkv, lq1, lk1, lq2, lk2, head_g, lam_init, tq=512, tk=256):
    bsz, seq, _ = qkv.shape
    h, dv = DA_HEADS, 2 * DA_HEAD_DIM
    vec = lambda a: a.reshape(1, -1).astype(F32)
    small = lambda n: pl.BlockSpec((1, n), lambda b, hh: (0, 0))
    return pl.pallas_call(
        functools.partial(_attn_kernel, lam_init=lam_init, tq=tq, tk=tk, seq=seq),
        grid=(bsz, h),
        in_specs=[pl.BlockSpec((None, seq, dv), lambda b, hh: (b, 0, hh)),
                  pl.BlockSpec((None, seq, dv), lambda b, hh: (b, 0, h + hh)),
                  pl.BlockSpec((None, seq, dv), lambda b, hh: (b, 0, 2 * h + hh)),
                  small(DA_HEAD_DIM), small(DA_HEAD_DIM), small(DA_HEAD_DIM), small(DA_HEAD_DIM),
                  small(dv)],
        out_specs=pl.BlockSpec((None, seq, dv), lambda b, hh: (b, 0, hh)),
        out_shape=jax.ShapeDtypeStruct((bsz, seq, h * dv), BF16),
        scratch_shapes=[pltpu.VMEM((2, seq // tk, tq, tk), F32), pltpu.VMEM((2, tq, LANES), F32),
                        pltpu.VMEM((2, tq, LANES), F32), pltpu.VMEM((2, tq, dv), F32),
                        pltpu.VMEM((tq // tk + 1, tq, tk), F32)],
        compiler_params=_cparams(("arbitrary", "arbitrary")),
        name="diff_attention",
    )(qkv, qkv, qkv, vec(lq1), vec(lk1), vec(lq2), vec(lk2), vec(head_g))


def _conv_silu_kernel(x_ref, w_ref, b_ref, o_ref, *, k_tile0):
    j = pl.program_id(1)
    scale = jnp.where(j >= k_tile0, ML_QK_DIM ** -0.5, 1.0).astype(F32)
    taps = [w_ref[ML_CONV - 1 - s:ML_CONV - s, :] for s in range(ML_CONV)]

    def conv(x, shifted):
        y = x * taps[0] + b_ref[...]
        for s in range(1, ML_CONV):
            y = y + shifted(x, s) * taps[s]
        return (y * jax.nn.sigmoid(y) * scale).astype(o_ref.dtype)

    o_ref[...] = conv(x_ref[...].astype(F32), lambda x, s: pltpu.roll(x, s, 0))
    head = ROW_TILE
    row = lax.broadcasted_iota(I32, (head, x_ref.shape[1]), 0)
    o_ref[0:head, :] = conv(x_ref[0:head, :].astype(F32),
                            lambda x, s: jnp.where(row >= s, pltpu.roll(x, s, 0), 0.0))


def conv_silu(proj, conv_w, conv_b, tc=256):
    bsz, seq, _ = proj.shape
    cols = conv_w.shape[1]
    return pl.pallas_call(
        functools.partial(_conv_silu_kernel, k_tile0=(cols // 2) // tc),
        grid=(bsz, cols // tc),
        in_specs=[pl.BlockSpec((None, seq, tc), lambda b, j: (b, 0, j)),
                  pl.BlockSpec((ML_CONV, tc), lambda b, j: (0, j)),
                  pl.BlockSpec((1, tc), lambda b, j: (0, j))],
        out_specs=pl.BlockSpec((None, seq, tc), lambda b, j: (b, 0, j)),
        out_shape=jax.ShapeDtypeStruct((bsz, seq, cols), BF16),
        compiler_params=_cparams(("arbitrary", "arbitrary")),
        name="conv_silu",
    )(proj, conv_w, conv_b.reshape(1, cols))


def _mlstm_kernel(q_ref, k_ref, v_ref, op_ref, gc_ref, gr_ref, hg_ref, o_ref,
                  cx_scr, m_scr, *, chunk):
    c = pl.program_id(1)
    nh, dqk, dv = ML_HEADS, ML_QK_DIM, ML_V_DIM

    @pl.when(c == 0)
    def _():
        cx_scr[...] = jnp.zeros_like(cx_scr)
        m_scr[...] = jnp.zeros_like(m_scr)

    gc = gc_ref[...]
    gr = gr_ref[...]
    lf_c = jax.nn.log_sigmoid(gc)
    lf_r = jax.nn.log_sigmoid(gr)
    r_i = lax.broadcasted_iota(I32, (chunk, chunk), 0)
    c_i = lax.broadcasted_iota(I32, (chunk, chunk), 1)
    causal = c_i <= r_i
    tril = causal.astype(F32)
    triu = (r_i <= c_i).astype(F32)
    b_c = _dot3(tril, lf_c)
    b_r = _dot3(lf_r, triu)
    nt = (((1,), (1,)), ((), ()))
    tn_ = (((0,), (0,)), ((), ()))
    ones_l = jnp.ones((chunk, LANES), BF16)
    ones_v = jnp.ones((dv, LANES), BF16)
    wide = lambda a: jnp.concatenate([a] * (dv // LANES), axis=1)

    for h in range(nh):
        q = q_ref[:, h * dqk:(h + 1) * dqk]
        k = k_ref[:, h * dqk:(h + 1) * dqk]
        v_ext = jnp.concatenate([v_ref[:, h * dv:(h + 1) * dv], ones_l], axis=1)
        bc = b_c[:, nh + h:nh + h + 1]
        br = b_r[nh + h:nh + h + 1, :]
        ig_c = gc[:, h:h + 1]
        ig_r = gr[h:h + 1, :]
        m_prev = m_scr[h:h + 1, :]
        dmat = jnp.where(causal, bc - br + ig_r, NEG_BIG)
        inter = bc + m_prev
        m_t = jnp.maximum(inter, jnp.max(dmat, axis=-1, keepdims=True))
        w = jnp.exp(dmat - m_t)
        s = lax.dot_general(q, k, nt, preferred_element_type=F32) * w
        decay = jnp.exp(inter - m_t)
        cx = cx_scr[h]
        tot = (jnp.dot(s.astype(BF16), v_ext, preferred_element_type=F32)
               + decay * jnp.dot(q, cx.astype(BF16), preferred_element_type=F32))
        den = jnp.maximum(jnp.abs(tot[:, dv:]), jnp.exp(-m_t))
        hh = tot[:, :dv] / wide(den)
        b_last = bc[chunk - 1:chunk, :]
        g = b_last - bc + ig_c
        m_new = jnp.maximum(b_last + m_prev, jnp.max(g, axis=0, keepdims=True))
        carry_decay = jnp.exp(b_last + m_prev - m_new)
        wg = jnp.exp(g - m_new)
        wv = (wg * v_ext.astype(F32)).astype(BF16)
        cx_scr[h] = carry_decay * cx + lax.dot_general(k, wv, tn_, preferred_element_type=F32)
        m_scr[h:h + 1, :] = m_new
        ms = jnp.dot((hh * hh).astype(BF16), ones_v, preferred_element_type=F32) * (1.0 / dv)
        hn = hh * wide(lax.rsqrt(ms + NORM_EPS)) * hg_ref[:, h * dv:(h + 1) * dv]
        og = jax.nn.sigmoid(op_ref[:, h * dv:(h + 1) * dv].astype(F32))
        o_ref[:, h * dv:(h + 1) * dv] = (og * hn).astype(o_ref.dtype)


def mlstm(qk, proj, gates_c, gates_r, head_g, chunk=256):
    bsz, seq, _ = qk.shape
    nh = ML_HEADS
    qw, vw = nh * ML_QK_DIM, nh * ML_V_DIM
    v_blk = (2 * qw) // vw
    return pl.pallas_call(
        functools.partial(_mlstm_kernel, chunk=chunk),
        grid=(bsz, seq // chunk),
        in_specs=[pl.BlockSpec((None, chunk, qw), lambda b, c: (b, c, 0)),
                  pl.BlockSpec((None, chunk, qw), lambda b, c: (b, c, 1)),
                  pl.BlockSpec((None, chunk, vw), lambda b, c: (b, c, v_blk)),
                  pl.BlockSpec((None, chunk, vw), lambda b, c: (b, c, v_blk + 1)),
                  pl.BlockSpec((None, chunk, LANES), lambda b, c: (b, c, 0)),
                  pl.BlockSpec((2 * nh, chunk), lambda b, c: (0, b * (seq // chunk) + c)),
                  pl.BlockSpec((1, vw), lambda b, c: (0, 0))],
        out_specs=pl.BlockSpec((None, chunk, vw), lambda b, c: (b, c, 0)),
        out_shape=jax.ShapeDtypeStruct((bsz, seq, vw), BF16),
        scratch_shapes=[pltpu.VMEM((nh, ML_QK_DIM, ML_V_DIM + LANES), F32),
                        pltpu.VMEM((nh, 1), F32)],
        compiler_params=_cparams(("arbitrary", "arbitrary")),
        name="mlstm",
    )(qk, qk, proj, proj, gates_c, gates_r, head_g.reshape(1, vw))


def _dest_kernel(rt_ref, ps_ref, o_ref):
    rt = rt_ref[...]
    lane = lax.broadcasted_iota(I32, rt.shape, 1).astype(F32)
    ng = float(MOE_GROUPS)
    ps = ps_ref[...]
    d1 = jnp.sum(jnp.where(lane == rt[:, 0:1] + ng, ps, 0.0), axis=-1, keepdims=True) + rt[:, 4:5]
    d2 = jnp.sum(jnp.where(lane == rt[:, 1:2] + ng, ps, 0.0), axis=-1, keepdims=True) + rt[:, 5:6]
    o_ref[...] = jnp.where(lane == 0.0, d1, jnp.where(lane == 1.0, d2, 0.0)).astype(I32)


def moe_dest(route, pad_start_lanes, tm=2048):
    n = route.shape[0]
    return pl.pallas_call(
        _dest_kernel,
        grid=(n // tm,),
        in_specs=[pl.BlockSpec((tm, LANES), lambda i: (i, 0)),
                  pl.BlockSpec((1, LANES), lambda i: (0, 0))],
        out_specs=pl.BlockSpec((tm, LANES), lambda i: (i, 0)),
        out_shape=jax.ShapeDtypeStruct((n, LANES), I32),
        compiler_params=_cparams(("arbitrary",)),
        name="moe_dest",
    )(route, pad_start_lanes)


DMA_UNROLL = 16


def _wait_rows(src_rows, dst_rows, sem, copies):
    for _ in range(copies):
        pltpu.make_async_copy(src_rows, dst_rows, sem).wait()


def _dispatch_kernel(zb_ref, dest_ref, hp_ref, xin_ref, zero_scr, ring, sems, *, tt, tm, nb, ntiles):
    i = pl.program_id(0)
    slot = i % 2
    blk_rows = tm * ROW_TILE
    tile_rows = tt * ROW_TILE

    @pl.when(i == 0)
    def _():
        zero_scr[...] = jnp.zeros_like(zero_scr)

        def zero_copy(blk):
            return pltpu.make_async_copy(zero_scr, xin_ref.at[pl.ds(pl.multiple_of(blk * blk_rows, blk_rows), blk_rows)],
                                         sems.at[2])

        def start(blk, c):
            @pl.when(zb_ref[blk] == 1)
            def _():
                zero_copy(blk).start()
            return c

        def wait(blk, c):
            @pl.when(zb_ref[blk] == 1)
            def _():
                zero_copy(blk).wait()
            return c

        lax.fori_loop(0, nb, start, 0)
        lax.fori_loop(0, nb, wait, 0)

    ring[slot] = hp_ref[...]

    def start_rows(r, c):
        src = ring.at[slot, pl.ds(pl.multiple_of(r * ROW_TILE, ROW_TILE), ROW_TILE)]
        for kk in range(2):
            row = pl.multiple_of(dest_ref[0, 0, 2 * r + kk] * ROW_TILE, ROW_TILE)
            pltpu.make_async_copy(src, xin_ref.at[pl.ds(row, ROW_TILE)], sems.at[slot]).start(priority=kk)
        return c

    lax.fori_loop(0, tt, start_rows, 0, unroll=DMA_UNROLL)
    whole = xin_ref.at[pl.ds(0, tile_rows)]

    @pl.when(i > 0)
    def _():
        _wait_rows(ring.at[1 - slot], whole, sems.at[1 - slot], 2)

    @pl.when(i == ntiles - 1)
    def _():
        _wait_rows(ring.at[slot], whole, sems.at[slot], 2)


def moe_dispatch(hpack, dest, zero_blocks, rows, tm, tt=1024):
    n = hpack.shape[0] // ROW_TILE
    tt = min(tt, n)
    grid_spec = pltpu.PrefetchScalarGridSpec(
        num_scalar_prefetch=1,
        grid=(n // tt,),
        in_specs=[pl.BlockSpec((1, 1, 2 * tt), lambda i, zb: (i, 0, 0), memory_space=pltpu.SMEM),
                  pl.BlockSpec((tt * ROW_TILE, LANES), lambda i, zb: (i, 0))],
        out_specs=pl.BlockSpec(memory_space=pl.ANY),
        scratch_shapes=[pltpu.VMEM((tm * ROW_TILE, LANES), U32), pltpu.VMEM((2, tt * ROW_TILE, LANES), U32),
                        pltpu.SemaphoreType.DMA((3,))],
    )
    return pl.pallas_call(
        functools.partial(_dispatch_kernel, tt=tt, tm=tm, nb=rows // tm, ntiles=n // tt),
        grid_spec=grid_spec,
        out_shape=jax.ShapeDtypeStruct((rows * ROW_TILE, LANES), U32),
        compiler_params=_cparams(("arbitrary",)),
        name="moe_dispatch",
    )(zero_blocks, dest.reshape(n // tt, 1, 2 * tt), hpack)


def _expert_kernel(be_ref, first_ref, nxt_ref, nv_ref, nu_ref, x_ref, wgu_hbm, wd_hbm, y_ref,
                   wgu_bf, wd_bf, stg_gu, stg_d, xs_scr, acc_scr, sems, *, layer, th, cr):
    i = pl.program_id(0)

    def fetch(e):
        return (pltpu.make_async_copy(wgu_hbm.at[layer, e], stg_gu, sems.at[0]),
                pltpu.make_async_copy(wd_hbm.at[layer, e], stg_d, sems.at[1]))

    @pl.when(i < nu_ref[0])
    def _():
        @pl.when(first_ref[i] == 1)
        def _():
            @pl.when(i == 0)
            def _():
                for cp in fetch(be_ref[i]):
                    cp.start()

            for cp in fetch(be_ref[i]):
                cp.wait()

            def cast_rows(src, dst):
                def body(r, c):
                    rs = pl.multiple_of(r * cr, cr)
                    dst[pl.ds(rs, cr), :] = src[pl.ds(rs, cr), :].astype(BF16)
                    return c
                lax.fori_loop(0, src.shape[0] // cr, body, 0)

            cast_rows(stg_gu, wgu_bf)
            cast_rows(stg_d, wd_bf)

            @pl.when(nxt_ref[i] >= 0)
            def _():
                for cp in fetch(nxt_ref[i]):
                    cp.start()

        tm = xs_scr.shape[0]
        half = xs_scr.shape[1] // 2
        hid = wd_bf.shape[0]

        def mlp(rows):
            for sub in range(ROW_TILE):
                lo, hi = _unpack_bf16_pair(_load_token_rows(x_ref, sub, rows))
                xs_scr[0:rows, sub * LANES:(sub + 1) * LANES] = lo.astype(BF16)
                xs_scr[0:rows, half + sub * LANES:half + (sub + 1) * LANES] = hi.astype(BF16)
            xs = xs_scr[0:rows, :]
            for c in range(hid // th):
                gt = jnp.dot(xs, wgu_bf[:, c * th:(c + 1) * th], preferred_element_type=F32)
                up = jnp.dot(xs, wgu_bf[:, hid + c * th:hid + (c + 1) * th], preferred_element_type=F32)
                act = (gt * jax.nn.sigmoid(gt) * up).astype(BF16)
                part = jnp.dot(act, wd_bf[c * th:(c + 1) * th, :], preferred_element_type=F32)
                if c == 0:
                    acc_scr[0:rows, :] = part
                else:
                    acc_scr[0:rows, :] += part
            y = acc_scr[0:rows, :]
            _store_token_rows(y_ref, _pack_bf16_pair(y[:, :half], y[:, half:]))
            if rows < tm:
                y_ref[rows * ROW_TILE:, :] = jnp.zeros(((tm - rows) * ROW_TILE, LANES), U32)

        @pl.when(nv_ref[i] > tm // 2)
        def _():
            mlp(tm)

        @pl.when(nv_ref[i] <= tm // 2)
        def _():
            mlp(tm // 2)


def moe_experts(xin, sched, wgu_all, wd_all, layer, tm, th=256, cr=256):
    _, _, d, hid2 = wgu_all.shape
    hid = hid2 // 2
    nb = xin.shape[0] // (tm * ROW_TILE)
    block_e, first, nxt, valid, n_used = sched

    def blk(i, be, fi, nx, nv, nu):
        return (jnp.minimum(i, nu[0] - 1), 0)

    grid_spec = pltpu.PrefetchScalarGridSpec(
        num_scalar_prefetch=5,
        grid=(nb,),
        in_specs=[pl.BlockSpec((tm * ROW_TILE, LANES), blk),
                  pl.BlockSpec(memory_space=pl.ANY),
                  pl.BlockSpec(memory_space=pl.ANY)],
        out_specs=pl.BlockSpec((tm * ROW_TILE, LANES), blk),
        scratch_shapes=[pltpu.VMEM((d, hid2), BF16), pltpu.VMEM((hid, d), BF16),
                        pltpu.VMEM((d, hid2), F32), pltpu.VMEM((hid, d), F32),
                        pltpu.VMEM((tm, d), BF16), pltpu.VMEM((tm, d), F32),
                        pltpu.SemaphoreType.DMA((2,))],
    )
    return pl.pallas_call(
        functools.partial(_expert_kernel, layer=layer, th=th, cr=cr),
        grid_spec=grid_spec,
        out_shape=jax.ShapeDtypeStruct(xin.shape, U32),
        input_output_aliases={5: 0},
        compiler_params=_cparams(("arbitrary",)),
        name="moe_experts",
    )(block_e, first, nxt, valid, n_used, xin, wgu_all, wd_all)


def _combine_kernel(dcur_ref, dnxt_ref, x_ref, rt_ref, mod_ref, fg_ref, y_ref, o_ref, ya, yb, sems,
                    *, tt, ntiles, final_norm):
    i = pl.program_id(0)
    slot = i % 2

    def issue(dref, sl):
        def body(r, c):
            dst = pl.ds(pl.multiple_of(r * ROW_TILE, ROW_TILE), ROW_TILE)
            for kk, buf in enumerate((ya, yb)):
                row = pl.multiple_of(dref[0, 0, 2 * r + kk] * ROW_TILE, ROW_TILE)
                pltpu.make_async_copy(y_ref.at[pl.ds(row, ROW_TILE)], buf.at[sl, dst], sems.at[sl]).start(priority=kk)
            return c
        lax.fori_loop(0, tt, body, 0, unroll=DMA_UNROLL)

    @pl.when(i == 0)
    def _():
        issue(dcur_ref, 0)

    @pl.when(i + 1 < ntiles)
    def _():
        issue(dnxt_ref, 1 - slot)

    _wait_rows(y_ref.at[pl.ds(0, tt * ROW_TILE)], ya.at[slot], sems.at[slot], 2)
    half = x_ref.shape[1] // 2
    w1 = rt_ref[:, 2:3]
    w2 = rt_ref[:, 3:4]
    ssq = jnp.zeros((tt, 1), F32)
    for sub in range(ROW_TILE):
        a_lo, a_hi = _unpack_bf16_pair(_load_token_rows(ya.at[slot], sub, tt))
        b_lo, b_hi = _unpack_bf16_pair(_load_token_rows(yb.at[slot], sub, tt))
        for base, a, b in ((sub * LANES, a_lo, b_lo), (half + sub * LANES, a_hi, b_hi)):
            cols = slice(base, base + LANES)
            out = x_ref[:, cols] + mod_ref[5:6, cols] * (a * w1 + b * w2)
            o_ref[:, cols] = out
            if final_norm:
                ssq = ssq + jnp.sum(out * out, axis=-1, keepdims=True)
    if final_norm:
        r = lax.rsqrt(ssq / (2 * half) + NORM_EPS)
        o_ref[...] = o_ref[...] * r * fg_ref[...]


def moe_combine(x, y, dest, route, mod, final_g, seq, final_norm, tt=256):
    n, d = x.shape
    tiles_per_seq = seq // tt
    ntiles = n // tt
    dest3 = dest.reshape(ntiles, 1, 2 * tt)
    return pl.pallas_call(
        functools.partial(_combine_kernel, tt=tt, ntiles=ntiles, final_norm=final_norm),
        grid=(ntiles,),
        in_specs=[pl.BlockSpec((1, 1, 2 * tt), lambda i: (i, 0, 0), memory_space=pltpu.SMEM),
                  pl.BlockSpec((1, 1, 2 * tt), lambda i: (jnp.minimum(i + 1, ntiles - 1), 0, 0),
                               memory_space=pltpu.SMEM),
                  pl.BlockSpec((tt, d), lambda i: (i, 0)),
                  pl.BlockSpec((tt, LANES), lambda i: (i, 0)),
                  pl.BlockSpec((None, 6, d), lambda i: (i // tiles_per_seq, 0, 0)),
                  pl.BlockSpec((1, d), lambda i: (0, 0)),
                  pl.BlockSpec(memory_space=pl.ANY)],
        out_specs=pl.BlockSpec((tt, d), lambda i: (i, 0)),
        out_shape=jax.ShapeDtypeStruct((n, d), F32),
        scratch_shapes=[pltpu.VMEM((2, tt * ROW_TILE, LANES), U32), pltpu.VMEM((2, tt * ROW_TILE, LANES), U32),
                        pltpu.SemaphoreType.DMA((2,))],
        compiler_params=_cparams(("arbitrary",)),
        name="moe_combine",
    )(dest3, dest3, x, route, mod, final_g.reshape(1, d), y)


def _expert_schedule(counts, tm, nb):
    ne = counts.shape[0]
    ids = jnp.arange(ne, dtype=I32)
    padded = ((counts + tm - 1) // tm) * tm
    pad_end = jnp.sum(jnp.where(ids[None, :] <= ids[:, None], padded[None, :], 0), axis=1)
    pad_start = pad_end - padded
    total = jnp.sum(padded)
    n_used = total // tm
    blk0 = jnp.arange(nb, dtype=I32) * tm
    block_e = jnp.minimum(jnp.sum((pad_end[None, :] <= blk0[:, None]).astype(I32), axis=1), ne - 1)
    onehot = block_e[:, None] == ids[None, :]
    pick = lambda v: jnp.sum(jnp.where(onehot, v[None, :], 0), axis=1)
    first = (blk0 == pick(pad_start)).astype(I32)
    later = jnp.min(jnp.where((ids[None, :] > ids[:, None]) & (counts[None, :] > 0), ids[None, :], ne), axis=1)
    nxt = pick(jnp.where(later < ne, later, -1))
    has_padding = (blk0 + tm == pick(pad_end)) & (pick(counts % tm) != 0)
    zero_blocks = ((blk0 >= total) | has_padding).astype(I32)
    valid = jnp.clip(pick(pad_start + counts) - blk0, 0, tm)
    return pad_start, zero_blocks, (block_e, first, nxt.astype(I32), valid.astype(I32),
                                    n_used.astype(I32).reshape(1))


def moe_router_weights(w_group, b_group, w_expert, b_expert):
    d = w_group.shape[0]
    ng, ne = w_group.shape[1], w_expert.shape[1]
    w_r = jnp.concatenate([w_group, w_expert, jnp.zeros((d, LANES - ng - ne), F32)], axis=1)
    b_r = jnp.concatenate([b_group, b_expert, jnp.zeros((LANES - ng - ne,), F32)]).reshape(1, LANES)
    return w_r, b_r


def hier_moe_layer(x, g, mod, w_r, b_r, wgu_all, wd_all, layer, final_g, seq, final_norm, tm=256):
    n, d = x.shape
    ng, ne = MOE_GROUPS, MOE_EXPERTS
    hpack, route, cnt = moe_router(x, g, mod, w_r, b_r, seq)
    counts = cnt[0, ng:ng + ne].astype(I32)
    rows = 2 * n + ne * tm
    pad_start, zero_blocks, sched = _expert_schedule(counts, tm, rows // tm)
    ps_lanes = jnp.concatenate([jnp.zeros((ng,), F32), pad_start.astype(F32),
                                jnp.zeros((LANES - ng - ne,), F32)]).reshape(1, LANES)
    dest = moe_dest(route, ps_lanes, tm=min(2048, n))[:, :2].reshape(-1)
    xin = moe_dispatch(hpack, dest, zero_blocks, rows, tm)
    y = moe_experts(xin, sched, wgu_all, wd_all, layer, tm)
    return moe_combine(x, y, dest, route, mod, final_g, seq, final_norm)


def _rope_tables(positions):
    half = ROPE_DIM // 2
    inv_freq = ROPE_THETA ** (-jnp.arange(half, dtype=F32) * 2.0 / ROPE_DIM)
    gap = jnp.zeros((LANES // 2 - half,), F32)
    freq = jnp.concatenate([-inv_freq, gap, inv_freq, gap])
    ang = positions.astype(F32).reshape(-1, 1) * freq[None, :]
    return jnp.cos(ang), jnp.sin(ang)


def _weight_prep_kernel(w_ref, o_ref, *, pair_tiles, tn):
    j = pl.program_id(0)
    half = ROPE_DIM // 2
    mid = LANES // 2

    @pl.when(j >= pair_tiles)
    def _():
        o_ref[...] = w_ref[...].astype(BF16)

    if pair_tiles:
        @pl.when(j < pair_tiles)
        def _():
            lane = lax.broadcasted_iota(I32, (w_ref.shape[0], LANES), 1)
            for m in range(tn // LANES):
                t = w_ref[:, m * LANES:(m + 1) * LANES]
                up = pltpu.roll(t, LANES - half, 1)
                down = pltpu.roll(t, mid - half, 1)
                new = jnp.where(lane < half, t, jnp.where(lane < mid, up, jnp.where(lane < mid + half, down, t)))
                o_ref[:, m * LANES:(m + 1) * LANES] = new.astype(BF16)


def weight_prep(w, cols, pair_cols=0, tn=512, w_rows=False):
    if w_rows:
        k = w.shape[1]
        return pl.pallas_call(
            functools.partial(_weight_prep_kernel, pair_tiles=0, tn=tn),
            grid=(cols // tn,),
            in_specs=[pl.BlockSpec((tn, k), lambda j: (j, 0))],
            out_specs=pl.BlockSpec((tn, k), lambda j: (j, 0)),
            out_shape=jax.ShapeDtypeStruct((cols, k), BF16),
            compiler_params=_cparams(("arbitrary",)),
            name="weight_prep",
        )(w)
    k = w.shape[0]
    return pl.pallas_call(
        functools.partial(_weight_prep_kernel, pair_tiles=pair_cols // tn, tn=tn),
        grid=(cols // tn,),
        in_specs=[pl.BlockSpec((k, tn), lambda j: (0, j))],
        out_specs=pl.BlockSpec((k, tn), lambda j: (0, j)),
        out_shape=jax.ShapeDtypeStruct((k, cols), BF16),
        compiler_params=_cparams(("arbitrary",)),
        name="weight_prep",
    )(w)


def kernel(x, c, positions, ada_w, ada_b, norm_mix_g, norm_ffn_g, final_norm_g, attn_w_in, attn_w_out, attn_lambda_q1, attn_lambda_k1, attn_lambda_q2, attn_lambda_k2, attn_head_norm_g, mlstm_w_in, mlstm_conv_w, mlstm_conv_b, mlstm_gate_b, mlstm_head_norm_g, mlstm_w_out, moe_w_group, moe_b_group, moe_w_expert, moe_b_expert, moe_w_gu, moe_w_down):
    bsz, seq, d = x.shape
    n = bsz * seq
    depth = ada_w.shape[0]
    mod = adaln(c, ada_w, ada_b)
    xf = x.reshape(n, d)
    for i in range(depth):
        jm = i // 2
        if i % 2 == 0:
            qk_cols = 2 * DA_HEADS * 2 * DA_HEAD_DIM
            tabs = _rope_tables(positions)
            w_in = weight_prep(attn_w_in[jm], attn_w_in.shape[2], pair_cols=qk_cols)
            qkv = norm_matmul(xf, norm_mix_g[i], mod[i], w_in, seq, rope=(*tabs, qk_cols, qk_cols // 2))
            lam_init = 0.8 - 0.6 * math.exp(-0.3 * i)
            mixed = diff_attention(qkv.reshape(bsz, seq, -1), attn_lambda_q1[jm], attn_lambda_k1[jm],
                                   attn_lambda_q2[jm], attn_lambda_k2[jm], attn_head_norm_g[jm], lam_init)
            w_out = attn_w_out[jm]
        else:
            qk_cols = 2 * ML_HEADS * ML_QK_DIM
            main_cols = qk_cols + 2 * ML_HEADS * ML_V_DIM
            w_in_t = jnp.swapaxes(mlstm_w_in[jm], 0, 1)
            ngate = 2 * ML_HEADS
            w_gate = jnp.concatenate([w_in_t[main_cols:], jnp.zeros((LANES - ngate, d), F32)], axis=0)
            b_gate = jnp.concatenate([mlstm_gate_b[jm], jnp.zeros((LANES - ngate,), F32)]).reshape(1, LANES)
            proj, gates, gates_t = norm_matmul(xf, norm_mix_g[i], mod[i],
                                               weight_prep(w_in_t, main_cols, w_rows=True), seq,
                                               extra=(w_gate, b_gate), w_rows=True)
            proj = proj.reshape(bsz, seq, main_cols)
            qk = conv_silu(proj, mlstm_conv_w[jm], mlstm_conv_b[jm])
            mixed = mlstm(qk, proj, gates.reshape(bsz, seq, LANES), gates_t, mlstm_head_norm_g[jm])
            w_out = mlstm_w_out[jm]
        w_r, b_r = moe_router_weights(moe_w_group[i], moe_b_group[i], moe_w_expert[i], moe_b_expert[i])
        xf = matmul_res(mixed.reshape(n, -1), w_out.astype(BF16), xf, mod[i], seq, gate_row=2)
        xf = hier_moe_layer(xf, norm_ffn_g[i], mod[i], w_r, b_r, moe_w_gu, moe_w_down, i, final_norm_g, seq,
                            final_norm=(i == depth - 1))
    return xf.reshape(bsz, seq, d)
```

```python
import functools
import math

import jax
import jax.numpy as jnp
from jax import lax
from jax.experimental import pallas as pl
from jax.experimental.pallas import tpu as pltpu

F32 = jnp.float32
BF16 = jnp.bfloat16
U32 = jnp.uint32
I32 = jnp.int32

NORM_EPS = 1e-6
ROPE_THETA = 500000.0
ATTN_CHUNK = 64
DA_HEADS = 8
DA_HEAD_DIM = 128
ROPE_DIM = 32
ML_HEADS = 8
ML_QK_DIM = 128
ML_V_DIM = 256
ML_CONV = 4
MOE_GROUPS = 4
MOE_PER_GROUP = 8
MOE_EXPERTS = 32
LANES = 128
NEG_BIG = -1e30
LOG2_E = math.log2(math.e)

VMEM_LIMIT = 56 * 1024 * 1024


def _cparams(sem):
    return pltpu.CompilerParams(dimension_semantics=sem, vmem_limit_bytes=VMEM_LIMIT)


def _split_hi_lo(a):
    hi = a.astype(BF16)
    lo = (a - hi.astype(F32)).astype(BF16)
    return hi, lo


def _dot3(a, w, w_rows=False):
    ah, al = _split_hi_lo(a)
    wh, wl = _split_hi_lo(w)
    contract = (((1,), (1 if w_rows else 0,)), ((), ()))
    d = functools.partial(lax.dot_general, dimension_numbers=contract, preferred_element_type=F32)
    return d(ah, wh) + (d(ah, wl) + d(al, wh))


def _pack_bf16_pair(lo_f32, hi_f32):
    lo_bits = lax.bitcast_convert_type(lo_f32.astype(BF16).astype(F32), U32)
    hi_bits = lax.bitcast_convert_type(hi_f32.astype(BF16).astype(F32), U32)
    return hi_bits | (lo_bits >> 16)


def _unpack_bf16_pair(word):
    lo = lax.bitcast_convert_type(word << 16, F32)
    hi = lax.bitcast_convert_type(word & jnp.uint32(0xFFFF0000), F32)
    return lo, hi


ROW_TILE = 8


def _store_token_rows(ref, words):
    t = words.shape[0]
    for sub in range(ROW_TILE):
        ref[pl.ds(sub, t, stride=ROW_TILE), :] = words[:, sub * LANES:(sub + 1) * LANES]


def _load_token_rows(ref, sub, t):
    return ref[pl.ds(sub, t, stride=ROW_TILE), :]


def _rms_modulate(x, g, shift, scale):
    ms = jnp.mean(x * x, axis=-1, keepdims=True)
    y = x * lax.rsqrt(ms + NORM_EPS) * g
    return y * (1.0 + scale) + shift


def _adaln_kernel(c_ref, w_ref, b_ref, o_ref):
    c = c_ref[...]
    cond = c * jax.nn.sigmoid(c)
    acc = jnp.dot(cond.astype(BF16), w_ref[...].astype(BF16), preferred_element_type=F32)
    o_ref[...] = acc + b_ref[...]


def adaln(c, ada_w, ada_b, tn=1024):
    depth, d, n6 = ada_w.shape
    bsz = c.shape[0]
    rows = 8
    cp = jnp.zeros((rows, d), F32).at[:bsz].set(c)
    out = pl.pallas_call(
        _adaln_kernel,
        grid=(depth, n6 // tn),
        in_specs=[pl.BlockSpec((rows, d), lambda l, j: (0, 0)),
                  pl.BlockSpec((None, d, tn), lambda l, j: (l, 0, j)),
                  pl.BlockSpec((None, 1, tn), lambda l, j: (l, 0, j))],
        out_specs=pl.BlockSpec((None, rows, tn), lambda l, j: (l, 0, j)),
        out_shape=jax.ShapeDtypeStruct((depth, rows, n6), F32),
        compiler_params=_cparams(("arbitrary", "arbitrary")),
        name="adaln",
    )(cp, ada_w, ada_b.reshape(depth, 1, n6))
    return out[:, :bsz].reshape(depth, bsz, 6, d)


ROPE_ROWS = 64


def _norm_matmul_kernel(*refs, rope_tiles, q_tiles, has_extra, tn, nj, nsteps, w_rows):
    if has_extra:
        x_ref, g_ref, mod_ref, w_ref, c_ref, s_ref, we_ref, be_ref, o_ref, oe_ref, oet_ref, h_scr, acc_scr = refs
    else:
        x_ref, g_ref, mod_ref, w_ref, c_ref, s_ref, o_ref, h_scr, acc_scr = refs
    t = pl.program_id(0)
    j = t % nj

    @pl.when((j == 0) & (t < nsteps))
    def _():
        h = _rms_modulate(x_ref[...], g_ref[...], mod_ref[0:1, :], mod_ref[1:2, :])
        h_scr[...] = h.astype(BF16)
        if has_extra:
            extra = _dot3(h, we_ref[...], w_rows) + be_ref[...]
            oe_ref[...] = extra
            oet_ref[...] = extra.T

    @pl.when(t == 0)
    def _():
        acc_scr[...] = jnp.zeros_like(acc_scr)

    contract = (((1,), (1 if w_rows else 0,)), ((), ()))
    acc = lax.dot_general(h_scr[...], w_ref[...], contract, preferred_element_type=F32)
    tm = acc_scr.shape[0]
    if rope_tiles:
        jp = (t + nj - 1) % nj
        rope_on = jp < rope_tiles
        scale = jnp.where(jp < q_tiles, DA_HEAD_DIM ** -0.5 * LOG2_E, 1.0).astype(F32)
    for r in range(tm // ROPE_ROWS):
        rows = slice(r * ROPE_ROWS, (r + 1) * ROPE_ROWS)
        if rope_tiles:
            cs = jnp.where(rope_on, c_ref[rows, :] * scale, 1.0)
            sn = jnp.where(rope_on, s_ref[rows, :] * scale, 0.0)
        for m in range(tn // LANES):
            cols = slice(m * LANES, (m + 1) * LANES)
            prev = acc_scr[rows, cols]
            if rope_tiles:
                prev = prev * cs + pltpu.roll(prev, LANES // 2, 1) * sn
            o_ref[rows, cols] = prev.astype(o_ref.dtype)
    acc_scr[...] = acc


def norm_matmul(x, g, mod, w, seq, rope=None, extra=None, w_rows=False, tm=1024, tn=1024):
    n, k = x.shape
    m = w.shape[0 if w_rows else 1]
    tiles_per_seq = seq // tm
    nj = m // tn
    nsteps = (n // tm) * nj
    has_extra = extra is not None
    cur = lambda t: jnp.minimum(t, nsteps - 1)
    prv = lambda t: jnp.maximum(t - 1, 0)
    if rope is None:
        dummy = jnp.zeros((8, LANES), F32)
        tabs = (dummy, dummy)
        tab_spec = pl.BlockSpec((8, LANES), lambda t: (0, 0))
        rope_tiles = q_tiles = 0
    else:
        tabs = rope[:2]
        tab_spec = pl.BlockSpec((tm, LANES), lambda t: (prv(t) // nj, 0))
        rope_tiles, q_tiles = rope[2] // tn, rope[3] // tn
    in_specs = [pl.BlockSpec((tm, k), lambda t: (cur(t) // nj, 0)),
                pl.BlockSpec((1, k), lambda t: (0, 0)),
                pl.BlockSpec((None, 6, k), lambda t: (cur(t) // nj // tiles_per_seq, 0, 0)),
                (pl.BlockSpec((tn, k), lambda t: (cur(t) % nj, 0)) if w_rows
                 else pl.BlockSpec((k, tn), lambda t: (0, cur(t) % nj))),
                tab_spec, tab_spec]
    args = [x, g.reshape(1, k), mod, w, *tabs]
    out_specs = pl.BlockSpec((tm, tn), lambda t: (prv(t) // nj, prv(t) % nj))
    out_shape = jax.ShapeDtypeStruct((n, m), BF16)
    if has_extra:
        we, be = extra
        in_specs += [pl.BlockSpec((LANES, k) if w_rows else (k, LANES), lambda t: (0, 0)),
                     pl.BlockSpec((1, LANES), lambda t: (0, 0))]
        args += [we, be]
        out_specs = [out_specs, pl.BlockSpec((tm, LANES), lambda t: (cur(t) // nj, 0)),
                     pl.BlockSpec((LANES, tm), lambda t: (0, cur(t) // nj))]
        out_shape = [out_shape, jax.ShapeDtypeStruct((n, LANES), F32), jax.ShapeDtypeStruct((LANES, n), F32)]
    return pl.pallas_call(
        functools.partial(_norm_matmul_kernel, rope_tiles=rope_tiles, q_tiles=q_tiles,
                          has_extra=has_extra, tn=tn, nj=nj, nsteps=nsteps, w_rows=w_rows),
        grid=(nsteps + 1,),
        in_specs=in_specs,
        out_specs=out_specs,
        out_shape=out_shape,
        scratch_shapes=[pltpu.VMEM((tm, k), BF16), pltpu.VMEM((tm, tn), F32)],
        compiler_params=_cparams(("arbitrary",)),
        name="norm_matmul",
    )(*args)


def _router_kernel(x_ref, g_ref, mod_ref, w_ref, b_ref, hp_ref, rt_ref, cnt_ref, run_scr):
    @pl.when(pl.program_id(0) == 0)
    def _():
        run_scr[...] = jnp.zeros_like(run_scr)

    h = _rms_modulate(x_ref[...], g_ref[...], mod_ref[3:4, :], mod_ref[4:5, :])
    half = h.shape[1] // 2
    _store_token_rows(hp_ref, _pack_bf16_pair(h[:, :half], h[:, half:]))
    tm = h.shape[0]
    hh, hl = _split_hi_lo(h)
    wh, wl = _split_hi_lo(w_ref[...])
    quad = jnp.dot(jnp.concatenate([hh, hl], axis=0), jnp.concatenate([wh, wl], axis=1),
                   preferred_element_type=F32)
    logits = (quad[:tm, :LANES] + (quad[:tm, LANES:] + quad[tm:, :LANES]) + quad[tm:, LANES:]
              + b_ref[...])
    lane = lax.broadcasted_iota(I32, logits.shape, 1).astype(F32)
    ng = float(MOE_GROUPS)
    is_g = lane < ng
    gl = jnp.where(is_g, logits, NEG_BIG)
    gmax = jnp.max(gl, axis=-1, keepdims=True)
    grp = jnp.min(jnp.where(gl == gmax, lane, float(LANES)), axis=-1, keepdims=True)
    p_group = 1.0 / jnp.sum(jnp.where(is_g, jnp.exp(gl - gmax), 0.0), axis=-1, keepdims=True)
    lo = ng + float(MOE_PER_GROUP) * grp
    el = jnp.where((lane >= lo) & (lane < lo + float(MOE_PER_GROUP)), logits, NEG_BIG)
    v1 = jnp.max(el, axis=-1, keepdims=True)
    i1 = jnp.min(jnp.where(el == v1, lane, float(LANES)), axis=-1, keepdims=True)
    el2 = jnp.where(lane == i1, NEG_BIG, el)
    v2 = jnp.max(el2, axis=-1, keepdims=True)
    i2 = jnp.min(jnp.where(el2 == v2, lane, float(LANES)), axis=-1, keepdims=True)
    ex = jnp.exp(v2 - v1)
    w1 = p_group / (1.0 + ex)
    w2 = p_group * (ex / (1.0 + ex))
    oh1 = (lane == i1).astype(F32)
    oh2 = (lane == i2).astype(F32)
    oh = oh1 + oh2
    r_i = lax.broadcasted_iota(I32, (tm, tm), 0)
    c_i = lax.broadcasted_iota(I32, (tm, tm), 1)
    before = (c_i < r_i).astype(BF16)
    prior = jnp.dot(before, oh.astype(BF16), preferred_element_type=F32) + run_scr[...]
    rank1 = jnp.sum(oh1 * prior, axis=-1, keepdims=True)
    rank2 = jnp.sum(oh2 * prior, axis=-1, keepdims=True)
    run_scr[...] += jnp.sum(oh, axis=0, keepdims=True)
    cnt_ref[...] = run_scr[...]
    rt_ref[...] = jnp.where(lane == 0.0, i1 - ng,
                  jnp.where(lane == 1.0, i2 - ng,
                  jnp.where(lane == 2.0, w1,
                  jnp.where(lane == 3.0, w2,
                  jnp.where(lane == 4.0, rank1,
                  jnp.where(lane == 5.0, rank2, 0.0))))))


def moe_router(x, g, mod, w_r, b_r, seq, tm=512):
    n, d = x.shape
    tiles_per_seq = seq // tm
    return pl.pallas_call(
        _router_kernel,
        grid=(n // tm,),
        in_specs=[pl.BlockSpec((tm, d), lambda i: (i, 0)),
                  pl.BlockSpec((1, d), lambda i: (0, 0)),
                  pl.BlockSpec((None, 6, d), lambda i: (i // tiles_per_seq, 0, 0)),
                  pl.BlockSpec((d, LANES), lambda i: (0, 0)),
                  pl.BlockSpec((1, LANES), lambda i: (0, 0))],
        out_specs=[pl.BlockSpec((tm * ROW_TILE, LANES), lambda i: (i, 0)),
                   pl.BlockSpec((tm, LANES), lambda i: (i, 0)),
                   pl.BlockSpec((1, LANES), lambda i: (0, 0))],
        out_shape=[jax.ShapeDtypeStruct((n * ROW_TILE, LANES), U32),
                   jax.ShapeDtypeStruct((n, LANES), F32),
                   jax.ShapeDtypeStruct((1, LANES), F32)],
        scratch_shapes=[pltpu.VMEM((1, LANES), F32)],
        compiler_params=_cparams(("arbitrary",)),
        name="moe_router",
    )(x, g.reshape(1, d), mod, w_r, b_r)


def _matmul_res_kernel(a_ref, w_ref, res_ref, mod_ref, o_ref, *, gate_row):
    acc = jnp.dot(a_ref[...], w_ref[...], preferred_element_type=F32)
    o_ref[...] = res_ref[...] + mod_ref[gate_row:gate_row + 1, :] * acc


def matmul_res(a, w, res, mod, seq, gate_row, tm=512, tn=2048):
    n, k = a.shape
    m = w.shape[1]
    tiles_per_seq = seq // tm
    return pl.pallas_call(
        functools.partial(_matmul_res_kernel, gate_row=gate_row),
        grid=(n // tm, m // tn),
        in_specs=[pl.BlockSpec((tm, k), lambda i, j: (i, 0)),
                  pl.BlockSpec((k, tn), lambda i, j: (0, j)),
                  pl.BlockSpec((tm, tn), lambda i, j: (i, j)),
                  pl.BlockSpec((None, 6, tn), lambda i, j: (i // tiles_per_seq, 0, j))],
        out_specs=pl.BlockSpec((tm, tn), lambda i, j: (i, j)),
        out_shape=jax.ShapeDtypeStruct((n, m), F32),
        compiler_params=_cparams(("arbitrary", "arbitrary")),
        name="matmul_res",
    )(a, w, res, mod)


ATTN_GROUPS = (8, 4, 2)


def _grouped_loop(n, body, groups):
    done = 0
    for group in groups:
        trips = (n - done) // group

        def grouped(gi, c, group=group, done=done):
            for t in range(group):
                c = body(done + gi * group + t, c)
            return c

        lax.fori_loop(0, trips, grouped, 0)
        done = done + trips * group
    lax.fori_loop(done, n, body, 0)


def _attn_kernel(q_ref, k_ref, v_ref, lq1_ref, lk1_ref, lq2_ref, lk2_ref, g_ref, o_ref,
                 s_scr, m_scr, l_scr, acc_scr, bias_scr, *, lam_init, tq, tk, seq):
    d = DA_HEAD_DIM
    nlane = tk // LANES
    ndiag = tq // tk
    lam = (jnp.exp(jnp.sum(lq1_ref[...] * lk1_ref[...], axis=-1, keepdims=True))
           - jnp.exp(jnp.sum(lq2_ref[...] * lk2_ref[...], axis=-1, keepdims=True)) + lam_init)
    shift = ATTN_CHUNK.bit_length() - 1
    row_chunk = jnp.right_shift(lax.broadcasted_iota(I32, (tq, tk), 0), shift)
    col_chunk = jnp.right_shift(lax.broadcasted_iota(I32, (tq, tk), 1), shift)
    bias_scr[0] = jnp.zeros((tq, tk), F32)
    for t in range(ndiag):
        bias_scr[t + 1] = jnp.where(col_chunk + t * (tk // ATTN_CHUNK) <= row_chunk, 0.0, NEG_BIG)
    nt = (((1,), (1,)), ((), ()))

    def lane_fold(a, op):
        part = a[:, 0:LANES]
        for cb in range(1, nlane):
            part = op(part, a[:, cb * LANES:(cb + 1) * LANES])
        return part

    def q_body(qi, _):
        qs = pl.multiple_of(qi * tq, tq)
        qm = (q_ref[pl.ds(qs, tq), 0:d], q_ref[pl.ds(qs, tq), d:2 * d])
        m_scr[...] = jnp.full_like(m_scr, NEG_BIG)

        nfull = qi * ndiag

        def score_body(j, c):
            ks = pl.multiple_of(j * tk, tk)
            bias = bias_scr[jnp.maximum(j - nfull + 1, 0)]
            for mp in range(2):
                k = k_ref[pl.ds(ks, tk), mp * d:(mp + 1) * d]
                s = lax.dot_general(qm[mp], k, nt, preferred_element_type=F32) + bias
                s_scr[mp, j] = s
                m_scr[mp] = jnp.maximum(m_scr[mp], lane_fold(s, jnp.maximum))
            return c

        _grouped_loop(nfull + ndiag, score_body, ATTN_GROUPS)
        m = [jnp.max(m_scr[mp], axis=-1, keepdims=True) for mp in range(2)]
        l_scr[...] = jnp.zeros_like(l_scr)
        acc_scr[...] = jnp.zeros_like(acc_scr)

        def pv_body(j, c):
            ks = pl.multiple_of(j * tk, tk)
            v = v_ref[pl.ds(ks, tk), :]
            for mp in range(2):
                p = jnp.exp2(s_scr[mp, j] - m[mp])
                l_scr[mp] += lane_fold(p, jnp.add)
                acc_scr[mp] += jnp.dot(p.astype(BF16), v, preferred_element_type=F32)
            return c

        _grouped_loop(nfull + ndiag, pv_body, ATTN_GROUPS)
        l = [jnp.sum(l_scr[mp], axis=-1, keepdims=True) for mp in range(2)]
        o = acc_scr[0] / l[0] - lam * (acc_scr[1] / l[1])
        ms = jnp.mean(o * o, axis=-1, keepdims=True)
        o = o * lax.rsqrt(ms + NORM_EPS) * g_ref[...] * (1.0 - lam_init)
        o_ref[pl.ds(qs, tq), :] = o.astype(o_ref.dtype)
        return 0

    lax.fori_loop(0, seq // tq, q_body, 0)


def diff_attention(qkv, lq1, lk1, lq2, lk2, head_g, lam_init, tq=512, tk=256):
    bsz, seq, _ = qkv.shape
    h, dv = DA_HEADS, 2 * DA_HEAD_DIM
    vec = lambda a: a.reshape(1, -1).astype(F32)
    small = lambda n: pl.BlockSpec((1, n), lambda b, hh: (0, 0))
    return pl.pallas_call(
        functools.partial(_attn_kernel, lam_init=lam_init, tq=tq, tk=tk, seq=seq),
        grid=(bsz, h),
        in_specs=[pl.BlockSpec((None, seq, dv), lambda b, hh: (b, 0, hh)),
                  pl.BlockSpec((None, seq, dv), lambda b, hh: (b, 0, h + hh)),
                  pl.BlockSpec((None, seq, dv), lambda b, hh: (b, 0, 2 * h + hh)),
                  small(DA_HEAD_DIM), small(DA_HEAD_DIM), small(DA_HEAD_DIM), small(DA_HEAD_DIM),
                  small(dv)],
        out_specs=pl.BlockSpec((None, seq, dv), lambda b, hh: (b, 0, hh)),
        out_shape=jax.ShapeDtypeStruct((bsz, seq, h * dv), BF16),
        scratch_shapes=[pltpu.VMEM((2, seq // tk, tq, tk), F32), pltpu.VMEM((2, tq, LANES), F32),
                        pltpu.VMEM((2, tq, LANES), F32), pltpu.VMEM((2, tq, dv), F32),
                        pltpu.VMEM((tq // tk + 1, tq, tk), F32)],
        compiler_params=_cparams(("arbitrary", "arbitrary")),
        name="diff_attention",
    )(qkv, qkv, qkv, vec(lq1), vec(lk1), vec(lq2), vec(lk2), vec(head_g))


def _conv_silu_kernel(x_ref, w_ref, b_ref, o_ref, *, k_tile0):
    j = pl.program_id(1)
    scale = jnp.where(j >= k_tile0, ML_QK_DIM ** -0.5, 1.0).astype(F32)
    taps = [w_ref[ML_CONV - 1 - s:ML_CONV - s, :] for s in range(ML_CONV)]

    def conv(x, shifted):
        y = x * taps[0] + b_ref[...]
        for s in range(1, ML_CONV):
            y = y + shifted(x, s) * taps[s]
        return (y * jax.nn.sigmoid(y) * scale).astype(o_ref.dtype)

    o_ref[...] = conv(x_ref[...].astype(F32), lambda x, s: pltpu.roll(x, s, 0))
    head = ROW_TILE
    row = lax.broadcasted_iota(I32, (head, x_ref.shape[1]), 0)
    o_ref[0:head, :] = conv(x_ref[0:head, :].astype(F32),
                            lambda x, s: jnp.where(row >= s, pltpu.roll(x, s, 0), 0.0))


def conv_silu(proj, conv_w, conv_b, tc=256):
    bsz, seq, _ = proj.shape
    cols = conv_w.shape[1]
    return pl.pallas_call(
        functools.partial(_conv_silu_kernel, k_tile0=(cols // 2) // tc),
        grid=(bsz, cols // tc),
        in_specs=[pl.BlockSpec((None, seq, tc), lambda b, j: (b, 0, j)),
                  pl.BlockSpec((ML_CONV, tc), lambda b, j: (0, j)),
                  pl.BlockSpec((1, tc), lambda b, j: (0, j))],
        out_specs=pl.BlockSpec((None, seq, tc), lambda b, j: (b, 0, j)),
        out_shape=jax.ShapeDtypeStruct((bsz, seq, cols), BF16),
        compiler_params=_cparams(("arbitrary", "arbitrary")),
        name="conv_silu",
    )(proj, conv_w, conv_b.reshape(1, cols))


def _mlstm_kernel(q_ref, k_ref, v_ref, op_ref, gc_ref, gr_ref, hg_ref, o_ref,
                  cx_scr, m_scr, *, chunk):
    c = pl.program_id(1)
    nh, dqk, dv = ML_HEADS, ML_QK_DIM, ML_V_DIM

    @pl.when(c == 0)
    def _():
        cx_scr[...] = jnp.zeros_like(cx_scr)
        m_scr[...] = jnp.zeros_like(m_scr)

    gc = gc_ref[...]
    gr = gr_ref[...]
    lf_c = jax.nn.log_sigmoid(gc)
    lf_r = jax.nn.log_sigmoid(gr)
    r_i = lax.broadcasted_iota(I32, (chunk, chunk), 0)
    c_i = lax.broadcasted_iota(I32, (chunk, chunk), 1)
    causal = c_i <= r_i
    tril = causal.astype(F32)
    triu = (r_i <= c_i).astype(F32)
    b_c = _dot3(tril, lf_c)
    b_r = _dot3(lf_r, triu)
    nt = (((1,), (1,)), ((), ()))
    tn_ = (((0,), (0,)), ((), ()))
    ones_l = jnp.ones((chunk, LANES), BF16)
    ones_v = jnp.ones((dv, LANES), BF16)
    wide = lambda a: jnp.concatenate([a] * (dv // LANES), axis=1)

    for h in range(nh):
        q = q_ref[:, h * dqk:(h + 1) * dqk]
        k = k_ref[:, h * dqk:(h + 1) * dqk]
        v_ext = jnp.concatenate([v_ref[:, h * dv:(h + 1) * dv], ones_l], axis=1)
        bc = b_c[:, nh + h:nh + h + 1]
        br = b_r[nh + h:nh + h + 1, :]
        ig_c = gc[:, h:h + 1]
        ig_r = gr[h:h + 1, :]
        m_prev = m_scr[h:h + 1, :]
        dmat = jnp.where(causal, bc - br + ig_r, NEG_BIG)
        inter = bc + m_prev
        m_t = jnp.maximum(inter, jnp.max(dmat, axis=-1, keepdims=True))
        w = jnp.exp(dmat - m_t)
        s = lax.dot_general(q, k, nt, preferred_element_type=F32) * w
        decay = jnp.exp(inter - m_t)
        cx = cx_scr[h]
        tot = (jnp.dot(s.astype(BF16), v_ext, preferred_element_type=F32)
               + decay * jnp.dot(q, cx.astype(BF16), preferred_element_type=F32))
        den = jnp.maximum(jnp.abs(tot[:, dv:]), jnp.exp(-m_t))
        hh = tot[:, :dv] / wide(den)
        b_last = bc[chunk - 1:chunk, :]
        g = b_last - bc + ig_c
        m_new = jnp.maximum(b_last + m_prev, jnp.max(g, axis=0, keepdims=True))
        carry_decay = jnp.exp(b_last + m_prev - m_new)
        wg = jnp.exp(g - m_new)
        wv = (wg * v_ext.astype(F32)).astype(BF16)
        cx_scr[h] = carry_decay * cx + lax.dot_general(k, wv, tn_, preferred_element_type=F32)
        m_scr[h:h + 1, :] = m_new
        ms = jnp.dot((hh * hh).astype(BF16), ones_v, preferred_element_type=F32) * (1.0 / dv)
        hn = hh * wide(lax.rsqrt(ms + NORM_EPS)) * hg_ref[:, h * dv:(h + 1) * dv]
        og = jax.nn.sigmoid(op_ref[:, h * dv:(h + 1) * dv].astype(F32))
        o_ref[:, h * dv:(h + 1) * dv] = (og * hn).astype(o_ref.dtype)


def mlstm(qk, proj, gates_c, gates_r, head_g, chunk=256):
    bsz, seq, _ = qk.shape
    nh = ML_HEADS
    qw, vw = nh * ML_QK_DIM, nh * ML_V_DIM
    v_blk = (2 * qw) // vw
    return pl.pallas_call(
        functools.partial(_mlstm_kernel, chunk=chunk),
        grid=(bsz, seq // chunk),
        in_specs=[pl.BlockSpec((None, chunk, qw), lambda b, c: (b, c, 0)),
                  pl.BlockSpec((None, chunk, qw), lambda b, c: (b, c, 1)),
                  pl.BlockSpec((None, chunk, vw), lambda b, c: (b, c, v_blk)),
                  pl.BlockSpec((None, chunk, vw), lambda b, c: (b, c, v_blk + 1)),
                  pl.BlockSpec((None, chunk, LANES), lambda b, c: (b, c, 0)),
                  pl.BlockSpec((2 * nh, chunk), lambda b, c: (0, b * (seq // chunk) + c)),
                  pl.BlockSpec((1, vw), lambda b, c: (0, 0))],
        out_specs=pl.BlockSpec((None, chunk, vw), lambda b, c: (b, c, 0)),
        out_shape=jax.ShapeDtypeStruct((bsz, seq, vw), BF16),
        scratch_shapes=[pltpu.VMEM((nh, ML_QK_DIM, ML_V_DIM + LANES), F32),
                        pltpu.VMEM((nh, 1), F32)],
        compiler_params=_cparams(("arbitrary", "arbitrary")),
        name="mlstm",
    )(qk, qk, proj, proj, gates_c, gates_r, head_g.reshape(1, vw))


def _dest_kernel(rt_ref, ps_ref, o_ref):
    rt = rt_ref[...]
    lane = lax.broadcasted_iota(I32, rt.shape, 1).astype(F32)
    ng = float(MOE_GROUPS)
    ps = ps_ref[...]
    d1 = jnp.sum(jnp.where(lane == rt[:, 0:1] + ng, ps, 0.0), axis=-1, keepdims=True) + rt[:, 4:5]
    d2 = jnp.sum(jnp.where(lane == rt[:, 1:2] + ng, ps, 0.0), axis=-1, keepdims=True) + rt[:, 5:6]
    o_ref[...] = jnp.where(lane == 0.0, d1, jnp.where(lane == 1.0, d2, 0.0)).astype(I32)


def moe_dest(route, pad_start_lanes, tm=2048):
    n = route.shape[0]
    return pl.pallas_call(
        _dest_kernel,
        grid=(n // tm,),
        in_specs=[pl.BlockSpec((tm, LANES), lambda i: (i, 0)),
                  pl.BlockSpec((1, LANES), lambda i: (0, 0))],
        out_specs=pl.BlockSpec((tm, LANES), lambda i: (i, 0)),
        out_shape=jax.ShapeDtypeStruct((n, LANES), I32),
        compiler_params=_cparams(("arbitrary",)),
        name="moe_dest",
    )(route, pad_start_lanes)


DMA_UNROLL = 16


def _wait_rows(src_rows, dst_rows, sem, copies):
    for _ in range(copies):
        pltpu.make_async_copy(src_rows, dst_rows, sem).wait()


def _dispatch_kernel(zb_ref, dest_ref, hp_ref, xin_ref, zero_scr, ring, sems, *, tt, tm, nb, ntiles):
    i = pl.program_id(0)
    slot = i % 2
    blk_rows = tm * ROW_TILE
    tile_rows = tt * ROW_TILE

    @pl.when(i == 0)
    def _():
        zero_scr[...] = jnp.zeros_like(zero_scr)

        def zero_copy(blk):
            return pltpu.make_async_copy(zero_scr, xin_ref.at[pl.ds(pl.multiple_of(blk * blk_rows, blk_rows), blk_rows)],
                                         sems.at[2])

        def start(blk, c):
            @pl.when(zb_ref[blk] == 1)
            def _():
                zero_copy(blk).start()
            return c

        def wait(blk, c):
            @pl.when(zb_ref[blk] == 1)
            def _():
                zero_copy(blk).wait()
            return c

        lax.fori_loop(0, nb, start, 0)
        lax.fori_loop(0, nb, wait, 0)

    ring[slot] = hp_ref[...]

    def start_rows(r, c):
        src = ring.at[slot, pl.ds(pl.multiple_of(r * ROW_TILE, ROW_TILE), ROW_TILE)]
        for kk in range(2):
            row = pl.multiple_of(dest_ref[0, 0, 2 * r + kk] * ROW_TILE, ROW_TILE)
            pltpu.make_async_copy(src, xin_ref.at[pl.ds(row, ROW_TILE)], sems.at[slot]).start(priority=kk)
        return c

    lax.fori_loop(0, tt, start_rows, 0, unroll=DMA_UNROLL)
    whole = xin_ref.at[pl.ds(0, tile_rows)]

    @pl.when(i > 0)
    def _():
        _wait_rows(ring.at[1 - slot], whole, sems.at[1 - slot], 2)

    @pl.when(i == ntiles - 1)
    def _():
        _wait_rows(ring.at[slot], whole, sems.at[slot], 2)


def moe_dispatch(hpack, dest, zero_blocks, rows, tm, tt=1024):
    n = hpack.shape[0] // ROW_TILE
    tt = min(tt, n)
    grid_spec = pltpu.PrefetchScalarGridSpec(
        num_scalar_prefetch=1,
        grid=(n // tt,),
        in_specs=[pl.BlockSpec((1, 1, 2 * tt), lambda i, zb: (i, 0, 0), memory_space=pltpu.SMEM),
                  pl.BlockSpec((tt * ROW_TILE, LANES), lambda i, zb: (i, 0))],
        out_specs=pl.BlockSpec(memory_space=pl.ANY),
        scratch_shapes=[pltpu.VMEM((tm * ROW_TILE, LANES), U32), pltpu.VMEM((2, tt * ROW_TILE, LANES), U32),
                        pltpu.SemaphoreType.DMA((3,))],
    )
    return pl.pallas_call(
        functools.partial(_dispatch_kernel, tt=tt, tm=tm, nb=rows // tm, ntiles=n // tt),
        grid_spec=grid_spec,
        out_shape=jax.ShapeDtypeStruct((rows * ROW_TILE, LANES), U32),
        compiler_params=_cparams(("arbitrary",)),
        name="moe_dispatch",
    )(zero_blocks, dest.reshape(n // tt, 1, 2 * tt), hpack)


def _expert_kernel(be_ref, first_ref, nxt_ref, nv_ref, nu_ref, x_ref, wgu_hbm, wd_hbm, y_ref,
                   wgu_bf, wd_bf, stg_gu, stg_d, xs_scr, acc_scr, sems, *, layer, th, cr):
    i = pl.program_id(0)

    def fetch(e):
        return (pltpu.make_async_copy(wgu_hbm.at[layer, e], stg_gu, sems.at[0]),
                pltpu.make_async_copy(wd_hbm.at[layer, e], stg_d, sems.at[1]))

    @pl.when(i < nu_ref[0])
    def _():
        @pl.when(first_ref[i] == 1)
        def _():
            @pl.when(i == 0)
            def _():
                for cp in fetch(be_ref[i]):
                    cp.start()

            for cp in fetch(be_ref[i]):
                cp.wait()

            def cast_rows(src, dst):
                def body(r, c):
                    rs = pl.multiple_of(r * cr, cr)
                    dst[pl.ds(rs, cr), :] = src[pl.ds(rs, cr), :].astype(BF16)
                    return c
                lax.fori_loop(0, src.shape[0] // cr, body, 0)

            cast_rows(stg_gu, wgu_bf)
            cast_rows(stg_d, wd_bf)

            @pl.when(nxt_ref[i] >= 0)
            def _():
                for cp in fetch(nxt_ref[i]):
                    cp.start()

        tm = xs_scr.shape[0]
        half = xs_scr.shape[1] // 2
        hid = wd_bf.shape[0]

        def mlp(rows):
            for sub in range(ROW_TILE):
                lo, hi = _unpack_bf16_pair(_load_token_rows(x_ref, sub, rows))
                xs_scr[0:rows, sub * LANES:(sub + 1) * LANES] = lo.astype(BF16)
                xs_scr[0:rows, half + sub * LANES:half + (sub + 1) * LANES] = hi.astype(BF16)
            xs = xs_scr[0:rows, :]
            for c in range(hid // th):
                gt = jnp.dot(xs, wgu_bf[:, c * th:(c + 1) * th], preferred_element_type=F32)
                up = jnp.dot(xs, wgu_bf[:, hid + c * th:hid + (c + 1) * th], preferred_element_type=F32)
                act = (gt * jax.nn.sigmoid(gt) * up).astype(BF16)
                part = jnp.dot(act, wd_bf[c * th:(c + 1) * th, :], preferred_element_type=F32)
                if c == 0:
                    acc_scr[0:rows, :] = part
                else:
                    acc_scr[0:rows, :] += part
            y = acc_scr[0:rows, :]
            _store_token_rows(y_ref, _pack_bf16_pair(y[:, :half], y[:, half:]))
            if rows < tm:
                y_ref[rows * ROW_TILE:, :] = jnp.zeros(((tm - rows) * ROW_TILE, LANES), U32)

        @pl.when(nv_ref[i] > tm // 2)
        def _():
            mlp(tm)

        @pl.when(nv_ref[i] <= tm // 2)
        def _():
            mlp(tm // 2)


def moe_experts(xin, sched, wgu_all, wd_all, layer, tm, th=512, cr=256):
    _, _, d, hid2 = wgu_all.shape
    hid = hid2 // 2
    nb = xin.shape[0] // (tm * ROW_TILE)
    block_e, first, nxt, valid, n_used = sched

    def blk(i, be, fi, nx, nv, nu):
        return (jnp.minimum(i, nu[0] - 1), 0)

    grid_spec = pltpu.PrefetchScalarGridSpec(
        num_scalar_prefetch=5,
        grid=(nb,),
        in_specs=[pl.BlockSpec((tm * ROW_TILE, LANES), blk),
                  pl.BlockSpec(memory_space=pl.ANY),
                  pl.BlockSpec(memory_space=pl.ANY)],
        out_specs=pl.BlockSpec((tm * ROW_TILE, LANES), blk),
        scratch_shapes=[pltpu.VMEM((d, hid2), BF16), pltpu.VMEM((hid, d), BF16),
                        pltpu.VMEM((d, hid2), F32), pltpu.VMEM((hid, d), F32),
                        pltpu.VMEM((tm, d), BF16), pltpu.VMEM((tm, d), F32),
                        pltpu.SemaphoreType.DMA((2,))],
    )
    return pl.pallas_call(
        functools.partial(_expert_kernel, layer=layer, th=th, cr=cr),
        grid_spec=grid_spec,
        out_shape=jax.ShapeDtypeStruct(xin.shape, U32),
        input_output_aliases={5: 0},
        compiler_params=_cparams(("arbitrary",)),
        name="moe_experts",
    )(block_e, first, nxt, valid, n_used, xin, wgu_all, wd_all)


def _combine_kernel(dcur_ref, dnxt_ref, x_ref, rt_ref, mod_ref, fg_ref, y_ref, o_ref, ya, yb, sems,
                    *, tt, ntiles, final_norm):
    i = pl.program_id(0)
    slot = i % 2

    def issue(dref, sl):
        def body(r, c):
            dst = pl.ds(pl.multiple_of(r * ROW_TILE, ROW_TILE), ROW_TILE)
            for kk, buf in enumerate((ya, yb)):
                row = pl.multiple_of(dref[0, 0, 2 * r + kk] * ROW_TILE, ROW_TILE)
                pltpu.make_async_copy(y_ref.at[pl.ds(row, ROW_TILE)], buf.at[sl, dst], sems.at[sl]).start(priority=kk)
            return c
        lax.fori_loop(0, tt, body, 0, unroll=DMA_UNROLL)

    @pl.when(i == 0)
    def _():
        issue(dcur_ref, 0)

    @pl.when(i + 1 < ntiles)
    def _():
        issue(dnxt_ref, 1 - slot)

    _wait_rows(y_ref.at[pl.ds(0, tt * ROW_TILE)], ya.at[slot], sems.at[slot], 2)
    half = x_ref.shape[1] // 2
    w1 = rt_ref[:, 2:3]
    w2 = rt_ref[:, 3:4]
    ssq = jnp.zeros((tt, 1), F32)
    for sub in range(ROW_TILE):
        a_lo, a_hi = _unpack_bf16_pair(_load_token_rows(ya.at[slot], sub, tt))
        b_lo, b_hi = _unpack_bf16_pair(_load_token_rows(yb.at[slot], sub, tt))
        for base, a, b in ((sub * LANES, a_lo, b_lo), (half + sub * LANES, a_hi, b_hi)):
            cols = slice(base, base + LANES)
            out = x_ref[:, cols] + mod_ref[5:6, cols] * (a * w1 + b * w2)
            o_ref[:, cols] = out
            if final_norm:
                ssq = ssq + jnp.sum(out * out, axis=-1, keepdims=True)
    if final_norm:
        r = lax.rsqrt(ssq / (2 * half) + NORM_EPS)
        o_ref[...] = o_ref[...] * r * fg_ref[...]


def moe_combine(x, y, dest, route, mod, final_g, seq, final_norm, tt=256):
    n, d = x.shape
    tiles_per_seq = seq // tt
    ntiles = n // tt
    dest3 = dest.reshape(ntiles, 1, 2 * tt)
    return pl.pallas_call(
        functools.partial(_combine_kernel, tt=tt, ntiles=ntiles, final_norm=final_norm),
        grid=(ntiles,),
        in_specs=[pl.BlockSpec((1, 1, 2 * tt), lambda i: (i, 0, 0), memory_space=pltpu.SMEM),
                  pl.BlockSpec((1, 1, 2 * tt), lambda i: (jnp.minimum(i + 1, ntiles - 1), 0, 0),
                               memory_space=pltpu.SMEM),
                  pl.BlockSpec((tt, d), lambda i: (i, 0)),
                  pl.BlockSpec((tt, LANES), lambda i: (i, 0)),
                  pl.BlockSpec((None, 6, d), lambda i: (i // tiles_per_seq, 0, 0)),
                  pl.BlockSpec((1, d), lambda i: (0, 0)),
                  pl.BlockSpec(memory_space=pl.ANY)],
        out_specs=pl.BlockSpec((tt, d), lambda i: (i, 0)),
        out_shape=jax.ShapeDtypeStruct((n, d), F32),
        scratch_shapes=[pltpu.VMEM((2, tt * ROW_TILE, LANES), U32), pltpu.VMEM((2, tt * ROW_TILE, LANES), U32),
                        pltpu.SemaphoreType.DMA((2,))],
        compiler_params=_cparams(("arbitrary",)),
        name="moe_combine",
    )(dest3, dest3, x, route, mod, final_g.reshape(1, d), y)


def _expert_schedule(counts, tm, nb):
    ne = counts.shape[0]
    ids = jnp.arange(ne, dtype=I32)
    padded = ((counts + tm - 1) // tm) * tm
    pad_end = jnp.sum(jnp.where(ids[None, :] <= ids[:, None], padded[None, :], 0), axis=1)
    pad_start = pad_end - padded
    total = jnp.sum(padded)
    n_used = total // tm
    blk0 = jnp.arange(nb, dtype=I32) * tm
    block_e = jnp.minimum(jnp.sum((pad_end[None, :] <= blk0[:, None]).astype(I32), axis=1), ne - 1)
    onehot = block_e[:, None] == ids[None, :]
    pick = lambda v: jnp.sum(jnp.where(onehot, v[None, :], 0), axis=1)
    first = (blk0 == pick(pad_start)).astype(I32)
    later = jnp.min(jnp.where((ids[None, :] > ids[:, None]) & (counts[None, :] > 0), ids[None, :], ne), axis=1)
    nxt = pick(jnp.where(later < ne, later, -1))
    has_padding = (blk0 + tm == pick(pad_end)) & (pick(counts % tm) != 0)
    zero_blocks = ((blk0 >= total) | has_padding).astype(I32)
    valid = jnp.clip(pick(pad_start + counts) - blk0, 0, tm)
    return pad_start, zero_blocks, (block_e, first, nxt.astype(I32), valid.astype(I32),
                                    n_used.astype(I32).reshape(1))


def moe_router_weights(w_group, b_group, w_expert, b_expert):
    d = w_group.shape[0]
    ng, ne = w_group.shape[1], w_expert.shape[1]
    w_r = jnp.concatenate([w_group, w_expert, jnp.zeros((d, LANES - ng - ne), F32)], axis=1)
    b_r = jnp.concatenate([b_group, b_expert, jnp.zeros((LANES - ng - ne,), F32)]).reshape(1, LANES)
    return w_r, b_r


def hier_moe_layer(x, g, mod, w_r, b_r, wgu_all, wd_all, layer, final_g, seq, final_norm, tm=256):
    n, d = x.shape
    ng, ne = MOE_GROUPS, MOE_EXPERTS
    hpack, route, cnt = moe_router(x, g, mod, w_r, b_r, seq)
    counts = cnt[0, ng:ng + ne].astype(I32)
    rows = 2 * n + ne * tm
    pad_start, zero_blocks, sched = _expert_schedule(counts, tm, rows // tm)
    ps_lanes = jnp.concatenate([jnp.zeros((ng,), F32), pad_start.astype(F32),
                                jnp.zeros((LANES - ng - ne,), F32)]).reshape(1, LANES)
    dest = moe_dest(route, ps_lanes, tm=min(2048, n))[:, :2].reshape(-1)
    xin = moe_dispatch(hpack, dest, zero_blocks, rows, tm)
    y = moe_experts(xin, sched, wgu_all, wd_all, layer, tm)
    return moe_combine(x, y, dest, route, mod, final_g, seq, final_norm)


def _rope_tables(positions):
    half = ROPE_DIM // 2
    inv_freq = ROPE_THETA ** (-jnp.arange(half, dtype=F32) * 2.0 / ROPE_DIM)
    gap = jnp.zeros((LANES // 2 - half,), F32)
    freq = jnp.concatenate([-inv_freq, gap, inv_freq, gap])
    ang = positions.astype(F32).reshape(-1, 1) * freq[None, :]
    return jnp.cos(ang), jnp.sin(ang)


def _weight_prep_kernel(w_ref, o_ref, *, pair_tiles, tn):
    j = pl.program_id(0)
    half = ROPE_DIM // 2
    mid = LANES // 2

    @pl.when(j >= pair_tiles)
    def _():
        o_ref[...] = w_ref[...].astype(BF16)

    if pair_tiles:
        @pl.when(j < pair_tiles)
        def _():
            lane = lax.broadcasted_iota(I32, (w_ref.shape[0], LANES), 1)
            for m in range(tn // LANES):
                t = w_ref[:, m * LANES:(m + 1) * LANES]
                up = pltpu.roll(t, LANES - half, 1)
                down = pltpu.roll(t, mid - half, 1)
                new = jnp.where(lane < half, t, jnp.where(lane < mid, up, jnp.where(lane < mid + half, down, t)))
                o_ref[:, m * LANES:(m + 1) * LANES] = new.astype(BF16)


def weight_prep(w, cols, pair_cols=0, tn=512, w_rows=False):
    if w_rows:
        k = w.shape[1]
        return pl.pallas_call(
            functools.partial(_weight_prep_kernel, pair_tiles=0, tn=tn),
            grid=(cols // tn,),
            in_specs=[pl.BlockSpec((tn, k), lambda j: (j, 0))],
            out_specs=pl.BlockSpec((tn, k), lambda j: (j, 0)),
            out_shape=jax.ShapeDtypeStruct((cols, k), BF16),
            compiler_params=_cparams(("arbitrary",)),
            name="weight_prep",
        )(w)
    k = w.shape[0]
    return pl.pallas_call(
        functools.partial(_weight_prep_kernel, pair_tiles=pair_cols // tn, tn=tn),
        grid=(cols // tn,),
        in_specs=[pl.BlockSpec((k, tn), lambda j: (0, j))],
        out_specs=pl.BlockSpec((k, tn), lambda j: (0, j)),
        out_shape=jax.ShapeDtypeStruct((k, cols), BF16),
        compiler_params=_cparams(("arbitrary",)),
        name="weight_prep",
    )(w)


def kernel(x, c, positions, ada_w, ada_b, norm_mix_g, norm_ffn_g, final_norm_g, attn_w_in, attn_w_out, attn_lambda_q1, attn_lambda_k1, attn_lambda_q2, attn_lambda_k2, attn_head_norm_g, mlstm_w_in, mlstm_conv_w, mlstm_conv_b, mlstm_gate_b, mlstm_head_norm_g, mlstm_w_out, moe_w_group, moe_b_group, moe_w_expert, moe_b_expert, moe_w_gu, moe_w_down):
    bsz, seq, d = x.shape
    n = bsz * seq
    depth = ada_w.shape[0]
    mod = adaln(c, ada_w, ada_b)
    xf = x.reshape(n, d)
    for i in range(depth):
        jm = i // 2
        if i % 2 == 0:
            qk_cols = 2 * DA_HEADS * 2 * DA_HEAD_DIM
            tabs = _rope_tables(positions)
            w_in = weight_prep(attn_w_in[jm], attn_w_in.shape[2], pair_cols=qk_cols)
            qkv = norm_matmul(xf, norm_mix_g[i], mod[i], w_in, seq, rope=(*tabs, qk_cols, qk_cols // 2))
            lam_init = 0.8 - 0.6 * math.exp(-0.3 * i)
            mixed = diff_attention(qkv.reshape(bsz, seq, -1), attn_lambda_q1[jm], attn_lambda_k1[jm],
                                   attn_lambda_q2[jm], attn_lambda_k2[jm], attn_head_norm_g[jm], lam_init)
            w_out = attn_w_out[jm]
        else:
            qk_cols = 2 * ML_HEADS * ML_QK_DIM
            main_cols = qk_cols + 2 * ML_HEADS * ML_V_DIM
            w_in_t = jnp.swapaxes(mlstm_w_in[jm], 0, 1)
            ngate = 2 * ML_HEADS
            w_gate = jnp.concatenate([w_in_t[main_cols:], jnp.zeros((LANES - ngate, d), F32)], axis=0)
            b_gate = jnp.concatenate([mlstm_gate_b[jm], jnp.zeros((LANES - ngate,), F32)]).reshape(1, LANES)
            proj, gates, gates_t = norm_matmul(xf, norm_mix_g[i], mod[i],
                                               weight_prep(w_in_t, main_cols, w_rows=True), seq,
                                               extra=(w_gate, b_gate), w_rows=True)
            proj = proj.reshape(bsz, seq, main_cols)
            qk = conv_silu(proj, mlstm_conv_w[jm], mlstm_conv_b[jm])
            mixed = mlstm(qk, proj, gates.reshape(bsz, seq, LANES), gates_t, mlstm_head_norm_g[jm])
            w_out = mlstm_w_out[jm]
        w_r, b_r = moe_router_weights(moe_w_group[i], moe_b_group[i], moe_w_expert[i], moe_b_expert[i])
        xf = matmul_res(mixed.reshape(n, -1), w_out.astype(BF16), xf, mod[i], seq, gate_row=2)
        xf = hier_moe_layer(xf, norm_ffn_g[i], mod[i], w_r, b_r, moe_w_gu, moe_w_down, i, final_norm_g, seq,
                            final_norm=(i == depth - 1))
    return xf.reshape(bsz, seq, d)
```
